```python
import jax, jax.numpy as jnp
from jax import lax
import numpy as np

D_MODEL = 2048
BATCH = 2
SEQ = 4096
DEPTH = 1
DEC_BATCH = 32
DEC_SEQ = 1
PAST_LEN = 16384
PAGE_SIZE = 128

D_POOL = D_MODEL // 2
POOL_WINDOWS = (2, 4, 8, 16)
N_POOL_GROUPS = len(POOL_WINDOWS)
POOL_GC = D_POOL // N_POOL_GROUPS
POOL_CTX = max(POOL_WINDOWS) - 1
HEAD_DIM = 128
N_HEADS = (D_MODEL // 2) // HEAD_DIM
N_KV_HEADS = 2
GROUP = N_HEADS // N_KV_HEADS
D_ATTN = N_HEADS * HEAD_DIM
D_KV = N_KV_HEADS * HEAD_DIM
CMP_BLOCK = 32
SLC_BLOCK = 64
SLC_RATIO = SLC_BLOCK // CMP_BLOCK
TOP_N = 16
WINDOW = 512
Q_BLOCK = 128
N_BRANCH_NSA = 3
N_MERGE = 2
ATTN_SCALE = HEAD_DIM ** -0.5
ALPHA = (2.0 * DEPTH) ** 0.25
BETA = (8.0 * DEPTH) ** -0.25
LN_EPS = 1e-5
NEG = -1e30
FORCE = 1e6
SPLITS = (D_POOL, D_POOL, D_ATTN, 6 * D_KV, N_BRANCH_NSA * N_HEADS, D_ATTN, N_MERGE * D_MODEL)
D_IN = sum(SPLITS)
SPLIT_IDX = [int(i) for i in np.cumsum(SPLITS)[:-1]]

kernel_name = "pool_nsa_gated_parallel_decoder_step"


def layer_norm(x, g, b):
    xf = x.astype(jnp.float32)
    mu = xf.mean(-1, keepdims=True)
    var = jnp.square(xf - mu).mean(-1, keepdims=True)
    return ((xf - mu) * lax.rsqrt(var + LN_EPS)).astype(x.dtype) * g + b


def masked_softmax(s, mask):
    p = jax.nn.softmax(jnp.where(mask, s.astype(jnp.float32), NEG), axis=-1)
    return jnp.where(mask, p, 0.0)


def project(x, w_in, b_in):
    B, S, _ = x.shape
    h = x @ w_in + b_in
    u, z_pool, q, kv, g_nsa, z_nsa, g_mrg = jnp.split(h, SPLIT_IDX, axis=-1)
    q = q.reshape(B, S, N_HEADS, HEAD_DIM)
    kv = kv.reshape(B, S, 6, N_KV_HEADS, HEAD_DIM)
    g_nsa = jax.nn.sigmoid(g_nsa).reshape(B, S, N_BRANCH_NSA, N_HEADS)
    g_mrg = jax.nn.sigmoid(g_mrg).reshape(B, S, N_MERGE, D_MODEL)
    return u, z_pool, q, kv, g_nsa, z_nsa, g_mrg


def pool_mix(u, n_ctx, start_pos, pool_w, pool_scale):
    B, L, _ = u.shape
    uf = u.astype(jnp.float32)
    c = jnp.pad(jnp.cumsum(uf, axis=1), ((0, 0), (1, 0), (0, 0)))
    pos = start_pos + jnp.arange(n_ctx, L)
    outs = []
    for g, w in enumerate(POOL_WINDOWS):
        sl = slice(g * POOL_GC, (g + 1) * POOL_GC)
        cg = c[:, :, sl]
        hi = cg[:, n_ctx + 1:]
        lo = jnp.pad(cg, ((0, 0), (w, 0), (0, 0)))[:, n_ctx + 1:L + 1]
        cnt = jnp.minimum(pos + 1, w).astype(jnp.float32)[None, :, None]
        outs.append((hi - lo) / cnt - uf[:, n_ctx:, sl])
    m = jnp.stack(outs, axis=2).astype(u.dtype)
    y = jnp.einsum('bsgc,gce->bsge', m, pool_w)
    return y.reshape(B, L - n_ctx, D_POOL) * pool_scale


def compress(k, pe, w1, w2):
    B, L = k.shape[:2]
    n = L // CMP_BLOCK
    blk = k[:, :n * CMP_BLOCK].reshape(B, n, CMP_BLOCK, N_KV_HEADS, HEAD_DIM) + pe[:, None, :]
    h = jax.nn.silu(jnp.einsum('bnjhd,jde->bnhe', blk, w1))
    return h @ w2


def to_sel_blocks(k):
    B, L = k.shape[:2]
    n = -(-L // SLC_BLOCK)
    k = jnp.pad(k, ((0, 0), (0, n * SLC_BLOCK - L), (0, 0), (0, 0)))
    return k.reshape(B, n, SLC_BLOCK, N_KV_HEADS, HEAD_DIM).transpose(0, 3, 1, 2, 4)


def nsa_attend(q, q_pos, kc, vc, ks_b, vs_b, kw, vw, kw_pos, gates):
    B, Q = q.shape[:2]
    qg = q.reshape(B, Q, N_KV_HEADS, GROUP, HEAD_DIM) * ATTN_SCALE
    n_cmp = kc.shape[1]
    cmp_end = (jnp.arange(n_cmp) + 1) * CMP_BLOCK - 1
    m_cmp = cmp_end[None, :] <= q_pos[:, None]
    p_cmp = masked_softmax(jnp.einsum('bqhgd,bnhd->bhgqn', qg, kc), m_cmp)
    o_cmp = jnp.einsum('bhgqn,bnhd->bqhgd', p_cmp.astype(vc.dtype), vc)
    n_slc = ks_b.shape[2]
    p_grp = jnp.pad(p_cmp.sum(axis=2), ((0, 0), (0, 0), (0, 0), (0, n_slc * SLC_RATIO - n_cmp)))
    p_slc = p_grp.reshape(B, N_KV_HEADS, Q, n_slc, SLC_RATIO).sum(-1)
    blk = jnp.arange(n_slc)[None, :]
    cur = (q_pos // SLC_BLOCK)[:, None]
    valid = blk * SLC_BLOCK <= q_pos[:, None]
    forced = (blk == 0) | (blk == cur) | (blk == cur - 1)
    score = jnp.where(valid & forced, FORCE, jnp.where(valid, p_slc, -FORCE))
    _, idx = lax.top_k(score, min(TOP_N, n_slc))
    n_top = idx.shape[-1]
    bi = jnp.arange(B)[:, None, None, None]
    hi = jnp.arange(N_KV_HEADS)[None, :, None, None]
    k_sel = ks_b[bi, hi, idx]
    v_sel = vs_b[bi, hi, idx]
    pos_sel = idx[..., None] * SLC_BLOCK + jnp.arange(SLC_BLOCK)
    m_sel = (pos_sel <= q_pos[None, None, :, None, None]).reshape(B, N_KV_HEADS, 1, Q, n_top * SLC_BLOCK)
    s_sel = jnp.einsum('bqhgd,bhqnsd->bhgqns', qg, k_sel)
    p_sel = masked_softmax(s_sel.reshape(B, N_KV_HEADS, GROUP, Q, n_top * SLC_BLOCK), m_sel)
    o_sel = jnp.einsum('bhgqns,bhqnsd->bqhgd', p_sel.reshape(s_sel.shape).astype(v_sel.dtype), v_sel)
    dist = q_pos[:, None] - kw_pos[None, :]
    m_win = (kw_pos[None, :] >= 0) & (dist >= 0) & (dist <= WINDOW)
    p_win = masked_softmax(jnp.einsum('bqhgd,blhd->bhgql', qg, kw), m_win)
    o_win = jnp.einsum('bhgql,blhd->bqhgd', p_win.astype(vw.dtype), vw)
    g = gates.reshape(B, Q, N_BRANCH_NSA, N_KV_HEADS, GROUP)[..., None]
    o = g[:, :, 0] * o_cmp + g[:, :, 1] * o_sel + g[:, :, 2] * o_win
    return o.reshape(B, Q, D_ATTN)


def merge_out(x, y_pool, z_pool, o_nsa, z_nsa, g_mrg, w_up_pool, w_up_nsa, w_out, ln_g, ln_b):
    a = (y_pool * jax.nn.silu(z_pool)) @ w_up_pool
    b = (o_nsa * jax.nn.silu(z_nsa)) @ w_up_nsa
    h = (g_mrg[:, :, 0] * a + g_mrg[:, :, 1] * b) @ w_out
    return layer_norm(ALPHA * x + h, ln_g, ln_b)


def prompt_layer(x, w_in, b_in, pool_w, pool_scale, cmp_w, w_up_pool, w_up_nsa, w_out, ln_g, ln_b):
    B, S, _ = x.shape
    u, zp, q, kv, gn, zn, gm = project(x, w_in, b_in)
    y_pool = pool_mix(u, 0, 0, pool_w, pool_scale)
    pe_k, w1_k, w2_k, pe_v, w1_v, w2_v = cmp_w
    kc = compress(kv[:, :, 0], pe_k, w1_k, w2_k)
    vc = compress(kv[:, :, 1], pe_v, w1_v, w2_v)
    ks_b = to_sel_blocks(kv[:, :, 2])
    vs_b = to_sel_blocks(kv[:, :, 3])
    pad = ((0, 0), (WINDOW, 0), (0, 0), (0, 0))
    kw_pad = jnp.pad(kv[:, :, 4], pad)
    vw_pad = jnp.pad(kv[:, :, 5], pad)

    def block(b):
        q0 = b * Q_BLOCK
        qb = lax.dynamic_slice_in_dim(q, q0, Q_BLOCK, axis=1)
        gb = lax.dynamic_slice_in_dim(gn, q0, Q_BLOCK, axis=1)
        kwb = lax.dynamic_slice_in_dim(kw_pad, q0, Q_BLOCK + WINDOW, axis=1)
        vwb = lax.dynamic_slice_in_dim(vw_pad, q0, Q_BLOCK + WINDOW, axis=1)
        q_pos = q0 + jnp.arange(Q_BLOCK)
        kw_pos = q0 - WINDOW + jnp.arange(Q_BLOCK + WINDOW)
        return nsa_attend(qb, q_pos, kc, vc, ks_b, vs_b, kwb, vwb, kw_pos, gb)

    o = lax.map(block, jnp.arange(S // Q_BLOCK))
    o = o.transpose(1, 0, 2, 3).reshape(B, S, D_ATTN)
    y = merge_out(x, y_pool, zp, o, zn, gm, w_up_pool, w_up_nsa, w_out, ln_g, ln_b)
    wl = min(WINDOW, S)
    return y, (kv[:, :, 0], kv[:, :, 1], kv[:, :, 2], kv[:, :, 3], kv[:, S - wl:, 4], kv[:, S - wl:, 5], u[:, S - POOL_CTX:])


def gather_past(cache, page_table):
    db, n_pages = page_table.shape
    return cache[page_table].reshape(db, n_pages * PAGE_SIZE, N_KV_HEADS, HEAD_DIM)


def sample_layer(x, c_cmp_k, c_cmp_v, c_slc_k, c_slc_v, c_win_k, c_win_v, s_pool, page_table,
                 w_in, b_in, pool_w, pool_scale, cmp_w, w_up_pool, w_up_nsa, w_out, ln_g, ln_b):
    D, S_new, _ = x.shape
    past = page_table.shape[1] * PAGE_SIZE
    u, zp, q, kv, gn, zn, gm = project(x, w_in, b_in)
    ctx = jnp.concatenate([s_pool, u], axis=1)
    y_pool = pool_mix(ctx, POOL_CTX, past - POOL_CTX, pool_w, pool_scale)
    pe_k, w1_k, w2_k, pe_v, w1_v, w2_v = cmp_w
    k_cmp = jnp.concatenate([gather_past(c_cmp_k, page_table), kv[:, :, 0]], axis=1)
    v_cmp = jnp.concatenate([gather_past(c_cmp_v, page_table), kv[:, :, 1]], axis=1)
    kc = compress(k_cmp, pe_k, w1_k, w2_k)
    vc = compress(v_cmp, pe_v, w1_v, w2_v)
    ks_b = to_sel_blocks(jnp.concatenate([gather_past(c_slc_k, page_table), kv[:, :, 2]], axis=1))
    vs_b = to_sel_blocks(jnp.concatenate([gather_past(c_slc_v, page_table), kv[:, :, 3]], axis=1))
    wb = c_win_k.shape[1]
    kw = jnp.concatenate([c_win_k, kv[:, :, 4]], axis=1)
    vw = jnp.concatenate([c_win_v, kv[:, :, 5]], axis=1)
    kw_pos = past - wb + jnp.arange(wb + S_new)
    q_pos = past + jnp.arange(S_new)
    o = nsa_attend(q, q_pos, kc, vc, ks_b, vs_b, kw, vw, kw_pos, gn)
    y = merge_out(x, y_pool, zp, o, zn, gm, w_up_pool, w_up_nsa, w_out, ln_g, ln_b)
    return y, (kv[:, :, 0], kv[:, :, 1], kv[:, :, 2], kv[:, :, 3], kw[:, S_new:], vw[:, S_new:], ctx[:, S_new:])


def setup_inputs(seed: int = 0) -> dict:
    key = jax.random.key(seed)
    k = jax.random.split(key, 26)
    n_pages = PAST_LEN // PAGE_SIZE
    n_used = DEC_BATCH * n_pages
    n_phys = n_used + max(n_used // 4, 1)
    win_buf = min(WINDOW, PAST_LEN)

    def nrm(kk, shape, scale=1.0):
        return scale * jax.random.normal(kk, shape, jnp.float32)

    paged = (DEPTH, n_phys, PAGE_SIZE, N_KV_HEADS, HEAD_DIM)
    win = (DEPTH, DEC_BATCH, win_buf, N_KV_HEADS, HEAD_DIM)
    page_table = jax.random.permutation(k[0], n_phys)[:n_used].reshape(DEC_BATCH, n_pages).astype(jnp.int32)
    return {
        'x_prompt': nrm(k[1], (BATCH, SEQ, D_MODEL)),
        'x_sample': nrm(k[2], (DEC_BATCH, DEC_SEQ, D_MODEL)),
        'cache_cmp_k': nrm(k[3], paged),
        'cache_cmp_v': nrm(k[4], paged),
        'cache_slc_k': nrm(k[5], paged),
        'cache_slc_v': nrm(k[6], paged),
        'cache_win_k': nrm(k[7], win),
        'cache_win_v': nrm(k[8], win),
        'state_pool': nrm(k[9], (DEPTH, DEC_BATCH, POOL_CTX, D_POOL)),
        'page_table': page_table,
        'w_in': nrm(k[10], (DEPTH, D_MODEL, D_IN), D_MODEL ** -0.5),
        'b_in': nrm(k[11], (DEPTH, D_IN), 0.01),
        'pool_w': nrm(k[12], (DEPTH, N_POOL_GROUPS, POOL_GC, POOL_GC), POOL_GC ** -0.5),
        'pool_scale': 1.0 + nrm(k[13], (DEPTH, D_POOL), 0.1),
        'cmp_pe_k': nrm(k[14], (DEPTH, CMP_BLOCK, HEAD_DIM), 0.1),
        'cmp_w1_k': nrm(k[15], (DEPTH, CMP_BLOCK, HEAD_DIM, HEAD_DIM), (CMP_BLOCK * HEAD_DIM) ** -0.5),
        'cmp_w2_k': nrm(k[16], (DEPTH, HEAD_DIM, HEAD_DIM), HEAD_DIM ** -0.5),
        'cmp_pe_v': nrm(k[17], (DEPTH, CMP_BLOCK, HEAD_DIM), 0.1),
        'cmp_w1_v': nrm(k[18], (DEPTH, CMP_BLOCK, HEAD_DIM, HEAD_DIM), (CMP_BLOCK * HEAD_DIM) ** -0.5),
        'cmp_w2_v': nrm(k[19], (DEPTH, HEAD_DIM, HEAD_DIM), HEAD_DIM ** -0.5),
        'w_up_pool': nrm(k[20], (DEPTH, D_POOL, D_MODEL), BETA * D_POOL ** -0.5),
        'w_up_nsa': nrm(k[21], (DEPTH, D_ATTN, D_MODEL), BETA * D_ATTN ** -0.5),
        'w_out': nrm(k[22], (DEPTH, D_MODEL, D_MODEL), BETA * D_MODEL ** -0.5),
        'ln_g': 1.0 + nrm(k[23], (DEPTH, D_MODEL), 0.01),
        'ln_b': nrm(k[24], (DEPTH, D_MODEL), 0.01),
    }


def reference(x_prompt, x_sample, cache_cmp_k, cache_cmp_v, cache_slc_k, cache_slc_v, cache_win_k, cache_win_v,
              state_pool, page_table, w_in, b_in, pool_w, pool_scale, cmp_pe_k, cmp_w1_k, cmp_w2_k,
              cmp_pe_v, cmp_w1_v, cmp_w2_v, w_up_pool, w_up_nsa, w_out, ln_g, ln_b):
    xp, xs = x_prompt, x_sample
    p_new = [[] for _ in range(7)]
    s_new = [[] for _ in range(7)]
    for l in range(DEPTH):
        cmp_w = (cmp_pe_k[l], cmp_w1_k[l], cmp_w2_k[l], cmp_pe_v[l], cmp_w1_v[l], cmp_w2_v[l])
        xp, p_st = prompt_layer(xp, w_in[l], b_in[l], pool_w[l], pool_scale[l], cmp_w,
                                w_up_pool[l], w_up_nsa[l], w_out[l], ln_g[l], ln_b[l])
        xs, s_st = sample_layer(xs, cache_cmp_k[l], cache_cmp_v[l], cache_slc_k[l], cache_slc_v[l],
                                cache_win_k[l], cache_win_v[l], state_pool[l], page_table,
                                w_in[l], b_in[l], pool_w[l], pool_scale[l], cmp_w,
                                w_up_pool[l], w_up_nsa[l], w_out[l], ln_g[l], ln_b[l])
        for i in range(7):
            p_new[i].append(p_st[i])
            s_new[i].append(s_st[i])
    return (xp, xs,
            jnp.stack(p_new[0]), jnp.stack(p_new[1]), jnp.stack(p_new[2]), jnp.stack(p_new[3]),
            jnp.stack(p_new[4]), jnp.stack(p_new[5]), jnp.stack(p_new[6]),
            jnp.stack(s_new[0]), jnp.stack(s_new[1]), jnp.stack(s_new[2]), jnp.stack(s_new[3]),
            jnp.stack(s_new[4]), jnp.stack(s_new[5]), jnp.stack(s_new[6]))
```

```python
import functools

import jax
import jax.numpy as jnp
import numpy as np
from jax import lax
from jax.experimental import pallas as pl
from jax.experimental.pallas import tpu as pltpu

D_MODEL = 2048
D_POOL = 1024
POOL_WINDOWS = (2, 4, 8, 16)
POOL_GC = D_POOL // len(POOL_WINDOWS)
POOL_CTX = max(POOL_WINDOWS) - 1
HEAD_DIM = 128
N_HEADS = 8
N_KV_HEADS = 2
GROUP = N_HEADS // N_KV_HEADS
D_ATTN = N_HEADS * HEAD_DIM
D_KV = N_KV_HEADS * HEAD_DIM
CMP_BLOCK = 32
SLC_BLOCK = 64
TOP_N = 16
WINDOW = 512
PAGE_SIZE = 128
ATTN_SCALE = HEAD_DIM ** -0.5
DEPTH = 1
ALPHA = (2.0 * DEPTH) ** 0.25
LN_EPS = 1e-5
NEG = -1e30
FORCE = 1e6

F32 = jnp.float32
BF16 = jnp.bfloat16

SUBLANES = 8
LANES = 128
VMEM_LIMIT_BYTES = 56 * 1024 * 1024

PROJ_TN = 512
COL_U = 0
COL_ZP = 1024
COL_Q = 2048
COL_ZN = 3072
COL_GM = 4096
COL_KV = 8192
COL_GN = 9728
PROJ_COLS = 10240
N_GATE = 3 * N_HEADS

CMP_ROWS = CMP_BLOCK * N_KV_HEADS
CMP_PITCH = 72
BLOCKS_PER_PAGE = PAGE_SIZE // CMP_BLOCK


def _params(sem):
    return pltpu.CompilerParams(dimension_semantics=sem, vmem_limit_bytes=VMEM_LIMIT_BYTES)


def _sigmoid(x):
    return 1.0 / (1.0 + jnp.exp(-x))


def _proj_body(x_ref, w_ref, b_ref, o_ref):
    j = pl.program_id(1)
    col = j * PROJ_TN
    acc = jnp.dot(x_ref[...], w_ref[...], preferred_element_type=F32) + b_ref[...]
    is_silu = ((col >= COL_ZP) & (col < COL_Q)) | ((col >= COL_ZN) & (col < COL_GM))
    is_sig = ((col >= COL_GM) & (col < COL_KV)) | (col >= COL_GN)
    is_q = (col >= COL_Q) & (col < COL_ZN)

    @pl.when(is_silu)
    def _():
        o_ref[...] = acc * _sigmoid(acc)

    @pl.when(is_sig)
    def _():
        o_ref[...] = _sigmoid(acc)

    @pl.when(is_q)
    def _():
        o_ref[...] = acc * ATTN_SCALE

    @pl.when(jnp.logical_not(is_silu | is_sig | is_q))
    def _():
        o_ref[...] = acc


def _proj(x, w, b, tm):
    m = x.shape[0]
    return pl.pallas_call(
        _proj_body,
        grid=(m // tm, PROJ_COLS // PROJ_TN),
        in_specs=[
            pl.BlockSpec((tm, D_MODEL), lambda i, j: (i, 0)),
            pl.BlockSpec((D_MODEL, PROJ_TN), lambda i, j: (0, j)),
            pl.BlockSpec((1, PROJ_TN), lambda i, j: (0, j)),
        ],
        out_specs=pl.BlockSpec((tm, PROJ_TN), lambda i, j: (i, j)),
        out_shape=jax.ShapeDtypeStruct((m, PROJ_COLS), F32),
        compiler_params=_params(("arbitrary", "arbitrary")),
        name="proj",
    )(x, w, b)


def _pool_body(halo_ref, u_ref, m_ref, *, tm, seq, fixed_pos):
    i = pl.program_id(0)
    row0 = (i * tm) % seq
    halo = jnp.where(row0 == 0, 0.0, halo_ref[...])
    ext = jnp.concatenate([halo, u_ref[...]], axis=0)
    if fixed_pos is None:
        pos = row0 + lax.broadcasted_iota(jnp.int32, (tm, 1), 0)
    else:
        pos = jnp.full((tm, 1), fixed_pos, jnp.int32)
    for g, w in enumerate(POOL_WINDOWS):
        a = ext[:, g * POOL_GC:(g + 1) * POOL_GC]
        s = a
        k = 1
        while k < w:
            s = s + pltpu.roll(s, k, 0)
            k *= 2
        cnt = jnp.minimum(pos + 1, w).astype(F32)
        m = s[2 * SUBLANES:] / cnt - a[2 * SUBLANES:]
        m_ref[:, g * POOL_GC:(g + 1) * POOL_GC] = m.astype(m_ref.dtype)


def _pool_m(src, col_block, rows, tm, seq, fixed_pos):
    halo_rows = 2 * SUBLANES
    per = tm // halo_rows
    return pl.pallas_call(
        functools.partial(_pool_body, tm=tm, seq=seq, fixed_pos=fixed_pos),
        grid=(rows // tm,),
        in_specs=[
            pl.BlockSpec((halo_rows, D_POOL), lambda i: (jnp.maximum(i * per - 1, 0), col_block)),
            pl.BlockSpec((tm, D_POOL), lambda i: (i, col_block)),
        ],
        out_specs=pl.BlockSpec((tm, D_POOL), lambda i: (i, 0)),
        out_shape=jax.ShapeDtypeStruct((rows, D_POOL), BF16),
        compiler_params=_params(("arbitrary",)),
        name="pool",
    )(src, src)


def _tail_body(m_ref, zp_ref, o_ref, zn_ref, ga_ref, gb_ref, x_ref, pw_ref, ps_ref, wup_ref, wun_ref, wo_ref,
               lg_ref, lb_ref, y_ref):
    m = m_ref[...]
    ys = [jnp.dot(m[:, g * POOL_GC:(g + 1) * POOL_GC], pw_ref[g], preferred_element_type=F32)
          for g in range(len(POOL_WINDOWS))]
    y_pool = jnp.concatenate(ys, axis=1) * ps_ref[...]
    a = jnp.dot((y_pool * zp_ref[...]).astype(BF16), wup_ref[...], preferred_element_type=F32)
    b = jnp.dot((o_ref[...] * zn_ref[...]).astype(BF16), wun_ref[...], preferred_element_type=F32)
    mix = ga_ref[...] * a + gb_ref[...] * b
    h = jnp.dot(mix.astype(BF16), wo_ref[...], preferred_element_type=F32)
    z = ALPHA * x_ref[...] + h
    mu = jnp.mean(z, axis=-1, keepdims=True)
    zc = z - mu
    var = jnp.mean(zc * zc, axis=-1, keepdims=True)
    y_ref[...] = zc * lax.rsqrt(var + LN_EPS) * lg_ref[...] + lb_ref[...]


def _tail(m, h, o, x, pw, ps, wup, wun, wo, lg, lb, tm):
    rows = x.shape[0]
    once = pl.Buffered(1)

    def const(shape):
        return pl.BlockSpec(shape, lambda i: (0,) * len(shape), pipeline_mode=once)

    return pl.pallas_call(
        _tail_body,
        grid=(rows // tm,),
        in_specs=[
            pl.BlockSpec((tm, D_POOL), lambda i: (i, 0)),
            pl.BlockSpec((tm, D_POOL), lambda i: (i, COL_ZP // D_POOL)),
            pl.BlockSpec((tm, D_ATTN), lambda i: (i, 0)),
            pl.BlockSpec((tm, D_ATTN), lambda i: (i, COL_ZN // D_ATTN)),
            pl.BlockSpec((tm, D_MODEL), lambda i: (i, COL_GM // D_MODEL)),
            pl.BlockSpec((tm, D_MODEL), lambda i: (i, COL_GM // D_MODEL + 1)),
            pl.BlockSpec((tm, D_MODEL), lambda i: (i, 0)),
            const((len(POOL_WINDOWS), POOL_GC, POOL_GC)),
            const((1, D_POOL)),
            const((D_POOL, D_MODEL)),
            const((D_ATTN, D_MODEL)),
            const((D_MODEL, D_MODEL)),
            const((1, D_MODEL)),
            const((1, D_MODEL)),
        ],
        out_specs=pl.BlockSpec((tm, D_MODEL), lambda i: (i, 0)),
        out_shape=jax.ShapeDtypeStruct((rows, D_MODEL), F32),
        compiler_params=_params(("arbitrary",)),
        name="tail",
    )(m, h, o, h, h, h, x, pw, ps, wup, wun, wo, lg, lb)


def _compress_body(pt_ref, ck_hbm, cv_hbm, pe_ref, w1_ref, w2_ref, ok_ref, ov_ref, buf, lhs, sem, *, pps, n_chunks):
    b = pl.program_id(0)
    c = pl.program_id(1)
    step = b * n_chunks + c
    n_steps = pl.num_programs(0) * n_chunks
    nb = pps * BLOCKS_PER_PAGE
    half = nb // 2

    def page_copies(bb, cc, slot, p):
        phys = pt_ref[bb, cc * pps + p]
        out = []
        for t, hbm in enumerate((ck_hbm, cv_hbm)):
            for par in range(2):
                out.append(pltpu.make_async_copy(
                    hbm.at[phys, :, par],
                    buf.at[t, slot, par, pl.ds(2 * p, 2), pl.ds(0, CMP_ROWS), :],
                    sem.at[t, slot]))
        return out

    def start_step(s, slot):
        bb = s // n_chunks
        cc = s % n_chunks

        def one(p, carry):
            for cp in page_copies(bb, cc, slot, p):
                cp.start()
            return carry
        lax.fori_loop(0, pps, one, 0)

    @pl.when(step == 0)
    def _():
        start_step(0, 0)

    @pl.when(step + 1 < n_steps)
    def _():
        start_step(step + 1, (step + 1) % 2)

    slot = step % 2

    def wait_one(p, carry):
        for cp in page_copies(b, c, slot, p):
            cp.wait()
        return carry
    lax.fori_loop(0, pps, wait_one, 0)

    for t, out_ref in enumerate((ok_ref, ov_ref)):
        rows = buf.at[t, slot].reshape(nb * CMP_PITCH, LANES)
        for j in range(CMP_BLOCK):
            x0 = rows[pl.ds(2 * j, nb, stride=CMP_PITCH), :]
            x1 = rows[pl.ds(2 * j + 1, nb, stride=CMP_PITCH), :]
            pe = pe_ref[t, j:j + 1, :]
            lhs[:, (2 * j) * HEAD_DIM:(2 * j + 1) * HEAD_DIM] = (x0 + pe).astype(BF16)
            lhs[:, (2 * j + 1) * HEAD_DIM:(2 * j + 2) * HEAD_DIM] = (x1 + pe).astype(BF16)
        hid = jnp.dot(lhs[...], w1_ref[t], preferred_element_type=F32)
        hid = hid * _sigmoid(hid)
        out_ref[0] = jnp.dot(hid.astype(BF16), w2_ref[t], preferred_element_type=F32)


def _compress(page_table, cache_k, cache_v, pe, w1, w2, pps):
    nbatch, n_pages = page_table.shape
    n_chunks = n_pages // pps
    nb = pps * BLOCKS_PER_PAGE
    out = jax.ShapeDtypeStruct((nbatch, n_chunks * nb, D_KV), F32)
    kdim = CMP_BLOCK * D_KV
    return pl.pallas_call(
        functools.partial(_compress_body, pps=pps, n_chunks=n_chunks),
        grid_spec=pltpu.PrefetchScalarGridSpec(
            num_scalar_prefetch=1,
            grid=(nbatch, n_chunks),
            in_specs=[
                pl.BlockSpec(memory_space=pl.ANY),
                pl.BlockSpec(memory_space=pl.ANY),
                pl.BlockSpec((2, CMP_BLOCK, HEAD_DIM), lambda b, c, pt: (0, 0, 0)),
                pl.BlockSpec((2, kdim, D_KV), lambda b, c, pt: (0, 0, 0), pipeline_mode=pl.Buffered(1)),
                pl.BlockSpec((2, D_KV, D_KV), lambda b, c, pt: (0, 0, 0)),
            ],
            out_specs=[
                pl.BlockSpec((1, nb, D_KV), lambda b, c, pt: (b, c, 0)),
                pl.BlockSpec((1, nb, D_KV), lambda b, c, pt: (b, c, 0)),
            ],
            scratch_shapes=[
                pltpu.VMEM((2, 2, 2, nb // 2, CMP_PITCH, LANES), F32),
                pltpu.VMEM((nb, kdim), BF16),
                pltpu.SemaphoreType.DMA((2, 2)),
            ],
        ),
        out_shape=[out, out],
        compiler_params=_params(("arbitrary", "arbitrary")),
        name="compress",
    )(page_table, cache_k, cache_v, pe, w1, w2)


def _kprep_body(ks_ref, vs_ref, kw_ref, vw_ref, kts_ref, vso_ref, ktw_ref, vwo_ref, *, tk):
    t = pl.program_id(1)
    ks = ks_ref[...]
    kw = kw_ref[...]
    kpos = t * tk + lax.broadcasted_iota(jnp.int32, (SLC_BLOCK, tk), 1)
    blk = lax.broadcasted_iota(jnp.int32, (SLC_BLOCK, tk), 0)
    onehot = jnp.where(kpos // SLC_BLOCK == blk, 1.0, 0.0).astype(BF16)
    for g in range(N_KV_HEADS):
        kts_ref[0, g, 0, 0:HEAD_DIM, :] = ks[:, g * HEAD_DIM:(g + 1) * HEAD_DIM].T.astype(BF16)
        kts_ref[0, g, 0, HEAD_DIM:HEAD_DIM + SLC_BLOCK, :] = onehot
        kts_ref[0, g, 0, HEAD_DIM + SLC_BLOCK:, :] = jnp.zeros((SLC_BLOCK, tk), BF16)
        for c in range(tk // LANES):
            ktw_ref[0, g, c] = kw[c * LANES:(c + 1) * LANES, g * HEAD_DIM:(g + 1) * HEAD_DIM].T.astype(BF16)
    vso_ref[0] = vs_ref[...].astype(BF16)
    vwo_ref[0] = vw_ref[...].astype(BF16)


def _kprep(h, nbatch, seq, tk):
    nt = seq // tk
    cb = COL_KV // D_KV

    def src(k):
        return pl.BlockSpec((tk, D_KV), lambda b, t: (b * nt + t, cb + k))

    return pl.pallas_call(
        functools.partial(_kprep_body, tk=tk),
        grid=(nbatch, nt),
        in_specs=[src(2), src(3), src(4), src(5)],
        out_specs=[
            pl.BlockSpec((1, N_KV_HEADS, 1, 2 * HEAD_DIM, tk), lambda b, t: (b, 0, t, 0, 0)),
            pl.BlockSpec((1, tk, D_KV), lambda b, t: (b, t, 0)),
            pl.BlockSpec((1, N_KV_HEADS, tk // LANES, HEAD_DIM, LANES), lambda b, t: (b, 0, t, 0, 0)),
            pl.BlockSpec((1, tk, D_KV), lambda b, t: (b, t, 0)),
        ],
        out_shape=[
            jax.ShapeDtypeStruct((nbatch, N_KV_HEADS, nt, 2 * HEAD_DIM, tk), BF16),
            jax.ShapeDtypeStruct((nbatch, seq, D_KV), BF16),
            jax.ShapeDtypeStruct((nbatch, N_KV_HEADS, seq // LANES, HEAD_DIM, LANES), BF16),
            jax.ShapeDtypeStruct((nbatch, seq, D_KV), BF16),
        ],
        compiler_params=_params(("arbitrary", "arbitrary")),
        name="kprep",
    )(h, h, h, h)


def _masked_softmax(s, mask):
    s = jnp.where(mask, s, NEG)
    mx = jnp.max(s, axis=-1, keepdims=True)
    e = jnp.where(mask, jnp.exp(s - mx), 0.0)
    l = jnp.sum(e, axis=-1, keepdims=True)
    return e / jnp.where(l > 0.0, l, 1.0)


def _select_bias(p_slc_t, qpos_row, n_slc):
    shape = p_slc_t.shape
    blk = lax.broadcasted_iota(jnp.int32, shape, 0)
    real = blk < n_slc
    valid = real & (blk * SLC_BLOCK <= qpos_row)
    cur = qpos_row // SLC_BLOCK
    forced = (blk == 0) | (blk == cur) | (blk == cur - 1)
    score = jnp.where(valid & forced, FORCE, jnp.where(valid, p_slc_t, -FORCE))
    score = jnp.where(real, score, -2.0 * FORCE)
    rank = jnp.zeros(shape, jnp.int32)
    for j in range(n_slc):
        row = score[j:j + 1, :]
        ahead = (row > score) | ((row == score) & (blk > j))
        rank = rank + ahead.astype(jnp.int32)
    return jnp.where(rank < min(TOP_N, n_slc), 0.0, NEG)


def _attn_body(q_ref, gn_ref, kc_ref, vc_ref, kts_ref, vs_ref, ktw_ref, vw_ref, o_ref, *, tq, tk, seq):
    q0 = pl.program_id(1) * tq
    n_cmp = seq // CMP_BLOCK
    n_slc = -(-seq // SLC_BLOCK)
    half = n_cmp // 2
    rows = GROUP * tq
    qpos = q0 + lax.broadcasted_iota(jnp.int32, (tq, 1), 0)
    qpos4 = jnp.concatenate([qpos] * GROUP, axis=0)
    qpos_row = q0 + lax.broadcasted_iota(jnp.int32, (LANES, tq), 1)
    lane = lax.broadcasted_iota(jnp.int32, (1, n_cmp), 1)
    cmp_blk = 2 * (lane % half) + lane // half
    m_cmp = ((cmp_blk + 1) * CMP_BLOCK - 1) <= qpos4
    q = q_ref[...]
    gates = gn_ref[...]
    span = tq + WINDOW
    w0 = pl.multiple_of(jnp.maximum(q0 - WINDOW, 0), LANES)
    n_tiles = (q0 + tq + tk - 1) // tk

    for g in range(N_KV_HEADS):
        hs = slice(g * HEAD_DIM, (g + 1) * HEAD_DIM)
        qg = jnp.concatenate([q[:, (GROUP * g + h) * HEAD_DIM:(GROUP * g + h + 1) * HEAD_DIM]
                              for h in range(GROUP)], axis=0).astype(BF16)
        kc = kc_ref[0][:, hs].astype(BF16)
        vc = vc_ref[0][:, hs].astype(BF16)
        s = lax.dot_general(qg, kc, (((1,), (1,)), ((), ())), preferred_element_type=F32)
        p = _masked_softmax(s, m_cmp)
        o_cmp = jnp.dot(p.astype(BF16), vc, preferred_element_type=F32)
        p_grp = p[0:tq]
        for h in range(1, GROUP):
            p_grp = p_grp + p[h * tq:(h + 1) * tq]
        p_slc = p_grp + pltpu.roll(p_grp, half, 1)
        bias = _select_bias(p_slc.T, qpos_row, n_slc).T.astype(BF16)
        q_aug = jnp.concatenate([qg, jnp.concatenate([bias] * GROUP, axis=0)], axis=1)

        def sel_step(t, carry):
            m_run, l_run, acc = carry
            k0 = pl.multiple_of(t * tk, tk)
            sc = jnp.dot(q_aug, kts_ref[0, g, t], preferred_element_type=F32)
            kpos = k0 + lax.broadcasted_iota(jnp.int32, (1, tk), 1)
            sc = jnp.where(kpos <= qpos4, sc, NEG)
            m_new = jnp.maximum(m_run, jnp.max(sc, axis=-1, keepdims=True))
            a = jnp.exp(m_run - m_new)
            pe = jnp.exp(sc - m_new)
            l_new = a * l_run + jnp.sum(pe, axis=-1, keepdims=True)
            v = vs_ref[0, pl.ds(k0, tk), hs]
            acc = a * acc + jnp.dot(pe.astype(BF16), v, preferred_element_type=F32)
            return m_new, l_new, acc

        init = (jnp.full((rows, 1), -jnp.inf, F32), jnp.zeros((rows, 1), F32), jnp.zeros((rows, HEAD_DIM), F32))
        _, l_sel, acc_sel = lax.fori_loop(0, n_tiles, sel_step, init)
        o_sel = acc_sel / l_sel

        ktw = jnp.concatenate([ktw_ref[0, g, w0 // LANES + c] for c in range(span // LANES)], axis=1)
        sw = jnp.dot(qg, ktw, preferred_element_type=F32)
        dist = qpos4 - (w0 + lax.broadcasted_iota(jnp.int32, (1, span), 1))
        pw = _masked_softmax(sw, (dist >= 0) & (dist <= WINDOW))
        o_win = jnp.dot(pw.astype(BF16), vw_ref[0, pl.ds(w0, span), hs], preferred_element_type=F32)

        for h in range(GROUP):
            hd = GROUP * g + h
            r = slice(h * tq, (h + 1) * tq)
            o_ref[:, hd * HEAD_DIM:(hd + 1) * HEAD_DIM] = (
                gates[:, hd:hd + 1] * o_cmp[r]
                + gates[:, N_HEADS + hd:N_HEADS + hd + 1] * o_sel[r]
                + gates[:, 2 * N_HEADS + hd:2 * N_HEADS + hd + 1] * o_win[r])


def _prompt_attn(h, kc, vc, kts, vs, ktw, vw, nbatch, seq, tq, tk):
    nq = seq // tq
    n_cmp = seq // CMP_BLOCK
    nt = seq // tk
    return pl.pallas_call(
        functools.partial(_attn_body, tq=tq, tk=tk, seq=seq),
        grid=(nbatch, nq),
        in_specs=[
            pl.BlockSpec((tq, D_ATTN), lambda b, i: (b * nq + i, COL_Q // D_ATTN)),
            pl.BlockSpec((tq, LANES), lambda b, i: (b * nq + i, COL_GN // LANES)),
            pl.BlockSpec((1, n_cmp, D_KV), lambda b, i: (b, 0, 0)),
            pl.BlockSpec((1, n_cmp, D_KV), lambda b, i: (b, 0, 0)),
            pl.BlockSpec((1, N_KV_HEADS, nt, 2 * HEAD_DIM, tk), lambda b, i: (b, 0, 0, 0, 0)),
            pl.BlockSpec((1, seq, D_KV), lambda b, i: (b, 0, 0)),
            pl.BlockSpec((1, N_KV_HEADS, seq // LANES, HEAD_DIM, LANES), lambda b, i: (b, 0, 0, 0, 0)),
            pl.BlockSpec((1, seq, D_KV), lambda b, i: (b, 0, 0)),
        ],
        out_specs=pl.BlockSpec((tq, D_ATTN), lambda b, i: (b * nq + i, 0)),
        out_shape=jax.ShapeDtypeStruct((nbatch * seq, D_ATTN), F32),
        compiler_params=_params(("arbitrary", "arbitrary")),
        name="attn",
    )(h, h, kc, vc, kts, vs, ktw, vw)


def _group_queries(q_row, g):
    heads = [q_row[:, (GROUP * g + h) * HEAD_DIM:(GROUP * g + h + 1) * HEAD_DIM] for h in range(GROUP)]
    pad = jnp.zeros((2 * SUBLANES - GROUP, HEAD_DIM), F32)
    return jnp.concatenate(heads + [pad], axis=0).astype(BF16)


def _scmp_body(q_ref, kc_ref, vc_ref, ocmp_ref, pslc_ref, *, q_pos, chunk):
    b = pl.program_id(0)
    q_row = q_ref[pl.ds(b, 1), :]
    n_cmp = kc_ref.shape[1]
    half = chunk // 2
    lane = lax.broadcasted_iota(jnp.int32, (1, n_cmp), 1)
    within = lane % chunk
    cmp_blk = (lane // chunk) * chunk + 2 * (within % half) + within // half
    m_cmp = ((cmp_blk + 1) * CMP_BLOCK - 1) <= q_pos
    outs = []
    for g in range(N_KV_HEADS):
        hs = slice(g * HEAD_DIM, (g + 1) * HEAD_DIM)
        qg = _group_queries(q_row, g)
        s = lax.dot_general(qg, kc_ref[0][:, hs].astype(BF16), (((1,), (1,)), ((), ())), preferred_element_type=F32)
        p = _masked_softmax(s, m_cmp)
        o = jnp.dot(p.astype(BF16), vc_ref[0][:, hs].astype(BF16), preferred_element_type=F32)
        outs.append(o[0:GROUP])
        p_grp = p[0:1]
        for h in range(1, GROUP):
            p_grp = p_grp + p[h:h + 1]
        parts = []
        for c in range(n_cmp // chunk):
            pc = p_grp[:, c * chunk:(c + 1) * chunk]
            parts.append(pc + pltpu.roll(pc, half, 1))
        pslc_ref[0, g:g + 1, :] = jnp.concatenate(parts, axis=1)
    ocmp_ref[0] = jnp.concatenate(outs, axis=0)


def _sample_cmp(q, kc, vc, q_pos, chunk):
    nbatch, n_cmp, _ = kc.shape
    return pl.pallas_call(
        functools.partial(_scmp_body, q_pos=q_pos, chunk=chunk),
        grid=(nbatch,),
        in_specs=[
            pl.BlockSpec((nbatch, D_ATTN), lambda b: (0, COL_Q // D_ATTN)),
            pl.BlockSpec((1, n_cmp, D_KV), lambda b: (b, 0, 0)),
            pl.BlockSpec((1, n_cmp, D_KV), lambda b: (b, 0, 0)),
        ],
        out_specs=[
            pl.BlockSpec((1, N_HEADS, HEAD_DIM), lambda b: (b, 0, 0)),
            pl.BlockSpec((1, N_KV_HEADS, n_cmp), lambda b: (b, 0, 0)),
        ],
        out_shape=[
            jax.ShapeDtypeStruct((nbatch, N_HEADS, HEAD_DIM), F32),
            jax.ShapeDtypeStruct((nbatch, N_KV_HEADS, n_cmp), F32),
        ],
        compiler_params=_params(("arbitrary",)),
        name="s_cmp",
    )(q, kc, vc)


def _stopk_body(p_ref, idx_ref, *, q_pos, chunk, n_slc):
    p = p_ref[...]
    rows, width = p.shape
    half = chunk // 2
    n_in = (width // chunk) * half
    lane = lax.broadcasted_iota(jnp.int32, (1, width), 1)
    within = lane % chunk
    blk = jnp.where(within < half, (lane // chunk) * half + within, -1)
    extra = (lane >= half) & (lane < half + (n_slc - n_in))
    blk = jnp.where(extra, n_in + lane - half, blk)
    real = blk >= 0
    valid = real & (blk * SLC_BLOCK <= q_pos)
    cur = q_pos // SLC_BLOCK
    forced = (blk == 0) | (blk == cur) | (blk == cur - 1)
    base = jnp.where(extra, 0.0, p)
    x = jnp.where(valid & forced, FORCE, jnp.where(valid, base, -FORCE))
    x = jnp.where(real, x, -jnp.inf)
    blk_f = blk.astype(F32)
    out_lane = lax.broadcasted_iota(jnp.int32, (rows, LANES), 1)
    out = jnp.zeros((rows, LANES), F32)
    for r in range(min(TOP_N, n_slc)):
        mx = jnp.max(x, axis=-1, keepdims=True)
        pick = jnp.min(jnp.where(x == mx, blk_f, float(2 ** 30)), axis=-1, keepdims=True)
        out = jnp.where(out_lane == r, pick, out)
        x = jnp.where(blk_f == pick, -jnp.inf, x)
    idx_ref[...] = out.astype(jnp.int32)


def _sample_topk(pslc, q_pos, chunk, n_slc):
    rows, width = pslc.shape
    return pl.pallas_call(
        functools.partial(_stopk_body, q_pos=q_pos, chunk=chunk, n_slc=n_slc),
        grid=(1,),
        in_specs=[pl.BlockSpec((rows, width), lambda i: (0, 0))],
        out_specs=pl.BlockSpec((rows, LANES), lambda i: (0, 0)),
        out_shape=jax.ShapeDtypeStruct((rows, LANES), jnp.int32),
        compiler_params=_params(("arbitrary",)),
        name="s_topk",
    )(pslc)


def _sattn_body(idx_ref, pt_ref, q_ref, gn_ref, ocmp_ref, knew_ref, vnew_ref, kwn_ref, vwn_ref, wk_ref, wv_ref,
                sk_hbm, sv_hbm, o_ref, kbuf, vbuf, sem, *, q_pos, n_top, n_pages, past):
    b = pl.program_id(0)
    half_rows = SLC_BLOCK * N_KV_HEADS
    tail_blk = past // SLC_BLOCK

    def copies(g, i):
        blk = idx_ref[b * N_KV_HEADS + g, i]
        page = jnp.minimum(blk // 2, n_pages - 1)
        start = pl.multiple_of(pt_ref[b, page] * (PAGE_SIZE * N_KV_HEADS) + (blk % 2) * half_rows, half_rows)
        return [pltpu.make_async_copy(hbm.at[pl.ds(start, half_rows), :], dst.at[g, i], sem.at[t, g, i])
                for t, (hbm, dst) in enumerate(((sk_hbm, kbuf), (sv_hbm, vbuf)))]

    for g in range(N_KV_HEADS):
        for i in range(n_top):
            for cp in copies(g, i):
                cp.start()
    for g in range(N_KV_HEADS):
        for i in range(n_top):
            for cp in copies(g, i):
                cp.wait()

    q_row = q_ref[pl.ds(b, 1), :]
    gates = gn_ref[pl.ds(b, 1), :]
    pad = jnp.zeros((2 * SUBLANES - 1, D_KV), F32)
    knew_all = jnp.concatenate([knew_ref[pl.ds(b, 1), :], pad], axis=0).astype(BF16)
    vnew_all = jnp.concatenate([vnew_ref[pl.ds(b, 1), :], pad], axis=0).astype(BF16)
    kwn_all = jnp.concatenate([kwn_ref[pl.ds(b, 1), :], pad], axis=0).astype(BF16)
    vwn_all = jnp.concatenate([vwn_ref[pl.ds(b, 1), :], pad], axis=0).astype(BF16)
    outs = []
    for g in range(N_KV_HEADS):
        hs = slice(g * HEAD_DIM, (g + 1) * HEAD_DIM)
        qg = _group_queries(q_row, g)
        ks, vs = [], []
        key = lax.broadcasted_iota(jnp.int32, (1, n_top * SLC_BLOCK), 1)
        key_blk = jnp.zeros((1, n_top * SLC_BLOCK), jnp.int32)
        for i in range(n_top):
            blk = idx_ref[b * N_KV_HEADS + g, i]
            ks.append(kbuf[g, i, pl.ds(g, SLC_BLOCK, stride=N_KV_HEADS), :])
            vs.append(vbuf[g, i, pl.ds(g, SLC_BLOCK, stride=N_KV_HEADS), :])
            key_blk = jnp.where(key // SLC_BLOCK == i, blk, key_blk)
        k_sel = jnp.concatenate(ks, axis=0).astype(BF16)
        v_sel = jnp.concatenate(vs, axis=0).astype(BF16)
        m_sel = (key_blk * SLC_BLOCK + key % SLC_BLOCK <= q_pos) & (key_blk < tail_blk)
        tail_sel = jnp.max(jnp.where(key_blk == tail_blk, 1.0, 0.0), axis=-1, keepdims=True) > 0.5
        first = lax.broadcasted_iota(jnp.int32, (1, 2 * SUBLANES), 1) == 0
        o_sel = _two_part_attention(qg, k_sel, v_sel, m_sel, knew_all[:, hs], vnew_all[:, hs], first & tail_sel)
        wbuf = wk_ref.shape[1] // N_KV_HEADS
        k_win = wk_ref[0, pl.ds(g, wbuf, stride=N_KV_HEADS), :].astype(BF16)
        v_win = wv_ref[0, pl.ds(g, wbuf, stride=N_KV_HEADS), :].astype(BF16)
        dist = q_pos - (past - wbuf + lax.broadcasted_iota(jnp.int32, (1, wbuf), 1))
        m_win = (dist >= 0) & (dist <= WINDOW)
        o_win = _two_part_attention(qg, k_win, v_win, m_win, kwn_all[:, hs], vwn_all[:, hs], first)
        o_cmp = ocmp_ref[0, GROUP * g:GROUP * (g + 1), :]
        for h in range(GROUP):
            hd = GROUP * g + h
            outs.append(gates[:, hd:hd + 1] * o_cmp[h:h + 1]
                        + gates[:, N_HEADS + hd:N_HEADS + hd + 1] * o_sel[h:h + 1]
                        + gates[:, 2 * N_HEADS + hd:2 * N_HEADS + hd + 1] * o_win[h:h + 1])
    o_ref[0] = jnp.concatenate(outs, axis=0)


def _two_part_attention(q, k1, v1, m1, k2, v2, m2):
    dn = (((1,), (1,)), ((), ()))
    s1 = jnp.where(m1, lax.dot_general(q, k1, dn, preferred_element_type=F32), NEG)
    s2 = jnp.where(m2, lax.dot_general(q, k2, dn, preferred_element_type=F32), NEG)
    mx = jnp.maximum(jnp.max(s1, axis=-1, keepdims=True), jnp.max(s2, axis=-1, keepdims=True))
    e1 = jnp.where(m1, jnp.exp(s1 - mx), 0.0)
    e2 = jnp.where(m2, jnp.exp(s2 - mx), 0.0)
    l = jnp.sum(e1, axis=-1, keepdims=True) + jnp.sum(e2, axis=-1, keepdims=True)
    inv = 1.0 / jnp.where(l > 0.0, l, 1.0)
    o = (jnp.dot((e1 * inv).astype(BF16), v1, preferred_element_type=F32)
         + jnp.dot((e2 * inv).astype(BF16), v2, preferred_element_type=F32))
    return o


def _sample_attn(idx, page_table, hs, ocmp, win_k, win_v, slc_k, slc_v, q_pos, past):
    nbatch, n_pages = page_table.shape
    n_top = min(TOP_N, past // SLC_BLOCK + 1)
    half_rows = SLC_BLOCK * N_KV_HEADS
    kvb = COL_KV // D_KV

    def whole(col_block, width):
        return pl.BlockSpec((nbatch, width), lambda b, idx, pt: (0, col_block))

    return pl.pallas_call(
        functools.partial(_sattn_body, q_pos=q_pos, n_top=n_top, n_pages=n_pages, past=past),
        grid_spec=pltpu.PrefetchScalarGridSpec(
            num_scalar_prefetch=2,
            grid=(nbatch,),
            in_specs=[
                whole(COL_Q // D_ATTN, D_ATTN),
                whole(COL_GN // LANES, LANES),
                pl.BlockSpec((1, N_HEADS, HEAD_DIM), lambda b, idx, pt: (b, 0, 0)),
                whole(kvb + 2, D_KV),
                whole(kvb + 3, D_KV),
                whole(kvb + 4, D_KV),
                whole(kvb + 5, D_KV),
                pl.BlockSpec((1,) + win_k.shape[1:], lambda b, idx, pt: (b, 0, 0)),
                pl.BlockSpec((1,) + win_v.shape[1:], lambda b, idx, pt: (b, 0, 0)),
                pl.BlockSpec(memory_space=pl.ANY),
                pl.BlockSpec(memory_space=pl.ANY),
            ],
            out_specs=pl.BlockSpec((1, N_HEADS, HEAD_DIM), lambda b, idx, pt: (b, 0, 0)),
            scratch_shapes=[
                pltpu.VMEM((N_KV_HEADS, n_top, half_rows, LANES), F32),
                pltpu.VMEM((N_KV_HEADS, n_top, half_rows, LANES), F32),
                pltpu.SemaphoreType.DMA((2, N_KV_HEADS, n_top)),
            ],
        ),
        out_shape=jax.ShapeDtypeStruct((nbatch, N_HEADS, HEAD_DIM), F32),
        compiler_params=_params(("arbitrary",)),
        name="s_attn",
    )(idx, page_table, hs, hs, ocmp, hs, hs, hs, hs, win_k, win_v, slc_k, slc_v)


def _regroup_cols(a):
    kv0 = 3 * D_POOL
    gn0 = kv0 + 6 * D_KV
    zn0 = gn0 + N_GATE
    gm0 = zn0 + D_ATTN
    pad = jnp.zeros(a.shape[:-1] + (PROJ_COLS - COL_GN - N_GATE,), a.dtype)
    return jnp.concatenate([a[..., :kv0], a[..., zn0:gm0], a[..., gm0:], a[..., kv0:gn0], a[..., gn0:zn0], pad], axis=-1)


def _block_diag2(w):
    z = jnp.zeros_like(w)
    return jnp.concatenate([jnp.concatenate([w, z], axis=-1), jnp.concatenate([z, w], axis=-1)], axis=-2)


def kernel(x_prompt, x_sample, cache_cmp_k, cache_cmp_v, cache_slc_k, cache_slc_v, cache_win_k, cache_win_v,
           state_pool, page_table, w_in, b_in, pool_w, pool_scale, cmp_pe_k, cmp_w1_k, cmp_w2_k, cmp_pe_v, cmp_w1_v,
           cmp_w2_v, w_up_pool, w_up_nsa, w_out, ln_g, ln_b):
    nb_p, seq, _ = x_prompt.shape
    nb_s = x_sample.shape[0]
    n_pages = page_table.shape[1]
    past = n_pages * PAGE_SIZE
    n_phys = cache_cmp_k.shape[1]
    wbuf = cache_win_k.shape[2]

    w_p = _regroup_cols(w_in[0]).astype(BF16)
    b_p = _regroup_cols(b_in[0])[None, :]
    pe = jnp.stack([cmp_pe_k[0], cmp_pe_v[0]])
    w1 = jnp.stack([_block_diag2(cmp_w1_k[0]), _block_diag2(cmp_w1_v[0])]).reshape(2, CMP_BLOCK * D_KV, D_KV).astype(BF16)
    w2 = jnp.stack([_block_diag2(cmp_w2_k[0]), _block_diag2(cmp_w2_v[0])]).astype(BF16)
    pw = pool_w[0].astype(BF16)
    ps = pool_scale[0][None, :]
    wup = w_up_pool[0].astype(BF16)
    wun = w_up_nsa[0].astype(BF16)
    wo = w_out[0].astype(BF16)
    lg = ln_g[0][None, :]
    lb = ln_b[0][None, :]

    xp = x_prompt.reshape(nb_p * seq, D_MODEL)
    xs = x_sample.reshape(nb_s, D_MODEL)
    h_p = _proj(xp.astype(BF16), w_p, b_p, tm=1024)
    h_s = _proj(xs.astype(BF16), w_p, b_p, tm=nb_s)

    def kv_cols(h, k):
        return h[:, COL_KV + k * D_KV:COL_KV + (k + 1) * D_KV]

    kv_p = [kv_cols(h_p, k).reshape(1, nb_p, seq, N_KV_HEADS, HEAD_DIM) for k in range(6)]
    paged = (2, 2, CMP_ROWS, LANES)
    ident = jnp.arange(nb_p * (seq // PAGE_SIZE), dtype=jnp.int32).reshape(nb_p, seq // PAGE_SIZE)
    kc_p, vc_p = _compress(ident, kv_p[0].reshape((-1,) + paged), kv_p[1].reshape((-1,) + paged), pe, w1, w2,
                           pps=seq // PAGE_SIZE)
    kts, vs, ktw, vw = _kprep(h_p, nb_p, seq, tk=512)
    o_p = _prompt_attn(h_p, kc_p, vc_p, kts, vs, ktw, vw, nb_p, seq, tq=128, tk=512)
    m_p = _pool_m(h_p, COL_U // D_POOL, nb_p * seq, tm=512, seq=seq, fixed_pos=None)
    y_p = _tail(m_p, h_p, o_p, xp, pw, ps, wup, wun, wo, lg, lb, tm=256)

    chunk_pages = 32
    kc_s, vc_s = _compress(page_table, cache_cmp_k.reshape((n_phys,) + paged), cache_cmp_v.reshape((n_phys,) + paged),
                           pe, w1, w2, pps=chunk_pages)
    chunk = chunk_pages * BLOCKS_PER_PAGE
    n_slc = past // SLC_BLOCK + 1
    ocmp_s, pslc_s = _sample_cmp(h_s, kc_s, vc_s, past, chunk)
    idx = _sample_topk(pslc_s.reshape(nb_s * N_KV_HEADS, -1), past, chunk, n_slc)
    flat = (n_phys * PAGE_SIZE * N_KV_HEADS, HEAD_DIM)
    o_s = _sample_attn(idx, page_table, h_s, ocmp_s,
                       cache_win_k.reshape(nb_s, wbuf * N_KV_HEADS, HEAD_DIM),
                       cache_win_v.reshape(nb_s, wbuf * N_KV_HEADS, HEAD_DIM),
                       cache_slc_k.reshape(flat), cache_slc_v.reshape(flat), past, past)
    u_s = h_s[:, COL_U:COL_U + D_POOL]
    ctx = jnp.concatenate([state_pool[0], u_s[:, None, :]], axis=1)
    m_s = _pool_m(ctx.reshape(nb_s * (POOL_CTX + 1), D_POOL), 0, nb_s * (POOL_CTX + 1), tm=nb_s * (POOL_CTX + 1),
                  seq=nb_s * (POOL_CTX + 1), fixed_pos=past)
    m_s = m_s.reshape(nb_s, POOL_CTX + 1, D_POOL)[:, POOL_CTX]
    y_s = _tail(m_s, h_s, o_s.reshape(nb_s, D_ATTN), xs, pw, ps, wup, wun, wo, lg, lb, tm=nb_s)

    wl = min(WINDOW, seq)
    kv_s = [kv_cols(h_s, k).reshape(1, nb_s, 1, N_KV_HEADS, HEAD_DIM) for k in range(6)]
    return (
        y_p.reshape(nb_p, seq, D_MODEL),
        y_s.reshape(nb_s, 1, D_MODEL),
        kv_p[0], kv_p[1], kv_p[2], kv_p[3],
        kv_p[4][:, :, seq - wl:], kv_p[5][:, :, seq - wl:],
        h_p[:, COL_U:COL_U + D_POOL].reshape(nb_p, seq, D_POOL)[None, :, seq - POOL_CTX:],
        kv_s[0], kv_s[1], kv_s[2], kv_s[3],
        jnp.concatenate([cache_win_k, kv_s[4]], axis=2)[:, :, 1:],
        jnp.concatenate([cache_win_v, kv_s[5]], axis=2)[:, :, 1:],
        ctx[None, :, 1:],
    )
```

```python
import functools

import jax
import jax.numpy as jnp
import numpy as np
from jax import lax
from jax.experimental import pallas as pl
from jax.experimental.pallas import tpu as pltpu

D_MODEL = 2048
D_POOL = 1024
POOL_WINDOWS = (2, 4, 8, 16)
POOL_GC = D_POOL // len(POOL_WINDOWS)
POOL_CTX = max(POOL_WINDOWS) - 1
HEAD_DIM = 128
N_HEADS = 8
N_KV_HEADS = 2
GROUP = N_HEADS // N_KV_HEADS
D_ATTN = N_HEADS * HEAD_DIM
D_KV = N_KV_HEADS * HEAD_DIM
CMP_BLOCK = 32
SLC_BLOCK = 64
TOP_N = 16
WINDOW = 512
PAGE_SIZE = 128
ATTN_SCALE = HEAD_DIM ** -0.5
DEPTH = 1
ALPHA = (2.0 * DEPTH) ** 0.25
LN_EPS = 1e-5
NEG = -1e30
FORCE = 1e6

F32 = jnp.float32
BF16 = jnp.bfloat16

SUBLANES = 8
LANES = 128
VMEM_LIMIT_BYTES = 56 * 1024 * 1024

PROJ_TN = 512
A_U = 0
A_ZP = 1024
A_Q = 2048
A_COLS = 3072
B_GM = 0
B_ZN = 4096
B_GN = 5120
B_COLS = 5632
N_MERGE_COLS = 2 * D_MODEL
N_GATE = 3 * N_HEADS
N_KV_PROJ = 6
ACT_NONE, ACT_SILU, ACT_SIGMOID, ACT_SCALE = range(4)

CMP_ROWS = CMP_BLOCK * N_KV_HEADS
CMP_PITCH = 72
BLOCKS_PER_PAGE = PAGE_SIZE // CMP_BLOCK


def _params(sem):
    return pltpu.CompilerParams(dimension_semantics=sem, vmem_limit_bytes=VMEM_LIMIT_BYTES)


def _sigmoid(x):
    return 1.0 / (1.0 + jnp.exp(-x))


def _activate(acc, act):
    if act == ACT_SILU:
        return acc * _sigmoid(acc)
    if act == ACT_SIGMOID:
        return _sigmoid(acc)
    if act == ACT_SCALE:
        return acc * ATTN_SCALE
    return acc


def _xwt(x_ref, wt_ref, b_ref):
    return lax.dot_general(x_ref[...], wt_ref[...].astype(BF16), (((1,), (1,)), ((), ())),
                           preferred_element_type=F32) + b_ref[...]


def _wt_spec(tn):
    return pl.BlockSpec((pl.Element(tn), pl.Element(D_MODEL)), lambda i, j, starts8: (starts8[j] * SUBLANES, 0))


def _proj_body(starts8_ref, x_ref, wt_ref, b_ref, o_ref, *, acts):
    j = pl.program_id(1)
    acc = _xwt(x_ref, wt_ref, b_ref)
    for act in sorted(set(acts)):
        hit = functools.reduce(jnp.logical_or, [j == t for t, a in enumerate(acts) if a == act])

        @pl.when(hit)
        def _(act=act):
            o_ref[...] = _activate(acc, act)


def _proj(x, wt, b, starts, acts, tm):
    m = x.shape[0]
    starts8 = jnp.asarray([s // SUBLANES for s in starts], jnp.int32)
    return pl.pallas_call(
        functools.partial(_proj_body, acts=acts),
        grid_spec=pltpu.PrefetchScalarGridSpec(
            num_scalar_prefetch=1,
            grid=(m // tm, len(starts)),
            in_specs=[
                pl.BlockSpec((tm, D_MODEL), lambda i, j, starts8: (i, 0)),
                _wt_spec(PROJ_TN),
                pl.BlockSpec((1, PROJ_TN), lambda i, j, starts8: (0, j)),
            ],
            out_specs=pl.BlockSpec((tm, PROJ_TN), lambda i, j, starts8: (i, j)),
        ),
        out_shape=jax.ShapeDtypeStruct((m, len(starts) * PROJ_TN), F32),
        compiler_params=_params(("arbitrary", "arbitrary")),
        name="proj",
    )(starts8, x, wt, b)


def _proj_kv_body(starts8_ref, x_ref, wt_ref, b_ref, *o_refs):
    j = pl.program_id(1)
    tm = x_ref.shape[0]
    acc = _xwt(x_ref, wt_ref, b_ref)
    for k, o_ref in enumerate(o_refs):
        @pl.when(j == k)
        def _(o_ref=o_ref):
            for h in range(N_KV_HEADS):
                o_ref[pl.ds(h, tm, stride=N_KV_HEADS), :] = acc[:, h * HEAD_DIM:(h + 1) * HEAD_DIM]


def _proj_kv(x, wt, b, start, tm):
    m = x.shape[0]
    flat = jax.ShapeDtypeStruct((m * N_KV_HEADS, HEAD_DIM), F32)
    starts8 = jnp.asarray([(start + k * D_KV) // SUBLANES for k in range(N_KV_PROJ)], jnp.int32)
    return pl.pallas_call(
        _proj_kv_body,
        grid_spec=pltpu.PrefetchScalarGridSpec(
            num_scalar_prefetch=1,
            grid=(m // tm, N_KV_PROJ),
            in_specs=[
                pl.BlockSpec((tm, D_MODEL), lambda i, j, starts8: (i, 0)),
                _wt_spec(D_KV),
                pl.BlockSpec((1, D_KV), lambda i, j, starts8: (0, j)),
            ],
            out_specs=[pl.BlockSpec((tm * N_KV_HEADS, HEAD_DIM), lambda i, j, starts8: (i, 0))] * N_KV_PROJ,
        ),
        out_shape=[flat] * N_KV_PROJ,
        compiler_params=_params(("arbitrary", "arbitrary")),
        name="proj_kv",
    )(starts8, x, wt, b)


def _pool_body(halo_ref, u_ref, m_ref, *, tm, seq, fixed_pos):
    i = pl.program_id(0)
    row0 = (i * tm) % seq
    halo = jnp.where(row0 == 0, 0.0, halo_ref[...])
    ext = jnp.concatenate([halo, u_ref[...]], axis=0)
    if fixed_pos is None:
        pos = row0 + lax.broadcasted_iota(jnp.int32, (tm, 1), 0)
    else:
        pos = jnp.full((tm, 1), fixed_pos, jnp.int32)
    for g, w in enumerate(POOL_WINDOWS):
        a = ext[:, g * POOL_GC:(g + 1) * POOL_GC]
        s = a
        k = 1
        while k < w:
            s = s + pltpu.roll(s, k, 0)
            k *= 2
        cnt = jnp.minimum(pos + 1, w).astype(F32)
        m = s[2 * SUBLANES:] / cnt - a[2 * SUBLANES:]
        m_ref[:, g * POOL_GC:(g + 1) * POOL_GC] = m.astype(m_ref.dtype)


def _pool_m(src, col_block, rows, tm, seq, fixed_pos):
    halo_rows = 2 * SUBLANES
    per = tm // halo_rows
    return pl.pallas_call(
        functools.partial(_pool_body, tm=tm, seq=seq, fixed_pos=fixed_pos),
        grid=(rows // tm,),
        in_specs=[
            pl.BlockSpec((halo_rows, D_POOL), lambda i: (jnp.maximum(i * per - 1, 0), col_block)),
            pl.BlockSpec((tm, D_POOL), lambda i: (i, col_block)),
        ],
        out_specs=pl.BlockSpec((tm, D_POOL), lambda i: (i, 0)),
        out_shape=jax.ShapeDtypeStruct((rows, D_POOL), BF16),
        compiler_params=_params(("arbitrary",)),
        name="pool",
    )(src, src)


def _tail_body(m_ref, zp_ref, o_ref, zn_ref, ga_ref, gb_ref, x_ref, pw_ref, ps_ref, wup_ref, wun_ref, wo_ref,
               lg_ref, lb_ref, y_ref):
    m = m_ref[...]
    ys = [jnp.dot(m[:, g * POOL_GC:(g + 1) * POOL_GC], pw_ref[g], preferred_element_type=F32)
          for g in range(len(POOL_WINDOWS))]
    y_pool = jnp.concatenate(ys, axis=1) * ps_ref[...]
    a = jnp.dot((y_pool * zp_ref[...]).astype(BF16), wup_ref[...], preferred_element_type=F32)
    b = jnp.dot((o_ref[...] * zn_ref[...]).astype(BF16), wun_ref[...], preferred_element_type=F32)
    mix = ga_ref[...] * a + gb_ref[...] * b
    h = jnp.dot(mix.astype(BF16), wo_ref[...], preferred_element_type=F32)
    z = ALPHA * x_ref[...] + h
    mu = jnp.mean(z, axis=-1, keepdims=True)
    zc = z - mu
    var = jnp.mean(zc * zc, axis=-1, keepdims=True)
    y_ref[...] = zc * lax.rsqrt(var + LN_EPS) * lg_ref[...] + lb_ref[...]


def _tail(m, h_a, h_b, o, x, pw, ps, wup, wun, wo, lg, lb, tm):
    rows = x.shape[0]
    once = pl.Buffered(1)

    def const(shape):
        return pl.BlockSpec(shape, lambda i: (0,) * len(shape), pipeline_mode=once)

    return pl.pallas_call(
        _tail_body,
        grid=(rows // tm,),
        in_specs=[
            pl.BlockSpec((tm, D_POOL), lambda i: (i, 0)),
            pl.BlockSpec((tm, D_POOL), lambda i: (i, A_ZP // D_POOL)),
            pl.BlockSpec((tm, D_ATTN), lambda i: (i, 0)),
            pl.BlockSpec((tm, D_ATTN), lambda i: (i, B_ZN // D_ATTN)),
            pl.BlockSpec((tm, D_MODEL), lambda i: (i, B_GM // D_MODEL)),
            pl.BlockSpec((tm, D_MODEL), lambda i: (i, B_GM // D_MODEL + 1)),
            pl.BlockSpec((tm, D_MODEL), lambda i: (i, 0)),
            const((len(POOL_WINDOWS), POOL_GC, POOL_GC)),
            const((1, D_POOL)),
            const((D_POOL, D_MODEL)),
            const((D_ATTN, D_MODEL)),
            const((D_MODEL, D_MODEL)),
            const((1, D_MODEL)),
            const((1, D_MODEL)),
        ],
        out_specs=pl.BlockSpec((tm, D_MODEL), lambda i: (i, 0)),
        out_shape=jax.ShapeDtypeStruct((rows, D_MODEL), F32),
        compiler_params=_params(("arbitrary",)),
        name="tail",
    )(m, h_a, o, h_b, h_b, h_b, x, pw, ps, wup, wun, wo, lg, lb)


def _compress_body(pt_ref, c_hbm, pe_ref, w1_ref, w2_ref, o_ref, buf, lhs, sem, *, pps, n_chunks):
    b = pl.program_id(0)
    c = pl.program_id(1)
    step = b * n_chunks + c
    n_steps = pl.num_programs(0) * n_chunks
    nb = pps * BLOCKS_PER_PAGE

    def page_copies(bb, cc, slot, p):
        phys = pt_ref[bb, cc * pps + p]
        return [pltpu.make_async_copy(c_hbm.at[phys, :, par],
                                      buf.at[slot, par, pl.ds(2 * p, 2), pl.ds(0, CMP_ROWS), :],
                                      sem.at[slot])
                for par in range(2)]

    def start_step(s, slot):
        bb = s // n_chunks
        cc = s % n_chunks

        def one(p, carry):
            for cp in page_copies(bb, cc, slot, p):
                cp.start()
            return carry
        lax.fori_loop(0, pps, one, 0)

    @pl.when(step == 0)
    def _():
        start_step(0, 0)

    @pl.when(step + 1 < n_steps)
    def _():
        start_step(step + 1, (step + 1) % 2)

    slot = step % 2

    def wait_one(p, carry):
        for cp in page_copies(b, c, slot, p):
            cp.wait()
        return carry
    lax.fori_loop(0, pps, wait_one, 0)

    rows = buf.at[slot].reshape(nb * CMP_PITCH, LANES)
    for j in range(CMP_BLOCK):
        x0 = rows[pl.ds(2 * j, nb, stride=CMP_PITCH), :]
        x1 = rows[pl.ds(2 * j + 1, nb, stride=CMP_PITCH), :]
        pe = pe_ref[j:j + 1, :]
        lhs[:, (2 * j) * HEAD_DIM:(2 * j + 1) * HEAD_DIM] = (x0 + pe).astype(BF16)
        lhs[:, (2 * j + 1) * HEAD_DIM:(2 * j + 2) * HEAD_DIM] = (x1 + pe).astype(BF16)
    hid = jnp.dot(lhs[...], w1_ref[...], preferred_element_type=F32)
    hid = hid * _sigmoid(hid)
    o_ref[0] = jnp.dot(hid.astype(BF16), w2_ref[...], preferred_element_type=F32)


def _compress(page_table, cache, pe, w1, w2, pps):
    nbatch, n_pages = page_table.shape
    n_chunks = n_pages // pps
    nb = pps * BLOCKS_PER_PAGE
    kdim = CMP_BLOCK * D_KV
    return pl.pallas_call(
        functools.partial(_compress_body, pps=pps, n_chunks=n_chunks),
        grid_spec=pltpu.PrefetchScalarGridSpec(
            num_scalar_prefetch=1,
            grid=(nbatch, n_chunks),
            in_specs=[
                pl.BlockSpec(memory_space=pl.ANY),
                pl.BlockSpec((CMP_BLOCK, HEAD_DIM), lambda b, c, pt: (0, 0)),
                pl.BlockSpec((kdim, D_KV), lambda b, c, pt: (0, 0), pipeline_mode=pl.Buffered(1)),
                pl.BlockSpec((D_KV, D_KV), lambda b, c, pt: (0, 0)),
            ],
            out_specs=pl.BlockSpec((1, nb, D_KV), lambda b, c, pt: (b, c, 0)),
            scratch_shapes=[
                pltpu.VMEM((2, 2, nb // 2, CMP_PITCH, LANES), F32),
                pltpu.VMEM((nb, kdim), BF16),
                pltpu.SemaphoreType.DMA((2,)),
            ],
        ),
        out_shape=jax.ShapeDtypeStruct((nbatch, n_chunks * nb, D_KV), F32),
        compiler_params=_params(("arbitrary", "arbitrary")),
        name="compress",
    )(page_table, cache, pe, w1, w2)


def _kprep_body(ks_ref, vs_ref, kw_ref, vw_ref, kts_ref, vso_ref, ktw_ref, vwo_ref, *, tk):
    t = pl.program_id(1)
    kpos = t * tk + lax.broadcasted_iota(jnp.int32, (SLC_BLOCK, tk), 1)
    blk = lax.broadcasted_iota(jnp.int32, (SLC_BLOCK, tk), 0)
    onehot = jnp.where(kpos // SLC_BLOCK == blk, 1.0, 0.0).astype(BF16)
    for g in range(N_KV_HEADS):
        hs = slice(g * HEAD_DIM, (g + 1) * HEAD_DIM)
        head = pl.ds(g, tk, stride=N_KV_HEADS)
        kts_ref[0, g, 0, 0:HEAD_DIM, :] = ks_ref[head, :].T.astype(BF16)
        kts_ref[0, g, 0, HEAD_DIM:HEAD_DIM + SLC_BLOCK, :] = onehot
        kts_ref[0, g, 0, HEAD_DIM + SLC_BLOCK:, :] = jnp.zeros((SLC_BLOCK, tk), BF16)
        kw = kw_ref[head, :]
        for c in range(tk // LANES):
            ktw_ref[0, g, c] = kw[c * LANES:(c + 1) * LANES, :].T.astype(BF16)
        vso_ref[0, :, hs] = vs_ref[head, :].astype(BF16)
        vwo_ref[0, :, hs] = vw_ref[head, :].astype(BF16)


def _kprep(ks, vs, kw, vw, nbatch, seq, tk):
    nt = seq // tk
    src = pl.BlockSpec((tk * N_KV_HEADS, HEAD_DIM), lambda b, t: (b * nt + t, 0))
    return pl.pallas_call(
        functools.partial(_kprep_body, tk=tk),
        grid=(nbatch, nt),
        in_specs=[src, src, src, src],
        out_specs=[
            pl.BlockSpec((1, N_KV_HEADS, 1, 2 * HEAD_DIM, tk), lambda b, t: (b, 0, t, 0, 0)),
            pl.BlockSpec((1, tk, D_KV), lambda b, t: (b, t, 0)),
            pl.BlockSpec((1, N_KV_HEADS, tk // LANES, HEAD_DIM, LANES), lambda b, t: (b, 0, t, 0, 0)),
            pl.BlockSpec((1, tk, D_KV), lambda b, t: (b, t, 0)),
        ],
        out_shape=[
            jax.ShapeDtypeStruct((nbatch, N_KV_HEADS, nt, 2 * HEAD_DIM, tk), BF16),
            jax.ShapeDtypeStruct((nbatch, seq, D_KV), BF16),
            jax.ShapeDtypeStruct((nbatch, N_KV_HEADS, seq // LANES, HEAD_DIM, LANES), BF16),
            jax.ShapeDtypeStruct((nbatch, seq, D_KV), BF16),
        ],
        compiler_params=_params(("arbitrary", "arbitrary")),
        name="kprep",
    )(ks, vs, kw, vw)


def _masked_softmax(s, mask):
    s = jnp.where(mask, s, NEG)
    mx = jnp.max(s, axis=-1, keepdims=True)
    e = jnp.where(mask, jnp.exp(s - mx), 0.0)
    l = jnp.sum(e, axis=-1, keepdims=True)
    return e / jnp.where(l > 0.0, l, 1.0)


def _select_bias(p_slc_t, qpos_row, n_slc):
    shape = p_slc_t.shape
    blk = lax.broadcasted_iota(jnp.int32, shape, 0)
    real = blk < n_slc
    valid = real & (blk * SLC_BLOCK <= qpos_row)
    cur = qpos_row // SLC_BLOCK
    forced = (blk == 0) | (blk == cur) | (blk == cur - 1)
    score = jnp.where(valid & forced, FORCE, jnp.where(valid, p_slc_t, -FORCE))
    score = jnp.where(real, score, -2.0 * FORCE)
    rank = jnp.zeros(shape, jnp.int32)
    for j in range(n_slc):
        row = score[j:j + 1, :]
        ahead = (row > score) | ((row == score) & (blk > j))
        rank = rank + ahead.astype(jnp.int32)
    return jnp.where(rank < min(TOP_N, n_slc), 0.0, NEG)


def _attn_body(q_ref, gn_ref, kc_ref, vc_ref, kts_ref, vs_ref, ktw_ref, vw_ref, o_ref, *, tq, tk, seq):
    q0 = pl.program_id(1) * tq
    n_cmp = seq // CMP_BLOCK
    n_slc = -(-seq // SLC_BLOCK)
    half = n_cmp // 2
    rows = GROUP * tq
    qpos = q0 + lax.broadcasted_iota(jnp.int32, (tq, 1), 0)
    qpos4 = jnp.concatenate([qpos] * GROUP, axis=0)
    qpos_row = q0 + lax.broadcasted_iota(jnp.int32, (LANES, tq), 1)
    lane = lax.broadcasted_iota(jnp.int32, (1, n_cmp), 1)
    cmp_blk = 2 * (lane % half) + lane // half
    m_cmp = ((cmp_blk + 1) * CMP_BLOCK - 1) <= qpos4
    q = q_ref[...]
    gates = gn_ref[...]
    span = tq + WINDOW
    w0 = pl.multiple_of(jnp.maximum(q0 - WINDOW, 0), LANES)
    n_tiles = (q0 + tq + tk - 1) // tk

    for g in range(N_KV_HEADS):
        hs = slice(g * HEAD_DIM, (g + 1) * HEAD_DIM)
        qg = jnp.concatenate([q[:, (GROUP * g + h) * HEAD_DIM:(GROUP * g + h + 1) * HEAD_DIM]
                              for h in range(GROUP)], axis=0).astype(BF16)
        kc = kc_ref[0][:, hs].astype(BF16)
        vc = vc_ref[0][:, hs].astype(BF16)
        s = lax.dot_general(qg, kc, (((1,), (1,)), ((), ())), preferred_element_type=F32)
        p = _masked_softmax(s, m_cmp)
        o_cmp = jnp.dot(p.astype(BF16), vc, preferred_element_type=F32)
        p_grp = p[0:tq]
        for h in range(1, GROUP):
            p_grp = p_grp + p[h * tq:(h + 1) * tq]
        p_slc = p_grp + pltpu.roll(p_grp, half, 1)
        bias = _select_bias(p_slc.T, qpos_row, n_slc).T.astype(BF16)
        q_aug = jnp.concatenate([qg, jnp.concatenate([bias] * GROUP, axis=0)], axis=1)

        def sel_step(t, carry):
            m_run, l_run, acc = carry
            k0 = pl.multiple_of(t * tk, tk)
            sc = jnp.dot(q_aug, kts_ref[0, g, t], preferred_element_type=F32)
            kpos = k0 + lax.broadcasted_iota(jnp.int32, (1, tk), 1)
            sc = jnp.where(kpos <= qpos4, sc, NEG)
            m_new = jnp.maximum(m_run, jnp.max(sc, axis=-1, keepdims=True))
            a = jnp.exp(m_run - m_new)
            pe = jnp.exp(sc - m_new)
            l_new = a * l_run + jnp.sum(pe, axis=-1, keepdims=True)
            v = vs_ref[0, pl.ds(k0, tk), hs]
            acc = a * acc + jnp.dot(pe.astype(BF16), v, preferred_element_type=F32)
            return m_new, l_new, acc

        init = (jnp.full((rows, 1), -jnp.inf, F32), jnp.zeros((rows, 1), F32), jnp.zeros((rows, HEAD_DIM), F32))
        _, l_sel, acc_sel = lax.fori_loop(0, n_tiles, sel_step, init)
        o_sel = acc_sel / l_sel

        ktw = jnp.concatenate([ktw_ref[0, g, w0 // LANES + c] for c in range(span // LANES)], axis=1)
        sw = jnp.dot(qg, ktw, preferred_element_type=F32)
        dist = qpos4 - (w0 + lax.broadcasted_iota(jnp.int32, (1, span), 1))
        pw = _masked_softmax(sw, (dist >= 0) & (dist <= WINDOW))
        o_win = jnp.dot(pw.astype(BF16), vw_ref[0, pl.ds(w0, span), hs], preferred_element_type=F32)

        for h in range(GROUP):
            hd = GROUP * g + h
            r = slice(h * tq, (h + 1) * tq)
            o_ref[:, hd * HEAD_DIM:(hd + 1) * HEAD_DIM] = (
                gates[:, hd:hd + 1] * o_cmp[r]
                + gates[:, N_HEADS + hd:N_HEADS + hd + 1] * o_sel[r]
                + gates[:, 2 * N_HEADS + hd:2 * N_HEADS + hd + 1] * o_win[r])


def _prompt_attn(h_a, h_b, kc, vc, kts, vs, ktw, vw, nbatch, seq, tq, tk):
    nq = seq // tq
    n_cmp = seq // CMP_BLOCK
    nt = seq // tk
    return pl.pallas_call(
        functools.partial(_attn_body, tq=tq, tk=tk, seq=seq),
        grid=(nbatch, nq),
        in_specs=[
            pl.BlockSpec((tq, D_ATTN), lambda b, i: (b * nq + i, A_Q // D_ATTN)),
            pl.BlockSpec((tq, LANES), lambda b, i: (b * nq + i, B_GN // LANES)),
            pl.BlockSpec((1, n_cmp, D_KV), lambda b, i: (b, 0, 0)),
            pl.BlockSpec((1, n_cmp, D_KV), lambda b, i: (b, 0, 0)),
            pl.BlockSpec((1, N_KV_HEADS, nt, 2 * HEAD_DIM, tk), lambda b, i: (b, 0, 0, 0, 0)),
            pl.BlockSpec((1, seq, D_KV), lambda b, i: (b, 0, 0)),
            pl.BlockSpec((1, N_KV_HEADS, seq // LANES, HEAD_DIM, LANES), lambda b, i: (b, 0, 0, 0, 0)),
            pl.BlockSpec((1, seq, D_KV), lambda b, i: (b, 0, 0)),
        ],
        out_specs=pl.BlockSpec((tq, D_ATTN), lambda b, i: (b * nq + i, 0)),
        out_shape=jax.ShapeDtypeStruct((nbatch * seq, D_ATTN), F32),
        compiler_params=_params(("arbitrary", "arbitrary")),
        name="attn",
    )(h_a, h_b, kc, vc, kts, vs, ktw, vw)


def _group_queries(q_row, g):
    heads = [q_row[:, (GROUP * g + h) * HEAD_DIM:(GROUP * g + h + 1) * HEAD_DIM] for h in range(GROUP)]
    pad = jnp.zeros((2 * SUBLANES - GROUP, HEAD_DIM), F32)
    return jnp.concatenate(heads + [pad], axis=0).astype(BF16)


def _scmp_body(q_ref, kc_ref, vc_ref, ocmp_ref, pslc_ref, *, q_pos, chunk):
    b = pl.program_id(0)
    q_row = q_ref[pl.ds(b, 1), :]
    n_cmp = kc_ref.shape[1]
    half = chunk // 2
    lane = lax.broadcasted_iota(jnp.int32, (1, n_cmp), 1)
    within = lane % chunk
    cmp_blk = (lane // chunk) * chunk + 2 * (within % half) + within // half
    m_cmp = ((cmp_blk + 1) * CMP_BLOCK - 1) <= q_pos
    outs = []
    for g in range(N_KV_HEADS):
        hs = slice(g * HEAD_DIM, (g + 1) * HEAD_DIM)
        qg = _group_queries(q_row, g)
        s = lax.dot_general(qg, kc_ref[0][:, hs].astype(BF16), (((1,), (1,)), ((), ())), preferred_element_type=F32)
        p = _masked_softmax(s, m_cmp)
        o = jnp.dot(p.astype(BF16), vc_ref[0][:, hs].astype(BF16), preferred_element_type=F32)
        outs.append(o[0:GROUP])
        p_grp = p[0:1]
        for h in range(1, GROUP):
            p_grp = p_grp + p[h:h + 1]
        parts = []
        for c in range(n_cmp // chunk):
            pc = p_grp[:, c * chunk:(c + 1) * chunk]
            parts.append(pc + pltpu.roll(pc, half, 1))
        pslc_ref[0, g:g + 1, :] = jnp.concatenate(parts, axis=1)
    ocmp_ref[0] = jnp.concatenate(outs, axis=0)


def _sample_cmp(q, kc, vc, q_pos, chunk):
    nbatch, n_cmp, _ = kc.shape
    return pl.pallas_call(
        functools.partial(_scmp_body, q_pos=q_pos, chunk=chunk),
        grid=(nbatch,),
        in_specs=[
            pl.BlockSpec((nbatch, D_ATTN), lambda b: (0, A_Q // D_ATTN)),
            pl.BlockSpec((1, n_cmp, D_KV), lambda b: (b, 0, 0)),
            pl.BlockSpec((1, n_cmp, D_KV), lambda b: (b, 0, 0)),
        ],
        out_specs=[
            pl.BlockSpec((1, N_HEADS, HEAD_DIM), lambda b: (b, 0, 0)),
            pl.BlockSpec((1, N_KV_HEADS, n_cmp), lambda b: (b, 0, 0)),
        ],
        out_shape=[
            jax.ShapeDtypeStruct((nbatch, N_HEADS, HEAD_DIM), F32),
            jax.ShapeDtypeStruct((nbatch, N_KV_HEADS, n_cmp), F32),
        ],
        compiler_params=_params(("arbitrary",)),
        name="s_cmp",
    )(q, kc, vc)


def _stopk_body(p_ref, idx_ref, *, q_pos, chunk, n_slc):
    p = p_ref[...]
    rows, width = p.shape
    half = chunk // 2
    n_in = (width // chunk) * half
    lane = lax.broadcasted_iota(jnp.int32, (1, width), 1)
    within = lane % chunk
    blk = jnp.where(within < half, (lane // chunk) * half + within, -1)
    extra = (lane >= half) & (lane < half + (n_slc - n_in))
    blk = jnp.where(extra, n_in + lane - half, blk)
    real = blk >= 0
    valid = real & (blk * SLC_BLOCK <= q_pos)
    cur = q_pos // SLC_BLOCK
    forced = (blk == 0) | (blk == cur) | (blk == cur - 1)
    base = jnp.where(extra, 0.0, p)
    x = jnp.where(valid & forced, FORCE, jnp.where(valid, base, -FORCE))
    x = jnp.where(real, x, -jnp.inf)
    blk_f = blk.astype(F32)
    out_lane = lax.broadcasted_iota(jnp.int32, (rows, LANES), 1)
    out = jnp.zeros((rows, LANES), F32)
    for r in range(min(TOP_N, n_slc)):
        mx = jnp.max(x, axis=-1, keepdims=True)
        pick = jnp.min(jnp.where(x == mx, blk_f, float(2 ** 30)), axis=-1, keepdims=True)
        out = jnp.where(out_lane == r, pick, out)
        x = jnp.where(blk_f == pick, -jnp.inf, x)
    idx_ref[...] = out.astype(jnp.int32)


def _sample_topk(pslc, q_pos, chunk, n_slc):
    rows, width = pslc.shape
    return pl.pallas_call(
        functools.partial(_stopk_body, q_pos=q_pos, chunk=chunk, n_slc=n_slc),
        grid=(1,),
        in_specs=[pl.BlockSpec((rows, width), lambda i: (0, 0))],
        out_specs=pl.BlockSpec((rows, LANES), lambda i: (0, 0)),
        out_shape=jax.ShapeDtypeStruct((rows, LANES), jnp.int32),
        compiler_params=_params(("arbitrary",)),
        name="s_topk",
    )(pslc)


def _sattn_body(idx_ref, pt_ref, q_ref, gn_ref, ocmp_ref, knew_ref, vnew_ref, kwn_ref, vwn_ref, wk_ref, wv_ref,
                sk_hbm, sv_hbm, o_ref, kbuf, vbuf, sem, *, q_pos, n_top, n_pages, past):
    b = pl.program_id(0)
    nbatch = pl.num_programs(0)
    half_rows = SLC_BLOCK * N_KV_HEADS
    tail_blk = past // SLC_BLOCK

    def copies(bb, slot, g, i):
        blk = idx_ref[bb * N_KV_HEADS + g, i]
        page = jnp.minimum(blk // 2, n_pages - 1)
        start = pl.multiple_of(pt_ref[bb, page] * (PAGE_SIZE * N_KV_HEADS) + (blk % 2) * half_rows, half_rows)
        return [pltpu.make_async_copy(hbm.at[pl.ds(start, half_rows), :], dst.at[slot, g, i], sem.at[t, slot, g, i])
                for t, (hbm, dst) in enumerate(((sk_hbm, kbuf), (sv_hbm, vbuf)))]

    def start_all(bb, slot):
        for g in range(N_KV_HEADS):
            for i in range(n_top):
                for cp in copies(bb, slot, g, i):
                    cp.start()

    @pl.when(b == 0)
    def _():
        start_all(0, 0)

    @pl.when(b + 1 < nbatch)
    def _():
        start_all(b + 1, (b + 1) % 2)

    slot = b % 2
    for g in range(N_KV_HEADS):
        for i in range(n_top):
            for cp in copies(b, slot, g, i):
                cp.wait()

    q_row = q_ref[pl.ds(b, 1), :]
    gates = gn_ref[pl.ds(b, 1), :]
    pad = jnp.zeros((2 * SUBLANES - 1, HEAD_DIM), F32)

    def new_row(ref, g):
        return jnp.concatenate([ref[pl.ds(b * N_KV_HEADS + g, 1), :], pad], axis=0).astype(BF16)

    outs = []
    for g in range(N_KV_HEADS):
        qg = _group_queries(q_row, g)
        ks, vs = [], []
        key = lax.broadcasted_iota(jnp.int32, (1, n_top * SLC_BLOCK), 1)
        key_blk = jnp.zeros((1, n_top * SLC_BLOCK), jnp.int32)
        for i in range(n_top):
            blk = idx_ref[b * N_KV_HEADS + g, i]
            ks.append(kbuf[slot, g, i, pl.ds(g, SLC_BLOCK, stride=N_KV_HEADS), :])
            vs.append(vbuf[slot, g, i, pl.ds(g, SLC_BLOCK, stride=N_KV_HEADS), :])
            key_blk = jnp.where(key // SLC_BLOCK == i, blk, key_blk)
        k_sel = jnp.concatenate(ks, axis=0).astype(BF16)
        v_sel = jnp.concatenate(vs, axis=0).astype(BF16)
        m_sel = (key_blk * SLC_BLOCK + key % SLC_BLOCK <= q_pos) & (key_blk < tail_blk)
        tail_sel = jnp.max(jnp.where(key_blk == tail_blk, 1.0, 0.0), axis=-1, keepdims=True) > 0.5
        first = lax.broadcasted_iota(jnp.int32, (1, 2 * SUBLANES), 1) == 0
        o_sel = _two_part_attention(qg, k_sel, v_sel, m_sel, new_row(knew_ref, g), new_row(vnew_ref, g),
                                    first & tail_sel)
        wbuf = wk_ref.shape[1] // N_KV_HEADS
        k_win = wk_ref[0, pl.ds(g, wbuf, stride=N_KV_HEADS), :].astype(BF16)
        v_win = wv_ref[0, pl.ds(g, wbuf, stride=N_KV_HEADS), :].astype(BF16)
        dist = q_pos - (past - wbuf + lax.broadcasted_iota(jnp.int32, (1, wbuf), 1))
        m_win = (dist >= 0) & (dist <= WINDOW)
        o_win = _two_part_attention(qg, k_win, v_win, m_win, new_row(kwn_ref, g), new_row(vwn_ref, g), first)
        o_cmp = ocmp_ref[0, GROUP * g:GROUP * (g + 1), :]
        for h in range(GROUP):
            hd = GROUP * g + h
            outs.append(gates[:, hd:hd + 1] * o_cmp[h:h + 1]
                        + gates[:, N_HEADS + hd:N_HEADS + hd + 1] * o_sel[h:h + 1]
                        + gates[:, 2 * N_HEADS + hd:2 * N_HEADS + hd + 1] * o_win[h:h + 1])
    o_ref[0] = jnp.concatenate(outs, axis=0)


def _two_part_attention(q, k1, v1, m1, k2, v2, m2):
    dn = (((1,), (1,)), ((), ()))
    s1 = jnp.where(m1, lax.dot_general(q, k1, dn, preferred_element_type=F32), NEG)
    s2 = jnp.where(m2, lax.dot_general(q, k2, dn, preferred_element_type=F32), NEG)
    mx = jnp.maximum(jnp.max(s1, axis=-1, keepdims=True), jnp.max(s2, axis=-1, keepdims=True))
    e1 = jnp.where(m1, jnp.exp(s1 - mx), 0.0)
    e2 = jnp.where(m2, jnp.exp(s2 - mx), 0.0)
    l = jnp.sum(e1, axis=-1, keepdims=True) + jnp.sum(e2, axis=-1, keepdims=True)
    inv = 1.0 / jnp.where(l > 0.0, l, 1.0)
    o = (jnp.dot((e1 * inv).astype(BF16), v1, preferred_element_type=F32)
         + jnp.dot((e2 * inv).astype(BF16), v2, preferred_element_type=F32))
    return o


def _sample_attn(idx, page_table, h_a, h_b, ocmp, new_rows, win_k, win_v, slc_k, slc_v, q_pos, past):
    nbatch, n_pages = page_table.shape
    n_top = min(TOP_N, past // SLC_BLOCK + 1)
    half_rows = SLC_BLOCK * N_KV_HEADS
    flat_new = pl.BlockSpec((nbatch * N_KV_HEADS, HEAD_DIM), lambda b, idx, pt: (0, 0))

    def whole(col_block, width):
        return pl.BlockSpec((nbatch, width), lambda b, idx, pt: (0, col_block))

    return pl.pallas_call(
        functools.partial(_sattn_body, q_pos=q_pos, n_top=n_top, n_pages=n_pages, past=past),
        grid_spec=pltpu.PrefetchScalarGridSpec(
            num_scalar_prefetch=2,
            grid=(nbatch,),
            in_specs=[
                whole(A_Q // D_ATTN, D_ATTN),
                whole(B_GN // LANES, LANES),
                pl.BlockSpec((1, N_HEADS, HEAD_DIM), lambda b, idx, pt: (b, 0, 0)),
                flat_new, flat_new, flat_new, flat_new,
                pl.BlockSpec((1,) + win_k.shape[1:], lambda b, idx, pt: (b, 0, 0)),
                pl.BlockSpec((1,) + win_v.shape[1:], lambda b, idx, pt: (b, 0, 0)),
                pl.BlockSpec(memory_space=pl.ANY),
                pl.BlockSpec(memory_space=pl.ANY),
            ],
            out_specs=pl.BlockSpec((1, N_HEADS, HEAD_DIM), lambda b, idx, pt: (b, 0, 0)),
            scratch_shapes=[
                pltpu.VMEM((2, N_KV_HEADS, n_top, half_rows, LANES), F32),
                pltpu.VMEM((2, N_KV_HEADS, n_top, half_rows, LANES), F32),
                pltpu.SemaphoreType.DMA((2, 2, N_KV_HEADS, n_top)),
            ],
        ),
        out_shape=jax.ShapeDtypeStruct((nbatch, N_HEADS, HEAD_DIM), F32),
        compiler_params=_params(("arbitrary",)),
        name="s_attn",
    )(idx, page_table, h_a, h_b, ocmp, *new_rows, win_k, win_v, slc_k, slc_v)


REF_KV = 3 * D_POOL
REF_GN = REF_KV + N_KV_PROJ * D_KV
REF_ZN = REF_GN + N_GATE
REF_GM = REF_ZN + D_ATTN
STARTS_A = tuple(range(0, A_COLS, PROJ_TN))
STARTS_B = (tuple(range(REF_GM, REF_GM + N_MERGE_COLS, PROJ_TN)) + tuple(range(REF_ZN, REF_GM, PROJ_TN)) + (REF_GN,))


def _block_diag2(w):
    z = jnp.zeros_like(w)
    return jnp.concatenate([jnp.concatenate([w, z], axis=-1), jnp.concatenate([z, w], axis=-1)], axis=-2)


ACTS_A = ((ACT_NONE,) * (A_ZP // PROJ_TN) + (ACT_SILU,) * ((A_Q - A_ZP) // PROJ_TN)
          + (ACT_SCALE,) * ((A_COLS - A_Q) // PROJ_TN))
ACTS_B = ((ACT_SIGMOID,) * (B_ZN // PROJ_TN) + (ACT_SILU,) * ((B_GN - B_ZN) // PROJ_TN)
          + (ACT_SIGMOID,) * ((B_COLS - B_GN) // PROJ_TN))


def _project(x, wt, b, tm, tm_kv):
    b_b = jnp.concatenate([b[:, s:s + PROJ_TN] for s in STARTS_B], axis=1)
    h_a = _proj(x, wt, b[:, :A_COLS], STARTS_A, ACTS_A, tm)
    kv = _proj_kv(x, wt, b[:, REF_KV:REF_GN], REF_KV, tm_kv)
    h_b = _proj(x, wt, b_b, STARTS_B, ACTS_B, tm)
    return h_a, h_b, kv


def kernel(x_prompt, x_sample, cache_cmp_k, cache_cmp_v, cache_slc_k, cache_slc_v, cache_win_k, cache_win_v,
           state_pool, page_table, w_in, b_in, pool_w, pool_scale, cmp_pe_k, cmp_w1_k, cmp_w2_k, cmp_pe_v, cmp_w1_v,
           cmp_w2_v, w_up_pool, w_up_nsa, w_out, ln_g, ln_b):
    nb_p, seq, _ = x_prompt.shape
    nb_s = x_sample.shape[0]
    n_pages = page_table.shape[1]
    past = n_pages * PAGE_SIZE
    n_phys = cache_cmp_k.shape[1]
    wbuf = cache_win_k.shape[2]

    wt = w_in[0].T
    b = b_in[0][None, :]
    kdim = CMP_BLOCK * D_KV
    cmp_k = (cmp_pe_k[0], _block_diag2(cmp_w1_k[0]).reshape(kdim, D_KV).astype(BF16),
             _block_diag2(cmp_w2_k[0]).astype(BF16))
    cmp_v = (cmp_pe_v[0], _block_diag2(cmp_w1_v[0]).reshape(kdim, D_KV).astype(BF16),
             _block_diag2(cmp_w2_v[0]).astype(BF16))
    pw = pool_w[0].astype(BF16)
    ps = pool_scale[0][None, :]
    wup = w_up_pool[0].astype(BF16)
    wun = w_up_nsa[0].astype(BF16)
    wo = w_out[0].astype(BF16)
    lg = ln_g[0][None, :]
    lb = ln_b[0][None, :]

    xp = x_prompt.reshape(nb_p * seq, D_MODEL)
    xs = x_sample.reshape(nb_s, D_MODEL)
    ha_p, hb_p, kv_p = _project(xp.astype(BF16), wt, b, tm=2048, tm_kv=1024)
    ha_s, hb_s, kv_s = _project(xs.astype(BF16), wt, b, tm=nb_s, tm_kv=nb_s)

    paged = (2, 2, CMP_ROWS, LANES)
    prompt_pages = seq // PAGE_SIZE
    ident = jnp.arange(nb_p * prompt_pages, dtype=jnp.int32).reshape(nb_p, prompt_pages)
    kc_p = _compress(ident, kv_p[0].reshape((-1,) + paged), *cmp_k, pps=prompt_pages)
    vc_p = _compress(ident, kv_p[1].reshape((-1,) + paged), *cmp_v, pps=prompt_pages)
    kts, vs, ktw, vw = _kprep(kv_p[2], kv_p[3], kv_p[4], kv_p[5], nb_p, seq, tk=512)
    o_p = _prompt_attn(ha_p, hb_p, kc_p, vc_p, kts, vs, ktw, vw, nb_p, seq, tq=128, tk=512)
    m_p = _pool_m(ha_p, A_U // D_POOL, nb_p * seq, tm=512, seq=seq, fixed_pos=None)
    y_p = _tail(m_p, ha_p, hb_p, o_p, xp, pw, ps, wup, wun, wo, lg, lb, tm=256)

    chunk_pages = 64
    kc_s = _compress(page_table, cache_cmp_k.reshape((n_phys,) + paged), *cmp_k, pps=chunk_pages)
    vc_s = _compress(page_table, cache_cmp_v.reshape((n_phys,) + paged), *cmp_v, pps=chunk_pages)
    chunk = chunk_pages * BLOCKS_PER_PAGE
    n_slc = past // SLC_BLOCK + 1
    ocmp_s, pslc_s = _sample_cmp(ha_s, kc_s, vc_s, past, chunk)
    idx = _sample_topk(pslc_s.reshape(nb_s * N_KV_HEADS, -1), past, chunk, n_slc)
    flat = (n_phys * PAGE_SIZE * N_KV_HEADS, HEAD_DIM)
    o_s = _sample_attn(idx, page_table, ha_s, hb_s, ocmp_s, kv_s[2:6],
                       cache_win_k.reshape(nb_s, wbuf * N_KV_HEADS, HEAD_DIM),
                       cache_win_v.reshape(nb_s, wbuf * N_KV_HEADS, HEAD_DIM),
                       cache_slc_k.reshape(flat), cache_slc_v.reshape(flat), past, past)
    u_s = ha_s[:, A_U:A_U + D_POOL]
    ctx = jnp.concatenate([state_pool[0], u_s[:, None, :]], axis=1)
    m_s = _pool_m(ctx.reshape(nb_s * (POOL_CTX + 1), D_POOL), 0, nb_s * (POOL_CTX + 1), tm=nb_s * (POOL_CTX + 1),
                  seq=nb_s * (POOL_CTX + 1), fixed_pos=past)
    m_s = m_s.reshape(nb_s, POOL_CTX + 1, D_POOL)[:, POOL_CTX]
    y_s = _tail(m_s, ha_s, hb_s, o_s.reshape(nb_s, D_ATTN), xs, pw, ps, wup, wun, wo, lg, lb, tm=nb_s)

    wl = min(WINDOW, seq)
    kv_p = [a.reshape(1, nb_p, seq, N_KV_HEADS, HEAD_DIM) for a in kv_p]
    kv_s = [a.reshape(1, nb_s, 1, N_KV_HEADS, HEAD_DIM) for a in kv_s]
    return (
        y_p.reshape(nb_p, seq, D_MODEL),
        y_s.reshape(nb_s, 1, D_MODEL),
        kv_p[0], kv_p[1], kv_p[2], kv_p[3],
        kv_p[4][:, :, seq - wl:], kv_p[5][:, :, seq - wl:],
        ha_p[:, A_U:A_U + D_POOL].reshape(nb_p, seq, D_POOL)[None, :, seq - POOL_CTX:],
        kv_s[0], kv_s[1], kv_s[2], kv_s[3],
        jnp.concatenate([cache_win_k, kv_s[4]], axis=2)[:, :, 1:],
        jnp.concatenate([cache_win_v, kv_s[5]], axis=2)[:, :, 1:],
        ctx[None, :, 1:],
    )
```

```python
import functools

import jax
import jax.numpy as jnp
import numpy as np
from jax import lax
from jax.experimental import pallas as pl
from jax.experimental.pallas import tpu as pltpu

D_MODEL = 2048
D_POOL = 1024
POOL_WINDOWS = (2, 4, 8, 16)
POOL_GC = D_POOL // len(POOL_WINDOWS)
POOL_CTX = max(POOL_WINDOWS) - 1
HEAD_DIM = 128
N_HEADS = 8
N_KV_HEADS = 2
GROUP = N_HEADS // N_KV_HEADS
D_ATTN = N_HEADS * HEAD_DIM
D_KV = N_KV_HEADS * HEAD_DIM
CMP_BLOCK = 32
SLC_BLOCK = 64
TOP_N = 16
WINDOW = 512
PAGE_SIZE = 128
ATTN_SCALE = HEAD_DIM ** -0.5
DEPTH = 1
ALPHA = (2.0 * DEPTH) ** 0.25
LN_EPS = 1e-5
NEG = -1e30
FORCE = 1e6

F32 = jnp.float32
BF16 = jnp.bfloat16

SUBLANES = 8
LANES = 128
VMEM_LIMIT_BYTES = 56 * 1024 * 1024

PROJ_TN = 512
PROJ_RC = 512
N_MERGE_COLS = 2 * D_MODEL
SIG_GN = N_MERGE_COLS
N_GATE = 3 * N_HEADS
N_KV_PROJ = 6
ACT_NONE, ACT_SILU, ACT_SIGMOID, ACT_SCALE = range(4)

CMP_ROWS = CMP_BLOCK * N_KV_HEADS
CMP_PITCH = 72
BLOCKS_PER_PAGE = PAGE_SIZE // CMP_BLOCK


def _params(sem):
    return pltpu.CompilerParams(dimension_semantics=sem, vmem_limit_bytes=VMEM_LIMIT_BYTES)


def _sigmoid(x):
    return 1.0 / (1.0 + jnp.exp(-x))


def _activate(acc, act):
    if act == ACT_SILU:
        return acc * _sigmoid(acc)
    if act == ACT_SIGMOID:
        return _sigmoid(acc)
    if act == ACT_SCALE:
        return acc * ATTN_SCALE
    return acc


def _xwt(x, wt, b):
    return lax.dot_general(x, wt, (((1,), (1,)), ((), ())), preferred_element_type=F32) + b


def _wt_spec(tn):
    return pl.BlockSpec((pl.Element(tn), pl.Element(D_MODEL)), lambda i, j, starts8: (starts8[j] * SUBLANES, 0))


def _proj_body(starts8_ref, x_ref, wt_ref, b_ref, o_ref, *, act, rc):
    wt = wt_ref[...].astype(BF16)
    for c in range(x_ref.shape[0] // rc):
        r = slice(c * rc, (c + 1) * rc)
        o_ref[r, :] = _activate(_xwt(x_ref[r, :], wt, b_ref[...]), act)


def _proj(x, wt, b, starts, act, tm):
    m = x.shape[0]
    starts8 = jnp.asarray([s // SUBLANES for s in starts], jnp.int32)
    b = jnp.concatenate([b[:, s:s + PROJ_TN] for s in starts], axis=1)
    return pl.pallas_call(
        functools.partial(_proj_body, act=act, rc=min(tm, PROJ_RC)),
        grid_spec=pltpu.PrefetchScalarGridSpec(
            num_scalar_prefetch=1,
            grid=(m // tm, len(starts)),
            in_specs=[
                pl.BlockSpec((tm, D_MODEL), lambda i, j, starts8: (i, 0)),
                _wt_spec(PROJ_TN),
                pl.BlockSpec((1, PROJ_TN), lambda i, j, starts8: (0, j)),
            ],
            out_specs=pl.BlockSpec((tm, PROJ_TN), lambda i, j, starts8: (i, j)),
        ),
        out_shape=jax.ShapeDtypeStruct((m, len(starts) * PROJ_TN), F32),
        compiler_params=_params(("arbitrary", "arbitrary")),
        name="proj",
    )(starts8, x, wt, b)


def _proj_kv_body(starts8_ref, x_ref, wt_ref, b_ref, *o_refs, rc):
    j = pl.program_id(1)
    wt = wt_ref[...].astype(BF16)
    for k, o_ref in enumerate(o_refs):
        @pl.when(j == k)
        def _(o_ref=o_ref):
            for c in range(x_ref.shape[0] // rc):
                acc = _xwt(x_ref[c * rc:(c + 1) * rc, :], wt, b_ref[...])
                for h in range(N_KV_HEADS):
                    o_ref[pl.ds(N_KV_HEADS * c * rc + h, rc, stride=N_KV_HEADS), :] = (
                        acc[:, h * HEAD_DIM:(h + 1) * HEAD_DIM])


def _proj_kv(x, wt, b, start, tm):
    m = x.shape[0]
    flat = jax.ShapeDtypeStruct((m * N_KV_HEADS, HEAD_DIM), F32)
    starts8 = jnp.asarray([(start + k * D_KV) // SUBLANES for k in range(N_KV_PROJ)], jnp.int32)
    b = b[:, start:start + N_KV_PROJ * D_KV]
    return pl.pallas_call(
        functools.partial(_proj_kv_body, rc=min(tm, PROJ_RC)),
        grid_spec=pltpu.PrefetchScalarGridSpec(
            num_scalar_prefetch=1,
            grid=(m // tm, N_KV_PROJ),
            in_specs=[
                pl.BlockSpec((tm, D_MODEL), lambda i, j, starts8: (i, 0)),
                _wt_spec(D_KV),
                pl.BlockSpec((1, D_KV), lambda i, j, starts8: (0, j)),
            ],
            out_specs=[pl.BlockSpec((tm * N_KV_HEADS, HEAD_DIM), lambda i, j, starts8: (i, 0))] * N_KV_PROJ,
        ),
        out_shape=[flat] * N_KV_PROJ,
        compiler_params=_params(("arbitrary", "arbitrary")),
        name="proj_kv",
    )(starts8, x, wt, b)


def _pool_body(halo_ref, u_ref, m_ref, *, tm, seq, fixed_pos):
    i = pl.program_id(0)
    row0 = (i * tm) % seq
    halo = jnp.where(row0 == 0, 0.0, halo_ref[...])
    ext = jnp.concatenate([halo, u_ref[...]], axis=0)
    if fixed_pos is None:
        pos = row0 + lax.broadcasted_iota(jnp.int32, (tm, 1), 0)
    else:
        pos = jnp.full((tm, 1), fixed_pos, jnp.int32)
    for g, w in enumerate(POOL_WINDOWS):
        a = ext[:, g * POOL_GC:(g + 1) * POOL_GC]
        s = a
        k = 1
        while k < w:
            s = s + pltpu.roll(s, k, 0)
            k *= 2
        cnt = jnp.minimum(pos + 1, w).astype(F32)
        m = s[2 * SUBLANES:] / cnt - a[2 * SUBLANES:]
        m_ref[:, g * POOL_GC:(g + 1) * POOL_GC] = m.astype(m_ref.dtype)


def _pool_m(src, col_block, rows, tm, seq, fixed_pos):
    halo_rows = 2 * SUBLANES
    per = tm // halo_rows
    return pl.pallas_call(
        functools.partial(_pool_body, tm=tm, seq=seq, fixed_pos=fixed_pos),
        grid=(rows // tm,),
        in_specs=[
            pl.BlockSpec((halo_rows, D_POOL), lambda i: (jnp.maximum(i * per - 1, 0), col_block)),
            pl.BlockSpec((tm, D_POOL), lambda i: (i, col_block)),
        ],
        out_specs=pl.BlockSpec((tm, D_POOL), lambda i: (i, 0)),
        out_shape=jax.ShapeDtypeStruct((rows, D_POOL), BF16),
        compiler_params=_params(("arbitrary",)),
        name="pool",
    )(src, src)


def _tail_body(m_ref, zp_ref, o_ref, zn_ref, ga_ref, gb_ref, x_ref, pw_ref, ps_ref, wup_ref, wun_ref, wo_ref,
               lg_ref, lb_ref, y_ref):
    m = m_ref[...]
    ys = [jnp.dot(m[:, g * POOL_GC:(g + 1) * POOL_GC], pw_ref[g], preferred_element_type=F32)
          for g in range(len(POOL_WINDOWS))]
    y_pool = jnp.concatenate(ys, axis=1) * ps_ref[...]
    a = jnp.dot((y_pool * zp_ref[...]).astype(BF16), wup_ref[...], preferred_element_type=F32)
    b = jnp.dot((o_ref[...] * zn_ref[...]).astype(BF16), wun_ref[...], preferred_element_type=F32)
    mix = ga_ref[...] * a + gb_ref[...] * b
    h = jnp.dot(mix.astype(BF16), wo_ref[...], preferred_element_type=F32)
    z = ALPHA * x_ref[...] + h
    mu = jnp.mean(z, axis=-1, keepdims=True)
    zc = z - mu
    var = jnp.mean(zc * zc, axis=-1, keepdims=True)
    y_ref[...] = zc * lax.rsqrt(var + LN_EPS) * lg_ref[...] + lb_ref[...]


def _tail(m, h_silu, h_sig, o, x, pw, ps, wup, wun, wo, lg, lb, tm):
    rows = x.shape[0]
    once = pl.Buffered(1)

    def const(shape):
        return pl.BlockSpec(shape, lambda i: (0,) * len(shape), pipeline_mode=once)

    return pl.pallas_call(
        _tail_body,
        grid=(rows // tm,),
        in_specs=[
            pl.BlockSpec((tm, D_POOL), lambda i: (i, 0)),
            pl.BlockSpec((tm, D_POOL), lambda i: (i, 0)),
            pl.BlockSpec((tm, D_ATTN), lambda i: (i, 0)),
            pl.BlockSpec((tm, D_ATTN), lambda i: (i, D_POOL // D_ATTN)),
            pl.BlockSpec((tm, D_MODEL), lambda i: (i, 0)),
            pl.BlockSpec((tm, D_MODEL), lambda i: (i, 1)),
            pl.BlockSpec((tm, D_MODEL), lambda i: (i, 0)),
            const((len(POOL_WINDOWS), POOL_GC, POOL_GC)),
            const((1, D_POOL)),
            const((D_POOL, D_MODEL)),
            const((D_ATTN, D_MODEL)),
            const((D_MODEL, D_MODEL)),
            const((1, D_MODEL)),
            const((1, D_MODEL)),
        ],
        out_specs=pl.BlockSpec((tm, D_MODEL), lambda i: (i, 0)),
        out_shape=jax.ShapeDtypeStruct((rows, D_MODEL), F32),
        compiler_params=_params(("arbitrary",)),
        name="tail",
    )(m, h_silu, o, h_silu, h_sig, h_sig, x, pw, ps, wup, wun, wo, lg, lb)


def _compress_body(pt_ref, c_hbm, pe_ref, w1_ref, w2_ref, o_ref, buf, lhs, res, sem, *, pps, n_chunks):
    b = pl.program_id(0)
    c = pl.program_id(1)
    step = b * n_chunks + c
    n_steps = pl.num_programs(0) * n_chunks
    nb = pps * BLOCKS_PER_PAGE

    def page_copy(seq_row, page0, slot, p):
        phys = pt_ref[seq_row, page0 + p]
        return pltpu.make_async_copy(c_hbm.at[pl.ds(phys * BLOCKS_PER_PAGE, BLOCKS_PER_PAGE)],
                                     buf.at[slot, pl.ds(BLOCKS_PER_PAGE * p, BLOCKS_PER_PAGE), pl.ds(0, CMP_ROWS), :],
                                     sem.at[slot])

    def start_step(s, slot):
        seq_row = s // n_chunks
        page0 = (s % n_chunks) * pps

        def one(p, carry):
            page_copy(seq_row, page0, slot, p).start()
            return carry
        lax.fori_loop(0, pps, one, 0)

    @pl.when(step == 0)
    def _():
        start_step(0, 0)

    slot = step % 2

    @pl.when(step + 1 < n_steps)
    def _():
        start_step(step + 1, 1 - slot)

    for p in range(pps):
        page_copy(b, c * pps, slot, p).wait()

    rows = buf.at[slot].reshape(nb * CMP_PITCH, LANES)
    for j in range(CMP_BLOCK):
        x0 = rows[pl.ds(2 * j, nb, stride=CMP_PITCH), :]
        x1 = rows[pl.ds(2 * j + 1, nb, stride=CMP_PITCH), :]
        pe = pe_ref[j:j + 1, :]
        lhs[:, (2 * j) * HEAD_DIM:(2 * j + 1) * HEAD_DIM] = (x0 + pe).astype(BF16)
        lhs[:, (2 * j + 1) * HEAD_DIM:(2 * j + 2) * HEAD_DIM] = (x1 + pe).astype(BF16)
    hid = jnp.dot(lhs[...], w1_ref[...], preferred_element_type=F32)
    hid = hid * _sigmoid(hid)
    out = jnp.dot(hid.astype(BF16), w2_ref[...], preferred_element_type=F32)
    half = nb // 2
    for h in range(N_KV_HEADS):
        hs = slice(h * HEAD_DIM, (h + 1) * HEAD_DIM)
        res[h] = out[:, hs]
        o_ref[0, 0:half, hs] = res[h, pl.ds(0, half, stride=2), :]
        o_ref[0, half:nb, hs] = res[h, pl.ds(1, half, stride=2), :]


def _compress(page_table, cache, pe, w1, w2, pps):
    nbatch, n_pages = page_table.shape
    n_chunks = n_pages // pps
    nb = pps * BLOCKS_PER_PAGE
    kdim = CMP_BLOCK * D_KV
    return pl.pallas_call(
        functools.partial(_compress_body, pps=pps, n_chunks=n_chunks),
        grid_spec=pltpu.PrefetchScalarGridSpec(
            num_scalar_prefetch=1,
            grid=(nbatch, n_chunks),
            in_specs=[
                pl.BlockSpec(memory_space=pl.ANY),
                pl.BlockSpec((CMP_BLOCK, HEAD_DIM), lambda b, c, pt: (0, 0)),
                pl.BlockSpec((kdim, D_KV), lambda b, c, pt: (0, 0), pipeline_mode=pl.Buffered(1)),
                pl.BlockSpec((D_KV, D_KV), lambda b, c, pt: (0, 0)),
            ],
            out_specs=pl.BlockSpec((1, nb, D_KV), lambda b, c, pt: (b, c, 0)),
            scratch_shapes=[
                pltpu.VMEM((2, nb, CMP_PITCH, LANES), F32),
                pltpu.VMEM((nb, kdim), BF16),
                pltpu.VMEM((N_KV_HEADS, nb, HEAD_DIM), F32),
                pltpu.SemaphoreType.DMA((2,)),
            ],
        ),
        out_shape=jax.ShapeDtypeStruct((nbatch, n_chunks * nb, D_KV), F32),
        compiler_params=_params(("arbitrary", "arbitrary")),
        name="compress",
    )(page_table, cache, pe, w1, w2)


def _kprep_body(ks_ref, vs_ref, kw_ref, vw_ref, kts_ref, vso_ref, ktw_ref, vwo_ref, *, tk):
    t = pl.program_id(1)
    kpos = t * tk + lax.broadcasted_iota(jnp.int32, (SLC_BLOCK, tk), 1)
    blk = lax.broadcasted_iota(jnp.int32, (SLC_BLOCK, tk), 0)
    onehot = jnp.where(kpos // SLC_BLOCK == blk, 1.0, 0.0).astype(BF16)
    for g in range(N_KV_HEADS):
        hs = slice(g * HEAD_DIM, (g + 1) * HEAD_DIM)
        head = pl.ds(g, tk, stride=N_KV_HEADS)
        kts_ref[0, g, 0, 0:HEAD_DIM, :] = ks_ref[head, :].T.astype(BF16)
        kts_ref[0, g, 0, HEAD_DIM:HEAD_DIM + SLC_BLOCK, :] = onehot
        kts_ref[0, g, 0, HEAD_DIM + SLC_BLOCK:, :] = jnp.zeros((SLC_BLOCK, tk), BF16)
        kw = kw_ref[head, :]
        for c in range(tk // LANES):
            ktw_ref[0, g, c] = kw[c * LANES:(c + 1) * LANES, :].T.astype(BF16)
        vso_ref[0, :, hs] = vs_ref[head, :].astype(BF16)
        vwo_ref[0, :, hs] = vw_ref[head, :].astype(BF16)


def _kprep(ks, vs, kw, vw, nbatch, seq, tk):
    nt = seq // tk
    src = pl.BlockSpec((tk * N_KV_HEADS, HEAD_DIM), lambda b, t: (b * nt + t, 0))
    return pl.pallas_call(
        functools.partial(_kprep_body, tk=tk),
        grid=(nbatch, nt),
        in_specs=[src, src, src, src],
        out_specs=[
            pl.BlockSpec((1, N_KV_HEADS, 1, 2 * HEAD_DIM, tk), lambda b, t: (b, 0, t, 0, 0)),
            pl.BlockSpec((1, tk, D_KV), lambda b, t: (b, t, 0)),
            pl.BlockSpec((1, N_KV_HEADS, tk // LANES, HEAD_DIM, LANES), lambda b, t: (b, 0, t, 0, 0)),
            pl.BlockSpec((1, tk, D_KV), lambda b, t: (b, t, 0)),
        ],
        out_shape=[
            jax.ShapeDtypeStruct((nbatch, N_KV_HEADS, nt, 2 * HEAD_DIM, tk), BF16),
            jax.ShapeDtypeStruct((nbatch, seq, D_KV), BF16),
            jax.ShapeDtypeStruct((nbatch, N_KV_HEADS, seq // LANES, HEAD_DIM, LANES), BF16),
            jax.ShapeDtypeStruct((nbatch, seq, D_KV), BF16),
        ],
        compiler_params=_params(("arbitrary", "arbitrary")),
        name="kprep",
    )(ks, vs, kw, vw)


def _masked_softmax(s, mask):
    s = jnp.where(mask, s, NEG)
    mx = jnp.max(s, axis=-1, keepdims=True)
    e = jnp.where(mask, jnp.exp(s - mx), 0.0)
    l = jnp.sum(e, axis=-1, keepdims=True)
    return e / jnp.where(l > 0.0, l, 1.0)


def _select_bias(p_slc_t, qpos_row, n_slc):
    shape = p_slc_t.shape
    blk = lax.broadcasted_iota(jnp.int32, shape, 0)
    real = blk < n_slc
    valid = real & (blk * SLC_BLOCK <= qpos_row)
    cur = qpos_row // SLC_BLOCK
    forced = (blk == 0) | (blk == cur) | (blk == cur - 1)
    score = jnp.where(valid & forced, FORCE, jnp.where(valid, p_slc_t, -FORCE))
    score = jnp.where(real, score, -2.0 * FORCE)
    rank = jnp.zeros(shape, jnp.int32)
    for j in range(n_slc):
        row = score[j:j + 1, :]
        ahead = (row > score) | ((row == score) & (blk > j))
        rank = rank + ahead.astype(jnp.int32)
    return jnp.where(rank < min(TOP_N, n_slc), 0.0, NEG)


def _attn_body(q_ref, gn_ref, kc_ref, vc_ref, kts_ref, vs_ref, ktw_ref, vw_ref, o_ref, *, tq, tk, seq):
    q0 = pl.program_id(1) * tq
    n_cmp = seq // CMP_BLOCK
    n_slc = -(-seq // SLC_BLOCK)
    half = n_cmp // 2
    rows = GROUP * tq
    qpos = q0 + lax.broadcasted_iota(jnp.int32, (tq, 1), 0)
    qpos4 = jnp.concatenate([qpos] * GROUP, axis=0)
    qpos_row = q0 + lax.broadcasted_iota(jnp.int32, (LANES, tq), 1)
    lane = lax.broadcasted_iota(jnp.int32, (1, n_cmp), 1)
    cmp_blk = 2 * (lane % half) + lane // half
    m_cmp = ((cmp_blk + 1) * CMP_BLOCK - 1) <= qpos4
    q = q_ref[...]
    gates = gn_ref[...]
    span = tq + WINDOW
    w0 = pl.multiple_of(jnp.maximum(q0 - WINDOW, 0), LANES)
    n_tiles = (q0 + tq + tk - 1) // tk

    for g in range(N_KV_HEADS):
        hs = slice(g * HEAD_DIM, (g + 1) * HEAD_DIM)
        qg = jnp.concatenate([q[:, (GROUP * g + h) * HEAD_DIM:(GROUP * g + h + 1) * HEAD_DIM]
                              for h in range(GROUP)], axis=0).astype(BF16)
        kc = kc_ref[0][:, hs].astype(BF16)
        vc = vc_ref[0][:, hs].astype(BF16)
        s = lax.dot_general(qg, kc, (((1,), (1,)), ((), ())), preferred_element_type=F32)
        p = _masked_softmax(s, m_cmp)
        o_cmp = jnp.dot(p.astype(BF16), vc, preferred_element_type=F32)
        p_grp = p[0:tq]
        for h in range(1, GROUP):
            p_grp = p_grp + p[h * tq:(h + 1) * tq]
        p_slc = p_grp + pltpu.roll(p_grp, half, 1)
        bias = _select_bias(p_slc.T, qpos_row, n_slc).T.astype(BF16)
        q_aug = jnp.concatenate([qg, jnp.concatenate([bias] * GROUP, axis=0)], axis=1)

        def sel_step(t, carry):
            m_run, l_run, acc = carry
            k0 = pl.multiple_of(t * tk, tk)
            sc = jnp.dot(q_aug, kts_ref[0, g, t], preferred_element_type=F32)
            kpos = k0 + lax.broadcasted_iota(jnp.int32, (1, tk), 1)
            sc = jnp.where(kpos <= qpos4, sc, NEG)
            m_new = jnp.maximum(m_run, jnp.max(sc, axis=-1, keepdims=True))
            a = jnp.exp(m_run - m_new)
            pe = jnp.exp(sc - m_new)
            l_new = a * l_run + jnp.sum(pe, axis=-1, keepdims=True)
            v = vs_ref[0, pl.ds(k0, tk), hs]
            acc = a * acc + jnp.dot(pe.astype(BF16), v, preferred_element_type=F32)
            return m_new, l_new, acc

        init = (jnp.full((rows, 1), -jnp.inf, F32), jnp.zeros((rows, 1), F32), jnp.zeros((rows, HEAD_DIM), F32))
        _, l_sel, acc_sel = lax.fori_loop(0, n_tiles, sel_step, init)
        o_sel = acc_sel / l_sel

        ktw = jnp.concatenate([ktw_ref[0, g, w0 // LANES + c] for c in range(span // LANES)], axis=1)
        sw = jnp.dot(qg, ktw, preferred_element_type=F32)
        dist = qpos4 - (w0 + lax.broadcasted_iota(jnp.int32, (1, span), 1))
        pw = _masked_softmax(sw, (dist >= 0) & (dist <= WINDOW))
        o_win = jnp.dot(pw.astype(BF16), vw_ref[0, pl.ds(w0, span), hs], preferred_element_type=F32)

        for h in range(GROUP):
            hd = GROUP * g + h
            r = slice(h * tq, (h + 1) * tq)
            o_ref[:, hd * HEAD_DIM:(hd + 1) * HEAD_DIM] = (
                gates[:, hd:hd + 1] * o_cmp[r]
                + gates[:, N_HEADS + hd:N_HEADS + hd + 1] * o_sel[r]
                + gates[:, 2 * N_HEADS + hd:2 * N_HEADS + hd + 1] * o_win[r])


def _prompt_attn(h_q, h_sig, kc, vc, kts, vs, ktw, vw, nbatch, seq, tq, tk):
    nq = seq // tq
    n_cmp = seq // CMP_BLOCK
    nt = seq // tk
    return pl.pallas_call(
        functools.partial(_attn_body, tq=tq, tk=tk, seq=seq),
        grid=(nbatch, nq),
        in_specs=[
            pl.BlockSpec((tq, D_ATTN), lambda b, i: (b * nq + i, 0)),
            pl.BlockSpec((tq, LANES), lambda b, i: (b * nq + i, SIG_GN // LANES)),
            pl.BlockSpec((1, n_cmp, D_KV), lambda b, i: (b, 0, 0)),
            pl.BlockSpec((1, n_cmp, D_KV), lambda b, i: (b, 0, 0)),
            pl.BlockSpec((1, N_KV_HEADS, nt, 2 * HEAD_DIM, tk), lambda b, i: (b, 0, 0, 0, 0)),
            pl.BlockSpec((1, seq, D_KV), lambda b, i: (b, 0, 0)),
            pl.BlockSpec((1, N_KV_HEADS, seq // LANES, HEAD_DIM, LANES), lambda b, i: (b, 0, 0, 0, 0)),
            pl.BlockSpec((1, seq, D_KV), lambda b, i: (b, 0, 0)),
        ],
        out_specs=pl.BlockSpec((tq, D_ATTN), lambda b, i: (b * nq + i, 0)),
        out_shape=jax.ShapeDtypeStruct((nbatch * seq, D_ATTN), F32),
        compiler_params=_params(("arbitrary", "arbitrary")),
        name="attn",
    )(h_q, h_sig, kc, vc, kts, vs, ktw, vw)


def _group_queries(q_row, g):
    heads = [q_row[:, (GROUP * g + h) * HEAD_DIM:(GROUP * g + h + 1) * HEAD_DIM] for h in range(GROUP)]
    pad = jnp.zeros((2 * SUBLANES - GROUP, HEAD_DIM), F32)
    return jnp.concatenate(heads + [pad], axis=0).astype(BF16)


def _scmp_body(q_ref, kc_ref, vc_ref, ocmp_ref, pslc_ref, *, q_pos, chunk):
    b = pl.program_id(0)
    q_row = q_ref[pl.ds(b, 1), :]
    n_cmp = kc_ref.shape[1]
    half = chunk // 2
    lane = lax.broadcasted_iota(jnp.int32, (1, n_cmp), 1)
    within = lane % chunk
    cmp_blk = (lane // chunk) * chunk + 2 * (within % half) + within // half
    m_cmp = ((cmp_blk + 1) * CMP_BLOCK - 1) <= q_pos
    outs = []
    for g in range(N_KV_HEADS):
        hs = slice(g * HEAD_DIM, (g + 1) * HEAD_DIM)
        qg = _group_queries(q_row, g)
        s = lax.dot_general(qg, kc_ref[0][:, hs].astype(BF16), (((1,), (1,)), ((), ())), preferred_element_type=F32)
        p = _masked_softmax(s, m_cmp)
        o = jnp.dot(p.astype(BF16), vc_ref[0][:, hs].astype(BF16), preferred_element_type=F32)
        outs.append(o[0:GROUP])
        p_grp = p[0:1]
        for h in range(1, GROUP):
            p_grp = p_grp + p[h:h + 1]
        parts = []
        for c in range(n_cmp // chunk):
            pc = p_grp[:, c * chunk:(c + 1) * chunk]
            parts.append(pc + pltpu.roll(pc, half, 1))
        pslc_ref[0, g:g + 1, :] = jnp.concatenate(parts, axis=1)
    ocmp_ref[0] = jnp.concatenate(outs, axis=0)


def _sample_cmp(q, kc, vc, q_pos, chunk):
    nbatch, n_cmp, _ = kc.shape
    return pl.pallas_call(
        functools.partial(_scmp_body, q_pos=q_pos, chunk=chunk),
        grid=(nbatch,),
        in_specs=[
            pl.BlockSpec((nbatch, D_ATTN), lambda b: (0, 0)),
            pl.BlockSpec((1, n_cmp, D_KV), lambda b: (b, 0, 0)),
            pl.BlockSpec((1, n_cmp, D_KV), lambda b: (b, 0, 0)),
        ],
        out_specs=[
            pl.BlockSpec((1, N_HEADS, HEAD_DIM), lambda b: (b, 0, 0)),
            pl.BlockSpec((1, N_KV_HEADS, n_cmp), lambda b: (b, 0, 0)),
        ],
        out_shape=[
            jax.ShapeDtypeStruct((nbatch, N_HEADS, HEAD_DIM), F32),
            jax.ShapeDtypeStruct((nbatch, N_KV_HEADS, n_cmp), F32),
        ],
        compiler_params=_params(("arbitrary",)),
        name="s_cmp",
    )(q, kc, vc)


def _stopk_body(p_ref, idx_ref, *, q_pos, chunk, n_slc):
    p = p_ref[...]
    rows, width = p.shape
    half = chunk // 2
    n_in = (width // chunk) * half
    lane = lax.broadcasted_iota(jnp.int32, (1, width), 1)
    within = lane % chunk
    blk = jnp.where(within < half, (lane // chunk) * half + within, -1)
    extra = (lane >= half) & (lane < half + (n_slc - n_in))
    blk = jnp.where(extra, n_in + lane - half, blk)
    real = blk >= 0
    valid = real & (blk * SLC_BLOCK <= q_pos)
    cur = q_pos // SLC_BLOCK
    forced = (blk == 0) | (blk == cur) | (blk == cur - 1)
    base = jnp.where(extra, 0.0, p)
    x = jnp.where(valid & forced, FORCE, jnp.where(valid, base, -FORCE))
    x = jnp.where(real, x, -jnp.inf)
    blk_f = blk.astype(F32)
    out_lane = lax.broadcasted_iota(jnp.int32, (rows, LANES), 1)
    out = jnp.zeros((rows, LANES), F32)
    for r in range(min(TOP_N, n_slc)):
        mx = jnp.max(x, axis=-1, keepdims=True)
        pick = jnp.min(jnp.where(x == mx, blk_f, float(2 ** 30)), axis=-1, keepdims=True)
        out = jnp.where(out_lane == r, pick, out)
        x = jnp.where(blk_f == pick, -jnp.inf, x)
    idx_ref[...] = out.astype(jnp.int32)


def _sample_topk(pslc, q_pos, chunk, n_slc):
    rows, width = pslc.shape
    return pl.pallas_call(
        functools.partial(_stopk_body, q_pos=q_pos, chunk=chunk, n_slc=n_slc),
        grid=(1,),
        in_specs=[pl.BlockSpec((rows, width), lambda i: (0, 0))],
        out_specs=pl.BlockSpec((rows, LANES), lambda i: (0, 0)),
        out_shape=jax.ShapeDtypeStruct((rows, LANES), jnp.int32),
        compiler_params=_params(("arbitrary",)),
        name="s_topk",
    )(pslc)


def _sattn_body(idx_ref, pt_ref, q_ref, gn_ref, ocmp_ref, knew_ref, vnew_ref, kwn_ref, vwn_ref, wk_ref, wv_ref,
                sk_hbm, sv_hbm, o_ref, kbuf, vbuf, sem, *, q_pos, n_top, n_pages, past):
    b = pl.program_id(0)
    nbatch = pl.num_programs(0)
    half_rows = SLC_BLOCK * N_KV_HEADS
    tail_blk = past // SLC_BLOCK

    def copies(bb, slot, g, i):
        blk = idx_ref[bb * N_KV_HEADS + g, i]
        page = jnp.minimum(blk // 2, n_pages - 1)
        start = pl.multiple_of(pt_ref[bb, page] * (PAGE_SIZE * N_KV_HEADS) + (blk % 2) * half_rows, half_rows)
        return [pltpu.make_async_copy(hbm.at[pl.ds(start, half_rows), :], dst.at[slot, g, i], sem.at[t, slot, g, i])
                for t, (hbm, dst) in enumerate(((sk_hbm, kbuf), (sv_hbm, vbuf)))]

    def start_all(bb, slot):
        for g in range(N_KV_HEADS):
            for i in range(n_top):
                for cp in copies(bb, slot, g, i):
                    cp.start()

    @pl.when(b == 0)
    def _():
        start_all(0, 0)

    @pl.when(b + 1 < nbatch)
    def _():
        start_all(b + 1, (b + 1) % 2)

    slot = b % 2
    for g in range(N_KV_HEADS):
        for i in range(n_top):
            for cp in copies(b, slot, g, i):
                cp.wait()

    q_row = q_ref[pl.ds(b, 1), :]
    gates = gn_ref[pl.ds(b, 1), :]
    pad = jnp.zeros((2 * SUBLANES - 1, HEAD_DIM), F32)

    def new_row(ref, g):
        return jnp.concatenate([ref[pl.ds(b * N_KV_HEADS + g, 1), :], pad], axis=0).astype(BF16)

    outs = []
    for g in range(N_KV_HEADS):
        qg = _group_queries(q_row, g)
        ks, vs = [], []
        key = lax.broadcasted_iota(jnp.int32, (1, n_top * SLC_BLOCK), 1)
        key_blk = jnp.zeros((1, n_top * SLC_BLOCK), jnp.int32)
        for i in range(n_top):
            blk = idx_ref[b * N_KV_HEADS + g, i]
            ks.append(kbuf[slot, g, i, pl.ds(g, SLC_BLOCK, stride=N_KV_HEADS), :])
            vs.append(vbuf[slot, g, i, pl.ds(g, SLC_BLOCK, stride=N_KV_HEADS), :])
            key_blk = jnp.where(key // SLC_BLOCK == i, blk, key_blk)
        k_sel = jnp.concatenate(ks, axis=0).astype(BF16)
        v_sel = jnp.concatenate(vs, axis=0).astype(BF16)
        m_sel = (key_blk * SLC_BLOCK + key % SLC_BLOCK <= q_pos) & (key_blk < tail_blk)
        tail_sel = jnp.max(jnp.where(key_blk == tail_blk, 1.0, 0.0), axis=-1, keepdims=True) > 0.5
        first = lax.broadcasted_iota(jnp.int32, (1, 2 * SUBLANES), 1) == 0
        o_sel = _two_part_attention(qg, k_sel, v_sel, m_sel, new_row(knew_ref, g), new_row(vnew_ref, g),
                                    first & tail_sel)
        wbuf = wk_ref.shape[1] // N_KV_HEADS
        k_win = wk_ref[0, pl.ds(g, wbuf, stride=N_KV_HEADS), :].astype(BF16)
        v_win = wv_ref[0, pl.ds(g, wbuf, stride=N_KV_HEADS), :].astype(BF16)
        dist = q_pos - (past - wbuf + lax.broadcasted_iota(jnp.int32, (1, wbuf), 1))
        m_win = (dist >= 0) & (dist <= WINDOW)
        o_win = _two_part_attention(qg, k_win, v_win, m_win, new_row(kwn_ref, g), new_row(vwn_ref, g), first)
        o_cmp = ocmp_ref[0, GROUP * g:GROUP * (g + 1), :]
        for h in range(GROUP):
            hd = GROUP * g + h
            outs.append(gates[:, hd:hd + 1] * o_cmp[h:h + 1]
                        + gates[:, N_HEADS + hd:N_HEADS + hd + 1] * o_sel[h:h + 1]
                        + gates[:, 2 * N_HEADS + hd:2 * N_HEADS + hd + 1] * o_win[h:h + 1])
    o_ref[0] = jnp.concatenate(outs, axis=0)


def _two_part_attention(q, k1, v1, m1, k2, v2, m2):
    dn = (((1,), (1,)), ((), ()))
    s1 = jnp.where(m1, lax.dot_general(q, k1, dn, preferred_element_type=F32), NEG)
    s2 = jnp.where(m2, lax.dot_general(q, k2, dn, preferred_element_type=F32), NEG)
    mx = jnp.maximum(jnp.max(s1, axis=-1, keepdims=True), jnp.max(s2, axis=-1, keepdims=True))
    e1 = jnp.where(m1, jnp.exp(s1 - mx), 0.0)
    e2 = jnp.where(m2, jnp.exp(s2 - mx), 0.0)
    l = jnp.sum(e1, axis=-1, keepdims=True) + jnp.sum(e2, axis=-1, keepdims=True)
    inv = 1.0 / jnp.where(l > 0.0, l, 1.0)
    o = (jnp.dot((e1 * inv).astype(BF16), v1, preferred_element_type=F32)
         + jnp.dot((e2 * inv).astype(BF16), v2, preferred_element_type=F32))
    return o


def _sample_attn(idx, page_table, h_q, h_sig, ocmp, new_rows, win_k, win_v, slc_k, slc_v, q_pos, past):
    nbatch, n_pages = page_table.shape
    n_top = min(TOP_N, past // SLC_BLOCK + 1)
    half_rows = SLC_BLOCK * N_KV_HEADS
    flat_new = pl.BlockSpec((nbatch * N_KV_HEADS, HEAD_DIM), lambda b, idx, pt: (0, 0))

    def whole(col_block, width):
        return pl.BlockSpec((nbatch, width), lambda b, idx, pt: (0, col_block))

    return pl.pallas_call(
        functools.partial(_sattn_body, q_pos=q_pos, n_top=n_top, n_pages=n_pages, past=past),
        grid_spec=pltpu.PrefetchScalarGridSpec(
            num_scalar_prefetch=2,
            grid=(nbatch,),
            in_specs=[
                whole(0, D_ATTN),
                whole(SIG_GN // LANES, LANES),
                pl.BlockSpec((1, N_HEADS, HEAD_DIM), lambda b, idx, pt: (b, 0, 0)),
                flat_new, flat_new, flat_new, flat_new,
                pl.BlockSpec((1,) + win_k.shape[1:], lambda b, idx, pt: (b, 0, 0)),
                pl.BlockSpec((1,) + win_v.shape[1:], lambda b, idx, pt: (b, 0, 0)),
                pl.BlockSpec(memory_space=pl.ANY),
                pl.BlockSpec(memory_space=pl.ANY),
            ],
            out_specs=pl.BlockSpec((1, N_HEADS, HEAD_DIM), lambda b, idx, pt: (b, 0, 0)),
            scratch_shapes=[
                pltpu.VMEM((2, N_KV_HEADS, n_top, half_rows, LANES), F32),
                pltpu.VMEM((2, N_KV_HEADS, n_top, half_rows, LANES), F32),
                pltpu.SemaphoreType.DMA((2, 2, N_KV_HEADS, n_top)),
            ],
        ),
        out_shape=jax.ShapeDtypeStruct((nbatch, N_HEADS, HEAD_DIM), F32),
        compiler_params=_params(("arbitrary",)),
        name="s_attn",
    )(idx, page_table, h_q, h_sig, ocmp, *new_rows, win_k, win_v, slc_k, slc_v)


REF_KV = 3 * D_POOL
REF_GN = REF_KV + N_KV_PROJ * D_KV
REF_ZN = REF_GN + N_GATE
REF_GM = REF_ZN + D_ATTN


def _tiles(start, width):
    return tuple(range(start, start + width, PROJ_TN))


STARTS_U = _tiles(0, D_POOL)
STARTS_SILU = _tiles(D_POOL, D_POOL) + _tiles(REF_ZN, D_ATTN)
STARTS_Q = _tiles(2 * D_POOL, D_ATTN)
STARTS_SIG = _tiles(REF_GM, N_MERGE_COLS) + (REF_GN,)


def _block_diag2(w):
    z = jnp.zeros_like(w)
    return jnp.concatenate([jnp.concatenate([w, z], axis=-1), jnp.concatenate([z, w], axis=-1)], axis=-2)


def _project(x, wt, b, tm, tm_kv):
    u = _proj(x, wt, b, STARTS_U, ACT_NONE, tm)
    h_silu = _proj(x, wt, b, STARTS_SILU, ACT_SILU, tm)
    h_q = _proj(x, wt, b, STARTS_Q, ACT_SCALE, tm)
    h_sig = _proj(x, wt, b, STARTS_SIG, ACT_SIGMOID, tm)
    kv = _proj_kv(x, wt, b, REF_KV, tm_kv)
    return u, h_silu, h_q, h_sig, kv


def kernel(x_prompt, x_sample, cache_cmp_k, cache_cmp_v, cache_slc_k, cache_slc_v, cache_win_k, cache_win_v,
           state_pool, page_table, w_in, b_in, pool_w, pool_scale, cmp_pe_k, cmp_w1_k, cmp_w2_k, cmp_pe_v, cmp_w1_v,
           cmp_w2_v, w_up_pool, w_up_nsa, w_out, ln_g, ln_b):
    nb_p, seq, _ = x_prompt.shape
    nb_s = x_sample.shape[0]
    n_pages = page_table.shape[1]
    past = n_pages * PAGE_SIZE
    n_phys = cache_cmp_k.shape[1]
    wbuf = cache_win_k.shape[2]

    wt = w_in[0].T
    b = b_in[0][None, :]
    kdim = CMP_BLOCK * D_KV
    cmp_k = (cmp_pe_k[0], _block_diag2(cmp_w1_k[0]).reshape(kdim, D_KV).astype(BF16),
             _block_diag2(cmp_w2_k[0]).astype(BF16))
    cmp_v = (cmp_pe_v[0], _block_diag2(cmp_w1_v[0]).reshape(kdim, D_KV).astype(BF16),
             _block_diag2(cmp_w2_v[0]).astype(BF16))
    pw = pool_w[0].astype(BF16)
    ps = pool_scale[0][None, :]
    wup = w_up_pool[0].astype(BF16)
    wun = w_up_nsa[0].astype(BF16)
    wo = w_out[0].astype(BF16)
    lg = ln_g[0][None, :]
    lb = ln_b[0][None, :]

    xp = x_prompt.reshape(nb_p * seq, D_MODEL)
    xs = x_sample.reshape(nb_s, D_MODEL)
    u_p, hsilu_p, hq_p, hsig_p, kv_p = _project(xp.astype(BF16), wt, b, tm=2048, tm_kv=1024)
    u_s, hsilu_s, hq_s, hsig_s, kv_s = _project(xs.astype(BF16), wt, b, tm=nb_s, tm_kv=nb_s)

    blocks = (-1, CMP_ROWS, LANES)
    prompt_pages = seq // PAGE_SIZE
    ident = jnp.arange(nb_p * prompt_pages, dtype=jnp.int32).reshape(nb_p, prompt_pages)
    kc_p = _compress(ident, kv_p[0].reshape(blocks), *cmp_k, pps=prompt_pages)
    vc_p = _compress(ident, kv_p[1].reshape(blocks), *cmp_v, pps=prompt_pages)
    kts, vs, ktw, vw = _kprep(kv_p[2], kv_p[3], kv_p[4], kv_p[5], nb_p, seq, tk=512)
    o_p = _prompt_attn(hq_p, hsig_p, kc_p, vc_p, kts, vs, ktw, vw, nb_p, seq, tq=128, tk=512)
    m_p = _pool_m(u_p, 0, nb_p * seq, tm=512, seq=seq, fixed_pos=None)
    y_p = _tail(m_p, hsilu_p, hsig_p, o_p, xp, pw, ps, wup, wun, wo, lg, lb, tm=256)

    chunk_pages = 64
    kc_s = _compress(page_table, cache_cmp_k.reshape(blocks), *cmp_k, pps=chunk_pages)
    vc_s = _compress(page_table, cache_cmp_v.reshape(blocks), *cmp_v, pps=chunk_pages)
    chunk = chunk_pages * BLOCKS_PER_PAGE
    n_slc = past // SLC_BLOCK + 1
    ocmp_s, pslc_s = _sample_cmp(hq_s, kc_s, vc_s, past, chunk)
    idx = _sample_topk(pslc_s.reshape(nb_s * N_KV_HEADS, -1), past, chunk, n_slc)
    flat = (n_phys * PAGE_SIZE * N_KV_HEADS, HEAD_DIM)
    o_s = _sample_attn(idx, page_table, hq_s, hsig_s, ocmp_s, kv_s[2:6],
                       cache_win_k.reshape(nb_s, wbuf * N_KV_HEADS, HEAD_DIM),
                       cache_win_v.reshape(nb_s, wbuf * N_KV_HEADS, HEAD_DIM),
                       cache_slc_k.reshape(flat), cache_slc_v.reshape(flat), past, past)
    ctx = jnp.concatenate([state_pool[0], u_s[:, None, :]], axis=1)
    m_s = _pool_m(ctx.reshape(nb_s * (POOL_CTX + 1), D_POOL), 0, nb_s * (POOL_CTX + 1), tm=nb_s * (POOL_CTX + 1),
                  seq=nb_s * (POOL_CTX + 1), fixed_pos=past)
    m_s = m_s.reshape(nb_s, POOL_CTX + 1, D_POOL)[:, POOL_CTX]
    y_s = _tail(m_s, hsilu_s, hsig_s, o_s.reshape(nb_s, D_ATTN), xs, pw, ps, wup, wun, wo, lg, lb, tm=nb_s)

    wl = min(WINDOW, seq)
    kv_p = [a.reshape(1, nb_p, seq, N_KV_HEADS, HEAD_DIM) for a in kv_p]
    kv_s = [a.reshape(1, nb_s, 1, N_KV_HEADS, HEAD_DIM) for a in kv_s]
    return (
        y_p.reshape(nb_p, seq, D_MODEL),
        y_s.reshape(nb_s, 1, D_MODEL),
        kv_p[0], kv_p[1], kv_p[2], kv_p[3],
        kv_p[4][:, :, seq - wl:], kv_p[5][:, :, seq - wl:],
        u_p.reshape(nb_p, seq, D_POOL)[None, :, seq - POOL_CTX:],
        kv_s[0], kv_s[1], kv_s[2], kv_s[3],
        jnp.concatenate([cache_win_k, kv_s[4]], axis=2)[:, :, 1:],
        jnp.concatenate([cache_win_v, kv_s[5]], axis=2)[:, :, 1:],
        ctx[None, :, 1:],
    )
```

```python
import functools

import jax
import jax.numpy as jnp
import numpy as np
from jax import lax
from jax.experimental import pallas as pl
from jax.experimental.pallas import tpu as pltpu

D_MODEL = 2048
D_POOL = 1024
POOL_WINDOWS = (2, 4, 8, 16)
POOL_GC = D_POOL // len(POOL_WINDOWS)
POOL_CTX = max(POOL_WINDOWS) - 1
HEAD_DIM = 128
N_HEADS = 8
N_KV_HEADS = 2
GROUP = N_HEADS // N_KV_HEADS
D_ATTN = N_HEADS * HEAD_DIM
D_KV = N_KV_HEADS * HEAD_DIM
CMP_BLOCK = 32
SLC_BLOCK = 64
TOP_N = 16
WINDOW = 512
PAGE_SIZE = 128
ATTN_SCALE = HEAD_DIM ** -0.5
DEPTH = 1
ALPHA = (2.0 * DEPTH) ** 0.25
LN_EPS = 1e-5
NEG = -1e30
FORCE = 1e6

F32 = jnp.float32
BF16 = jnp.bfloat16

SUBLANES = 8
LANES = 128
VMEM_LIMIT_BYTES = 56 * 1024 * 1024

PROJ_TN = 512
PROJ_RC = 512
N_MERGE_COLS = 2 * D_MODEL
SIG_GN = N_MERGE_COLS
N_GATE = 3 * N_HEADS
N_KV_PROJ = 6
ACT_NONE, ACT_SILU, ACT_SIGMOID, ACT_SCALE = range(4)
LOG2E = 1.4426950408889634
ROW_BLOCK = 128
ROW_CHUNK = 32

CMP_ROWS = CMP_BLOCK * N_KV_HEADS
CMP_PITCH = 72
BLOCKS_PER_PAGE = PAGE_SIZE // CMP_BLOCK


def _params(sem):
    return pltpu.CompilerParams(dimension_semantics=sem, vmem_limit_bytes=VMEM_LIMIT_BYTES)


def _sigmoid(x):
    return 1.0 / (1.0 + jnp.exp(-x))


def _activate(acc, act):
    if act == ACT_SILU:
        return acc * _sigmoid(acc)
    if act == ACT_SIGMOID:
        return _sigmoid(acc)
    if act == ACT_SCALE:
        return acc * (ATTN_SCALE * LOG2E)
    return acc


def _xwt(x, wt, b):
    return lax.dot_general(x, wt, (((1,), (1,)), ((), ())), preferred_element_type=F32) + b


def _wt_spec(tn):
    return pl.BlockSpec((pl.Element(tn), pl.Element(D_MODEL)), lambda i, j, starts8: (starts8[j] * SUBLANES, 0))


def _proj_body(starts8_ref, x_ref, wt_ref, b_ref, o_ref, *, act, rc):
    wt = wt_ref[...].astype(BF16)
    for c in range(x_ref.shape[0] // rc):
        r = slice(c * rc, (c + 1) * rc)
        o_ref[r, :] = _activate(_xwt(x_ref[r, :], wt, b_ref[...]), act).astype(o_ref.dtype)


def _proj(x, wt, b, starts, act, tm, out_dtype):
    m = x.shape[0]
    starts8 = jnp.asarray([s // SUBLANES for s in starts], jnp.int32)
    b = jnp.concatenate([b[:, s:s + PROJ_TN] for s in starts], axis=1)
    return pl.pallas_call(
        functools.partial(_proj_body, act=act, rc=min(tm, PROJ_RC)),
        grid_spec=pltpu.PrefetchScalarGridSpec(
            num_scalar_prefetch=1,
            grid=(m // tm, len(starts)),
            in_specs=[
                pl.BlockSpec((tm, D_MODEL), lambda i, j, starts8: (i, 0)),
                _wt_spec(PROJ_TN),
                pl.BlockSpec((1, PROJ_TN), lambda i, j, starts8: (0, j)),
            ],
            out_specs=pl.BlockSpec((tm, PROJ_TN), lambda i, j, starts8: (i, j)),
        ),
        out_shape=jax.ShapeDtypeStruct((m, len(starts) * PROJ_TN), out_dtype),
        compiler_params=_params(("arbitrary", "arbitrary")),
        name="proj",
    )(starts8, x, wt, b)


def _proj_kv_body(starts8_ref, x_ref, wt_ref, b_ref, *o_refs, rc):
    j = pl.program_id(1)
    wt = wt_ref[...].astype(BF16)
    for k, o_ref in enumerate(o_refs):
        @pl.when(j == k)
        def _(o_ref=o_ref):
            for c in range(x_ref.shape[0] // rc):
                acc = _xwt(x_ref[c * rc:(c + 1) * rc, :], wt, b_ref[...])
                for h in range(N_KV_HEADS):
                    o_ref[pl.ds(N_KV_HEADS * c * rc + h, rc, stride=N_KV_HEADS), :] = (
                        acc[:, h * HEAD_DIM:(h + 1) * HEAD_DIM])


def _proj_kv(x, wt, b, start, tm):
    m = x.shape[0]
    flat = jax.ShapeDtypeStruct((m * N_KV_HEADS, HEAD_DIM), F32)
    starts8 = jnp.asarray([(start + k * D_KV) // SUBLANES for k in range(N_KV_PROJ)], jnp.int32)
    b = b[:, start:start + N_KV_PROJ * D_KV]
    return pl.pallas_call(
        functools.partial(_proj_kv_body, rc=min(tm, PROJ_RC)),
        grid_spec=pltpu.PrefetchScalarGridSpec(
            num_scalar_prefetch=1,
            grid=(m // tm, N_KV_PROJ),
            in_specs=[
                pl.BlockSpec((tm, D_MODEL), lambda i, j, starts8: (i, 0)),
                _wt_spec(D_KV),
                pl.BlockSpec((1, D_KV), lambda i, j, starts8: (0, j)),
            ],
            out_specs=[pl.BlockSpec((tm * N_KV_HEADS, HEAD_DIM), lambda i, j, starts8: (i, 0))] * N_KV_PROJ,
        ),
        out_shape=[flat] * N_KV_PROJ,
        compiler_params=_params(("arbitrary", "arbitrary")),
        name="proj_kv",
    )(starts8, x, wt, b)


def _pool_body(halo_ref, u_ref, m_ref, *, tm, seq, fixed_pos):
    i = pl.program_id(0)
    row0 = (i * tm) % seq
    halo = jnp.where(row0 == 0, 0.0, halo_ref[...])
    ext = jnp.concatenate([halo, u_ref[...]], axis=0)
    if fixed_pos is None:
        pos = row0 + lax.broadcasted_iota(jnp.int32, (tm, 1), 0)
    else:
        pos = jnp.full((tm, 1), fixed_pos, jnp.int32)
    for g, w in enumerate(POOL_WINDOWS):
        a = ext[:, g * POOL_GC:(g + 1) * POOL_GC]
        s = a
        k = 1
        while k < w:
            s = s + pltpu.roll(s, k, 0)
            k *= 2
        cnt = jnp.minimum(pos + 1, w).astype(F32)
        m = s[2 * SUBLANES:] / cnt - a[2 * SUBLANES:]
        m_ref[:, g * POOL_GC:(g + 1) * POOL_GC] = m.astype(m_ref.dtype)


def _pool_m(src, col_block, rows, tm, seq, fixed_pos):
    halo_rows = 2 * SUBLANES
    per = tm // halo_rows
    return pl.pallas_call(
        functools.partial(_pool_body, tm=tm, seq=seq, fixed_pos=fixed_pos),
        grid=(rows // tm,),
        in_specs=[
            pl.BlockSpec((halo_rows, D_POOL), lambda i: (jnp.maximum(i * per - 1, 0), col_block)),
            pl.BlockSpec((tm, D_POOL), lambda i: (i, col_block)),
        ],
        out_specs=pl.BlockSpec((tm, D_POOL), lambda i: (i, 0)),
        out_shape=jax.ShapeDtypeStruct((rows, D_POOL), BF16),
        compiler_params=_params(("arbitrary",)),
        name="pool",
    )(src, src)


def _tail_body(m_ref, zp_ref, o_ref, zn_ref, ga_ref, gb_ref, x_ref, pw_ref, ps_ref, wup_ref, wun_ref, wo_ref,
               lg_ref, lb_ref, y_ref):
    m = m_ref[...]
    ys = [jnp.dot(m[:, g * POOL_GC:(g + 1) * POOL_GC], pw_ref[g], preferred_element_type=F32)
          for g in range(len(POOL_WINDOWS))]
    y_pool = jnp.concatenate(ys, axis=1) * ps_ref[...]
    a = jnp.dot((y_pool * zp_ref[...]).astype(BF16), wup_ref[...], preferred_element_type=F32)
    b = jnp.dot((o_ref[...] * zn_ref[...]).astype(BF16), wun_ref[...], preferred_element_type=F32)
    mix = ga_ref[...] * a + gb_ref[...] * b
    h = jnp.dot(mix.astype(BF16), wo_ref[...], preferred_element_type=F32)
    z = ALPHA * x_ref[...] + h
    mu = jnp.mean(z, axis=-1, keepdims=True)
    zc = z - mu
    var = jnp.mean(zc * zc, axis=-1, keepdims=True)
    y_ref[...] = zc * lax.rsqrt(var + LN_EPS) * lg_ref[...] + lb_ref[...]


def _tail(m, h_silu, h_sig, o, x, pw, ps, wup, wun, wo, lg, lb, tm):
    rows = x.shape[0]
    once = pl.Buffered(1)

    def const(shape):
        return pl.BlockSpec(shape, lambda i: (0,) * len(shape), pipeline_mode=once)

    return pl.pallas_call(
        _tail_body,
        grid=(rows // tm,),
        in_specs=[
            pl.BlockSpec((tm, D_POOL), lambda i: (i, 0)),
            pl.BlockSpec((tm, D_POOL), lambda i: (i, 0)),
            pl.BlockSpec((tm, D_ATTN), lambda i: (i, 0)),
            pl.BlockSpec((tm, D_ATTN), lambda i: (i, D_POOL // D_ATTN)),
            pl.BlockSpec((tm, D_MODEL), lambda i: (i, 0)),
            pl.BlockSpec((tm, D_MODEL), lambda i: (i, 1)),
            pl.BlockSpec((tm, D_MODEL), lambda i: (i, 0)),
            const((len(POOL_WINDOWS), POOL_GC, POOL_GC)),
            const((1, D_POOL)),
            const((D_POOL, D_MODEL)),
            const((D_ATTN, D_MODEL)),
            const((D_MODEL, D_MODEL)),
            const((1, D_MODEL)),
            const((1, D_MODEL)),
        ],
        out_specs=pl.BlockSpec((tm, D_MODEL), lambda i: (i, 0)),
        out_shape=jax.ShapeDtypeStruct((rows, D_MODEL), F32),
        compiler_params=_params(("arbitrary",)),
        name="tail",
    )(m, h_silu, o, h_silu, h_sig, h_sig, x, pw, ps, wup, wun, wo, lg, lb)


def _compress_body(pt_ref, c_hbm, pe_ref, w1_ref, w2_ref, o_ref, buf, lhs, res, sem, *, pps, n_chunks):
    b = pl.program_id(0)
    c = pl.program_id(1)
    step = b * n_chunks + c
    n_steps = pl.num_programs(0) * n_chunks
    nb = pps * BLOCKS_PER_PAGE

    def page_copy(seq_row, page0, slot, p):
        phys = pt_ref[seq_row, page0 + p]
        return pltpu.make_async_copy(c_hbm.at[pl.ds(phys * BLOCKS_PER_PAGE, BLOCKS_PER_PAGE)],
                                     buf.at[slot, pl.ds(BLOCKS_PER_PAGE * p, BLOCKS_PER_PAGE), pl.ds(0, CMP_ROWS), :],
                                     sem.at[slot])

    def start_step(s, slot):
        seq_row = s // n_chunks
        page0 = (s % n_chunks) * pps

        def one(p, carry):
            page_copy(seq_row, page0, slot, p).start()
            return carry
        lax.fori_loop(0, pps, one, 0)

    @pl.when(step == 0)
    def _():
        start_step(0, 0)

    slot = step % 2

    @pl.when(step + 1 < n_steps)
    def _():
        start_step(step + 1, 1 - slot)

    for p in range(pps):
        page_copy(b, c * pps, slot, p).wait()

    rows = buf.at[slot].reshape(nb * CMP_PITCH, LANES)
    for j in range(CMP_BLOCK):
        x0 = rows[pl.ds(2 * j, nb, stride=CMP_PITCH), :]
        x1 = rows[pl.ds(2 * j + 1, nb, stride=CMP_PITCH), :]
        pe = pe_ref[j:j + 1, :]
        lhs[:, (2 * j) * HEAD_DIM:(2 * j + 1) * HEAD_DIM] = (x0 + pe).astype(BF16)
        lhs[:, (2 * j + 1) * HEAD_DIM:(2 * j + 2) * HEAD_DIM] = (x1 + pe).astype(BF16)
    hid = jnp.dot(lhs[...], w1_ref[...], preferred_element_type=F32)
    hid = hid * _sigmoid(hid)
    out = jnp.dot(hid.astype(BF16), w2_ref[...], preferred_element_type=F32)
    half = nb // 2
    for h in range(N_KV_HEADS):
        hs = slice(h * HEAD_DIM, (h + 1) * HEAD_DIM)
        res[h] = out[:, hs]
        o_ref[0, 0:half, hs] = res[h, pl.ds(0, half, stride=2), :]
        o_ref[0, half:nb, hs] = res[h, pl.ds(1, half, stride=2), :]


def _compress(page_table, cache, pe, w1, w2, pps):
    nbatch, n_pages = page_table.shape
    n_chunks = n_pages // pps
    nb = pps * BLOCKS_PER_PAGE
    kdim = CMP_BLOCK * D_KV
    return pl.pallas_call(
        functools.partial(_compress_body, pps=pps, n_chunks=n_chunks),
        grid_spec=pltpu.PrefetchScalarGridSpec(
            num_scalar_prefetch=1,
            grid=(nbatch, n_chunks),
            in_specs=[
                pl.BlockSpec(memory_space=pl.ANY),
                pl.BlockSpec((CMP_BLOCK, HEAD_DIM), lambda b, c, pt: (0, 0)),
                pl.BlockSpec((kdim, D_KV), lambda b, c, pt: (0, 0), pipeline_mode=pl.Buffered(1)),
                pl.BlockSpec((D_KV, D_KV), lambda b, c, pt: (0, 0)),
            ],
            out_specs=pl.BlockSpec((1, nb, D_KV), lambda b, c, pt: (b, c, 0)),
            scratch_shapes=[
                pltpu.VMEM((2, nb, CMP_PITCH, LANES), F32),
                pltpu.VMEM((nb, kdim), BF16),
                pltpu.VMEM((N_KV_HEADS, nb, HEAD_DIM), F32),
                pltpu.SemaphoreType.DMA((2,)),
            ],
        ),
        out_shape=jax.ShapeDtypeStruct((nbatch, n_chunks * nb, D_KV), F32),
        compiler_params=_params(("arbitrary", "arbitrary")),
        name="compress",
    )(page_table, cache, pe, w1, w2)


def _kprep_body(ks_ref, vs_ref, kw_ref, vw_ref, kts_ref, vso_ref, ktw_ref, vwo_ref, *, tk):
    t = pl.program_id(1)
    kpos = t * tk + lax.broadcasted_iota(jnp.int32, (SLC_BLOCK, tk), 1)
    blk = lax.broadcasted_iota(jnp.int32, (SLC_BLOCK, tk), 0)
    onehot = jnp.where(kpos // SLC_BLOCK == blk, 1.0, 0.0).astype(BF16)
    for g in range(N_KV_HEADS):
        head = pl.ds(g, tk, stride=N_KV_HEADS)
        kts_ref[0, g, 0, 0:HEAD_DIM, :] = ks_ref[head, :].T.astype(BF16)
        kts_ref[0, g, 0, HEAD_DIM:HEAD_DIM + SLC_BLOCK, :] = onehot
        kts_ref[0, g, 0, HEAD_DIM + SLC_BLOCK:, :] = jnp.zeros((SLC_BLOCK, tk), BF16)
        kw = kw_ref[head, :]
        for c in range(tk // LANES):
            ktw_ref[0, g, c] = kw[c * LANES:(c + 1) * LANES, :].T.astype(BF16)
        ones = jnp.ones((tk, HEAD_DIM), BF16)
        vso_ref[0, g, :, 0:HEAD_DIM] = vs_ref[head, :].astype(BF16)
        vso_ref[0, g, :, HEAD_DIM:] = ones
        vwo_ref[0, g, :, 0:HEAD_DIM] = vw_ref[head, :].astype(BF16)
        vwo_ref[0, g, :, HEAD_DIM:] = ones


def _kprep(ks, vs, kw, vw, nbatch, seq, tk):
    nt = seq // tk
    src = pl.BlockSpec((tk * N_KV_HEADS, HEAD_DIM), lambda b, t: (b * nt + t, 0))
    return pl.pallas_call(
        functools.partial(_kprep_body, tk=tk),
        grid=(nbatch, nt),
        in_specs=[src, src, src, src],
        out_specs=[
            pl.BlockSpec((1, N_KV_HEADS, 1, 2 * HEAD_DIM, tk), lambda b, t: (b, 0, t, 0, 0)),
            pl.BlockSpec((1, N_KV_HEADS, tk, 2 * HEAD_DIM), lambda b, t: (b, 0, t, 0)),
            pl.BlockSpec((1, N_KV_HEADS, tk // LANES, HEAD_DIM, LANES), lambda b, t: (b, 0, t, 0, 0)),
            pl.BlockSpec((1, N_KV_HEADS, tk, 2 * HEAD_DIM), lambda b, t: (b, 0, t, 0)),
        ],
        out_shape=[
            jax.ShapeDtypeStruct((nbatch, N_KV_HEADS, nt, 2 * HEAD_DIM, tk), BF16),
            jax.ShapeDtypeStruct((nbatch, N_KV_HEADS, seq, 2 * HEAD_DIM), BF16),
            jax.ShapeDtypeStruct((nbatch, N_KV_HEADS, seq // LANES, HEAD_DIM, LANES), BF16),
            jax.ShapeDtypeStruct((nbatch, N_KV_HEADS, seq, 2 * HEAD_DIM), BF16),
        ],
        compiler_params=_params(("arbitrary", "arbitrary")),
        name="kprep",
    )(ks, vs, kw, vw)


def _masked_softmax(s, mask):
    s = jnp.where(mask, s, NEG)
    mx = jnp.max(s, axis=-1, keepdims=True)
    e = jnp.where(mask, jnp.exp2(s - mx), 0.0)
    l = jnp.sum(e, axis=-1, keepdims=True)
    return e / jnp.where(l > 0.0, l, 1.0)


def _select_bias(p_slc_t, qpos_row, n_slc):
    shape = p_slc_t.shape
    blk = lax.broadcasted_iota(jnp.int32, shape, 0)
    valid = blk * SLC_BLOCK <= qpos_row
    cur = qpos_row // SLC_BLOCK
    forced = (blk == 0) | (blk == cur) | (blk == cur - 1)
    score = jnp.where(valid & forced, FORCE, jnp.where(valid, p_slc_t, -FORCE))
    n_chunks = n_slc // SUBLANES
    chunks = [score[c * SUBLANES:(c + 1) * SUBLANES] for c in range(n_chunks)]
    ranks = [jnp.zeros((SUBLANES, shape[1]), F32) for _ in range(n_chunks)]
    sub = lax.broadcasted_iota(jnp.int32, (SUBLANES, shape[1]), 0)
    for j in range(n_slc):
        row = jnp.broadcast_to(score[j:j + 1, :], (SUBLANES, shape[1]))
        for c in range(n_chunks):
            lo = c * SUBLANES
            if lo > j:
                ahead = row >= chunks[c]
            elif lo + SUBLANES - 1 < j:
                ahead = row > chunks[c]
            else:
                ahead = (row > chunks[c]) | ((row == chunks[c]) & (sub + lo > j))
            ranks[c] = ranks[c] + jnp.where(ahead, 1.0, 0.0)
    rank = jnp.concatenate(ranks, axis=0)
    return jnp.where(rank < min(TOP_N, n_slc), 0.0, NEG)


def _attend_tile(q_ref, qcols, kt, v, width, mask_fn, s_ref, m_scr, acc_scr):
    rows = s_ref.shape[0]
    reps = width // LANES
    for blk in range(rows // ROW_BLOCK):
        rb = slice(blk * ROW_BLOCK, (blk + 1) * ROW_BLOCK)
        s_ref[rb, 0:width] = jnp.dot(q_ref[rb, 0:qcols], kt, preferred_element_type=F32)
    for blk in range(rows // ROW_BLOCK):
        alphas, ps = [], []
        for c in range(ROW_BLOCK // ROW_CHUNK):
            r0 = blk * ROW_BLOCK + c * ROW_CHUNK
            r = slice(r0, r0 + ROW_CHUNK)
            s = s_ref[r, 0:width]
            if mask_fn is not None:
                s = jnp.where(mask_fn(r0, ROW_CHUNK), s, NEG)
            m_old = m_scr[r, :]
            m_new = jnp.maximum(m_old, jnp.max(s, axis=-1, keepdims=True))
            ps.append(jnp.exp2(s - pltpu.repeat(m_new, reps, axis=1)).astype(BF16))
            alphas.append(jnp.exp2(m_old - m_new))
            m_scr[r, :] = m_new
        alpha = jnp.concatenate(alphas, axis=0)
        rb = slice(blk * ROW_BLOCK, (blk + 1) * ROW_BLOCK)
        acc_scr[rb, :] = (jnp.concatenate([alpha, alpha], axis=1) * acc_scr[rb, :]
                          + jnp.dot(jnp.concatenate(ps, axis=0), v, preferred_element_type=F32))


def _attn_body(q_ref, gn_ref, kc_ref, vc_ref, kts_ref, vs_ref, ktw_ref, vw_ref, o_ref,
               qaug, s_scr, m_scr, acc_scr, *, tq, tk, seq):
    q0 = pl.program_id(1) * tq
    n_cmp = seq // CMP_BLOCK
    n_slc = -(-seq // SLC_BLOCK)
    half = n_cmp // 2
    qpos = q0 + lax.broadcasted_iota(jnp.int32, (tq, 1), 0)
    qpos4 = jnp.concatenate([qpos] * GROUP, axis=0)
    qpos_row = q0 + lax.broadcasted_iota(jnp.int32, (n_slc, tq), 1)
    lane = lax.broadcasted_iota(jnp.int32, (1, n_cmp), 1)
    cmp_blk = 2 * (lane % half) + lane // half
    m_cmp = ((cmp_blk + 1) * CMP_BLOCK - 1) <= qpos4
    gates = gn_ref[...].astype(F32)
    w0 = pl.multiple_of(jnp.maximum(q0 - WINDOW, 0), LANES)
    last = (q0 + tq + tk - 1) // tk - 1

    def chunk_qpos(r0, n):
        return q0 + r0 % tq + lax.broadcasted_iota(jnp.int32, (n, 1), 0)

    def reset(g):
        m_scr[g] = jnp.full(m_scr.shape[1:], -jnp.inf, F32)
        acc_scr[g] = jnp.zeros(acc_scr.shape[1:], F32)

    def result(g):
        acc = acc_scr[g]
        return acc[:, 0:HEAD_DIM] / acc[:, HEAD_DIM:]

    def attend(g, qcols, kt, v, width, mask_fn):
        _attend_tile(qaug.at[g], qcols, kt, v, width, mask_fn, s_scr.at[g], m_scr.at[g], acc_scr.at[g])

    groups = range(N_KV_HEADS)

    o_cmp = []
    for g in groups:
        hs = slice(g * HEAD_DIM, (g + 1) * HEAD_DIM)
        for h in range(GROUP):
            qaug[g, h * tq:(h + 1) * tq, 0:HEAD_DIM] = (
                q_ref[:, (GROUP * g + h) * HEAD_DIM:(GROUP * g + h + 1) * HEAD_DIM])
        kc = kc_ref[0][:, hs].astype(BF16)
        vc = vc_ref[0][:, hs].astype(BF16)
        s = lax.dot_general(qaug[g, :, 0:HEAD_DIM], kc, (((1,), (1,)), ((), ())), preferred_element_type=F32)
        p = _masked_softmax(s, m_cmp)
        o_cmp.append(jnp.dot(p.astype(BF16), vc, preferred_element_type=F32))
        p_grp = p[0:tq]
        for h in range(1, GROUP):
            p_grp = p_grp + p[h * tq:(h + 1) * tq]
        p_slc = p_grp + pltpu.roll(p_grp, half, 1)
        bias_t = _select_bias(p_slc.T[0:n_slc], qpos_row, n_slc)
        bias = jnp.concatenate([bias_t, jnp.zeros((LANES - n_slc, tq), F32)], axis=0).T.astype(BF16)
        for h in range(GROUP):
            qaug[g, h * tq:(h + 1) * tq, HEAD_DIM:] = bias
        reset(g)

    def interior(t, carry):
        k0 = pl.multiple_of(t * tk, tk)
        for g in groups:
            attend(g, 2 * HEAD_DIM, kts_ref[0, g, t], vs_ref[0, g, pl.ds(k0, tk), :], tk, None)
        return carry

    lax.fori_loop(0, last, interior, 0)
    k_last = pl.multiple_of(last * tk, tk)

    def causal(r0, n):
        return k_last + lax.broadcasted_iota(jnp.int32, (1, tk), 1) <= chunk_qpos(r0, n)

    o_sel = []
    for g in groups:
        attend(g, 2 * HEAD_DIM, kts_ref[0, g, last], vs_ref[0, g, pl.ds(k_last, tk), :], tk, causal)
        o_sel.append(result(g))
        reset(g)

    for off, width in ((0, WINDOW), (WINDOW, tq)):
        def in_window(r0, n, off=off, width=width):
            dist = chunk_qpos(r0, n) - (w0 + off + lax.broadcasted_iota(jnp.int32, (1, width), 1))
            return (dist >= 0) & (dist <= WINDOW)

        for g in groups:
            ktw = jnp.concatenate([ktw_ref[0, g, (w0 + off) // LANES + c] for c in range(width // LANES)], axis=1)
            attend(g, HEAD_DIM, ktw, vw_ref[0, g, pl.ds(w0 + off, width), :], width, in_window)

    for g in groups:
        o_win = result(g)
        for h in range(GROUP):
            hd = GROUP * g + h
            r = slice(h * tq, (h + 1) * tq)
            o_ref[:, hd * HEAD_DIM:(hd + 1) * HEAD_DIM] = (
                gates[:, hd:hd + 1] * o_cmp[g][r]
                + gates[:, N_HEADS + hd:N_HEADS + hd + 1] * o_sel[g][r]
                + gates[:, 2 * N_HEADS + hd:2 * N_HEADS + hd + 1] * o_win[r]).astype(o_ref.dtype)


def _prompt_attn(h_q, h_sig, kc, vc, kts, vs, ktw, vw, nbatch, seq, tq, tk):
    nq = seq // tq
    n_cmp = seq // CMP_BLOCK
    nt = seq // tk
    rows = GROUP * tq
    return pl.pallas_call(
        functools.partial(_attn_body, tq=tq, tk=tk, seq=seq),
        grid=(nbatch, nq),
        in_specs=[
            pl.BlockSpec((tq, D_ATTN), lambda b, i: (b * nq + i, 0)),
            pl.BlockSpec((tq, LANES), lambda b, i: (b * nq + i, SIG_GN // LANES)),
            pl.BlockSpec((1, n_cmp, D_KV), lambda b, i: (b, 0, 0)),
            pl.BlockSpec((1, n_cmp, D_KV), lambda b, i: (b, 0, 0)),
            pl.BlockSpec((1, N_KV_HEADS, nt, 2 * HEAD_DIM, tk), lambda b, i: (b, 0, 0, 0, 0)),
            pl.BlockSpec((1, N_KV_HEADS, seq, 2 * HEAD_DIM), lambda b, i: (b, 0, 0, 0)),
            pl.BlockSpec((1, N_KV_HEADS, seq // LANES, HEAD_DIM, LANES), lambda b, i: (b, 0, 0, 0, 0)),
            pl.BlockSpec((1, N_KV_HEADS, seq, 2 * HEAD_DIM), lambda b, i: (b, 0, 0, 0)),
        ],
        out_specs=pl.BlockSpec((tq, D_ATTN), lambda b, i: (b * nq + i, 0)),
        scratch_shapes=[
            pltpu.VMEM((N_KV_HEADS, rows, 2 * HEAD_DIM), BF16),
            pltpu.VMEM((N_KV_HEADS, rows, tk), F32),
            pltpu.VMEM((N_KV_HEADS, rows, LANES), F32),
            pltpu.VMEM((N_KV_HEADS, rows, 2 * HEAD_DIM), F32),
        ],
        out_shape=jax.ShapeDtypeStruct((nbatch * seq, D_ATTN), BF16),
        compiler_params=_params(("arbitrary", "arbitrary")),
        name="attn",
    )(h_q, h_sig, kc, vc, kts, vs, ktw, vw)


def _group_queries(q_row, g):
    heads = [q_row[:, (GROUP * g + h) * HEAD_DIM:(GROUP * g + h + 1) * HEAD_DIM] for h in range(GROUP)]
    pad = jnp.zeros((2 * SUBLANES - GROUP, HEAD_DIM), F32)
    return jnp.concatenate(heads + [pad], axis=0).astype(BF16)


def _scmp_body(q_ref, kc_ref, vc_ref, ocmp_ref, pslc_ref, *, q_pos, chunk):
    b = pl.program_id(0)
    q_row = q_ref[pl.ds(b, 1), :]
    n_cmp = kc_ref.shape[1]
    half = chunk // 2
    lane = lax.broadcasted_iota(jnp.int32, (1, n_cmp), 1)
    within = lane % chunk
    cmp_blk = (lane // chunk) * chunk + 2 * (within % half) + within // half
    m_cmp = ((cmp_blk + 1) * CMP_BLOCK - 1) <= q_pos
    outs = []
    for g in range(N_KV_HEADS):
        hs = slice(g * HEAD_DIM, (g + 1) * HEAD_DIM)
        qg = _group_queries(q_row, g)
        s = lax.dot_general(qg, kc_ref[0][:, hs].astype(BF16), (((1,), (1,)), ((), ())), preferred_element_type=F32)
        p = _masked_softmax(s, m_cmp)
        o = jnp.dot(p.astype(BF16), vc_ref[0][:, hs].astype(BF16), preferred_element_type=F32)
        outs.append(o[0:GROUP])
        p_grp = p[0:1]
        for h in range(1, GROUP):
            p_grp = p_grp + p[h:h + 1]
        parts = []
        for c in range(n_cmp // chunk):
            pc = p_grp[:, c * chunk:(c + 1) * chunk]
            parts.append(pc + pltpu.roll(pc, half, 1))
        pslc_ref[0, g:g + 1, :] = jnp.concatenate(parts, axis=1)
    ocmp_ref[0] = jnp.concatenate(outs, axis=0)


def _sample_cmp(q, kc, vc, q_pos, chunk):
    nbatch, n_cmp, _ = kc.shape
    return pl.pallas_call(
        functools.partial(_scmp_body, q_pos=q_pos, chunk=chunk),
        grid=(nbatch,),
        in_specs=[
            pl.BlockSpec((nbatch, D_ATTN), lambda b: (0, 0)),
            pl.BlockSpec((1, n_cmp, D_KV), lambda b: (b, 0, 0)),
            pl.BlockSpec((1, n_cmp, D_KV), lambda b: (b, 0, 0)),
        ],
        out_specs=[
            pl.BlockSpec((1, N_HEADS, HEAD_DIM), lambda b: (b, 0, 0)),
            pl.BlockSpec((1, N_KV_HEADS, n_cmp), lambda b: (b, 0, 0)),
        ],
        out_shape=[
            jax.ShapeDtypeStruct((nbatch, N_HEADS, HEAD_DIM), F32),
            jax.ShapeDtypeStruct((nbatch, N_KV_HEADS, n_cmp), F32),
        ],
        compiler_params=_params(("arbitrary",)),
        name="s_cmp",
    )(q, kc, vc)


def _stopk_body(p_ref, idx_ref, *, q_pos, chunk, n_slc):
    p = p_ref[...]
    rows, width = p.shape
    half = chunk // 2
    n_in = (width // chunk) * half
    lane = lax.broadcasted_iota(jnp.int32, (1, width), 1)
    within = lane % chunk
    blk = jnp.where(within < half, (lane // chunk) * half + within, -1)
    extra = (lane >= half) & (lane < half + (n_slc - n_in))
    blk = jnp.where(extra, n_in + lane - half, blk)
    real = blk >= 0
    valid = real & (blk * SLC_BLOCK <= q_pos)
    cur = q_pos // SLC_BLOCK
    forced = (blk == 0) | (blk == cur) | (blk == cur - 1)
    base = jnp.where(extra, 0.0, p)
    x = jnp.where(valid & forced, FORCE, jnp.where(valid, base, -FORCE))
    x = jnp.where(real, x, -jnp.inf)
    blk_f = blk.astype(F32)
    out_lane = lax.broadcasted_iota(jnp.int32, (rows, LANES), 1)
    out = jnp.zeros((rows, LANES), F32)
    for r in range(min(TOP_N, n_slc)):
        mx = jnp.max(x, axis=-1, keepdims=True)
        pick = jnp.min(jnp.where(x == mx, blk_f, float(2 ** 30)), axis=-1, keepdims=True)
        out = jnp.where(out_lane == r, pick, out)
        x = jnp.where(blk_f == pick, -jnp.inf, x)
    idx_ref[...] = out.astype(jnp.int32)


def _sample_topk(pslc, q_pos, chunk, n_slc):
    rows, width = pslc.shape
    return pl.pallas_call(
        functools.partial(_stopk_body, q_pos=q_pos, chunk=chunk, n_slc=n_slc),
        grid=(1,),
        in_specs=[pl.BlockSpec((rows, width), lambda i: (0, 0))],
        out_specs=pl.BlockSpec((rows, LANES), lambda i: (0, 0)),
        out_shape=jax.ShapeDtypeStruct((rows, LANES), jnp.int32),
        compiler_params=_params(("arbitrary",)),
        name="s_topk",
    )(pslc)


def _sattn_body(idx_ref, pt_ref, q_ref, gn_ref, ocmp_ref, knew_ref, vnew_ref, kwn_ref, vwn_ref, wk_ref, wv_ref,
                sk_hbm, sv_hbm, o_ref, kbuf, vbuf, sem, *, q_pos, n_top, n_pages, past):
    b = pl.program_id(0)
    nbatch = pl.num_programs(0)
    half_rows = SLC_BLOCK * N_KV_HEADS
    tail_blk = past // SLC_BLOCK

    def copies(bb, slot, g, i):
        blk = idx_ref[bb * N_KV_HEADS + g, i]
        page = jnp.minimum(blk // 2, n_pages - 1)
        start = pl.multiple_of(pt_ref[bb, page] * (PAGE_SIZE * N_KV_HEADS) + (blk % 2) * half_rows, half_rows)
        return [pltpu.make_async_copy(hbm.at[pl.ds(start, half_rows), :], dst.at[slot, g, i], sem.at[t, slot, g, i])
                for t, (hbm, dst) in enumerate(((sk_hbm, kbuf), (sv_hbm, vbuf)))]

    def start_all(bb, slot):
        for g in range(N_KV_HEADS):
            for i in range(n_top):
                for cp in copies(bb, slot, g, i):
                    cp.start()

    @pl.when(b == 0)
    def _():
        start_all(0, 0)

    @pl.when(b + 1 < nbatch)
    def _():
        start_all(b + 1, (b + 1) % 2)

    slot = b % 2
    for g in range(N_KV_HEADS):
        for i in range(n_top):
            for cp in copies(b, slot, g, i):
                cp.wait()

    q_row = q_ref[pl.ds(b, 1), :]
    gates = gn_ref[pl.ds(b, 1), :]
    pad = jnp.zeros((2 * SUBLANES - 1, HEAD_DIM), F32)

    def new_row(ref, g):
        return jnp.concatenate([ref[pl.ds(b * N_KV_HEADS + g, 1), :], pad], axis=0).astype(BF16)

    outs = []
    for g in range(N_KV_HEADS):
        qg = _group_queries(q_row, g)
        ks, vs = [], []
        key = lax.broadcasted_iota(jnp.int32, (1, n_top * SLC_BLOCK), 1)
        key_blk = jnp.zeros((1, n_top * SLC_BLOCK), jnp.int32)
        for i in range(n_top):
            blk = idx_ref[b * N_KV_HEADS + g, i]
            ks.append(kbuf[slot, g, i, pl.ds(g, SLC_BLOCK, stride=N_KV_HEADS), :])
            vs.append(vbuf[slot, g, i, pl.ds(g, SLC_BLOCK, stride=N_KV_HEADS), :])
            key_blk = jnp.where(key // SLC_BLOCK == i, blk, key_blk)
        k_sel = jnp.concatenate(ks, axis=0).astype(BF16)
        v_sel = jnp.concatenate(vs, axis=0).astype(BF16)
        m_sel = (key_blk * SLC_BLOCK + key % SLC_BLOCK <= q_pos) & (key_blk < tail_blk)
        tail_sel = jnp.max(jnp.where(key_blk == tail_blk, 1.0, 0.0), axis=-1, keepdims=True) > 0.5
        first = lax.broadcasted_iota(jnp.int32, (1, 2 * SUBLANES), 1) == 0
        o_sel = _two_part_attention(qg, k_sel, v_sel, m_sel, new_row(knew_ref, g), new_row(vnew_ref, g),
                                    first & tail_sel)
        wbuf = wk_ref.shape[1] // N_KV_HEADS
        k_win = wk_ref[0, pl.ds(g, wbuf, stride=N_KV_HEADS), :].astype(BF16)
        v_win = wv_ref[0, pl.ds(g, wbuf, stride=N_KV_HEADS), :].astype(BF16)
        dist = q_pos - (past - wbuf + lax.broadcasted_iota(jnp.int32, (1, wbuf), 1))
        m_win = (dist >= 0) & (dist <= WINDOW)
        o_win = _two_part_attention(qg, k_win, v_win, m_win, new_row(kwn_ref, g), new_row(vwn_ref, g), first)
        o_cmp = ocmp_ref[0, GROUP * g:GROUP * (g + 1), :]
        for h in range(GROUP):
            hd = GROUP * g + h
            outs.append(gates[:, hd:hd + 1] * o_cmp[h:h + 1]
                        + gates[:, N_HEADS + hd:N_HEADS + hd + 1] * o_sel[h:h + 1]
                        + gates[:, 2 * N_HEADS + hd:2 * N_HEADS + hd + 1] * o_win[h:h + 1])
    o_ref[0] = jnp.concatenate(outs, axis=0)


def _two_part_attention(q, k1, v1, m1, k2, v2, m2):
    dn = (((1,), (1,)), ((), ()))
    s1 = jnp.where(m1, lax.dot_general(q, k1, dn, preferred_element_type=F32), NEG)
    s2 = jnp.where(m2, lax.dot_general(q, k2, dn, preferred_element_type=F32), NEG)
    mx = jnp.maximum(jnp.max(s1, axis=-1, keepdims=True), jnp.max(s2, axis=-1, keepdims=True))
    e1 = jnp.where(m1, jnp.exp2(s1 - mx), 0.0)
    e2 = jnp.where(m2, jnp.exp2(s2 - mx), 0.0)
    l = jnp.sum(e1, axis=-1, keepdims=True) + jnp.sum(e2, axis=-1, keepdims=True)
    inv = 1.0 / jnp.where(l > 0.0, l, 1.0)
    o = (jnp.dot((e1 * inv).astype(BF16), v1, preferred_element_type=F32)
         + jnp.dot((e2 * inv).astype(BF16), v2, preferred_element_type=F32))
    return o


def _sample_attn(idx, page_table, h_q, h_sig, ocmp, new_rows, win_k, win_v, slc_k, slc_v, q_pos, past):
    nbatch, n_pages = page_table.shape
    n_top = min(TOP_N, past // SLC_BLOCK + 1)
    half_rows = SLC_BLOCK * N_KV_HEADS
    flat_new = pl.BlockSpec((nbatch * N_KV_HEADS, HEAD_DIM), lambda b, idx, pt: (0, 0))

    def whole(col_block, width):
        return pl.BlockSpec((nbatch, width), lambda b, idx, pt: (0, col_block))

    return pl.pallas_call(
        functools.partial(_sattn_body, q_pos=q_pos, n_top=n_top, n_pages=n_pages, past=past),
        grid_spec=pltpu.PrefetchScalarGridSpec(
            num_scalar_prefetch=2,
            grid=(nbatch,),
            in_specs=[
                whole(0, D_ATTN),
                whole(SIG_GN // LANES, LANES),
                pl.BlockSpec((1, N_HEADS, HEAD_DIM), lambda b, idx, pt: (b, 0, 0)),
                flat_new, flat_new, flat_new, flat_new,
                pl.BlockSpec((1,) + win_k.shape[1:], lambda b, idx, pt: (b, 0, 0)),
                pl.BlockSpec((1,) + win_v.shape[1:], lambda b, idx, pt: (b, 0, 0)),
                pl.BlockSpec(memory_space=pl.ANY),
                pl.BlockSpec(memory_space=pl.ANY),
            ],
            out_specs=pl.BlockSpec((1, N_HEADS, HEAD_DIM), lambda b, idx, pt: (b, 0, 0)),
            scratch_shapes=[
                pltpu.VMEM((2, N_KV_HEADS, n_top, half_rows, LANES), F32),
                pltpu.VMEM((2, N_KV_HEADS, n_top, half_rows, LANES), F32),
                pltpu.SemaphoreType.DMA((2, 2, N_KV_HEADS, n_top)),
            ],
        ),
        out_shape=jax.ShapeDtypeStruct((nbatch, N_HEADS, HEAD_DIM), F32),
        compiler_params=_params(("arbitrary",)),
        name="s_attn",
    )(idx, page_table, h_q, h_sig, ocmp, *new_rows, win_k, win_v, slc_k, slc_v)


REF_KV = 3 * D_POOL
REF_GN = REF_KV + N_KV_PROJ * D_KV
REF_ZN = REF_GN + N_GATE
REF_GM = REF_ZN + D_ATTN


def _tiles(start, width):
    return tuple(range(start, start + width, PROJ_TN))


STARTS_U = _tiles(0, D_POOL)
STARTS_SILU = _tiles(D_POOL, D_POOL) + _tiles(REF_ZN, D_ATTN)
STARTS_Q = _tiles(2 * D_POOL, D_ATTN)
STARTS_SIG = _tiles(REF_GM, N_MERGE_COLS) + (REF_GN,)


def _block_diag2(w):
    z = jnp.zeros_like(w)
    return jnp.concatenate([jnp.concatenate([w, z], axis=-1), jnp.concatenate([z, w], axis=-1)], axis=-2)


def _project(x, wt, b, tm, tm_kv, act_dtype):
    u = _proj(x, wt, b, STARTS_U, ACT_NONE, tm, F32)
    h_silu = _proj(x, wt, b, STARTS_SILU, ACT_SILU, tm, act_dtype)
    h_q = _proj(x, wt, b, STARTS_Q, ACT_SCALE, tm, act_dtype)
    h_sig = _proj(x, wt, b, STARTS_SIG, ACT_SIGMOID, tm, act_dtype)
    kv = _proj_kv(x, wt, b, REF_KV, tm_kv)
    return u, h_silu, h_q, h_sig, kv


def kernel(x_prompt, x_sample, cache_cmp_k, cache_cmp_v, cache_slc_k, cache_slc_v, cache_win_k, cache_win_v,
           state_pool, page_table, w_in, b_in, pool_w, pool_scale, cmp_pe_k, cmp_w1_k, cmp_w2_k, cmp_pe_v, cmp_w1_v,
           cmp_w2_v, w_up_pool, w_up_nsa, w_out, ln_g, ln_b):
    nb_p, seq, _ = x_prompt.shape
    nb_s = x_sample.shape[0]
    n_pages = page_table.shape[1]
    past = n_pages * PAGE_SIZE
    n_phys = cache_cmp_k.shape[1]
    wbuf = cache_win_k.shape[2]

    wt = w_in[0].T
    b = b_in[0][None, :]
    kdim = CMP_BLOCK * D_KV
    cmp_k = (cmp_pe_k[0], _block_diag2(cmp_w1_k[0]).reshape(kdim, D_KV).astype(BF16),
             _block_diag2(cmp_w2_k[0]).astype(BF16))
    cmp_v = (cmp_pe_v[0], _block_diag2(cmp_w1_v[0]).reshape(kdim, D_KV).astype(BF16),
             _block_diag2(cmp_w2_v[0]).astype(BF16))
    pw = pool_w[0].astype(BF16)
    ps = pool_scale[0][None, :]
    wup = w_up_pool[0].astype(BF16)
    wun = w_up_nsa[0].astype(BF16)
    wo = w_out[0].astype(BF16)
    lg = ln_g[0][None, :]
    lb = ln_b[0][None, :]

    xp = x_prompt.reshape(nb_p * seq, D_MODEL)
    xs = x_sample.reshape(nb_s, D_MODEL)
    u_p, hsilu_p, hq_p, hsig_p, kv_p = _project(xp.astype(BF16), wt, b, tm=2048, tm_kv=1024, act_dtype=BF16)
    u_s, hsilu_s, hq_s, hsig_s, kv_s = _project(xs.astype(BF16), wt, b, tm=nb_s, tm_kv=nb_s, act_dtype=F32)

    blocks = (-1, CMP_ROWS, LANES)
    prompt_pages = seq // PAGE_SIZE
    ident = jnp.arange(nb_p * prompt_pages, dtype=jnp.int32).reshape(nb_p, prompt_pages)
    kc_p = _compress(ident, kv_p[0].reshape(blocks), *cmp_k, pps=prompt_pages)
    vc_p = _compress(ident, kv_p[1].reshape(blocks), *cmp_v, pps=prompt_pages)
    kts, vs, ktw, vw = _kprep(kv_p[2], kv_p[3], kv_p[4], kv_p[5], nb_p, seq, tk=512)
    o_p = _prompt_attn(hq_p, hsig_p, kc_p, vc_p, kts, vs, ktw, vw, nb_p, seq, tq=128, tk=512)
    m_p = _pool_m(u_p, 0, nb_p * seq, tm=512, seq=seq, fixed_pos=None)
    y_p = _tail(m_p, hsilu_p, hsig_p, o_p, xp, pw, ps, wup, wun, wo, lg, lb, tm=256)

    chunk_pages = 64
    kc_s = _compress(page_table, cache_cmp_k.reshape(blocks), *cmp_k, pps=chunk_pages)
    vc_s = _compress(page_table, cache_cmp_v.reshape(blocks), *cmp_v, pps=chunk_pages)
    chunk = chunk_pages * BLOCKS_PER_PAGE
    n_slc = past // SLC_BLOCK + 1
    ocmp_s, pslc_s = _sample_cmp(hq_s, kc_s, vc_s, past, chunk)
    idx = _sample_topk(pslc_s.reshape(nb_s * N_KV_HEADS, -1), past, chunk, n_slc)
    flat = (n_phys * PAGE_SIZE * N_KV_HEADS, HEAD_DIM)
    o_s = _sample_attn(idx, page_table, hq_s, hsig_s, ocmp_s, kv_s[2:6],
                       cache_win_k.reshape(nb_s, wbuf * N_KV_HEADS, HEAD_DIM),
                       cache_win_v.reshape(nb_s, wbuf * N_KV_HEADS, HEAD_DIM),
                       cache_slc_k.reshape(flat), cache_slc_v.reshape(flat), past, past)
    ctx = jnp.concatenate([state_pool[0], u_s[:, None, :]], axis=1)
    m_s = _pool_m(ctx.reshape(nb_s * (POOL_CTX + 1), D_POOL), 0, nb_s * (POOL_CTX + 1), tm=nb_s * (POOL_CTX + 1),
                  seq=nb_s * (POOL_CTX + 1), fixed_pos=past)
    m_s = m_s.reshape(nb_s, POOL_CTX + 1, D_POOL)[:, POOL_CTX]
    y_s = _tail(m_s, hsilu_s, hsig_s, o_s.reshape(nb_s, D_ATTN), xs, pw, ps, wup, wun, wo, lg, lb, tm=nb_s)

    wl = min(WINDOW, seq)
    kv_p = [a.reshape(1, nb_p, seq, N_KV_HEADS, HEAD_DIM) for a in kv_p]
    kv_s = [a.reshape(1, nb_s, 1, N_KV_HEADS, HEAD_DIM) for a in kv_s]
    return (
        y_p.reshape(nb_p, seq, D_MODEL),
        y_s.reshape(nb_s, 1, D_MODEL),
        kv_p[0], kv_p[1], kv_p[2], kv_p[3],
        kv_p[4][:, :, seq - wl:], kv_p[5][:, :, seq - wl:],
        u_p.reshape(nb_p, seq, D_POOL)[None, :, seq - POOL_CTX:],
        kv_s[0], kv_s[1], kv_s[2], kv_s[3],
        jnp.concatenate([cache_win_k, kv_s[4]], axis=2)[:, :, 1:],
        jnp.concatenate([cache_win_v, kv_s[5]], axis=2)[:, :, 1:],
        ctx[None, :, 1:],
    )
```

```python
import functools

import jax
import jax.numpy as jnp
import numpy as np
from jax import lax
from jax.experimental import pallas as pl
from jax.experimental.pallas import tpu as pltpu

D_MODEL = 2048
D_POOL = 1024
POOL_WINDOWS = (2, 4, 8, 16)
POOL_GC = D_POOL // len(POOL_WINDOWS)
POOL_CTX = max(POOL_WINDOWS) - 1
HEAD_DIM = 128
N_HEADS = 8
N_KV_HEADS = 2
GROUP = N_HEADS // N_KV_HEADS
D_ATTN = N_HEADS * HEAD_DIM
D_KV = N_KV_HEADS * HEAD_DIM
CMP_BLOCK = 32
SLC_BLOCK = 64
TOP_N = 16
WINDOW = 512
PAGE_SIZE = 128
ATTN_SCALE = HEAD_DIM ** -0.5
DEPTH = 1
ALPHA = (2.0 * DEPTH) ** 0.25
LN_EPS = 1e-5
NEG = -1e30
FORCE = 1e6

F32 = jnp.float32
BF16 = jnp.bfloat16

SUBLANES = 8
LANES = 128
VMEM_LIMIT_BYTES = 56 * 1024 * 1024

PROJ_TN = 512
PROJ_RC = 512
N_MERGE_COLS = 2 * D_MODEL
SIG_GN = N_MERGE_COLS
N_GATE = 3 * N_HEADS
N_KV_PROJ = 6
ACT_NONE, ACT_SILU, ACT_SIGMOID, ACT_SCALE = range(4)
LOG2E = 1.4426950408889634
ROW_BLOCK = 128
ROW_CHUNK = 32

CMP_ROWS = CMP_BLOCK * N_KV_HEADS
CMP_PITCH = 72
BLOCKS_PER_PAGE = PAGE_SIZE // CMP_BLOCK
CMP_SLOTS = 3


def _params(sem):
    return pltpu.CompilerParams(dimension_semantics=sem, vmem_limit_bytes=VMEM_LIMIT_BYTES)


def _sigmoid(x):
    return 1.0 / (1.0 + jnp.exp(-x))


def _activate(acc, act):
    if act == ACT_SILU:
        return acc * _sigmoid(acc)
    if act == ACT_SIGMOID:
        return _sigmoid(acc)
    if act == ACT_SCALE:
        return acc * (ATTN_SCALE * LOG2E)
    return acc


def _xwt(x, wt, b):
    return lax.dot_general(x, wt, (((1,), (1,)), ((), ())), preferred_element_type=F32) + b


def _wt_spec(tn):
    return pl.BlockSpec((pl.Element(tn), pl.Element(D_MODEL)), lambda i, j, starts8: (starts8[j] * SUBLANES, 0))


def _proj_body(starts8_ref, x_ref, wt_ref, b_ref, o_ref, *, act, rc):
    wt = wt_ref[...].astype(BF16)
    for c in range(x_ref.shape[0] // rc):
        r = slice(c * rc, (c + 1) * rc)
        o_ref[r, :] = _activate(_xwt(x_ref[r, :], wt, b_ref[...]), act).astype(o_ref.dtype)


def _proj(x, wt, b, starts, act, tm, out_dtype):
    m = x.shape[0]
    starts8 = jnp.asarray([s // SUBLANES for s in starts], jnp.int32)
    b = jnp.concatenate([b[:, s:s + PROJ_TN] for s in starts], axis=1)
    return pl.pallas_call(
        functools.partial(_proj_body, act=act, rc=min(tm, PROJ_RC)),
        grid_spec=pltpu.PrefetchScalarGridSpec(
            num_scalar_prefetch=1,
            grid=(m // tm, len(starts)),
            in_specs=[
                pl.BlockSpec((tm, D_MODEL), lambda i, j, starts8: (i, 0)),
                _wt_spec(PROJ_TN),
                pl.BlockSpec((1, PROJ_TN), lambda i, j, starts8: (0, j)),
            ],
            out_specs=pl.BlockSpec((tm, PROJ_TN), lambda i, j, starts8: (i, j)),
        ),
        out_shape=jax.ShapeDtypeStruct((m, len(starts) * PROJ_TN), out_dtype),
        compiler_params=_params(("arbitrary", "arbitrary")),
        name="proj",
    )(starts8, x, wt, b)


def _proj_kv_body(starts8_ref, x_ref, wt_ref, b_ref, *o_refs, rc):
    j = pl.program_id(1)
    wt = wt_ref[...].astype(BF16)
    for k, o_ref in enumerate(o_refs):
        @pl.when(j == k)
        def _(o_ref=o_ref):
            for c in range(x_ref.shape[0] // rc):
                acc = _xwt(x_ref[c * rc:(c + 1) * rc, :], wt, b_ref[...])
                for h in range(N_KV_HEADS):
                    o_ref[pl.ds(N_KV_HEADS * c * rc + h, rc, stride=N_KV_HEADS), :] = (
                        acc[:, h * HEAD_DIM:(h + 1) * HEAD_DIM])


def _proj_kv(x, wt, b, start, tm):
    m = x.shape[0]
    flat = jax.ShapeDtypeStruct((m * N_KV_HEADS, HEAD_DIM), F32)
    starts8 = jnp.asarray([(start + k * D_KV) // SUBLANES for k in range(N_KV_PROJ)], jnp.int32)
    b = b[:, start:start + N_KV_PROJ * D_KV]
    return pl.pallas_call(
        functools.partial(_proj_kv_body, rc=min(tm, PROJ_RC)),
        grid_spec=pltpu.PrefetchScalarGridSpec(
            num_scalar_prefetch=1,
            grid=(m // tm, N_KV_PROJ),
            in_specs=[
                pl.BlockSpec((tm, D_MODEL), lambda i, j, starts8: (i, 0)),
                _wt_spec(D_KV),
                pl.BlockSpec((1, D_KV), lambda i, j, starts8: (0, j)),
            ],
            out_specs=[pl.BlockSpec((tm * N_KV_HEADS, HEAD_DIM), lambda i, j, starts8: (i, 0))] * N_KV_PROJ,
        ),
        out_shape=[flat] * N_KV_PROJ,
        compiler_params=_params(("arbitrary", "arbitrary")),
        name="proj_kv",
    )(starts8, x, wt, b)


def _window_means(halo, u, row0, fixed_pos):
    tm = u.shape[0]
    ext = jnp.concatenate([halo, u], axis=0)
    if fixed_pos is None:
        pos = row0 + lax.broadcasted_iota(jnp.int32, (tm, 1), 0)
    else:
        pos = jnp.full((tm, 1), fixed_pos, jnp.int32)
    outs = []
    for g, w in enumerate(POOL_WINDOWS):
        a = ext[:, g * POOL_GC:(g + 1) * POOL_GC]
        s = a
        k = 1
        while k < w:
            s = s + pltpu.roll(s, k, 0)
            k *= 2
        cnt = jnp.minimum(pos + 1, w).astype(F32)
        outs.append((s[2 * SUBLANES:] / cnt - a[2 * SUBLANES:]).astype(BF16))
    return jnp.concatenate(outs, axis=1)


def _pool_body(halo_ref, u_ref, m_ref, *, tm, seq, fixed_pos):
    row0 = (pl.program_id(0) * tm) % seq
    halo = jnp.where(row0 == 0, 0.0, halo_ref[...])
    m_ref[...] = _window_means(halo, u_ref[...], row0, fixed_pos)


def _pool_m(src, col_block, rows, tm, seq, fixed_pos):
    halo_rows = 2 * SUBLANES
    per = tm // halo_rows
    return pl.pallas_call(
        functools.partial(_pool_body, tm=tm, seq=seq, fixed_pos=fixed_pos),
        grid=(rows // tm,),
        in_specs=[
            pl.BlockSpec((halo_rows, D_POOL), lambda i: (jnp.maximum(i * per - 1, 0), col_block)),
            pl.BlockSpec((tm, D_POOL), lambda i: (i, col_block)),
        ],
        out_specs=pl.BlockSpec((tm, D_POOL), lambda i: (i, 0)),
        out_shape=jax.ShapeDtypeStruct((rows, D_POOL), BF16),
        compiler_params=_params(("arbitrary",)),
        name="pool",
    )(src, src)


def _tail_body(*refs, seq):
    if seq is None:
        m = refs[0][...]
        refs = refs[1:]
    else:
        halo_ref, u_ref = refs[:2]
        refs = refs[2:]
        row0 = (pl.program_id(0) * u_ref.shape[0]) % seq
        m = _window_means(jnp.where(row0 == 0, 0.0, halo_ref[...]), u_ref[...], row0, None)
    zp_ref, o_ref, zn_ref, ga_ref, gb_ref, x_ref, pw_ref, ps_ref, wup_ref, wun_ref, wo_ref, lg_ref, lb_ref, y_ref = refs
    ys = [jnp.dot(m[:, g * POOL_GC:(g + 1) * POOL_GC], pw_ref[g], preferred_element_type=F32)
          for g in range(len(POOL_WINDOWS))]
    y_pool = jnp.concatenate(ys, axis=1) * ps_ref[...]
    a = jnp.dot((y_pool * zp_ref[...]).astype(BF16), wup_ref[...], preferred_element_type=F32)
    b = jnp.dot((o_ref[...] * zn_ref[...]).astype(BF16), wun_ref[...], preferred_element_type=F32)
    mix = ga_ref[...] * a + gb_ref[...] * b
    h = jnp.dot(mix.astype(BF16), wo_ref[...], preferred_element_type=F32)
    z = ALPHA * x_ref[...] + h
    mu = jnp.mean(z, axis=-1, keepdims=True)
    zc = z - mu
    var = jnp.mean(zc * zc, axis=-1, keepdims=True)
    y_ref[...] = zc * lax.rsqrt(var + LN_EPS) * lg_ref[...] + lb_ref[...]


def _tail(pool_in, h_silu, h_sig, o, x, pw, ps, wup, wun, wo, lg, lb, tm, seq=None):
    rows = x.shape[0]
    once = pl.Buffered(1)

    def const(shape):
        return pl.BlockSpec(shape, lambda i: (0,) * len(shape), pipeline_mode=once)

    pool_specs = [pl.BlockSpec((tm, D_POOL), lambda i: (i, 0))]
    pool_args = [pool_in]
    if seq is not None:
        halo_rows = 2 * SUBLANES
        per = tm // halo_rows
        pool_specs.insert(0, pl.BlockSpec((halo_rows, D_POOL), lambda i: (jnp.maximum(i * per - 1, 0), 0)))
        pool_args.insert(0, pool_in)
    return pl.pallas_call(
        functools.partial(_tail_body, seq=seq),
        grid=(rows // tm,),
        in_specs=pool_specs + [
            pl.BlockSpec((tm, D_POOL), lambda i: (i, 0)),
            pl.BlockSpec((tm, D_ATTN), lambda i: (i, 0)),
            pl.BlockSpec((tm, D_ATTN), lambda i: (i, D_POOL // D_ATTN)),
            pl.BlockSpec((tm, D_MODEL), lambda i: (i, 0)),
            pl.BlockSpec((tm, D_MODEL), lambda i: (i, 1)),
            pl.BlockSpec((tm, D_MODEL), lambda i: (i, 0)),
            const((len(POOL_WINDOWS), POOL_GC, POOL_GC)),
            const((1, D_POOL)),
            const((D_POOL, D_MODEL)),
            const((D_ATTN, D_MODEL)),
            const((D_MODEL, D_MODEL)),
            const((1, D_MODEL)),
            const((1, D_MODEL)),
        ],
        out_specs=pl.BlockSpec((tm, D_MODEL), lambda i: (i, 0)),
        out_shape=jax.ShapeDtypeStruct((rows, D_MODEL), F32),
        compiler_params=_params(("arbitrary",)),
        name="tail",
    )(*pool_args, h_silu, o, h_silu, h_sig, h_sig, x, pw, ps, wup, wun, wo, lg, lb)


def _compress_body(pt_ref, c_hbm, pe_ref, w1_ref, w2_ref, o_ref, buf, lhs, res, sem, *, pps, n_chunks):
    b = pl.program_id(0)
    c = pl.program_id(1)
    step = b * n_chunks + c
    n_steps = pl.num_programs(0) * n_chunks
    nb = pps * BLOCKS_PER_PAGE

    def page_copy(seq_row, page0, slot, p):
        phys = pt_ref[seq_row, page0 + p]
        return pltpu.make_async_copy(c_hbm.at[pl.ds(phys * BLOCKS_PER_PAGE, BLOCKS_PER_PAGE)],
                                     buf.at[slot, pl.ds(BLOCKS_PER_PAGE * p, BLOCKS_PER_PAGE), pl.ds(0, CMP_ROWS), :],
                                     sem.at[slot])

    def start_step(s, slot):
        seq_row = s // n_chunks
        page0 = (s % n_chunks) * pps

        def one(p, carry):
            page_copy(seq_row, page0, slot, p).start()
            return carry
        lax.fori_loop(0, pps, one, 0)

    @pl.when(step == 0)
    def _():
        for s in range(CMP_SLOTS - 1):
            @pl.when(s < n_steps)
            def _(s=s):
                start_step(s, s)

    ahead = step + (CMP_SLOTS - 1)

    @pl.when(ahead < n_steps)
    def _():
        start_step(ahead, ahead % CMP_SLOTS)

    slot = step % CMP_SLOTS
    for p in range(pps):
        page_copy(b, c * pps, slot, p).wait()

    rows = buf.at[slot].reshape(nb * CMP_PITCH, LANES)
    for j in range(CMP_BLOCK):
        x0 = rows[pl.ds(2 * j, nb, stride=CMP_PITCH), :]
        x1 = rows[pl.ds(2 * j + 1, nb, stride=CMP_PITCH), :]
        pe = pe_ref[j:j + 1, :]
        lhs[:, (2 * j) * HEAD_DIM:(2 * j + 1) * HEAD_DIM] = (x0 + pe).astype(BF16)
        lhs[:, (2 * j + 1) * HEAD_DIM:(2 * j + 2) * HEAD_DIM] = (x1 + pe).astype(BF16)
    hid = jnp.dot(lhs[...], w1_ref[...], preferred_element_type=F32)
    hid = hid * _sigmoid(hid)
    out = jnp.dot(hid.astype(BF16), w2_ref[...], preferred_element_type=F32)
    half = nb // 2
    for h in range(N_KV_HEADS):
        hs = slice(h * HEAD_DIM, (h + 1) * HEAD_DIM)
        res[h] = out[:, hs]
        o_ref[0, 0:half, hs] = res[h, pl.ds(0, half, stride=2), :]
        o_ref[0, half:nb, hs] = res[h, pl.ds(1, half, stride=2), :]


def _compress(page_table, cache, pe, w1, w2, pps):
    nbatch, n_pages = page_table.shape
    n_chunks = n_pages // pps
    nb = pps * BLOCKS_PER_PAGE
    kdim = CMP_BLOCK * D_KV
    return pl.pallas_call(
        functools.partial(_compress_body, pps=pps, n_chunks=n_chunks),
        grid_spec=pltpu.PrefetchScalarGridSpec(
            num_scalar_prefetch=1,
            grid=(nbatch, n_chunks),
            in_specs=[
                pl.BlockSpec(memory_space=pl.ANY),
                pl.BlockSpec((CMP_BLOCK, HEAD_DIM), lambda b, c, pt: (0, 0)),
                pl.BlockSpec((kdim, D_KV), lambda b, c, pt: (0, 0), pipeline_mode=pl.Buffered(1)),
                pl.BlockSpec((D_KV, D_KV), lambda b, c, pt: (0, 0)),
            ],
            out_specs=pl.BlockSpec((1, nb, D_KV), lambda b, c, pt: (b, c, 0)),
            scratch_shapes=[
                pltpu.VMEM((CMP_SLOTS, nb, CMP_PITCH, LANES), F32),
                pltpu.VMEM((nb, kdim), BF16),
                pltpu.VMEM((N_KV_HEADS, nb, HEAD_DIM), F32),
                pltpu.SemaphoreType.DMA((CMP_SLOTS,)),
            ],
        ),
        out_shape=jax.ShapeDtypeStruct((nbatch, n_chunks * nb, D_KV), F32),
        compiler_params=_params(("arbitrary", "arbitrary")),
        name="compress",
    )(page_table, cache, pe, w1, w2)


def _kprep_body(ks_ref, vs_ref, kw_ref, vw_ref, kts_ref, vso_ref, ktw_ref, vwo_ref, *, tk):
    t = pl.program_id(1)
    kpos = t * tk + lax.broadcasted_iota(jnp.int32, (SLC_BLOCK, tk), 1)
    blk = lax.broadcasted_iota(jnp.int32, (SLC_BLOCK, tk), 0)
    onehot = jnp.where(kpos // SLC_BLOCK == blk, 1.0, 0.0).astype(BF16)
    for g in range(N_KV_HEADS):
        head = pl.ds(g, tk, stride=N_KV_HEADS)
        kts_ref[0, g, 0, 0:HEAD_DIM, :] = ks_ref[head, :].T.astype(BF16)
        kts_ref[0, g, 0, HEAD_DIM:HEAD_DIM + SLC_BLOCK, :] = onehot
        kts_ref[0, g, 0, HEAD_DIM + SLC_BLOCK:, :] = jnp.zeros((SLC_BLOCK, tk), BF16)
        kw = kw_ref[head, :]
        for c in range(tk // LANES):
            ktw_ref[0, g, c] = kw[c * LANES:(c + 1) * LANES, :].T.astype(BF16)
        ones = jnp.ones((tk, HEAD_DIM), BF16)
        vso_ref[0, g, :, 0:HEAD_DIM] = vs_ref[head, :].astype(BF16)
        vso_ref[0, g, :, HEAD_DIM:] = ones
        vwo_ref[0, g, :, 0:HEAD_DIM] = vw_ref[head, :].astype(BF16)
        vwo_ref[0, g, :, HEAD_DIM:] = ones


def _kprep(ks, vs, kw, vw, nbatch, seq, tk):
    nt = seq // tk
    src = pl.BlockSpec((tk * N_KV_HEADS, HEAD_DIM), lambda b, t: (b * nt + t, 0))
    return pl.pallas_call(
        functools.partial(_kprep_body, tk=tk),
        grid=(nbatch, nt),
        in_specs=[src, src, src, src],
        out_specs=[
            pl.BlockSpec((1, N_KV_HEADS, 1, 2 * HEAD_DIM, tk), lambda b, t: (b, 0, t, 0, 0)),
            pl.BlockSpec((1, N_KV_HEADS, tk, 2 * HEAD_DIM), lambda b, t: (b, 0, t, 0)),
            pl.BlockSpec((1, N_KV_HEADS, tk // LANES, HEAD_DIM, LANES), lambda b, t: (b, 0, t, 0, 0)),
            pl.BlockSpec((1, N_KV_HEADS, tk, 2 * HEAD_DIM), lambda b, t: (b, 0, t, 0)),
        ],
        out_shape=[
            jax.ShapeDtypeStruct((nbatch, N_KV_HEADS, nt, 2 * HEAD_DIM, tk), BF16),
            jax.ShapeDtypeStruct((nbatch, N_KV_HEADS, seq, 2 * HEAD_DIM), BF16),
            jax.ShapeDtypeStruct((nbatch, N_KV_HEADS, seq // LANES, HEAD_DIM, LANES), BF16),
            jax.ShapeDtypeStruct((nbatch, N_KV_HEADS, seq, 2 * HEAD_DIM), BF16),
        ],
        compiler_params=_params(("arbitrary", "arbitrary")),
        name="kprep",
    )(ks, vs, kw, vw)


def _masked_softmax(s, mask):
    s = jnp.where(mask, s, NEG)
    mx = jnp.max(s, axis=-1, keepdims=True)
    e = jnp.where(mask, jnp.exp2(s - mx), 0.0)
    l = jnp.sum(e, axis=-1, keepdims=True)
    return e / jnp.where(l > 0.0, l, 1.0)


def _select_bias(p_slc_t, qpos_row, n_slc):
    shape = p_slc_t.shape
    blk = lax.broadcasted_iota(jnp.int32, shape, 0)
    valid = blk * SLC_BLOCK <= qpos_row
    cur = qpos_row // SLC_BLOCK
    forced = (blk == 0) | (blk == cur) | (blk == cur - 1)
    score = jnp.where(valid & forced, FORCE, jnp.where(valid, p_slc_t, -FORCE))
    n_chunks = n_slc // SUBLANES
    chunks = [score[c * SUBLANES:(c + 1) * SUBLANES] for c in range(n_chunks)]
    ranks = [jnp.zeros((SUBLANES, shape[1]), F32) for _ in range(n_chunks)]
    sub = lax.broadcasted_iota(jnp.int32, (SUBLANES, shape[1]), 0)
    for j in range(n_slc):
        row = jnp.broadcast_to(score[j:j + 1, :], (SUBLANES, shape[1]))
        for c in range(n_chunks):
            lo = c * SUBLANES
            if lo > j:
                ahead = row >= chunks[c]
            elif lo + SUBLANES - 1 < j:
                ahead = row > chunks[c]
            else:
                ahead = (row > chunks[c]) | ((row == chunks[c]) & (sub + lo > j))
            ranks[c] = ranks[c] + jnp.where(ahead, 1.0, 0.0)
    rank = jnp.concatenate(ranks, axis=0)
    return jnp.where(rank < min(TOP_N, n_slc), 0.0, NEG)


def _attend_tile(q_ref, qcols, kt, v, width, mask_fn, s_ref, m_scr, acc_scr):
    rows = s_ref.shape[0]
    reps = width // LANES
    for blk in range(rows // ROW_BLOCK):
        rb = slice(blk * ROW_BLOCK, (blk + 1) * ROW_BLOCK)
        s_ref[rb, 0:width] = jnp.dot(q_ref[rb, 0:qcols], kt, preferred_element_type=F32)
    for blk in range(rows // ROW_BLOCK):
        alphas, ps = [], []
        for c in range(ROW_BLOCK // ROW_CHUNK):
            r0 = blk * ROW_BLOCK + c * ROW_CHUNK
            r = slice(r0, r0 + ROW_CHUNK)
            s = s_ref[r, 0:width]
            if mask_fn is not None:
                s = jnp.where(mask_fn(r0, ROW_CHUNK), s, NEG)
            m_old = m_scr[r, :]
            m_new = jnp.maximum(m_old, jnp.max(s, axis=-1, keepdims=True))
            ps.append(jnp.exp2(s - jnp.concatenate([m_new] * reps, axis=1)).astype(BF16))
            alphas.append(jnp.exp2(m_old - m_new))
            m_scr[r, :] = m_new
        alpha = jnp.concatenate(alphas, axis=0)
        rb = slice(blk * ROW_BLOCK, (blk + 1) * ROW_BLOCK)
        acc_scr[rb, :] = (jnp.concatenate([alpha, alpha], axis=1) * acc_scr[rb, :]
                          + jnp.dot(jnp.concatenate(ps, axis=0), v, preferred_element_type=F32))


def _attn_body(q_ref, gn_ref, kc_ref, vc_ref, kts_ref, vs_ref, ktw_ref, vw_ref, o_ref,
               qaug, s_scr, m_scr, acc_scr, *, tq, tk, seq):
    q0 = pl.program_id(1) * tq
    n_cmp = seq // CMP_BLOCK
    n_slc = -(-seq // SLC_BLOCK)
    half = n_cmp // 2
    qpos = q0 + lax.broadcasted_iota(jnp.int32, (tq, 1), 0)
    qpos4 = jnp.concatenate([qpos] * GROUP, axis=0)
    qpos_row = q0 + lax.broadcasted_iota(jnp.int32, (n_slc, tq), 1)
    lane = lax.broadcasted_iota(jnp.int32, (1, n_cmp), 1)
    cmp_blk = 2 * (lane % half) + lane // half
    m_cmp = ((cmp_blk + 1) * CMP_BLOCK - 1) <= qpos4
    gates = gn_ref[...].astype(F32)
    w0 = pl.multiple_of(jnp.maximum(q0 - WINDOW, 0), LANES)
    last = (q0 + tq + tk - 1) // tk - 1

    def chunk_qpos(r0, n):
        return q0 + r0 % tq + lax.broadcasted_iota(jnp.int32, (n, 1), 0)

    def reset(g):
        m_scr[g] = jnp.full(m_scr.shape[1:], -jnp.inf, F32)
        acc_scr[g] = jnp.zeros(acc_scr.shape[1:], F32)

    def result(g):
        acc = acc_scr[g]
        return acc[:, 0:HEAD_DIM] / acc[:, HEAD_DIM:]

    def attend(g, qcols, kt, v, width, mask_fn):
        _attend_tile(qaug.at[g], qcols, kt, v, width, mask_fn, s_scr.at[g], m_scr.at[g], acc_scr.at[g])

    groups = range(N_KV_HEADS)

    o_cmp = []
    for g in groups:
        hs = slice(g * HEAD_DIM, (g + 1) * HEAD_DIM)
        for h in range(GROUP):
            qaug[g, h * tq:(h + 1) * tq, 0:HEAD_DIM] = (
                q_ref[:, (GROUP * g + h) * HEAD_DIM:(GROUP * g + h + 1) * HEAD_DIM])
        kc = kc_ref[0][:, hs].astype(BF16)
        vc = vc_ref[0][:, hs].astype(BF16)
        s = lax.dot_general(qaug[g, :, 0:HEAD_DIM], kc, (((1,), (1,)), ((), ())), preferred_element_type=F32)
        p = _masked_softmax(s, m_cmp)
        o_cmp.append(jnp.dot(p.astype(BF16), vc, preferred_element_type=F32))
        p_grp = p[0:tq]
        for h in range(1, GROUP):
            p_grp = p_grp + p[h * tq:(h + 1) * tq]
        p_slc = p_grp + pltpu.roll(p_grp, half, 1)
        bias_t = _select_bias(p_slc.T[0:n_slc], qpos_row, n_slc)
        bias = jnp.concatenate([bias_t, jnp.zeros((LANES - n_slc, tq), F32)], axis=0).T.astype(BF16)
        for h in range(GROUP):
            qaug[g, h * tq:(h + 1) * tq, HEAD_DIM:] = bias
        reset(g)

    def interior(t, carry):
        k0 = pl.multiple_of(t * tk, tk)
        for g in groups:
            attend(g, 2 * HEAD_DIM, kts_ref[0, g, t], vs_ref[0, g, pl.ds(k0, tk), :], tk, None)
        return carry

    lax.fori_loop(0, last, interior, 0)
    k_last = pl.multiple_of(last * tk, tk)

    def causal(r0, n):
        return k_last + lax.broadcasted_iota(jnp.int32, (1, tk), 1) <= chunk_qpos(r0, n)

    o_sel = []
    for g in groups:
        attend(g, 2 * HEAD_DIM, kts_ref[0, g, last], vs_ref[0, g, pl.ds(k_last, tk), :], tk, causal)
        o_sel.append(result(g))
        reset(g)

    span = WINDOW + tq

    def in_window(r0, n):
        dist = chunk_qpos(r0, n) - (w0 + lax.broadcasted_iota(jnp.int32, (1, span), 1))
        return (dist >= 0) & (dist <= WINDOW)

    for g in groups:
        ktw = jnp.concatenate([ktw_ref[0, g, w0 // LANES + c] for c in range(span // LANES)], axis=1)
        attend(g, HEAD_DIM, ktw, vw_ref[0, g, pl.ds(w0, span), :], span, in_window)

    for g in groups:
        o_win = result(g)
        for h in range(GROUP):
            hd = GROUP * g + h
            r = slice(h * tq, (h + 1) * tq)
            o_ref[:, hd * HEAD_DIM:(hd + 1) * HEAD_DIM] = (
                gates[:, hd:hd + 1] * o_cmp[g][r]
                + gates[:, N_HEADS + hd:N_HEADS + hd + 1] * o_sel[g][r]
                + gates[:, 2 * N_HEADS + hd:2 * N_HEADS + hd + 1] * o_win[r]).astype(o_ref.dtype)


def _prompt_attn(h_q, h_sig, kc, vc, kts, vs, ktw, vw, nbatch, seq, tq, tk):
    nq = seq // tq
    n_cmp = seq // CMP_BLOCK
    nt = seq // tk
    rows = GROUP * tq
    return pl.pallas_call(
        functools.partial(_attn_body, tq=tq, tk=tk, seq=seq),
        grid=(nbatch, nq),
        in_specs=[
            pl.BlockSpec((tq, D_ATTN), lambda b, i: (b * nq + i, 0)),
            pl.BlockSpec((tq, LANES), lambda b, i: (b * nq + i, SIG_GN // LANES)),
            pl.BlockSpec((1, n_cmp, D_KV), lambda b, i: (b, 0, 0)),
            pl.BlockSpec((1, n_cmp, D_KV), lambda b, i: (b, 0, 0)),
            pl.BlockSpec((1, N_KV_HEADS, nt, 2 * HEAD_DIM, tk), lambda b, i: (b, 0, 0, 0, 0)),
            pl.BlockSpec((1, N_KV_HEADS, seq, 2 * HEAD_DIM), lambda b, i: (b, 0, 0, 0)),
            pl.BlockSpec((1, N_KV_HEADS, seq // LANES, HEAD_DIM, LANES), lambda b, i: (b, 0, 0, 0, 0)),
            pl.BlockSpec((1, N_KV_HEADS, seq, 2 * HEAD_DIM), lambda b, i: (b, 0, 0, 0)),
        ],
        out_specs=pl.BlockSpec((tq, D_ATTN), lambda b, i: (b * nq + i, 0)),
        scratch_shapes=[
            pltpu.VMEM((N_KV_HEADS, rows, 2 * HEAD_DIM), BF16),
            pltpu.VMEM((N_KV_HEADS, rows, max(tk, WINDOW + tq)), F32),
            pltpu.VMEM((N_KV_HEADS, rows, LANES), F32),
            pltpu.VMEM((N_KV_HEADS, rows, 2 * HEAD_DIM), F32),
        ],
        out_shape=jax.ShapeDtypeStruct((nbatch * seq, D_ATTN), BF16),
        compiler_params=_params(("arbitrary", "arbitrary")),
        name="attn",
    )(h_q, h_sig, kc, vc, kts, vs, ktw, vw)


def _group_queries(q_row, g):
    heads = [q_row[:, (GROUP * g + h) * HEAD_DIM:(GROUP * g + h + 1) * HEAD_DIM] for h in range(GROUP)]
    pad = jnp.zeros((2 * SUBLANES - GROUP, HEAD_DIM), F32)
    return jnp.concatenate(heads + [pad], axis=0).astype(BF16)


def _scmp_body(q_ref, kc_ref, vc_ref, ocmp_ref, pslc_ref, *, q_pos, chunk):
    b = pl.program_id(0)
    q_row = q_ref[pl.ds(b, 1), :]
    n_cmp = kc_ref.shape[1]
    half = chunk // 2
    lane = lax.broadcasted_iota(jnp.int32, (1, n_cmp), 1)
    within = lane % chunk
    cmp_blk = (lane // chunk) * chunk + 2 * (within % half) + within // half
    m_cmp = ((cmp_blk + 1) * CMP_BLOCK - 1) <= q_pos
    outs = []
    for g in range(N_KV_HEADS):
        hs = slice(g * HEAD_DIM, (g + 1) * HEAD_DIM)
        qg = _group_queries(q_row, g)
        s = lax.dot_general(qg, kc_ref[0][:, hs].astype(BF16), (((1,), (1,)), ((), ())), preferred_element_type=F32)
        p = _masked_softmax(s, m_cmp)
        o = jnp.dot(p.astype(BF16), vc_ref[0][:, hs].astype(BF16), preferred_element_type=F32)
        outs.append(o[0:GROUP])
        p_grp = p[0:1]
        for h in range(1, GROUP):
            p_grp = p_grp + p[h:h + 1]
        parts = []
        for c in range(n_cmp // chunk):
            pc = p_grp[:, c * chunk:(c + 1) * chunk]
            parts.append(pc + pltpu.roll(pc, half, 1))
        pslc_ref[0, g:g + 1, :] = jnp.concatenate(parts, axis=1)
    ocmp_ref[0] = jnp.concatenate(outs, axis=0)


def _sample_cmp(q, kc, vc, q_pos, chunk):
    nbatch, n_cmp, _ = kc.shape
    return pl.pallas_call(
        functools.partial(_scmp_body, q_pos=q_pos, chunk=chunk),
        grid=(nbatch,),
        in_specs=[
            pl.BlockSpec((nbatch, D_ATTN), lambda b: (0, 0)),
            pl.BlockSpec((1, n_cmp, D_KV), lambda b: (b, 0, 0)),
            pl.BlockSpec((1, n_cmp, D_KV), lambda b: (b, 0, 0)),
        ],
        out_specs=[
            pl.BlockSpec((1, N_HEADS, HEAD_DIM), lambda b: (b, 0, 0)),
            pl.BlockSpec((1, N_KV_HEADS, n_cmp), lambda b: (b, 0, 0)),
        ],
        out_shape=[
            jax.ShapeDtypeStruct((nbatch, N_HEADS, HEAD_DIM), F32),
            jax.ShapeDtypeStruct((nbatch, N_KV_HEADS, n_cmp), F32),
        ],
        compiler_params=_params(("arbitrary",)),
        name="s_cmp",
    )(q, kc, vc)


def _stopk_body(p_ref, idx_ref, *, q_pos, chunk, n_slc):
    p = p_ref[...]
    rows, width = p.shape
    half = chunk // 2
    n_in = (width // chunk) * half
    lane = lax.broadcasted_iota(jnp.int32, (1, width), 1)
    within = lane % chunk
    blk = jnp.where(within < half, (lane // chunk) * half + within, -1)
    extra = (lane >= half) & (lane < half + (n_slc - n_in))
    blk = jnp.where(extra, n_in + lane - half, blk)
    real = blk >= 0
    valid = real & (blk * SLC_BLOCK <= q_pos)
    cur = q_pos // SLC_BLOCK
    forced = (blk == 0) | (blk == cur) | (blk == cur - 1)
    base = jnp.where(extra, 0.0, p)
    x = jnp.where(valid & forced, FORCE, jnp.where(valid, base, -FORCE))
    x = jnp.where(real, x, -jnp.inf)
    blk_f = blk.astype(F32)
    out_lane = lax.broadcasted_iota(jnp.int32, (rows, LANES), 1)
    out = jnp.zeros((rows, LANES), F32)
    for r in range(min(TOP_N, n_slc)):
        mx = jnp.max(x, axis=-1, keepdims=True)
        pick = jnp.min(jnp.where(x == mx, blk_f, float(2 ** 30)), axis=-1, keepdims=True)
        out = jnp.where(out_lane == r, pick, out)
        x = jnp.where(blk_f == pick, -jnp.inf, x)
    idx_ref[...] = out.astype(jnp.int32)


def _sample_topk(pslc, q_pos, chunk, n_slc):
    rows, width = pslc.shape
    return pl.pallas_call(
        functools.partial(_stopk_body, q_pos=q_pos, chunk=chunk, n_slc=n_slc),
        grid=(1,),
        in_specs=[pl.BlockSpec((rows, width), lambda i: (0, 0))],
        out_specs=pl.BlockSpec((rows, LANES), lambda i: (0, 0)),
        out_shape=jax.ShapeDtypeStruct((rows, LANES), jnp.int32),
        compiler_params=_params(("arbitrary",)),
        name="s_topk",
    )(pslc)


def _sattn_body(idx_ref, pt_ref, q_ref, gn_ref, ocmp_ref, knew_ref, vnew_ref, kwn_ref, vwn_ref, wk_ref, wv_ref,
                sk_hbm, sv_hbm, o_ref, kbuf, vbuf, sem, *, q_pos, n_top, n_pages, past):
    b = pl.program_id(0)
    nbatch = pl.num_programs(0)
    half_rows = SLC_BLOCK * N_KV_HEADS
    tail_blk = past // SLC_BLOCK

    def copies(bb, slot, g, i):
        blk = idx_ref[bb * N_KV_HEADS + g, i]
        page = jnp.minimum(blk // 2, n_pages - 1)
        start = pl.multiple_of(pt_ref[bb, page] * (PAGE_SIZE * N_KV_HEADS) + (blk % 2) * half_rows, half_rows)
        return [pltpu.make_async_copy(hbm.at[pl.ds(start, half_rows), :], dst.at[slot, g, i], sem.at[t, slot, g, i])
                for t, (hbm, dst) in enumerate(((sk_hbm, kbuf), (sv_hbm, vbuf)))]

    def start_all(bb, slot):
        for g in range(N_KV_HEADS):
            for i in range(n_top):
                for cp in copies(bb, slot, g, i):
                    cp.start()

    @pl.when(b == 0)
    def _():
        start_all(0, 0)

    @pl.when(b + 1 < nbatch)
    def _():
        start_all(b + 1, (b + 1) % 2)

    slot = b % 2
    for g in range(N_KV_HEADS):
        for i in range(n_top):
            for cp in copies(b, slot, g, i):
                cp.wait()

    q_row = q_ref[pl.ds(b, 1), :]
    gates = gn_ref[pl.ds(b, 1), :]
    pad = jnp.zeros((2 * SUBLANES - 1, HEAD_DIM), F32)

    def new_row(ref, g):
        return jnp.concatenate([ref[pl.ds(b * N_KV_HEADS + g, 1), :], pad], axis=0).astype(BF16)

    outs = []
    for g in range(N_KV_HEADS):
        qg = _group_queries(q_row, g)
        ks, vs = [], []
        key = lax.broadcasted_iota(jnp.int32, (1, n_top * SLC_BLOCK), 1)
        key_blk = jnp.zeros((1, n_top * SLC_BLOCK), jnp.int32)
        for i in range(n_top):
            blk = idx_ref[b * N_KV_HEADS + g, i]
            ks.append(kbuf[slot, g, i, pl.ds(g, SLC_BLOCK, stride=N_KV_HEADS), :])
            vs.append(vbuf[slot, g, i, pl.ds(g, SLC_BLOCK, stride=N_KV_HEADS), :])
            key_blk = jnp.where(key // SLC_BLOCK == i, blk, key_blk)
        k_sel = jnp.concatenate(ks, axis=0).astype(BF16)
        v_sel = jnp.concatenate(vs, axis=0).astype(BF16)
        m_sel = (key_blk * SLC_BLOCK + key % SLC_BLOCK <= q_pos) & (key_blk < tail_blk)
        tail_sel = jnp.max(jnp.where(key_blk == tail_blk, 1.0, 0.0), axis=-1, keepdims=True) > 0.5
        first = lax.broadcasted_iota(jnp.int32, (1, 2 * SUBLANES), 1) == 0
        o_sel = _two_part_attention(qg, k_sel, v_sel, m_sel, new_row(knew_ref, g), new_row(vnew_ref, g),
                                    first & tail_sel)
        wbuf = wk_ref.shape[1] // N_KV_HEADS
        k_win = wk_ref[0, pl.ds(g, wbuf, stride=N_KV_HEADS), :].astype(BF16)
        v_win = wv_ref[0, pl.ds(g, wbuf, stride=N_KV_HEADS), :].astype(BF16)
        dist = q_pos - (past - wbuf + lax.broadcasted_iota(jnp.int32, (1, wbuf), 1))
        m_win = (dist >= 0) & (dist <= WINDOW)
        o_win = _two_part_attention(qg, k_win, v_win, m_win, new_row(kwn_ref, g), new_row(vwn_ref, g), first)
        o_cmp = ocmp_ref[0, GROUP * g:GROUP * (g + 1), :]
        for h in range(GROUP):
            hd = GROUP * g + h
            outs.append(gates[:, hd:hd + 1] * o_cmp[h:h + 1]
                        + gates[:, N_HEADS + hd:N_HEADS + hd + 1] * o_sel[h:h + 1]
                        + gates[:, 2 * N_HEADS + hd:2 * N_HEADS + hd + 1] * o_win[h:h + 1])
    o_ref[0] = jnp.concatenate(outs, axis=0)


def _two_part_attention(q, k1, v1, m1, k2, v2, m2):
    dn = (((1,), (1,)), ((), ()))
    s1 = jnp.where(m1, lax.dot_general(q, k1, dn, preferred_element_type=F32), NEG)
    s2 = jnp.where(m2, lax.dot_general(q, k2, dn, preferred_element_type=F32), NEG)
    mx = jnp.maximum(jnp.max(s1, axis=-1, keepdims=True), jnp.max(s2, axis=-1, keepdims=True))
    e1 = jnp.where(m1, jnp.exp2(s1 - mx), 0.0)
    e2 = jnp.where(m2, jnp.exp2(s2 - mx), 0.0)
    l = jnp.sum(e1, axis=-1, keepdims=True) + jnp.sum(e2, axis=-1, keepdims=True)
    inv = 1.0 / jnp.where(l > 0.0, l, 1.0)
    o = (jnp.dot((e1 * inv).astype(BF16), v1, preferred_element_type=F32)
         + jnp.dot((e2 * inv).astype(BF16), v2, preferred_element_type=F32))
    return o


def _sample_attn(idx, page_table, h_q, h_sig, ocmp, new_rows, win_k, win_v, slc_k, slc_v, q_pos, past):
    nbatch, n_pages = page_table.shape
    n_top = min(TOP_N, past // SLC_BLOCK + 1)
    half_rows = SLC_BLOCK * N_KV_HEADS
    flat_new = pl.BlockSpec((nbatch * N_KV_HEADS, HEAD_DIM), lambda b, idx, pt: (0, 0))

    def whole(col_block, width):
        return pl.BlockSpec((nbatch, width), lambda b, idx, pt: (0, col_block))

    return pl.pallas_call(
        functools.partial(_sattn_body, q_pos=q_pos, n_top=n_top, n_pages=n_pages, past=past),
        grid_spec=pltpu.PrefetchScalarGridSpec(
            num_scalar_prefetch=2,
            grid=(nbatch,),
            in_specs=[
                whole(0, D_ATTN),
                whole(SIG_GN // LANES, LANES),
                pl.BlockSpec((1, N_HEADS, HEAD_DIM), lambda b, idx, pt: (b, 0, 0)),
                flat_new, flat_new, flat_new, flat_new,
                pl.BlockSpec((1,) + win_k.shape[1:], lambda b, idx, pt: (b, 0, 0)),
                pl.BlockSpec((1,) + win_v.shape[1:], lambda b, idx, pt: (b, 0, 0)),
                pl.BlockSpec(memory_space=pl.ANY),
                pl.BlockSpec(memory_space=pl.ANY),
            ],
            out_specs=pl.BlockSpec((1, N_HEADS, HEAD_DIM), lambda b, idx, pt: (b, 0, 0)),
            scratch_shapes=[
                pltpu.VMEM((2, N_KV_HEADS, n_top, half_rows, LANES), F32),
                pltpu.VMEM((2, N_KV_HEADS, n_top, half_rows, LANES), F32),
                pltpu.SemaphoreType.DMA((2, 2, N_KV_HEADS, n_top)),
            ],
        ),
        out_shape=jax.ShapeDtypeStruct((nbatch, N_HEADS, HEAD_DIM), F32),
        compiler_params=_params(("arbitrary",)),
        name="s_attn",
    )(idx, page_table, h_q, h_sig, ocmp, *new_rows, win_k, win_v, slc_k, slc_v)


REF_KV = 3 * D_POOL
REF_GN = REF_KV + N_KV_PROJ * D_KV
REF_ZN = REF_GN + N_GATE
REF_GM = REF_ZN + D_ATTN


def _tiles(start, width):
    return tuple(range(start, start + width, PROJ_TN))


STARTS_U = _tiles(0, D_POOL)
STARTS_SILU = _tiles(D_POOL, D_POOL) + _tiles(REF_ZN, D_ATTN)
STARTS_Q = _tiles(2 * D_POOL, D_ATTN)
STARTS_SIG = _tiles(REF_GM, N_MERGE_COLS) + (REF_GN,)


def _block_diag2(w):
    z = jnp.zeros_like(w)
    return jnp.concatenate([jnp.concatenate([w, z], axis=-1), jnp.concatenate([z, w], axis=-1)], axis=-2)


def _project(x, wt, b, tm, tm_kv, act_dtype):
    u = _proj(x, wt, b, STARTS_U, ACT_NONE, tm, F32)
    h_silu = _proj(x, wt, b, STARTS_SILU, ACT_SILU, tm, act_dtype)
    h_q = _proj(x, wt, b, STARTS_Q, ACT_SCALE, tm, act_dtype)
    h_sig = _proj(x, wt, b, STARTS_SIG, ACT_SIGMOID, tm, act_dtype)
    kv = _proj_kv(x, wt, b, REF_KV, tm_kv)
    return u, h_silu, h_q, h_sig, kv


def kernel(x_prompt, x_sample, cache_cmp_k, cache_cmp_v, cache_slc_k, cache_slc_v, cache_win_k, cache_win_v,
           state_pool, page_table, w_in, b_in, pool_w, pool_scale, cmp_pe_k, cmp_w1_k, cmp_w2_k, cmp_pe_v, cmp_w1_v,
           cmp_w2_v, w_up_pool, w_up_nsa, w_out, ln_g, ln_b):
    nb_p, seq, _ = x_prompt.shape
    nb_s = x_sample.shape[0]
    n_pages = page_table.shape[1]
    past = n_pages * PAGE_SIZE
    n_phys = cache_cmp_k.shape[1]
    wbuf = cache_win_k.shape[2]

    wt = w_in[0].T
    b = b_in[0][None, :]
    kdim = CMP_BLOCK * D_KV
    cmp_k = (cmp_pe_k[0], _block_diag2(cmp_w1_k[0]).reshape(kdim, D_KV).astype(BF16),
             _block_diag2(cmp_w2_k[0]).astype(BF16))
    cmp_v = (cmp_pe_v[0], _block_diag2(cmp_w1_v[0]).reshape(kdim, D_KV).astype(BF16),
             _block_diag2(cmp_w2_v[0]).astype(BF16))
    pw = pool_w[0].astype(BF16)
    ps = pool_scale[0][None, :]
    wup = w_up_pool[0].astype(BF16)
    wun = w_up_nsa[0].astype(BF16)
    wo = w_out[0].astype(BF16)
    lg = ln_g[0][None, :]
    lb = ln_b[0][None, :]

    xp = x_prompt.reshape(nb_p * seq, D_MODEL)
    xs = x_sample.reshape(nb_s, D_MODEL)
    u_p, hsilu_p, hq_p, hsig_p, kv_p = _project(xp.astype(BF16), wt, b, tm=2048, tm_kv=1024, act_dtype=BF16)
    u_s, hsilu_s, hq_s, hsig_s, kv_s = _project(xs.astype(BF16), wt, b, tm=nb_s, tm_kv=nb_s, act_dtype=F32)

    blocks = (-1, CMP_ROWS, LANES)
    prompt_pages = seq // PAGE_SIZE
    ident = jnp.arange(nb_p * prompt_pages, dtype=jnp.int32).reshape(nb_p, prompt_pages)
    kc_p = _compress(ident, kv_p[0].reshape(blocks), *cmp_k, pps=prompt_pages)
    vc_p = _compress(ident, kv_p[1].reshape(blocks), *cmp_v, pps=prompt_pages)
    kts, vs, ktw, vw = _kprep(kv_p[2], kv_p[3], kv_p[4], kv_p[5], nb_p, seq, tk=512)
    o_p = _prompt_attn(hq_p, hsig_p, kc_p, vc_p, kts, vs, ktw, vw, nb_p, seq, tq=128, tk=512)
    y_p = _tail(u_p, hsilu_p, hsig_p, o_p, xp, pw, ps, wup, wun, wo, lg, lb, tm=256, seq=seq)

    chunk_pages = 64
    kc_s = _compress(page_table, cache_cmp_k.reshape(blocks), *cmp_k, pps=chunk_pages)
    vc_s = _compress(page_table, cache_cmp_v.reshape(blocks), *cmp_v, pps=chunk_pages)
    chunk = chunk_pages * BLOCKS_PER_PAGE
    n_slc = past // SLC_BLOCK + 1
    ocmp_s, pslc_s = _sample_cmp(hq_s, kc_s, vc_s, past, chunk)
    idx = _sample_topk(pslc_s.reshape(nb_s * N_KV_HEADS, -1), past, chunk, n_slc)
    flat = (n_phys * PAGE_SIZE * N_KV_HEADS, HEAD_DIM)
    o_s = _sample_attn(idx, page_table, hq_s, hsig_s, ocmp_s, kv_s[2:6],
                       cache_win_k.reshape(nb_s, wbuf * N_KV_HEADS, HEAD_DIM),
                       cache_win_v.reshape(nb_s, wbuf * N_KV_HEADS, HEAD_DIM),
                       cache_slc_k.reshape(flat), cache_slc_v.reshape(flat), past, past)
    ctx = jnp.concatenate([state_pool[0], u_s[:, None, :]], axis=1)
    m_s = _pool_m(ctx.reshape(nb_s * (POOL_CTX + 1), D_POOL), 0, nb_s * (POOL_CTX + 1), tm=nb_s * (POOL_CTX + 1),
                  seq=nb_s * (POOL_CTX + 1), fixed_pos=past)
    m_s = m_s.reshape(nb_s, POOL_CTX + 1, D_POOL)[:, POOL_CTX]
    y_s = _tail(m_s, hsilu_s, hsig_s, o_s.reshape(nb_s, D_ATTN), xs, pw, ps, wup, wun, wo, lg, lb, tm=nb_s)

    wl = min(WINDOW, seq)
    kv_p = [a.reshape(1, nb_p, seq, N_KV_HEADS, HEAD_DIM) for a in kv_p]
    kv_s = [a.reshape(1, nb_s, 1, N_KV_HEADS, HEAD_DIM) for a in kv_s]
    return (
        y_p.reshape(nb_p, seq, D_MODEL),
        y_s.reshape(nb_s, 1, D_MODEL),
        kv_p[0], kv_p[1], kv_p[2], kv_p[3],
        kv_p[4][:, :, seq - wl:], kv_p[5][:, :, seq - wl:],
        u_p.reshape(nb_p, seq, D_POOL)[None, :, seq - POOL_CTX:],
        kv_s[0], kv_s[1], kv_s[2], kv_s[3],
        jnp.concatenate([cache_win_k, kv_s[4]], axis=2)[:, :, 1:],
        jnp.concatenate([cache_win_v, kv_s[5]], axis=2)[:, :, 1:],
        ctx[None, :, 1:],
    )
```

```python
import functools

import jax
import jax.numpy as jnp
import numpy as np
from jax import lax
from jax.experimental import pallas as pl
from jax.experimental.pallas import tpu as pltpu

D_MODEL = 2048
D_POOL = 1024
POOL_WINDOWS = (2, 4, 8, 16)
POOL_GC = D_POOL // len(POOL_WINDOWS)
POOL_CTX = max(POOL_WINDOWS) - 1
HEAD_DIM = 128
N_HEADS = 8
N_KV_HEADS = 2
GROUP = N_HEADS // N_KV_HEADS
D_ATTN = N_HEADS * HEAD_DIM
D_KV = N_KV_HEADS * HEAD_DIM
CMP_BLOCK = 32
SLC_BLOCK = 64
TOP_N = 16
WINDOW = 512
PAGE_SIZE = 128
ATTN_SCALE = HEAD_DIM ** -0.5
DEPTH = 1
ALPHA = (2.0 * DEPTH) ** 0.25
LN_EPS = 1e-5
NEG = -1e30
FORCE = 1e6

F32 = jnp.float32
BF16 = jnp.bfloat16

SUBLANES = 8
LANES = 128
VMEM_LIMIT_BYTES = 56 * 1024 * 1024

PROJ_TN = 512
PROJ_RC = 512
N_MERGE_COLS = 2 * D_MODEL
SIG_GN = N_MERGE_COLS
N_GATE = 3 * N_HEADS
N_KV_PROJ = 6
ACT_NONE, ACT_SILU, ACT_SIGMOID, ACT_SCALE = range(4)
LOG2E = 1.4426950408889634
TAIL_RC = 128
ROW_BLOCK = 128
ROW_CHUNK = 32

CMP_ROWS = CMP_BLOCK * N_KV_HEADS
CMP_PITCH = 72
BLOCKS_PER_PAGE = PAGE_SIZE // CMP_BLOCK
CMP_SLOTS = 3


def _params(sem):
    return pltpu.CompilerParams(dimension_semantics=sem, vmem_limit_bytes=VMEM_LIMIT_BYTES)


def _sigmoid(x):
    return 1.0 / (1.0 + jnp.exp(-x))


def _activate(acc, act):
    if act == ACT_SILU:
        return acc * _sigmoid(acc)
    if act == ACT_SIGMOID:
        return _sigmoid(acc)
    if act == ACT_SCALE:
        return acc * (ATTN_SCALE * LOG2E)
    return acc


def _xwt(x, wt, b):
    return lax.dot_general(x, wt, (((1,), (1,)), ((), ())), preferred_element_type=F32) + b


def _wt_spec(tn):
    return pl.BlockSpec((pl.Element(tn), pl.Element(D_MODEL)), lambda i, j, starts8: (starts8[j] * SUBLANES, 0))


def _proj_body(starts8_ref, x_ref, wt_ref, b_ref, o_ref, *, act, rc):
    wt = wt_ref[...].astype(BF16)
    for c in range(x_ref.shape[0] // rc):
        r = slice(c * rc, (c + 1) * rc)
        o_ref[r, :] = _activate(_xwt(x_ref[r, :], wt, b_ref[...]), act).astype(o_ref.dtype)


def _proj(x, wt, b, starts, act, tm, out_dtype):
    m = x.shape[0]
    starts8 = jnp.asarray([s // SUBLANES for s in starts], jnp.int32)
    b = jnp.concatenate([b[:, s:s + PROJ_TN] for s in starts], axis=1)
    return pl.pallas_call(
        functools.partial(_proj_body, act=act, rc=min(tm, PROJ_RC)),
        grid_spec=pltpu.PrefetchScalarGridSpec(
            num_scalar_prefetch=1,
            grid=(m // tm, len(starts)),
            in_specs=[
                pl.BlockSpec((tm, D_MODEL), lambda i, j, starts8: (i, 0)),
                _wt_spec(PROJ_TN),
                pl.BlockSpec((1, PROJ_TN), lambda i, j, starts8: (0, j)),
            ],
            out_specs=pl.BlockSpec((tm, PROJ_TN), lambda i, j, starts8: (i, j)),
        ),
        out_shape=jax.ShapeDtypeStruct((m, len(starts) * PROJ_TN), out_dtype),
        compiler_params=_params(("arbitrary", "arbitrary")),
        name="proj",
    )(starts8, x, wt, b)


def _proj_kv_body(starts8_ref, x_ref, wt_ref, b_ref, *o_refs, rc):
    j = pl.program_id(1)
    wt = wt_ref[...].astype(BF16)
    for k, o_ref in enumerate(o_refs):
        @pl.when(j == k)
        def _(o_ref=o_ref):
            for c in range(x_ref.shape[0] // rc):
                acc = _xwt(x_ref[c * rc:(c + 1) * rc, :], wt, b_ref[...])
                for h in range(N_KV_HEADS):
                    o_ref[pl.ds(N_KV_HEADS * c * rc + h, rc, stride=N_KV_HEADS), :] = (
                        acc[:, h * HEAD_DIM:(h + 1) * HEAD_DIM])


def _proj_kv(x, wt, b, start, tm):
    m = x.shape[0]
    flat = jax.ShapeDtypeStruct((m * N_KV_HEADS, HEAD_DIM), F32)
    starts8 = jnp.asarray([(start + k * D_KV) // SUBLANES for k in range(N_KV_PROJ)], jnp.int32)
    b = b[:, start:start + N_KV_PROJ * D_KV]
    return pl.pallas_call(
        functools.partial(_proj_kv_body, rc=min(tm, PROJ_RC)),
        grid_spec=pltpu.PrefetchScalarGridSpec(
            num_scalar_prefetch=1,
            grid=(m // tm, N_KV_PROJ),
            in_specs=[
                pl.BlockSpec((tm, D_MODEL), lambda i, j, starts8: (i, 0)),
                _wt_spec(D_KV),
                pl.BlockSpec((1, D_KV), lambda i, j, starts8: (0, j)),
            ],
            out_specs=[pl.BlockSpec((tm * N_KV_HEADS, HEAD_DIM), lambda i, j, starts8: (i, 0))] * N_KV_PROJ,
        ),
        out_shape=[flat] * N_KV_PROJ,
        compiler_params=_params(("arbitrary", "arbitrary")),
        name="proj_kv",
    )(starts8, x, wt, b)


def _window_means(halo, u, row0, fixed_pos):
    tm = u.shape[0]
    ext = jnp.concatenate([halo, u], axis=0)
    if fixed_pos is None:
        pos = row0 + lax.broadcasted_iota(jnp.int32, (tm, 1), 0)
    else:
        pos = jnp.full((tm, 1), fixed_pos, jnp.int32)
    outs = []
    for g, w in enumerate(POOL_WINDOWS):
        a = ext[:, g * POOL_GC:(g + 1) * POOL_GC]
        s = a
        k = 1
        while k < w:
            s = s + pltpu.roll(s, k, 0)
            k *= 2
        cnt = jnp.minimum(pos + 1, w).astype(F32)
        outs.append((s[2 * SUBLANES:] / cnt - a[2 * SUBLANES:]).astype(BF16))
    return jnp.concatenate(outs, axis=1)


def _pool_body(halo_ref, u_ref, m_ref, *, tm, seq, fixed_pos):
    row0 = (pl.program_id(0) * tm) % seq
    halo = jnp.where(row0 == 0, 0.0, halo_ref[...])
    m_ref[...] = _window_means(halo, u_ref[...], row0, fixed_pos)


def _pool_m(src, col_block, rows, tm, seq, fixed_pos):
    halo_rows = 2 * SUBLANES
    per = tm // halo_rows
    return pl.pallas_call(
        functools.partial(_pool_body, tm=tm, seq=seq, fixed_pos=fixed_pos),
        grid=(rows // tm,),
        in_specs=[
            pl.BlockSpec((halo_rows, D_POOL), lambda i: (jnp.maximum(i * per - 1, 0), col_block)),
            pl.BlockSpec((tm, D_POOL), lambda i: (i, col_block)),
        ],
        out_specs=pl.BlockSpec((tm, D_POOL), lambda i: (i, 0)),
        out_shape=jax.ShapeDtypeStruct((rows, D_POOL), BF16),
        compiler_params=_params(("arbitrary",)),
        name="pool",
    )(src, src)


def _tail_body(*refs, seq):
    if seq is None:
        m = refs[0][...]
        refs = refs[1:]
    else:
        halo_ref, u_ref = refs[:2]
        refs = refs[2:]
        row0 = (pl.program_id(0) * u_ref.shape[0]) % seq
        m = _window_means(jnp.where(row0 == 0, 0.0, halo_ref[...]), u_ref[...], row0, None)
    zp_ref, o_ref, zn_ref, ga_ref, gb_ref, x_ref, pw_ref, ps_ref, wup_ref, wun_ref, wo_ref, lg_ref, lb_ref, y_ref = refs
    tm = x_ref.shape[0]
    rc = min(tm, TAIL_RC)
    for c in range(tm // rc):
        r = slice(c * rc, (c + 1) * rc)
        ys = [jnp.dot(m[r, g * POOL_GC:(g + 1) * POOL_GC], pw_ref[g], preferred_element_type=F32)
              for g in range(len(POOL_WINDOWS))]
        y_pool = jnp.concatenate(ys, axis=1) * ps_ref[...]
        a = jnp.dot((y_pool * zp_ref[r, :]).astype(BF16), wup_ref[...], preferred_element_type=F32)
        b = jnp.dot((o_ref[r, :] * zn_ref[r, :]).astype(BF16), wun_ref[...], preferred_element_type=F32)
        mix = ga_ref[r, :] * a + gb_ref[r, :] * b
        h = jnp.dot(mix.astype(BF16), wo_ref[...], preferred_element_type=F32)
        z = ALPHA * x_ref[r, :] + h
        mu = jnp.mean(z, axis=-1, keepdims=True)
        zc = z - mu
        var = jnp.mean(zc * zc, axis=-1, keepdims=True)
        y_ref[r, :] = zc * lax.rsqrt(var + LN_EPS) * lg_ref[...] + lb_ref[...]


def _tail(pool_in, h_silu, h_sig, o, x, pw, ps, wup, wun, wo, lg, lb, tm, seq=None):
    rows = x.shape[0]
    once = pl.Buffered(1)

    def const(shape):
        return pl.BlockSpec(shape, lambda i: (0,) * len(shape), pipeline_mode=once)

    pool_specs = [pl.BlockSpec((tm, D_POOL), lambda i: (i, 0))]
    pool_args = [pool_in]
    if seq is not None:
        halo_rows = 2 * SUBLANES
        per = tm // halo_rows
        pool_specs.insert(0, pl.BlockSpec((halo_rows, D_POOL), lambda i: (jnp.maximum(i * per - 1, 0), 0)))
        pool_args.insert(0, pool_in)
    return pl.pallas_call(
        functools.partial(_tail_body, seq=seq),
        grid=(rows // tm,),
        in_specs=pool_specs + [
            pl.BlockSpec((tm, D_POOL), lambda i: (i, 0)),
            pl.BlockSpec((tm, D_ATTN), lambda i: (i, 0)),
            pl.BlockSpec((tm, D_ATTN), lambda i: (i, D_POOL // D_ATTN)),
            pl.BlockSpec((tm, D_MODEL), lambda i: (i, 0)),
            pl.BlockSpec((tm, D_MODEL), lambda i: (i, 1)),
            pl.BlockSpec((tm, D_MODEL), lambda i: (i, 0)),
            const((len(POOL_WINDOWS), POOL_GC, POOL_GC)),
            const((1, D_POOL)),
            const((D_POOL, D_MODEL)),
            const((D_ATTN, D_MODEL)),
            const((D_MODEL, D_MODEL)),
            const((1, D_MODEL)),
            const((1, D_MODEL)),
        ],
        out_specs=pl.BlockSpec((tm, D_MODEL), lambda i: (i, 0)),
        out_shape=jax.ShapeDtypeStruct((rows, D_MODEL), F32),
        compiler_params=_params(("arbitrary",)),
        name="tail",
    )(*pool_args, h_silu, o, h_silu, h_sig, h_sig, x, pw, ps, wup, wun, wo, lg, lb)


def _compress_body(pt_ref, c_hbm, pe_ref, w1_ref, w2_ref, o_ref, buf, lhs, res, sem, *, pps, n_chunks):
    b = pl.program_id(0)
    c = pl.program_id(1)
    step = b * n_chunks + c
    n_steps = pl.num_programs(0) * n_chunks
    nb = pps * BLOCKS_PER_PAGE

    def page_copy(seq_row, page0, slot, p):
        phys = pt_ref[seq_row, page0 + p]
        return pltpu.make_async_copy(c_hbm.at[pl.ds(phys * BLOCKS_PER_PAGE, BLOCKS_PER_PAGE)],
                                     buf.at[slot, pl.ds(BLOCKS_PER_PAGE * p, BLOCKS_PER_PAGE), pl.ds(0, CMP_ROWS), :],
                                     sem.at[slot])

    def start_step(s, slot):
        seq_row = s // n_chunks
        page0 = (s % n_chunks) * pps

        def pair(i, carry):
            page_copy(seq_row, page0, slot, 2 * i).start(priority=0)
            page_copy(seq_row, page0, slot, 2 * i + 1).start(priority=1)
            return carry
        lax.fori_loop(0, pps // 2, pair, 0)

    @pl.when(step == 0)
    def _():
        for s in range(CMP_SLOTS - 1):
            @pl.when(s < n_steps)
            def _(s=s):
                start_step(s, s)

    ahead = step + (CMP_SLOTS - 1)

    @pl.when(ahead < n_steps)
    def _():
        start_step(ahead, ahead % CMP_SLOTS)

    slot = step % CMP_SLOTS
    for p in range(pps):
        page_copy(b, c * pps, slot, p).wait()

    rows = buf.at[slot].reshape(nb * CMP_PITCH, LANES)
    for j in range(CMP_BLOCK):
        x0 = rows[pl.ds(2 * j, nb, stride=CMP_PITCH), :]
        x1 = rows[pl.ds(2 * j + 1, nb, stride=CMP_PITCH), :]
        pe = pe_ref[j:j + 1, :]
        lhs[:, (2 * j) * HEAD_DIM:(2 * j + 1) * HEAD_DIM] = (x0 + pe).astype(BF16)
        lhs[:, (2 * j + 1) * HEAD_DIM:(2 * j + 2) * HEAD_DIM] = (x1 + pe).astype(BF16)
    hid = jnp.dot(lhs[...], w1_ref[...], preferred_element_type=F32)
    hid = hid * _sigmoid(hid)
    out = jnp.dot(hid.astype(BF16), w2_ref[...], preferred_element_type=F32)
    half = nb // 2
    for h in range(N_KV_HEADS):
        hs = slice(h * HEAD_DIM, (h + 1) * HEAD_DIM)
        res[h] = out[:, hs]
        o_ref[0, 0:half, hs] = res[h, pl.ds(0, half, stride=2), :]
        o_ref[0, half:nb, hs] = res[h, pl.ds(1, half, stride=2), :]


def _compress(page_table, cache, pe, w1, w2, pps):
    nbatch, n_pages = page_table.shape
    n_chunks = n_pages // pps
    nb = pps * BLOCKS_PER_PAGE
    kdim = CMP_BLOCK * D_KV
    return pl.pallas_call(
        functools.partial(_compress_body, pps=pps, n_chunks=n_chunks),
        grid_spec=pltpu.PrefetchScalarGridSpec(
            num_scalar_prefetch=1,
            grid=(nbatch, n_chunks),
            in_specs=[
                pl.BlockSpec(memory_space=pl.ANY),
                pl.BlockSpec((CMP_BLOCK, HEAD_DIM), lambda b, c, pt: (0, 0)),
                pl.BlockSpec((kdim, D_KV), lambda b, c, pt: (0, 0), pipeline_mode=pl.Buffered(1)),
                pl.BlockSpec((D_KV, D_KV), lambda b, c, pt: (0, 0)),
            ],
            out_specs=pl.BlockSpec((1, nb, D_KV), lambda b, c, pt: (b, c, 0)),
            scratch_shapes=[
                pltpu.VMEM((CMP_SLOTS, nb, CMP_PITCH, LANES), F32),
                pltpu.VMEM((nb, kdim), BF16),
                pltpu.VMEM((N_KV_HEADS, nb, HEAD_DIM), F32),
                pltpu.SemaphoreType.DMA((CMP_SLOTS,)),
            ],
        ),
        out_shape=jax.ShapeDtypeStruct((nbatch, n_chunks * nb, D_KV), F32),
        compiler_params=_params(("arbitrary", "arbitrary")),
        name="compress",
    )(page_table, cache, pe, w1, w2)


def _kprep_body(ks_ref, vs_ref, kw_ref, vw_ref, kts_ref, vso_ref, ktw_ref, vwo_ref, *, tk):
    t = pl.program_id(1)
    kpos = t * tk + lax.broadcasted_iota(jnp.int32, (SLC_BLOCK, tk), 1)
    blk = lax.broadcasted_iota(jnp.int32, (SLC_BLOCK, tk), 0)
    onehot = jnp.where(kpos // SLC_BLOCK == blk, 1.0, 0.0).astype(BF16)
    for g in range(N_KV_HEADS):
        head = pl.ds(g, tk, stride=N_KV_HEADS)
        kts_ref[0, g, 0, 0:HEAD_DIM, :] = ks_ref[head, :].T.astype(BF16)
        kts_ref[0, g, 0, HEAD_DIM:HEAD_DIM + SLC_BLOCK, :] = onehot
        kts_ref[0, g, 0, HEAD_DIM + SLC_BLOCK:, :] = jnp.zeros((SLC_BLOCK, tk), BF16)
        kw = kw_ref[head, :]
        for c in range(tk // LANES):
            ktw_ref[0, g, c] = kw[c * LANES:(c + 1) * LANES, :].T.astype(BF16)
        ones = jnp.ones((tk, HEAD_DIM), BF16)
        vso_ref[0, g, :, 0:HEAD_DIM] = vs_ref[head, :].astype(BF16)
        vso_ref[0, g, :, HEAD_DIM:] = ones
        vwo_ref[0, g, :, 0:HEAD_DIM] = vw_ref[head, :].astype(BF16)
        vwo_ref[0, g, :, HEAD_DIM:] = ones


def _kprep(ks, vs, kw, vw, nbatch, seq, tk):
    nt = seq // tk
    src = pl.BlockSpec((tk * N_KV_HEADS, HEAD_DIM), lambda b, t: (b * nt + t, 0))
    return pl.pallas_call(
        functools.partial(_kprep_body, tk=tk),
        grid=(nbatch, nt),
        in_specs=[src, src, src, src],
        out_specs=[
            pl.BlockSpec((1, N_KV_HEADS, 1, 2 * HEAD_DIM, tk), lambda b, t: (b, 0, t, 0, 0)),
            pl.BlockSpec((1, N_KV_HEADS, tk, 2 * HEAD_DIM), lambda b, t: (b, 0, t, 0)),
            pl.BlockSpec((1, N_KV_HEADS, tk // LANES, HEAD_DIM, LANES), lambda b, t: (b, 0, t, 0, 0)),
            pl.BlockSpec((1, N_KV_HEADS, tk, 2 * HEAD_DIM), lambda b, t: (b, 0, t, 0)),
        ],
        out_shape=[
            jax.ShapeDtypeStruct((nbatch, N_KV_HEADS, nt, 2 * HEAD_DIM, tk), BF16),
            jax.ShapeDtypeStruct((nbatch, N_KV_HEADS, seq, 2 * HEAD_DIM), BF16),
            jax.ShapeDtypeStruct((nbatch, N_KV_HEADS, seq // LANES, HEAD_DIM, LANES), BF16),
            jax.ShapeDtypeStruct((nbatch, N_KV_HEADS, seq, 2 * HEAD_DIM), BF16),
        ],
        compiler_params=_params(("arbitrary", "arbitrary")),
        name="kprep",
    )(ks, vs, kw, vw)


def _masked_softmax(s, mask):
    s = jnp.where(mask, s, NEG)
    mx = jnp.max(s, axis=-1, keepdims=True)
    e = jnp.where(mask, jnp.exp2(s - mx), 0.0)
    l = jnp.sum(e, axis=-1, keepdims=True)
    return e / jnp.where(l > 0.0, l, 1.0)


def _select_bias(p_slc_t, qpos_row, n_slc):
    shape = p_slc_t.shape
    blk = lax.broadcasted_iota(jnp.int32, shape, 0)
    valid = blk * SLC_BLOCK <= qpos_row
    cur = qpos_row // SLC_BLOCK
    forced = (blk == 0) | (blk == cur) | (blk == cur - 1)
    score = jnp.where(valid & forced, FORCE, jnp.where(valid, p_slc_t, -FORCE))
    n_chunks = n_slc // SUBLANES
    chunks = [score[c * SUBLANES:(c + 1) * SUBLANES] for c in range(n_chunks)]
    ranks = [jnp.zeros((SUBLANES, shape[1]), F32) for _ in range(n_chunks)]
    sub = lax.broadcasted_iota(jnp.int32, (SUBLANES, shape[1]), 0)
    for j in range(n_slc):
        row = jnp.broadcast_to(score[j:j + 1, :], (SUBLANES, shape[1]))
        for c in range(n_chunks):
            lo = c * SUBLANES
            if lo > j:
                ahead = row >= chunks[c]
            elif lo + SUBLANES - 1 < j:
                ahead = row > chunks[c]
            else:
                ahead = (row > chunks[c]) | ((row == chunks[c]) & (sub + lo > j))
            ranks[c] = ranks[c] + jnp.where(ahead, 1.0, 0.0)
    rank = jnp.concatenate(ranks, axis=0)
    return jnp.where(rank < min(TOP_N, n_slc), 0.0, NEG)


def _attend_tile(q_ref, qcols, kt, v, width, mask_fn, s_ref, m_scr, acc_scr):
    rows = s_ref.shape[0]
    reps = width // LANES
    for blk in range(rows // ROW_BLOCK):
        rb = slice(blk * ROW_BLOCK, (blk + 1) * ROW_BLOCK)
        s_ref[rb, 0:width] = jnp.dot(q_ref[rb, 0:qcols], kt, preferred_element_type=F32)
    for blk in range(rows // ROW_BLOCK):
        alphas, ps = [], []
        for c in range(ROW_BLOCK // ROW_CHUNK):
            r0 = blk * ROW_BLOCK + c * ROW_CHUNK
            r = slice(r0, r0 + ROW_CHUNK)
            s = s_ref[r, 0:width]
            if mask_fn is not None:
                s = jnp.where(mask_fn(r0, ROW_CHUNK), s, NEG)
            m_old = m_scr[r, :]
            m_new = jnp.maximum(m_old, jnp.max(s, axis=-1, keepdims=True))
            ps.append(jnp.exp2(s - jnp.concatenate([m_new] * reps, axis=1)).astype(BF16))
            alphas.append(jnp.exp2(m_old - m_new))
            m_scr[r, :] = m_new
        alpha = jnp.concatenate(alphas, axis=0)
        rb = slice(blk * ROW_BLOCK, (blk + 1) * ROW_BLOCK)
        acc_scr[rb, :] = (jnp.concatenate([alpha, alpha], axis=1) * acc_scr[rb, :]
                          + jnp.dot(jnp.concatenate(ps, axis=0), v, preferred_element_type=F32))


def _attn_body(q_ref, gn_ref, kc_ref, vc_ref, kts_ref, vs_ref, ktw_ref, vw_ref, o_ref,
               qaug, s_scr, m_scr, acc_scr, *, tq, tk, seq):
    q0 = pl.program_id(1) * tq
    n_cmp = seq // CMP_BLOCK
    n_slc = -(-seq // SLC_BLOCK)
    half = n_cmp // 2
    qpos = q0 + lax.broadcasted_iota(jnp.int32, (tq, 1), 0)
    qpos4 = jnp.concatenate([qpos] * GROUP, axis=0)
    qpos_row = q0 + lax.broadcasted_iota(jnp.int32, (n_slc, tq), 1)
    lane = lax.broadcasted_iota(jnp.int32, (1, n_cmp), 1)
    cmp_blk = 2 * (lane % half) + lane // half
    m_cmp = ((cmp_blk + 1) * CMP_BLOCK - 1) <= qpos4
    gates = gn_ref[...].astype(F32)
    w0 = pl.multiple_of(jnp.maximum(q0 - WINDOW, 0), LANES)
    last = (q0 + tq + tk - 1) // tk - 1

    def chunk_qpos(r0, n):
        return q0 + r0 % tq + lax.broadcasted_iota(jnp.int32, (n, 1), 0)

    def reset(g):
        m_scr[g] = jnp.full(m_scr.shape[1:], -jnp.inf, F32)
        acc_scr[g] = jnp.zeros(acc_scr.shape[1:], F32)

    def result(g):
        acc = acc_scr[g]
        return acc[:, 0:HEAD_DIM] / acc[:, HEAD_DIM:]

    def attend(g, qcols, kt, v, width, mask_fn):
        _attend_tile(qaug.at[g], qcols, kt, v, width, mask_fn, s_scr.at[g], m_scr.at[g], acc_scr.at[g])

    groups = range(N_KV_HEADS)

    o_cmp = []
    for g in groups:
        hs = slice(g * HEAD_DIM, (g + 1) * HEAD_DIM)
        for h in range(GROUP):
            qaug[g, h * tq:(h + 1) * tq, 0:HEAD_DIM] = (
                q_ref[:, (GROUP * g + h) * HEAD_DIM:(GROUP * g + h + 1) * HEAD_DIM])
        kc = kc_ref[0][:, hs].astype(BF16)
        vc = vc_ref[0][:, hs].astype(BF16)
        s = lax.dot_general(qaug[g, :, 0:HEAD_DIM], kc, (((1,), (1,)), ((), ())), preferred_element_type=F32)
        p = _masked_softmax(s, m_cmp)
        o_cmp.append(jnp.dot(p.astype(BF16), vc, preferred_element_type=F32))
        p_grp = p[0:tq]
        for h in range(1, GROUP):
            p_grp = p_grp + p[h * tq:(h + 1) * tq]
        p_slc = p_grp + pltpu.roll(p_grp, half, 1)
        bias_t = _select_bias(p_slc.T[0:n_slc], qpos_row, n_slc)
        bias = jnp.concatenate([bias_t, jnp.zeros((LANES - n_slc, tq), F32)], axis=0).T.astype(BF16)
        for h in range(GROUP):
            qaug[g, h * tq:(h + 1) * tq, HEAD_DIM:] = bias
        reset(g)

    def interior(t, carry):
        k0 = pl.multiple_of(t * tk, tk)
        for g in groups:
            attend(g, 2 * HEAD_DIM, kts_ref[0, g, t], vs_ref[0, g, pl.ds(k0, tk), :], tk, None)
        return carry

    lax.fori_loop(0, last, interior, 0)
    k_last = pl.multiple_of(last * tk, tk)

    def causal(r0, n):
        return k_last + lax.broadcasted_iota(jnp.int32, (1, tk), 1) <= chunk_qpos(r0, n)

    o_sel = []
    for g in groups:
        attend(g, 2 * HEAD_DIM, kts_ref[0, g, last], vs_ref[0, g, pl.ds(k_last, tk), :], tk, causal)
        o_sel.append(result(g))
        reset(g)

    span = WINDOW + tq

    def in_window(r0, n):
        dist = chunk_qpos(r0, n) - (w0 + lax.broadcasted_iota(jnp.int32, (1, span), 1))
        return (dist >= 0) & (dist <= WINDOW)

    for g in groups:
        ktw = jnp.concatenate([ktw_ref[0, g, w0 // LANES + c] for c in range(span // LANES)], axis=1)
        attend(g, HEAD_DIM, ktw, vw_ref[0, g, pl.ds(w0, span), :], span, in_window)

    for g in groups:
        o_win = result(g)
        for h in range(GROUP):
            hd = GROUP * g + h
            r = slice(h * tq, (h + 1) * tq)
            o_ref[:, hd * HEAD_DIM:(hd + 1) * HEAD_DIM] = (
                gates[:, hd:hd + 1] * o_cmp[g][r]
                + gates[:, N_HEADS + hd:N_HEADS + hd + 1] * o_sel[g][r]
                + gates[:, 2 * N_HEADS + hd:2 * N_HEADS + hd + 1] * o_win[r]).astype(o_ref.dtype)


def _prompt_attn(h_q, h_sig, kc, vc, kts, vs, ktw, vw, nbatch, seq, tq, tk):
    nq = seq // tq
    n_cmp = seq // CMP_BLOCK
    nt = seq // tk
    rows = GROUP * tq
    return pl.pallas_call(
        functools.partial(_attn_body, tq=tq, tk=tk, seq=seq),
        grid=(nbatch, nq),
        in_specs=[
            pl.BlockSpec((tq, D_ATTN), lambda b, i: (b * nq + i, 0)),
            pl.BlockSpec((tq, LANES), lambda b, i: (b * nq + i, SIG_GN // LANES)),
            pl.BlockSpec((1, n_cmp, D_KV), lambda b, i: (b, 0, 0)),
            pl.BlockSpec((1, n_cmp, D_KV), lambda b, i: (b, 0, 0)),
            pl.BlockSpec((1, N_KV_HEADS, nt, 2 * HEAD_DIM, tk), lambda b, i: (b, 0, 0, 0, 0)),
            pl.BlockSpec((1, N_KV_HEADS, seq, 2 * HEAD_DIM), lambda b, i: (b, 0, 0, 0)),
            pl.BlockSpec((1, N_KV_HEADS, seq // LANES, HEAD_DIM, LANES), lambda b, i: (b, 0, 0, 0, 0)),
            pl.BlockSpec((1, N_KV_HEADS, seq, 2 * HEAD_DIM), lambda b, i: (b, 0, 0, 0)),
        ],
        out_specs=pl.BlockSpec((tq, D_ATTN), lambda b, i: (b * nq + i, 0)),
        scratch_shapes=[
            pltpu.VMEM((N_KV_HEADS, rows, 2 * HEAD_DIM), BF16),
            pltpu.VMEM((N_KV_HEADS, rows, max(tk, WINDOW + tq)), F32),
            pltpu.VMEM((N_KV_HEADS, rows, LANES), F32),
            pltpu.VMEM((N_KV_HEADS, rows, 2 * HEAD_DIM), F32),
        ],
        out_shape=jax.ShapeDtypeStruct((nbatch * seq, D_ATTN), BF16),
        compiler_params=_params(("arbitrary", "arbitrary")),
        name="attn",
    )(h_q, h_sig, kc, vc, kts, vs, ktw, vw)


def _group_queries(q_row, g):
    heads = [q_row[:, (GROUP * g + h) * HEAD_DIM:(GROUP * g + h + 1) * HEAD_DIM] for h in range(GROUP)]
    pad = jnp.zeros((2 * SUBLANES - GROUP, HEAD_DIM), F32)
    return jnp.concatenate(heads + [pad], axis=0).astype(BF16)


def _scmp_body(q_ref, kc_ref, vc_ref, ocmp_ref, pslc_ref, *, q_pos, chunk):
    b = pl.program_id(0)
    q_row = q_ref[pl.ds(b, 1), :]
    n_cmp = kc_ref.shape[1]
    half = chunk // 2
    lane = lax.broadcasted_iota(jnp.int32, (1, n_cmp), 1)
    within = lane % chunk
    cmp_blk = (lane // chunk) * chunk + 2 * (within % half) + within // half
    m_cmp = ((cmp_blk + 1) * CMP_BLOCK - 1) <= q_pos
    outs = []
    for g in range(N_KV_HEADS):
        hs = slice(g * HEAD_DIM, (g + 1) * HEAD_DIM)
        qg = _group_queries(q_row, g)
        s = lax.dot_general(qg, kc_ref[0][:, hs].astype(BF16), (((1,), (1,)), ((), ())), preferred_element_type=F32)
        p = _masked_softmax(s, m_cmp)
        o = jnp.dot(p.astype(BF16), vc_ref[0][:, hs].astype(BF16), preferred_element_type=F32)
        outs.append(o[0:GROUP])
        p_grp = p[0:1]
        for h in range(1, GROUP):
            p_grp = p_grp + p[h:h + 1]
        parts = []
        for c in range(n_cmp // chunk):
            pc = p_grp[:, c * chunk:(c + 1) * chunk]
            parts.append(pc + pltpu.roll(pc, half, 1))
        pslc_ref[0, g:g + 1, :] = jnp.concatenate(parts, axis=1)
    ocmp_ref[0] = jnp.concatenate(outs, axis=0)


def _sample_cmp(q, kc, vc, q_pos, chunk):
    nbatch, n_cmp, _ = kc.shape
    return pl.pallas_call(
        functools.partial(_scmp_body, q_pos=q_pos, chunk=chunk),
        grid=(nbatch,),
        in_specs=[
            pl.BlockSpec((nbatch, D_ATTN), lambda b: (0, 0)),
            pl.BlockSpec((1, n_cmp, D_KV), lambda b: (b, 0, 0)),
            pl.BlockSpec((1, n_cmp, D_KV), lambda b: (b, 0, 0)),
        ],
        out_specs=[
            pl.BlockSpec((1, N_HEADS, HEAD_DIM), lambda b: (b, 0, 0)),
            pl.BlockSpec((1, N_KV_HEADS, n_cmp), lambda b: (b, 0, 0)),
        ],
        out_shape=[
            jax.ShapeDtypeStruct((nbatch, N_HEADS, HEAD_DIM), F32),
            jax.ShapeDtypeStruct((nbatch, N_KV_HEADS, n_cmp), F32),
        ],
        compiler_params=_params(("arbitrary",)),
        name="s_cmp",
    )(q, kc, vc)


def _stopk_body(p_ref, idx_ref, *, q_pos, chunk, n_slc):
    p = p_ref[...]
    rows, width = p.shape
    half = chunk // 2
    n_in = (width // chunk) * half
    lane = lax.broadcasted_iota(jnp.int32, (1, width), 1)
    within = lane % chunk
    blk = jnp.where(within < half, (lane // chunk) * half + within, -1)
    extra = (lane >= half) & (lane < half + (n_slc - n_in))
    blk = jnp.where(extra, n_in + lane - half, blk)
    real = blk >= 0
    valid = real & (blk * SLC_BLOCK <= q_pos)
    cur = q_pos // SLC_BLOCK
    forced = (blk == 0) | (blk == cur) | (blk == cur - 1)
    base = jnp.where(extra, 0.0, p)
    x = jnp.where(valid & forced, FORCE, jnp.where(valid, base, -FORCE))
    x = jnp.where(real, x, -jnp.inf)
    blk_f = blk.astype(F32)
    out_lane = lax.broadcasted_iota(jnp.int32, (rows, LANES), 1)
    out = jnp.zeros((rows, LANES), F32)
    for r in range(min(TOP_N, n_slc)):
        mx = jnp.max(x, axis=-1, keepdims=True)
        pick = jnp.min(jnp.where(x == mx, blk_f, float(2 ** 30)), axis=-1, keepdims=True)
        out = jnp.where(out_lane == r, pick, out)
        x = jnp.where(blk_f == pick, -jnp.inf, x)
    idx_ref[...] = out.astype(jnp.int32)


def _sample_topk(pslc, q_pos, chunk, n_slc):
    rows, width = pslc.shape
    return pl.pallas_call(
        functools.partial(_stopk_body, q_pos=q_pos, chunk=chunk, n_slc=n_slc),
        grid=(1,),
        in_specs=[pl.BlockSpec((rows, width), lambda i: (0, 0))],
        out_specs=pl.BlockSpec((rows, LANES), lambda i: (0, 0)),
        out_shape=jax.ShapeDtypeStruct((rows, LANES), jnp.int32),
        compiler_params=_params(("arbitrary",)),
        name="s_topk",
    )(pslc)


def _sattn_body(idx_ref, pt_ref, q_ref, gn_ref, ocmp_ref, knew_ref, vnew_ref, kwn_ref, vwn_ref, wk_ref, wv_ref,
                sk_hbm, sv_hbm, o_ref, kbuf, vbuf, sem, *, q_pos, n_top, n_pages, past):
    b = pl.program_id(0)
    nbatch = pl.num_programs(0)
    half_rows = SLC_BLOCK * N_KV_HEADS
    tail_blk = past // SLC_BLOCK

    def copies(bb, slot, g, i):
        blk = idx_ref[bb * N_KV_HEADS + g, i]
        page = jnp.minimum(blk // 2, n_pages - 1)
        start = pl.multiple_of(pt_ref[bb, page] * (PAGE_SIZE * N_KV_HEADS) + (blk % 2) * half_rows, half_rows)
        return [pltpu.make_async_copy(hbm.at[pl.ds(start, half_rows), :], dst.at[slot, g, i], sem.at[t, slot, g, i])
                for t, (hbm, dst) in enumerate(((sk_hbm, kbuf), (sv_hbm, vbuf)))]

    def start_all(bb, slot):
        for g in range(N_KV_HEADS):
            for i in range(n_top):
                for cp in copies(bb, slot, g, i):
                    cp.start()

    @pl.when(b == 0)
    def _():
        start_all(0, 0)

    @pl.when(b + 1 < nbatch)
    def _():
        start_all(b + 1, (b + 1) % 2)

    slot = b % 2
    for g in range(N_KV_HEADS):
        for i in range(n_top):
            for cp in copies(b, slot, g, i):
                cp.wait()

    q_row = q_ref[pl.ds(b, 1), :]
    gates = gn_ref[pl.ds(b, 1), :]
    pad = jnp.zeros((2 * SUBLANES - 1, HEAD_DIM), F32)

    def new_row(ref, g):
        return jnp.concatenate([ref[pl.ds(b * N_KV_HEADS + g, 1), :], pad], axis=0).astype(BF16)

    outs = []
    for g in range(N_KV_HEADS):
        qg = _group_queries(q_row, g)
        ks, vs = [], []
        key = lax.broadcasted_iota(jnp.int32, (1, n_top * SLC_BLOCK), 1)
        key_blk = jnp.zeros((1, n_top * SLC_BLOCK), jnp.int32)
        for i in range(n_top):
            blk = idx_ref[b * N_KV_HEADS + g, i]
            ks.append(kbuf[slot, g, i, pl.ds(g, SLC_BLOCK, stride=N_KV_HEADS), :])
            vs.append(vbuf[slot, g, i, pl.ds(g, SLC_BLOCK, stride=N_KV_HEADS), :])
            key_blk = jnp.where(key // SLC_BLOCK == i, blk, key_blk)
        k_sel = jnp.concatenate(ks, axis=0).astype(BF16)
        v_sel = jnp.concatenate(vs, axis=0).astype(BF16)
        m_sel = (key_blk * SLC_BLOCK + key % SLC_BLOCK <= q_pos) & (key_blk < tail_blk)
        tail_sel = jnp.max(jnp.where(key_blk == tail_blk, 1.0, 0.0), axis=-1, keepdims=True) > 0.5
        first = lax.broadcasted_iota(jnp.int32, (1, 2 * SUBLANES), 1) == 0
        o_sel = _two_part_attention(qg, k_sel, v_sel, m_sel, new_row(knew_ref, g), new_row(vnew_ref, g),
                                    first & tail_sel)
        wbuf = wk_ref.shape[1] // N_KV_HEADS
        k_win = wk_ref[0, pl.ds(g, wbuf, stride=N_KV_HEADS), :].astype(BF16)
        v_win = wv_ref[0, pl.ds(g, wbuf, stride=N_KV_HEADS), :].astype(BF16)
        dist = q_pos - (past - wbuf + lax.broadcasted_iota(jnp.int32, (1, wbuf), 1))
        m_win = (dist >= 0) & (dist <= WINDOW)
        o_win = _two_part_attention(qg, k_win, v_win, m_win, new_row(kwn_ref, g), new_row(vwn_ref, g), first)
        o_cmp = ocmp_ref[0, GROUP * g:GROUP * (g + 1), :]
        for h in range(GROUP):
            hd = GROUP * g + h
            outs.append(gates[:, hd:hd + 1] * o_cmp[h:h + 1]
                        + gates[:, N_HEADS + hd:N_HEADS + hd + 1] * o_sel[h:h + 1]
                        + gates[:, 2 * N_HEADS + hd:2 * N_HEADS + hd + 1] * o_win[h:h + 1])
    o_ref[0] = jnp.concatenate(outs, axis=0)


def _two_part_attention(q, k1, v1, m1, k2, v2, m2):
    dn = (((1,), (1,)), ((), ()))
    s1 = jnp.where(m1, lax.dot_general(q, k1, dn, preferred_element_type=F32), NEG)
    s2 = jnp.where(m2, lax.dot_general(q, k2, dn, preferred_element_type=F32), NEG)
    mx = jnp.maximum(jnp.max(s1, axis=-1, keepdims=True), jnp.max(s2, axis=-1, keepdims=True))
    e1 = jnp.where(m1, jnp.exp2(s1 - mx), 0.0)
    e2 = jnp.where(m2, jnp.exp2(s2 - mx), 0.0)
    l = jnp.sum(e1, axis=-1, keepdims=True) + jnp.sum(e2, axis=-1, keepdims=True)
    inv = 1.0 / jnp.where(l > 0.0, l, 1.0)
    o = (jnp.dot((e1 * inv).astype(BF16), v1, preferred_element_type=F32)
         + jnp.dot((e2 * inv).astype(BF16), v2, preferred_element_type=F32))
    return o


def _sample_attn(idx, page_table, h_q, h_sig, ocmp, new_rows, win_k, win_v, slc_k, slc_v, q_pos, past):
    nbatch, n_pages = page_table.shape
    n_top = min(TOP_N, past // SLC_BLOCK + 1)
    half_rows = SLC_BLOCK * N_KV_HEADS
    flat_new = pl.BlockSpec((nbatch * N_KV_HEADS, HEAD_DIM), lambda b, idx, pt: (0, 0))

    def whole(col_block, width):
        return pl.BlockSpec((nbatch, width), lambda b, idx, pt: (0, col_block))

    return pl.pallas_call(
        functools.partial(_sattn_body, q_pos=q_pos, n_top=n_top, n_pages=n_pages, past=past),
        grid_spec=pltpu.PrefetchScalarGridSpec(
            num_scalar_prefetch=2,
            grid=(nbatch,),
            in_specs=[
                whole(0, D_ATTN),
                whole(SIG_GN // LANES, LANES),
                pl.BlockSpec((1, N_HEADS, HEAD_DIM), lambda b, idx, pt: (b, 0, 0)),
                flat_new, flat_new, flat_new, flat_new,
                pl.BlockSpec((1,) + win_k.shape[1:], lambda b, idx, pt: (b, 0, 0)),
                pl.BlockSpec((1,) + win_v.shape[1:], lambda b, idx, pt: (b, 0, 0)),
                pl.BlockSpec(memory_space=pl.ANY),
                pl.BlockSpec(memory_space=pl.ANY),
            ],
            out_specs=pl.BlockSpec((1, N_HEADS, HEAD_DIM), lambda b, idx, pt: (b, 0, 0)),
            scratch_shapes=[
                pltpu.VMEM((2, N_KV_HEADS, n_top, half_rows, LANES), F32),
                pltpu.VMEM((2, N_KV_HEADS, n_top, half_rows, LANES), F32),
                pltpu.SemaphoreType.DMA((2, 2, N_KV_HEADS, n_top)),
            ],
        ),
        out_shape=jax.ShapeDtypeStruct((nbatch, N_HEADS, HEAD_DIM), F32),
        compiler_params=_params(("arbitrary",)),
        name="s_attn",
    )(idx, page_table, h_q, h_sig, ocmp, *new_rows, win_k, win_v, slc_k, slc_v)


REF_KV = 3 * D_POOL
REF_GN = REF_KV + N_KV_PROJ * D_KV
REF_ZN = REF_GN + N_GATE
REF_GM = REF_ZN + D_ATTN


def _tiles(start, width):
    return tuple(range(start, start + width, PROJ_TN))


STARTS_U = _tiles(0, D_POOL)
STARTS_SILU = _tiles(D_POOL, D_POOL) + _tiles(REF_ZN, D_ATTN)
STARTS_Q = _tiles(2 * D_POOL, D_ATTN)
STARTS_SIG = _tiles(REF_GM, N_MERGE_COLS) + (REF_GN,)


def _block_diag2(w):
    z = jnp.zeros_like(w)
    return jnp.concatenate([jnp.concatenate([w, z], axis=-1), jnp.concatenate([z, w], axis=-1)], axis=-2)


def _project(x, wt, b, tm, tm_kv, act_dtype):
    u = _proj(x, wt, b, STARTS_U, ACT_NONE, tm, F32)
    h_silu = _proj(x, wt, b, STARTS_SILU, ACT_SILU, tm, act_dtype)
    h_q = _proj(x, wt, b, STARTS_Q, ACT_SCALE, tm, act_dtype)
    h_sig = _proj(x, wt, b, STARTS_SIG, ACT_SIGMOID, tm, act_dtype)
    kv = _proj_kv(x, wt, b, REF_KV, tm_kv)
    return u, h_silu, h_q, h_sig, kv


def kernel(x_prompt, x_sample, cache_cmp_k, cache_cmp_v, cache_slc_k, cache_slc_v, cache_win_k, cache_win_v,
           state_pool, page_table, w_in, b_in, pool_w, pool_scale, cmp_pe_k, cmp_w1_k, cmp_w2_k, cmp_pe_v, cmp_w1_v,
           cmp_w2_v, w_up_pool, w_up_nsa, w_out, ln_g, ln_b):
    nb_p, seq, _ = x_prompt.shape
    nb_s = x_sample.shape[0]
    n_pages = page_table.shape[1]
    past = n_pages * PAGE_SIZE
    n_phys = cache_cmp_k.shape[1]
    wbuf = cache_win_k.shape[2]

    wt = w_in[0].T
    b = b_in[0][None, :]
    kdim = CMP_BLOCK * D_KV
    cmp_k = (cmp_pe_k[0], _block_diag2(cmp_w1_k[0]).reshape(kdim, D_KV).astype(BF16),
             _block_diag2(cmp_w2_k[0]).astype(BF16))
    cmp_v = (cmp_pe_v[0], _block_diag2(cmp_w1_v[0]).reshape(kdim, D_KV).astype(BF16),
             _block_diag2(cmp_w2_v[0]).astype(BF16))
    pw = pool_w[0].astype(BF16)
    ps = pool_scale[0][None, :]
    wup = w_up_pool[0].astype(BF16)
    wun = w_up_nsa[0].astype(BF16)
    wo = w_out[0].astype(BF16)
    lg = ln_g[0][None, :]
    lb = ln_b[0][None, :]

    xp = x_prompt.reshape(nb_p * seq, D_MODEL)
    xs = x_sample.reshape(nb_s, D_MODEL)
    u_p, hsilu_p, hq_p, hsig_p, kv_p = _project(xp.astype(BF16), wt, b, tm=2048, tm_kv=1024, act_dtype=BF16)
    u_s, hsilu_s, hq_s, hsig_s, kv_s = _project(xs.astype(BF16), wt, b, tm=nb_s, tm_kv=nb_s, act_dtype=F32)

    blocks = (-1, CMP_ROWS, LANES)
    prompt_pages = seq // PAGE_SIZE
    ident = jnp.arange(nb_p * prompt_pages, dtype=jnp.int32).reshape(nb_p, prompt_pages)
    kc_p = _compress(ident, kv_p[0].reshape(blocks), *cmp_k, pps=prompt_pages)
    vc_p = _compress(ident, kv_p[1].reshape(blocks), *cmp_v, pps=prompt_pages)
    kts, vs, ktw, vw = _kprep(kv_p[2], kv_p[3], kv_p[4], kv_p[5], nb_p, seq, tk=512)
    o_p = _prompt_attn(hq_p, hsig_p, kc_p, vc_p, kts, vs, ktw, vw, nb_p, seq, tq=256, tk=512)
    y_p = _tail(u_p, hsilu_p, hsig_p, o_p, xp, pw, ps, wup, wun, wo, lg, lb, tm=256, seq=seq)

    chunk_pages = 64
    kc_s = _compress(page_table, cache_cmp_k.reshape(blocks), *cmp_k, pps=chunk_pages)
    vc_s = _compress(page_table, cache_cmp_v.reshape(blocks), *cmp_v, pps=chunk_pages)
    chunk = chunk_pages * BLOCKS_PER_PAGE
    n_slc = past // SLC_BLOCK + 1
    ocmp_s, pslc_s = _sample_cmp(hq_s, kc_s, vc_s, past, chunk)
    idx = _sample_topk(pslc_s.reshape(nb_s * N_KV_HEADS, -1), past, chunk, n_slc)
    flat = (n_phys * PAGE_SIZE * N_KV_HEADS, HEAD_DIM)
    o_s = _sample_attn(idx, page_table, hq_s, hsig_s, ocmp_s, kv_s[2:6],
                       cache_win_k.reshape(nb_s, wbuf * N_KV_HEADS, HEAD_DIM),
                       cache_win_v.reshape(nb_s, wbuf * N_KV_HEADS, HEAD_DIM),
                       cache_slc_k.reshape(flat), cache_slc_v.reshape(flat), past, past)
    ctx = jnp.concatenate([state_pool[0], u_s[:, None, :]], axis=1)
    m_s = _pool_m(ctx.reshape(nb_s * (POOL_CTX + 1), D_POOL), 0, nb_s * (POOL_CTX + 1), tm=nb_s * (POOL_CTX + 1),
                  seq=nb_s * (POOL_CTX + 1), fixed_pos=past)
    m_s = m_s.reshape(nb_s, POOL_CTX + 1, D_POOL)[:, POOL_CTX]
    y_s = _tail(m_s, hsilu_s, hsig_s, o_s.reshape(nb_s, D_ATTN), xs, pw, ps, wup, wun, wo, lg, lb, tm=nb_s)

    wl = min(WINDOW, seq)
    kv_p = [a.reshape(1, nb_p, seq, N_KV_HEADS, HEAD_DIM) for a in kv_p]
    kv_s = [a.reshape(1, nb_s, 1, N_KV_HEADS, HEAD_DIM) for a in kv_s]
    return (
        y_p.reshape(nb_p, seq, D_MODEL),
        y_s.reshape(nb_s, 1, D_MODEL),
        kv_p[0], kv_p[1], kv_p[2], kv_p[3],
        kv_p[4][:, :, seq - wl:], kv_p[5][:, :, seq - wl:],
        u_p.reshape(nb_p, seq, D_POOL)[None, :, seq - POOL_CTX:],
        kv_s[0], kv_s[1], kv_s[2], kv_s[3],
        jnp.concatenate([cache_win_k, kv_s[4]], axis=2)[:, :, 1:],
        jnp.concatenate([cache_win_v, kv_s[5]], axis=2)[:, :, 1:],
        ctx[None, :, 1:],
    )
```

```python
import functools

import jax
import jax.numpy as jnp
import numpy as np
from jax import lax
from jax.experimental import pallas as pl
from jax.experimental.pallas import tpu as pltpu

D_MODEL = 2048
D_POOL = 1024
POOL_WINDOWS = (2, 4, 8, 16)
POOL_GC = D_POOL // len(POOL_WINDOWS)
POOL_CTX = max(POOL_WINDOWS) - 1
HEAD_DIM = 128
N_HEADS = 8
N_KV_HEADS = 2
GROUP = N_HEADS // N_KV_HEADS
D_ATTN = N_HEADS * HEAD_DIM
D_KV = N_KV_HEADS * HEAD_DIM
CMP_BLOCK = 32
SLC_BLOCK = 64
TOP_N = 16
WINDOW = 512
PAGE_SIZE = 128
ATTN_SCALE = HEAD_DIM ** -0.5
DEPTH = 1
ALPHA = (2.0 * DEPTH) ** 0.25
LN_EPS = 1e-5
NEG = -1e30
FORCE = 1e6

F32 = jnp.float32
BF16 = jnp.bfloat16

SUBLANES = 8
LANES = 128
VMEM_LIMIT_BYTES = 56 * 1024 * 1024

PROJ_TN = 512
PROJ_RC = 512
N_MERGE_COLS = 2 * D_MODEL
SIG_GN = N_MERGE_COLS
N_GATE = 3 * N_HEADS
N_KV_PROJ = 6
ACT_NONE, ACT_SILU, ACT_SIGMOID, ACT_SCALE = range(4)
LOG2E = 1.4426950408889634
SATTN_PER_STEP = 2
TAIL_RC = 128
ROW_BLOCK = 128
ROW_CHUNK = 32

CMP_ROWS = CMP_BLOCK * N_KV_HEADS
CMP_PITCH = 72
BLOCKS_PER_PAGE = PAGE_SIZE // CMP_BLOCK
CMP_SLOTS = 3


def _params(sem):
    return pltpu.CompilerParams(dimension_semantics=sem, vmem_limit_bytes=VMEM_LIMIT_BYTES)


def _sigmoid(x):
    return 1.0 / (1.0 + jnp.exp(-x))


def _activate(acc, act):
    if act == ACT_SILU:
        return acc * _sigmoid(acc)
    if act == ACT_SIGMOID:
        return _sigmoid(acc)
    if act == ACT_SCALE:
        return acc * (ATTN_SCALE * LOG2E)
    return acc


def _xwt(x, wt, b):
    return lax.dot_general(x, wt, (((1,), (1,)), ((), ())), preferred_element_type=F32) + b


def _wt_spec(tn):
    return pl.BlockSpec((pl.Element(tn), pl.Element(D_MODEL)), lambda i, j, starts8: (starts8[j] * SUBLANES, 0))


def _proj_body(starts8_ref, x_ref, wt_ref, b_ref, o_ref, *, act, rc):
    wt = wt_ref[...].astype(BF16)
    for c in range(x_ref.shape[0] // rc):
        r = slice(c * rc, (c + 1) * rc)
        o_ref[r, :] = _activate(_xwt(x_ref[r, :], wt, b_ref[...]), act).astype(o_ref.dtype)


def _proj(x, wt, b, starts, act, tm, out_dtype):
    m = x.shape[0]
    starts8 = jnp.asarray([s // SUBLANES for s in starts], jnp.int32)
    b = jnp.concatenate([b[:, s:s + PROJ_TN] for s in starts], axis=1)
    return pl.pallas_call(
        functools.partial(_proj_body, act=act, rc=min(tm, PROJ_RC)),
        grid_spec=pltpu.PrefetchScalarGridSpec(
            num_scalar_prefetch=1,
            grid=(m // tm, len(starts)),
            in_specs=[
                pl.BlockSpec((tm, D_MODEL), lambda i, j, starts8: (i, 0)),
                _wt_spec(PROJ_TN),
                pl.BlockSpec((1, PROJ_TN), lambda i, j, starts8: (0, j)),
            ],
            out_specs=pl.BlockSpec((tm, PROJ_TN), lambda i, j, starts8: (i, j)),
        ),
        out_shape=jax.ShapeDtypeStruct((m, len(starts) * PROJ_TN), out_dtype),
        compiler_params=_params(("arbitrary", "arbitrary")),
        name="proj",
    )(starts8, x, wt, b)


def _proj_kv_body(starts8_ref, x_ref, wt_ref, b_ref, *o_refs, rc):
    j = pl.program_id(1)
    wt = wt_ref[...].astype(BF16)
    per_tile = PROJ_TN // D_KV
    for k in range(len(o_refs) // per_tile):
        @pl.when(j == k)
        def _(k=k):
            for c in range(x_ref.shape[0] // rc):
                acc = _xwt(x_ref[c * rc:(c + 1) * rc, :], wt, b_ref[...])
                for n in range(per_tile * N_KV_HEADS):
                    o_ref = o_refs[per_tile * k + n // N_KV_HEADS]
                    o_ref[pl.ds(N_KV_HEADS * c * rc + n % N_KV_HEADS, rc, stride=N_KV_HEADS), :] = (
                        acc[:, n * HEAD_DIM:(n + 1) * HEAD_DIM])


def _proj_kv(x, wt, b, start, tm):
    m = x.shape[0]
    flat = jax.ShapeDtypeStruct((m * N_KV_HEADS, HEAD_DIM), F32)
    n_tiles = N_KV_PROJ * D_KV // PROJ_TN
    starts8 = jnp.asarray([(start + k * PROJ_TN) // SUBLANES for k in range(n_tiles)], jnp.int32)
    b = b[:, start:start + N_KV_PROJ * D_KV]
    return pl.pallas_call(
        functools.partial(_proj_kv_body, rc=min(tm, PROJ_RC)),
        grid_spec=pltpu.PrefetchScalarGridSpec(
            num_scalar_prefetch=1,
            grid=(m // tm, n_tiles),
            in_specs=[
                pl.BlockSpec((tm, D_MODEL), lambda i, j, starts8: (i, 0)),
                _wt_spec(PROJ_TN),
                pl.BlockSpec((1, PROJ_TN), lambda i, j, starts8: (0, j)),
            ],
            out_specs=[pl.BlockSpec((tm * N_KV_HEADS, HEAD_DIM), lambda i, j, starts8: (i, 0))] * N_KV_PROJ,
        ),
        out_shape=[flat] * N_KV_PROJ,
        compiler_params=_params(("arbitrary", "arbitrary")),
        name="proj_kv",
    )(starts8, x, wt, b)


def _window_means(halo, u, row0, fixed_pos):
    tm = u.shape[0]
    ext = jnp.concatenate([halo, u], axis=0)
    if fixed_pos is None:
        pos = row0 + lax.broadcasted_iota(jnp.int32, (tm, 1), 0)
    else:
        pos = jnp.full((tm, 1), fixed_pos, jnp.int32)
    outs = []
    for g, w in enumerate(POOL_WINDOWS):
        a = ext[:, g * POOL_GC:(g + 1) * POOL_GC]
        s = a
        k = 1
        while k < w:
            s = s + pltpu.roll(s, k, 0)
            k *= 2
        cnt = jnp.minimum(pos + 1, w).astype(F32)
        outs.append((s[2 * SUBLANES:] / cnt - a[2 * SUBLANES:]).astype(BF16))
    return jnp.concatenate(outs, axis=1)


def _pool_body(halo_ref, u_ref, m_ref, *, tm, seq, fixed_pos):
    row0 = (pl.program_id(0) * tm) % seq
    halo = jnp.where(row0 == 0, 0.0, halo_ref[...])
    m_ref[...] = _window_means(halo, u_ref[...], row0, fixed_pos)


def _pool_m(src, col_block, rows, tm, seq, fixed_pos):
    halo_rows = 2 * SUBLANES
    per = tm // halo_rows
    return pl.pallas_call(
        functools.partial(_pool_body, tm=tm, seq=seq, fixed_pos=fixed_pos),
        grid=(rows // tm,),
        in_specs=[
            pl.BlockSpec((halo_rows, D_POOL), lambda i: (jnp.maximum(i * per - 1, 0), col_block)),
            pl.BlockSpec((tm, D_POOL), lambda i: (i, col_block)),
        ],
        out_specs=pl.BlockSpec((tm, D_POOL), lambda i: (i, 0)),
        out_shape=jax.ShapeDtypeStruct((rows, D_POOL), BF16),
        compiler_params=_params(("arbitrary",)),
        name="pool",
    )(src, src)


def _tail_body(*refs, seq):
    if seq is None:
        m = refs[0][...]
        refs = refs[1:]
    else:
        halo_ref, u_ref = refs[:2]
        refs = refs[2:]
        row0 = (pl.program_id(0) * u_ref.shape[0]) % seq
        m = _window_means(jnp.where(row0 == 0, 0.0, halo_ref[...]), u_ref[...], row0, None)
    zp_ref, o_ref, zn_ref, ga_ref, gb_ref, x_ref, pw_ref, ps_ref, wup_ref, wun_ref, wo_ref, lg_ref, lb_ref, y_ref = refs
    tm = x_ref.shape[0]
    rc = min(tm, TAIL_RC)
    for c in range(tm // rc):
        r = slice(c * rc, (c + 1) * rc)
        ys = [jnp.dot(m[r, g * POOL_GC:(g + 1) * POOL_GC], pw_ref[g], preferred_element_type=F32)
              for g in range(len(POOL_WINDOWS))]
        y_pool = jnp.concatenate(ys, axis=1) * ps_ref[...]
        a = jnp.dot((y_pool * zp_ref[r, :]).astype(BF16), wup_ref[...], preferred_element_type=F32)
        b = jnp.dot((o_ref[r, :] * zn_ref[r, :]).astype(BF16), wun_ref[...], preferred_element_type=F32)
        mix = ga_ref[r, :] * a + gb_ref[r, :] * b
        h = jnp.dot(mix.astype(BF16), wo_ref[...], preferred_element_type=F32)
        z = ALPHA * x_ref[r, :] + h
        mu = jnp.mean(z, axis=-1, keepdims=True)
        zc = z - mu
        var = jnp.mean(zc * zc, axis=-1, keepdims=True)
        y_ref[r, :] = zc * lax.rsqrt(var + LN_EPS) * lg_ref[...] + lb_ref[...]


def _tail(pool_in, h_silu, h_sig, o, x, pw, ps, wup, wun, wo, lg, lb, tm, seq=None):
    rows = x.shape[0]
    once = pl.Buffered(1)

    def const(shape):
        return pl.BlockSpec(shape, lambda i: (0,) * len(shape), pipeline_mode=once)

    pool_specs = [pl.BlockSpec((tm, D_POOL), lambda i: (i, 0))]
    pool_args = [pool_in]
    if seq is not None:
        halo_rows = 2 * SUBLANES
        per = tm // halo_rows
        pool_specs.insert(0, pl.BlockSpec((halo_rows, D_POOL), lambda i: (jnp.maximum(i * per - 1, 0), 0)))
        pool_args.insert(0, pool_in)
    return pl.pallas_call(
        functools.partial(_tail_body, seq=seq),
        grid=(rows // tm,),
        in_specs=pool_specs + [
            pl.BlockSpec((tm, D_POOL), lambda i: (i, 0)),
            pl.BlockSpec((tm, D_ATTN), lambda i: (i, 0)),
            pl.BlockSpec((tm, D_ATTN), lambda i: (i, D_POOL // D_ATTN)),
            pl.BlockSpec((tm, D_MODEL), lambda i: (i, 0)),
            pl.BlockSpec((tm, D_MODEL), lambda i: (i, 1)),
            pl.BlockSpec((tm, D_MODEL), lambda i: (i, 0)),
            const((len(POOL_WINDOWS), POOL_GC, POOL_GC)),
            const((1, D_POOL)),
            const((D_POOL, D_MODEL)),
            const((D_ATTN, D_MODEL)),
            const((D_MODEL, D_MODEL)),
            const((1, D_MODEL)),
            const((1, D_MODEL)),
        ],
        out_specs=pl.BlockSpec((tm, D_MODEL), lambda i: (i, 0)),
        out_shape=jax.ShapeDtypeStruct((rows, D_MODEL), F32),
        compiler_params=_params(("arbitrary",)),
        name="tail",
    )(*pool_args, h_silu, o, h_silu, h_sig, h_sig, x, pw, ps, wup, wun, wo, lg, lb)


def _compress_body(pt_ref, c_hbm, pe_ref, w1_ref, w2_ref, o_ref, buf, lhs, res, sem, *, pps, n_chunks):
    b = pl.program_id(0)
    c = pl.program_id(1)
    step = b * n_chunks + c
    n_steps = pl.num_programs(0) * n_chunks
    nb = pps * BLOCKS_PER_PAGE

    def page_copy(seq_row, page0, slot, p):
        phys = pt_ref[seq_row, page0 + p]
        return pltpu.make_async_copy(c_hbm.at[pl.ds(phys * BLOCKS_PER_PAGE, BLOCKS_PER_PAGE)],
                                     buf.at[slot, pl.ds(BLOCKS_PER_PAGE * p, BLOCKS_PER_PAGE), pl.ds(0, CMP_ROWS), :],
                                     sem.at[slot])

    def start_step(s, slot):
        seq_row = s // n_chunks
        page0 = (s % n_chunks) * pps

        def pair(i, carry):
            page_copy(seq_row, page0, slot, 2 * i).start(priority=0)
            page_copy(seq_row, page0, slot, 2 * i + 1).start(priority=1)
            return carry
        lax.fori_loop(0, pps // 2, pair, 0)

    @pl.when(step == 0)
    def _():
        for s in range(CMP_SLOTS - 1):
            @pl.when(s < n_steps)
            def _(s=s):
                start_step(s, s)

    ahead = step + (CMP_SLOTS - 1)

    @pl.when(ahead < n_steps)
    def _():
        start_step(ahead, ahead % CMP_SLOTS)

    slot = step % CMP_SLOTS
    for p in range(pps):
        page_copy(b, c * pps, slot, p).wait()

    rows = buf.at[slot].reshape(nb * CMP_PITCH, LANES)
    for j in range(CMP_BLOCK):
        x0 = rows[pl.ds(2 * j, nb, stride=CMP_PITCH), :]
        x1 = rows[pl.ds(2 * j + 1, nb, stride=CMP_PITCH), :]
        pe = pe_ref[j:j + 1, :]
        lhs[:, (2 * j) * HEAD_DIM:(2 * j + 1) * HEAD_DIM] = (x0 + pe).astype(BF16)
        lhs[:, (2 * j + 1) * HEAD_DIM:(2 * j + 2) * HEAD_DIM] = (x1 + pe).astype(BF16)
    hid = jnp.dot(lhs[...], w1_ref[...], preferred_element_type=F32)
    hid = hid * _sigmoid(hid)
    out = jnp.dot(hid.astype(BF16), w2_ref[...], preferred_element_type=F32)
    half = nb // 2
    for h in range(N_KV_HEADS):
        hs = slice(h * HEAD_DIM, (h + 1) * HEAD_DIM)
        res[h] = out[:, hs]
        o_ref[0, 0:half, hs] = res[h, pl.ds(0, half, stride=2), :]
        o_ref[0, half:nb, hs] = res[h, pl.ds(1, half, stride=2), :]


def _compress(page_table, cache, pe, w1, w2, pps):
    nbatch, n_pages = page_table.shape
    n_chunks = n_pages // pps
    nb = pps * BLOCKS_PER_PAGE
    kdim = CMP_BLOCK * D_KV
    return pl.pallas_call(
        functools.partial(_compress_body, pps=pps, n_chunks=n_chunks),
        grid_spec=pltpu.PrefetchScalarGridSpec(
            num_scalar_prefetch=1,
            grid=(nbatch, n_chunks),
            in_specs=[
                pl.BlockSpec(memory_space=pl.ANY),
                pl.BlockSpec((CMP_BLOCK, HEAD_DIM), lambda b, c, pt: (0, 0)),
                pl.BlockSpec((kdim, D_KV), lambda b, c, pt: (0, 0), pipeline_mode=pl.Buffered(1)),
                pl.BlockSpec((D_KV, D_KV), lambda b, c, pt: (0, 0)),
            ],
            out_specs=pl.BlockSpec((1, nb, D_KV), lambda b, c, pt: (b, c, 0)),
            scratch_shapes=[
                pltpu.VMEM((CMP_SLOTS, nb, CMP_PITCH, LANES), F32),
                pltpu.VMEM((nb, kdim), BF16),
                pltpu.VMEM((N_KV_HEADS, nb, HEAD_DIM), F32),
                pltpu.SemaphoreType.DMA((CMP_SLOTS,)),
            ],
        ),
        out_shape=jax.ShapeDtypeStruct((nbatch, n_chunks * nb, D_KV), F32),
        compiler_params=_params(("arbitrary", "arbitrary")),
        name="compress",
    )(page_table, cache, pe, w1, w2)


def _kprep_body(ks_ref, vs_ref, kw_ref, vw_ref, kts_ref, vso_ref, ktw_ref, vwo_ref, *, tk):
    t = pl.program_id(1)
    kpos = t * tk + lax.broadcasted_iota(jnp.int32, (SLC_BLOCK, tk), 1)
    blk = lax.broadcasted_iota(jnp.int32, (SLC_BLOCK, tk), 0)
    onehot = jnp.where(kpos // SLC_BLOCK == blk, 1.0, 0.0).astype(BF16)
    for g in range(N_KV_HEADS):
        head = pl.ds(g, tk, stride=N_KV_HEADS)
        kts_ref[0, g, 0, 0:HEAD_DIM, :] = ks_ref[head, :].T.astype(BF16)
        kts_ref[0, g, 0, HEAD_DIM:HEAD_DIM + SLC_BLOCK, :] = onehot
        kts_ref[0, g, 0, HEAD_DIM + SLC_BLOCK:, :] = jnp.zeros((SLC_BLOCK, tk), BF16)
        kw = kw_ref[head, :]
        for c in range(tk // LANES):
            ktw_ref[0, g, c] = kw[c * LANES:(c + 1) * LANES, :].T.astype(BF16)
        ones = jnp.ones((tk, HEAD_DIM), BF16)
        vso_ref[0, g, :, 0:HEAD_DIM] = vs_ref[head, :].astype(BF16)
        vso_ref[0, g, :, HEAD_DIM:] = ones
        vwo_ref[0, g, :, 0:HEAD_DIM] = vw_ref[head, :].astype(BF16)
        vwo_ref[0, g, :, HEAD_DIM:] = ones


def _kprep(ks, vs, kw, vw, nbatch, seq, tk):
    nt = seq // tk
    src = pl.BlockSpec((tk * N_KV_HEADS, HEAD_DIM), lambda b, t: (b * nt + t, 0))
    return pl.pallas_call(
        functools.partial(_kprep_body, tk=tk),
        grid=(nbatch, nt),
        in_specs=[src, src, src, src],
        out_specs=[
            pl.BlockSpec((1, N_KV_HEADS, 1, 2 * HEAD_DIM, tk), lambda b, t: (b, 0, t, 0, 0)),
            pl.BlockSpec((1, N_KV_HEADS, tk, 2 * HEAD_DIM), lambda b, t: (b, 0, t, 0)),
            pl.BlockSpec((1, N_KV_HEADS, tk // LANES, HEAD_DIM, LANES), lambda b, t: (b, 0, t, 0, 0)),
            pl.BlockSpec((1, N_KV_HEADS, tk, 2 * HEAD_DIM), lambda b, t: (b, 0, t, 0)),
        ],
        out_shape=[
            jax.ShapeDtypeStruct((nbatch, N_KV_HEADS, nt, 2 * HEAD_DIM, tk), BF16),
            jax.ShapeDtypeStruct((nbatch, N_KV_HEADS, seq, 2 * HEAD_DIM), BF16),
            jax.ShapeDtypeStruct((nbatch, N_KV_HEADS, seq // LANES, HEAD_DIM, LANES), BF16),
            jax.ShapeDtypeStruct((nbatch, N_KV_HEADS, seq, 2 * HEAD_DIM), BF16),
        ],
        compiler_params=_params(("arbitrary", "arbitrary")),
        name="kprep",
    )(ks, vs, kw, vw)


def _masked_softmax(s, mask):
    s = jnp.where(mask, s, NEG)
    mx = jnp.max(s, axis=-1, keepdims=True)
    e = jnp.where(mask, jnp.exp2(s - mx), 0.0)
    l = jnp.sum(e, axis=-1, keepdims=True)
    return e / jnp.where(l > 0.0, l, 1.0)


def _select_bias(p_slc_t, qpos_row, n_slc):
    shape = p_slc_t.shape
    blk = lax.broadcasted_iota(jnp.int32, shape, 0)
    valid = blk * SLC_BLOCK <= qpos_row
    cur = qpos_row // SLC_BLOCK
    forced = (blk == 0) | (blk == cur) | (blk == cur - 1)
    score = jnp.where(valid & forced, FORCE, jnp.where(valid, p_slc_t, -FORCE))
    n_chunks = n_slc // SUBLANES
    chunks = [score[c * SUBLANES:(c + 1) * SUBLANES] for c in range(n_chunks)]
    ranks = [jnp.zeros((SUBLANES, shape[1]), F32) for _ in range(n_chunks)]
    sub = lax.broadcasted_iota(jnp.int32, (SUBLANES, shape[1]), 0)
    for j in range(n_slc):
        row = jnp.broadcast_to(score[j:j + 1, :], (SUBLANES, shape[1]))
        for c in range(n_chunks):
            lo = c * SUBLANES
            if lo > j:
                ahead = row >= chunks[c]
            elif lo + SUBLANES - 1 < j:
                ahead = row > chunks[c]
            else:
                ahead = (row > chunks[c]) | ((row == chunks[c]) & (sub + lo > j))
            ranks[c] = ranks[c] + jnp.where(ahead, 1.0, 0.0)
    rank = jnp.concatenate(ranks, axis=0)
    return jnp.where(rank < min(TOP_N, n_slc), 0.0, NEG)


def _attend_tile(q_ref, qcols, kt, v, width, mask_fn, s_ref, m_scr, acc_scr):
    rows = s_ref.shape[0]
    reps = width // LANES
    for blk in range(rows // ROW_BLOCK):
        rb = slice(blk * ROW_BLOCK, (blk + 1) * ROW_BLOCK)
        s_ref[rb, 0:width] = jnp.dot(q_ref[rb, 0:qcols], kt, preferred_element_type=F32)
    for blk in range(rows // ROW_BLOCK):
        alphas, ps = [], []
        for c in range(ROW_BLOCK // ROW_CHUNK):
            r0 = blk * ROW_BLOCK + c * ROW_CHUNK
            r = slice(r0, r0 + ROW_CHUNK)
            s = s_ref[r, 0:width]
            if mask_fn is not None:
                s = jnp.where(mask_fn(r0, ROW_CHUNK), s, NEG)
            m_old = m_scr[r, :]
            m_new = jnp.maximum(m_old, jnp.max(s, axis=-1, keepdims=True))
            ps.append(jnp.exp2(s - jnp.concatenate([m_new] * reps, axis=1)).astype(BF16))
            alphas.append(jnp.exp2(m_old - m_new))
            m_scr[r, :] = m_new
        alpha = jnp.concatenate(alphas, axis=0)
        rb = slice(blk * ROW_BLOCK, (blk + 1) * ROW_BLOCK)
        acc_scr[rb, :] = (jnp.concatenate([alpha, alpha], axis=1) * acc_scr[rb, :]
                          + jnp.dot(jnp.concatenate(ps, axis=0), v, preferred_element_type=F32))


def _attn_body(q_ref, gn_ref, kc_ref, vc_ref, kts_ref, vs_ref, ktw_ref, vw_ref, o_ref,
               qaug, s_scr, m_scr, acc_scr, *, tq, tk, seq):
    q0 = pl.program_id(1) * tq
    n_cmp = seq // CMP_BLOCK
    n_slc = -(-seq // SLC_BLOCK)
    half = n_cmp // 2
    qpos = q0 + lax.broadcasted_iota(jnp.int32, (tq, 1), 0)
    qpos4 = jnp.concatenate([qpos] * GROUP, axis=0)
    qpos_row = q0 + lax.broadcasted_iota(jnp.int32, (n_slc, tq), 1)
    lane = lax.broadcasted_iota(jnp.int32, (1, n_cmp), 1)
    cmp_blk = 2 * (lane % half) + lane // half
    m_cmp = ((cmp_blk + 1) * CMP_BLOCK - 1) <= qpos4
    gates = gn_ref[...].astype(F32)
    w0 = pl.multiple_of(jnp.maximum(q0 - WINDOW, 0), LANES)
    last = (q0 + tq + tk - 1) // tk - 1

    def chunk_qpos(r0, n):
        return q0 + r0 % tq + lax.broadcasted_iota(jnp.int32, (n, 1), 0)

    def reset(g):
        m_scr[g] = jnp.full(m_scr.shape[1:], -jnp.inf, F32)
        acc_scr[g] = jnp.zeros(acc_scr.shape[1:], F32)

    def result(g):
        acc = acc_scr[g]
        return acc[:, 0:HEAD_DIM] / acc[:, HEAD_DIM:]

    def attend(g, qcols, kt, v, width, mask_fn):
        _attend_tile(qaug.at[g], qcols, kt, v, width, mask_fn, s_scr.at[g], m_scr.at[g], acc_scr.at[g])

    groups = range(N_KV_HEADS)

    o_cmp = []
    for g in groups:
        hs = slice(g * HEAD_DIM, (g + 1) * HEAD_DIM)
        for h in range(GROUP):
            qaug[g, h * tq:(h + 1) * tq, 0:HEAD_DIM] = (
                q_ref[:, (GROUP * g + h) * HEAD_DIM:(GROUP * g + h + 1) * HEAD_DIM])
        kc = kc_ref[0][:, hs].astype(BF16)
        vc = vc_ref[0][:, hs].astype(BF16)
        s = lax.dot_general(qaug[g, :, 0:HEAD_DIM], kc, (((1,), (1,)), ((), ())), preferred_element_type=F32)
        p = _masked_softmax(s, m_cmp)
        o_cmp.append(jnp.dot(p.astype(BF16), vc, preferred_element_type=F32))
        p_grp = p[0:tq]
        for h in range(1, GROUP):
            p_grp = p_grp + p[h * tq:(h + 1) * tq]
        p_slc = p_grp + pltpu.roll(p_grp, half, 1)
        bias_t = _select_bias(p_slc.T[0:n_slc], qpos_row, n_slc)
        bias = jnp.concatenate([bias_t, jnp.zeros((LANES - n_slc, tq), F32)], axis=0).T.astype(BF16)
        for h in range(GROUP):
            qaug[g, h * tq:(h + 1) * tq, HEAD_DIM:] = bias
        reset(g)

    def interior(t, carry):
        k0 = pl.multiple_of(t * tk, tk)
        for g in groups:
            attend(g, 2 * HEAD_DIM, kts_ref[0, g, t], vs_ref[0, g, pl.ds(k0, tk), :], tk, None)
        return carry

    lax.fori_loop(0, last, interior, 0)
    k_last = pl.multiple_of(last * tk, tk)

    def causal(r0, n):
        return k_last + lax.broadcasted_iota(jnp.int32, (1, tk), 1) <= chunk_qpos(r0, n)

    o_sel = []
    for g in groups:
        attend(g, 2 * HEAD_DIM, kts_ref[0, g, last], vs_ref[0, g, pl.ds(k_last, tk), :], tk, causal)
        o_sel.append(result(g))
        reset(g)

    span = WINDOW + tq

    def in_window(r0, n):
        dist = chunk_qpos(r0, n) - (w0 + lax.broadcasted_iota(jnp.int32, (1, span), 1))
        return (dist >= 0) & (dist <= WINDOW)

    for g in groups:
        ktw = jnp.concatenate([ktw_ref[0, g, w0 // LANES + c] for c in range(span // LANES)], axis=1)
        attend(g, HEAD_DIM, ktw, vw_ref[0, g, pl.ds(w0, span), :], span, in_window)

    for g in groups:
        o_win = result(g)
        for h in range(GROUP):
            hd = GROUP * g + h
            r = slice(h * tq, (h + 1) * tq)
            o_ref[:, hd * HEAD_DIM:(hd + 1) * HEAD_DIM] = (
                gates[:, hd:hd + 1] * o_cmp[g][r]
                + gates[:, N_HEADS + hd:N_HEADS + hd + 1] * o_sel[g][r]
                + gates[:, 2 * N_HEADS + hd:2 * N_HEADS + hd + 1] * o_win[r]).astype(o_ref.dtype)


def _prompt_attn(h_q, h_sig, kc, vc, kts, vs, ktw, vw, nbatch, seq, tq, tk):
    nq = seq // tq
    n_cmp = seq // CMP_BLOCK
    nt = seq // tk
    rows = GROUP * tq
    return pl.pallas_call(
        functools.partial(_attn_body, tq=tq, tk=tk, seq=seq),
        grid=(nbatch, nq),
        in_specs=[
            pl.BlockSpec((tq, D_ATTN), lambda b, i: (b * nq + i, 0)),
            pl.BlockSpec((tq, LANES), lambda b, i: (b * nq + i, SIG_GN // LANES)),
            pl.BlockSpec((1, n_cmp, D_KV), lambda b, i: (b, 0, 0)),
            pl.BlockSpec((1, n_cmp, D_KV), lambda b, i: (b, 0, 0)),
            pl.BlockSpec((1, N_KV_HEADS, nt, 2 * HEAD_DIM, tk), lambda b, i: (b, 0, 0, 0, 0)),
            pl.BlockSpec((1, N_KV_HEADS, seq, 2 * HEAD_DIM), lambda b, i: (b, 0, 0, 0)),
            pl.BlockSpec((1, N_KV_HEADS, seq // LANES, HEAD_DIM, LANES), lambda b, i: (b, 0, 0, 0, 0)),
            pl.BlockSpec((1, N_KV_HEADS, seq, 2 * HEAD_DIM), lambda b, i: (b, 0, 0, 0)),
        ],
        out_specs=pl.BlockSpec((tq, D_ATTN), lambda b, i: (b * nq + i, 0)),
        scratch_shapes=[
            pltpu.VMEM((N_KV_HEADS, rows, 2 * HEAD_DIM), BF16),
            pltpu.VMEM((N_KV_HEADS, rows, max(tk, WINDOW + tq)), F32),
            pltpu.VMEM((N_KV_HEADS, rows, LANES), F32),
            pltpu.VMEM((N_KV_HEADS, rows, 2 * HEAD_DIM), F32),
        ],
        out_shape=jax.ShapeDtypeStruct((nbatch * seq, D_ATTN), BF16),
        compiler_params=_params(("arbitrary", "arbitrary")),
        name="attn",
    )(h_q, h_sig, kc, vc, kts, vs, ktw, vw)


def _group_queries(q_row, g):
    heads = [q_row[:, (GROUP * g + h) * HEAD_DIM:(GROUP * g + h + 1) * HEAD_DIM] for h in range(GROUP)]
    pad = jnp.zeros((2 * SUBLANES - GROUP, HEAD_DIM), F32)
    return jnp.concatenate(heads + [pad], axis=0).astype(BF16)


def _scmp_body(q_ref, kc_ref, vc_ref, ocmp_ref, pslc_ref, *, q_pos, chunk):
    b = pl.program_id(0)
    q_row = q_ref[pl.ds(b, 1), :]
    n_cmp = kc_ref.shape[1]
    half = chunk // 2
    lane = lax.broadcasted_iota(jnp.int32, (1, n_cmp), 1)
    within = lane % chunk
    cmp_blk = (lane // chunk) * chunk + 2 * (within % half) + within // half
    m_cmp = ((cmp_blk + 1) * CMP_BLOCK - 1) <= q_pos
    outs = []
    for g in range(N_KV_HEADS):
        hs = slice(g * HEAD_DIM, (g + 1) * HEAD_DIM)
        qg = _group_queries(q_row, g)
        s = lax.dot_general(qg, kc_ref[0][:, hs].astype(BF16), (((1,), (1,)), ((), ())), preferred_element_type=F32)
        p = _masked_softmax(s, m_cmp)
        o = jnp.dot(p.astype(BF16), vc_ref[0][:, hs].astype(BF16), preferred_element_type=F32)
        outs.append(o[0:GROUP])
        p_grp = p[0:1]
        for h in range(1, GROUP):
            p_grp = p_grp + p[h:h + 1]
        parts = []
        for c in range(n_cmp // chunk):
            pc = p_grp[:, c * chunk:(c + 1) * chunk]
            parts.append(pc + pltpu.roll(pc, half, 1))
        pslc_ref[0, g:g + 1, :] = jnp.concatenate(parts, axis=1)
    ocmp_ref[0] = jnp.concatenate(outs, axis=0)


def _sample_cmp(q, kc, vc, q_pos, chunk):
    nbatch, n_cmp, _ = kc.shape
    return pl.pallas_call(
        functools.partial(_scmp_body, q_pos=q_pos, chunk=chunk),
        grid=(nbatch,),
        in_specs=[
            pl.BlockSpec((nbatch, D_ATTN), lambda b: (0, 0)),
            pl.BlockSpec((1, n_cmp, D_KV), lambda b: (b, 0, 0)),
            pl.BlockSpec((1, n_cmp, D_KV), lambda b: (b, 0, 0)),
        ],
        out_specs=[
            pl.BlockSpec((1, N_HEADS, HEAD_DIM), lambda b: (b, 0, 0)),
            pl.BlockSpec((1, N_KV_HEADS, n_cmp), lambda b: (b, 0, 0)),
        ],
        out_shape=[
            jax.ShapeDtypeStruct((nbatch, N_HEADS, HEAD_DIM), F32),
            jax.ShapeDtypeStruct((nbatch, N_KV_HEADS, n_cmp), F32),
        ],
        compiler_params=_params(("arbitrary",)),
        name="s_cmp",
    )(q, kc, vc)


def _stopk_body(p_ref, idx_ref, *, q_pos, chunk, n_slc):
    p = p_ref[...]
    rows, width = p.shape
    half = chunk // 2
    n_in = (width // chunk) * half
    lane = lax.broadcasted_iota(jnp.int32, (1, width), 1)
    within = lane % chunk
    blk = jnp.where(within < half, (lane // chunk) * half + within, -1)
    extra = (lane >= half) & (lane < half + (n_slc - n_in))
    blk = jnp.where(extra, n_in + lane - half, blk)
    real = blk >= 0
    valid = real & (blk * SLC_BLOCK <= q_pos)
    cur = q_pos // SLC_BLOCK
    forced = (blk == 0) | (blk == cur) | (blk == cur - 1)
    base = jnp.where(extra, 0.0, p)
    x = jnp.where(valid & forced, FORCE, jnp.where(valid, base, -FORCE))
    x = jnp.where(real, x, -jnp.inf)
    blk_f = blk.astype(F32)
    out_lane = lax.broadcasted_iota(jnp.int32, (rows, LANES), 1)
    out = jnp.zeros((rows, LANES), F32)
    for r in range(min(TOP_N, n_slc)):
        mx = jnp.max(x, axis=-1, keepdims=True)
        pick = jnp.min(jnp.where(x == mx, blk_f, float(2 ** 30)), axis=-1, keepdims=True)
        out = jnp.where(out_lane == r, pick, out)
        x = jnp.where(blk_f == pick, -jnp.inf, x)
    idx_ref[...] = out.astype(jnp.int32)


def _sample_topk(pslc, q_pos, chunk, n_slc):
    rows, width = pslc.shape
    return pl.pallas_call(
        functools.partial(_stopk_body, q_pos=q_pos, chunk=chunk, n_slc=n_slc),
        grid=(1,),
        in_specs=[pl.BlockSpec((rows, width), lambda i: (0, 0))],
        out_specs=pl.BlockSpec((rows, LANES), lambda i: (0, 0)),
        out_shape=jax.ShapeDtypeStruct((rows, LANES), jnp.int32),
        compiler_params=_params(("arbitrary",)),
        name="s_topk",
    )(pslc)


def _sattn_body(idx_ref, pt_ref, q_ref, gn_ref, ocmp_ref, knew_ref, vnew_ref, kwn_ref, vwn_ref, wk_ref, wv_ref,
                sk_hbm, sv_hbm, o_ref, kbuf, vbuf, sem, *, q_pos, n_top, n_pages, past, per_step):
    step = pl.program_id(0)
    n_steps = pl.num_programs(0)
    half_rows = SLC_BLOCK * N_KV_HEADS

    def copies(s, slot, sub, g, i):
        bb = s * per_step + sub
        blk = idx_ref[bb * N_KV_HEADS + g, i]
        page = jnp.minimum(blk // 2, n_pages - 1)
        start = pl.multiple_of(pt_ref[bb, page] * (PAGE_SIZE * N_KV_HEADS) + (blk % 2) * half_rows, half_rows)
        return [pltpu.make_async_copy(hbm.at[pl.ds(start, half_rows), :], dst.at[slot, sub, g, i],
                                      sem.at[t, slot, sub, g, i])
                for t, (hbm, dst) in enumerate(((sk_hbm, kbuf), (sv_hbm, vbuf)))]

    def all_copies(s, slot):
        return [cp for sub in range(per_step) for g in range(N_KV_HEADS) for i in range(n_top)
                for cp in copies(s, slot, sub, g, i)]

    @pl.when(step == 0)
    def _():
        def one(n, carry):
            sub, rest = n // (N_KV_HEADS * n_top), n % (N_KV_HEADS * n_top)
            for cp in copies(0, 0, sub, rest // n_top, rest % n_top):
                cp.start()
            return carry
        lax.fori_loop(0, per_step * N_KV_HEADS * n_top, one, 0)

    @pl.when(step + 1 < n_steps)
    def _():
        for cp in all_copies(step + 1, (step + 1) % 2):
            cp.start()

    slot = step % 2
    for cp in all_copies(step, slot):
        cp.wait()

    pad = jnp.zeros((2 * SUBLANES - 1, HEAD_DIM), F32)
    for sub in range(per_step):
        _sattn_one(step * per_step + sub, slot, sub, pad, idx_ref, q_ref, gn_ref, ocmp_ref, knew_ref, vnew_ref, kwn_ref,
                   vwn_ref, wk_ref, wv_ref, o_ref, kbuf, vbuf, q_pos=q_pos, n_top=n_top, past=past)


def _sattn_one(b, slot, sub, pad, idx_ref, q_ref, gn_ref, ocmp_ref, knew_ref, vnew_ref, kwn_ref, vwn_ref, wk_ref, wv_ref,
               o_ref, kbuf, vbuf, *, q_pos, n_top, past):
    tail_blk = past // SLC_BLOCK
    q_row = q_ref[pl.ds(b, 1), :]
    gates = gn_ref[pl.ds(b, 1), :]

    def new_row(ref, g):
        return jnp.concatenate([ref[pl.ds(b * N_KV_HEADS + g, 1), :], pad], axis=0).astype(BF16)

    outs = []
    for g in range(N_KV_HEADS):
        qg = _group_queries(q_row, g)
        ks, vs = [], []
        key = lax.broadcasted_iota(jnp.int32, (1, n_top * SLC_BLOCK), 1)
        key_blk = jnp.zeros((1, n_top * SLC_BLOCK), jnp.int32)
        for i in range(n_top):
            blk = idx_ref[b * N_KV_HEADS + g, i]
            ks.append(kbuf[slot, sub, g, i, pl.ds(g, SLC_BLOCK, stride=N_KV_HEADS), :])
            vs.append(vbuf[slot, sub, g, i, pl.ds(g, SLC_BLOCK, stride=N_KV_HEADS), :])
            key_blk = jnp.where(key // SLC_BLOCK == i, blk, key_blk)
        k_sel = jnp.concatenate(ks, axis=0).astype(BF16)
        v_sel = jnp.concatenate(vs, axis=0).astype(BF16)
        m_sel = (key_blk * SLC_BLOCK + key % SLC_BLOCK <= q_pos) & (key_blk < tail_blk)
        tail_sel = jnp.max(jnp.where(key_blk == tail_blk, 1.0, 0.0), axis=-1, keepdims=True) > 0.5
        first = lax.broadcasted_iota(jnp.int32, (1, 2 * SUBLANES), 1) == 0
        o_sel = _two_part_attention(qg, k_sel, v_sel, m_sel, new_row(knew_ref, g), new_row(vnew_ref, g),
                                    first & tail_sel)
        wbuf = wk_ref.shape[1] // N_KV_HEADS
        k_win = wk_ref[sub, pl.ds(g, wbuf, stride=N_KV_HEADS), :].astype(BF16)
        v_win = wv_ref[sub, pl.ds(g, wbuf, stride=N_KV_HEADS), :].astype(BF16)
        dist = q_pos - (past - wbuf + lax.broadcasted_iota(jnp.int32, (1, wbuf), 1))
        m_win = (dist >= 0) & (dist <= WINDOW)
        o_win = _two_part_attention(qg, k_win, v_win, m_win, new_row(kwn_ref, g), new_row(vwn_ref, g), first)
        o_cmp = ocmp_ref[sub, GROUP * g:GROUP * (g + 1), :]
        for h in range(GROUP):
            hd = GROUP * g + h
            outs.append(gates[:, hd:hd + 1] * o_cmp[h:h + 1]
                        + gates[:, N_HEADS + hd:N_HEADS + hd + 1] * o_sel[h:h + 1]
                        + gates[:, 2 * N_HEADS + hd:2 * N_HEADS + hd + 1] * o_win[h:h + 1])
    o_ref[sub] = jnp.concatenate(outs, axis=0)


def _two_part_attention(q, k1, v1, m1, k2, v2, m2):
    dn = (((1,), (1,)), ((), ()))
    s1 = jnp.where(m1, lax.dot_general(q, k1, dn, preferred_element_type=F32), NEG)
    s2 = jnp.where(m2, lax.dot_general(q, k2, dn, preferred_element_type=F32), NEG)
    mx = jnp.maximum(jnp.max(s1, axis=-1, keepdims=True), jnp.max(s2, axis=-1, keepdims=True))
    e1 = jnp.where(m1, jnp.exp2(s1 - mx), 0.0)
    e2 = jnp.where(m2, jnp.exp2(s2 - mx), 0.0)
    l = jnp.sum(e1, axis=-1, keepdims=True) + jnp.sum(e2, axis=-1, keepdims=True)
    inv = 1.0 / jnp.where(l > 0.0, l, 1.0)
    o = (jnp.dot((e1 * inv).astype(BF16), v1, preferred_element_type=F32)
         + jnp.dot((e2 * inv).astype(BF16), v2, preferred_element_type=F32))
    return o


def _sample_attn(idx, page_table, h_q, h_sig, ocmp, new_rows, win_k, win_v, slc_k, slc_v, q_pos, past):
    nbatch, n_pages = page_table.shape
    n_top = min(TOP_N, past // SLC_BLOCK + 1)
    half_rows = SLC_BLOCK * N_KV_HEADS
    flat_new = pl.BlockSpec((nbatch * N_KV_HEADS, HEAD_DIM), lambda b, idx, pt: (0, 0))

    def whole(col_block, width):
        return pl.BlockSpec((nbatch, width), lambda b, idx, pt: (0, col_block))

    per_step = SATTN_PER_STEP
    return pl.pallas_call(
        functools.partial(_sattn_body, q_pos=q_pos, n_top=n_top, n_pages=n_pages, past=past, per_step=per_step),
        grid_spec=pltpu.PrefetchScalarGridSpec(
            num_scalar_prefetch=2,
            grid=(nbatch // per_step,),
            in_specs=[
                whole(0, D_ATTN),
                whole(SIG_GN // LANES, LANES),
                pl.BlockSpec((per_step, N_HEADS, HEAD_DIM), lambda b, idx, pt: (b, 0, 0)),
                flat_new, flat_new, flat_new, flat_new,
                pl.BlockSpec((per_step,) + win_k.shape[1:], lambda b, idx, pt: (b, 0, 0)),
                pl.BlockSpec((per_step,) + win_v.shape[1:], lambda b, idx, pt: (b, 0, 0)),
                pl.BlockSpec(memory_space=pl.ANY),
                pl.BlockSpec(memory_space=pl.ANY),
            ],
            out_specs=pl.BlockSpec((per_step, N_HEADS, HEAD_DIM), lambda b, idx, pt: (b, 0, 0)),
            scratch_shapes=[
                pltpu.VMEM((2, per_step, N_KV_HEADS, n_top, half_rows, LANES), F32),
                pltpu.VMEM((2, per_step, N_KV_HEADS, n_top, half_rows, LANES), F32),
                pltpu.SemaphoreType.DMA((2, 2, per_step, N_KV_HEADS, n_top)),
            ],
        ),
        out_shape=jax.ShapeDtypeStruct((nbatch, N_HEADS, HEAD_DIM), F32),
        compiler_params=_params(("arbitrary",)),
        name="s_attn",
    )(idx, page_table, h_q, h_sig, ocmp, *new_rows, win_k, win_v, slc_k, slc_v)


REF_KV = 3 * D_POOL
REF_GN = REF_KV + N_KV_PROJ * D_KV
REF_ZN = REF_GN + N_GATE
REF_GM = REF_ZN + D_ATTN


def _tiles(start, width):
    return tuple(range(start, start + width, PROJ_TN))


STARTS_U = _tiles(0, D_POOL)
STARTS_SILU = _tiles(D_POOL, D_POOL) + _tiles(REF_ZN, D_ATTN)
STARTS_Q = _tiles(2 * D_POOL, D_ATTN)
STARTS_SIG = _tiles(REF_GM, N_MERGE_COLS) + (REF_GN,)


def _block_diag2(w):
    z = jnp.zeros_like(w)
    return jnp.concatenate([jnp.concatenate([w, z], axis=-1), jnp.concatenate([z, w], axis=-1)], axis=-2)


def _project(x, wt, b, tm, tm_kv, act_dtype):
    u = _proj(x, wt, b, STARTS_U, ACT_NONE, tm, F32)
    h_silu = _proj(x, wt, b, STARTS_SILU, ACT_SILU, tm, act_dtype)
    h_q = _proj(x, wt, b, STARTS_Q, ACT_SCALE, tm, act_dtype)
    h_sig = _proj(x, wt, b, STARTS_SIG, ACT_SIGMOID, tm, act_dtype)
    kv = _proj_kv(x, wt, b, REF_KV, tm_kv)
    return u, h_silu, h_q, h_sig, kv


def kernel(x_prompt, x_sample, cache_cmp_k, cache_cmp_v, cache_slc_k, cache_slc_v, cache_win_k, cache_win_v,
           state_pool, page_table, w_in, b_in, pool_w, pool_scale, cmp_pe_k, cmp_w1_k, cmp_w2_k, cmp_pe_v, cmp_w1_v,
           cmp_w2_v, w_up_pool, w_up_nsa, w_out, ln_g, ln_b):
    nb_p, seq, _ = x_prompt.shape
    nb_s = x_sample.shape[0]
    n_pages = page_table.shape[1]
    past = n_pages * PAGE_SIZE
    n_phys = cache_cmp_k.shape[1]
    wbuf = cache_win_k.shape[2]

    wt = w_in[0].T
    b = b_in[0][None, :]
    kdim = CMP_BLOCK * D_KV
    cmp_k = (cmp_pe_k[0], _block_diag2(cmp_w1_k[0]).reshape(kdim, D_KV).astype(BF16),
             _block_diag2(cmp_w2_k[0]).astype(BF16))
    cmp_v = (cmp_pe_v[0], _block_diag2(cmp_w1_v[0]).reshape(kdim, D_KV).astype(BF16),
             _block_diag2(cmp_w2_v[0]).astype(BF16))
    pw = pool_w[0].astype(BF16)
    ps = pool_scale[0][None, :]
    wup = w_up_pool[0].astype(BF16)
    wun = w_up_nsa[0].astype(BF16)
    wo = w_out[0].astype(BF16)
    lg = ln_g[0][None, :]
    lb = ln_b[0][None, :]

    xp = x_prompt.reshape(nb_p * seq, D_MODEL)
    xs = x_sample.reshape(nb_s, D_MODEL)
    u_p, hsilu_p, hq_p, hsig_p, kv_p = _project(xp.astype(BF16), wt, b, tm=2048, tm_kv=2048, act_dtype=BF16)
    u_s, hsilu_s, hq_s, hsig_s, kv_s = _project(xs.astype(BF16), wt, b, tm=nb_s, tm_kv=nb_s, act_dtype=F32)

    blocks = (-1, CMP_ROWS, LANES)
    prompt_pages = seq // PAGE_SIZE
    ident = jnp.arange(nb_p * prompt_pages, dtype=jnp.int32).reshape(nb_p, prompt_pages)
    kc_p = _compress(ident, kv_p[0].reshape(blocks), *cmp_k, pps=prompt_pages)
    vc_p = _compress(ident, kv_p[1].reshape(blocks), *cmp_v, pps=prompt_pages)
    kts, vs, ktw, vw = _kprep(kv_p[2], kv_p[3], kv_p[4], kv_p[5], nb_p, seq, tk=512)
    o_p = _prompt_attn(hq_p, hsig_p, kc_p, vc_p, kts, vs, ktw, vw, nb_p, seq, tq=256, tk=512)
    y_p = _tail(u_p, hsilu_p, hsig_p, o_p, xp, pw, ps, wup, wun, wo, lg, lb, tm=512, seq=seq)

    chunk_pages = 64
    kc_s = _compress(page_table, cache_cmp_k.reshape(blocks), *cmp_k, pps=chunk_pages)
    vc_s = _compress(page_table, cache_cmp_v.reshape(blocks), *cmp_v, pps=chunk_pages)
    chunk = chunk_pages * BLOCKS_PER_PAGE
    n_slc = past // SLC_BLOCK + 1
    ocmp_s, pslc_s = _sample_cmp(hq_s, kc_s, vc_s, past, chunk)
    idx = _sample_topk(pslc_s.reshape(nb_s * N_KV_HEADS, -1), past, chunk, n_slc)
    flat = (n_phys * PAGE_SIZE * N_KV_HEADS, HEAD_DIM)
    o_s = _sample_attn(idx, page_table, hq_s, hsig_s, ocmp_s, kv_s[2:6],
                       cache_win_k.reshape(nb_s, wbuf * N_KV_HEADS, HEAD_DIM),
                       cache_win_v.reshape(nb_s, wbuf * N_KV_HEADS, HEAD_DIM),
                       cache_slc_k.reshape(flat), cache_slc_v.reshape(flat), past, past)
    ctx = jnp.concatenate([state_pool[0], u_s[:, None, :]], axis=1)
    m_s = _pool_m(ctx.reshape(nb_s * (POOL_CTX + 1), D_POOL), 0, nb_s * (POOL_CTX + 1), tm=nb_s * (POOL_CTX + 1),
                  seq=nb_s * (POOL_CTX + 1), fixed_pos=past)
    m_s = m_s.reshape(nb_s, POOL_CTX + 1, D_POOL)[:, POOL_CTX]
    y_s = _tail(m_s, hsilu_s, hsig_s, o_s.reshape(nb_s, D_ATTN), xs, pw, ps, wup, wun, wo, lg, lb, tm=nb_s)

    wl = min(WINDOW, seq)
    kv_p = [a.reshape(1, nb_p, seq, N_KV_HEADS, HEAD_DIM) for a in kv_p]
    kv_s = [a.reshape(1, nb_s, 1, N_KV_HEADS, HEAD_DIM) for a in kv_s]
    return (
        y_p.reshape(nb_p, seq, D_MODEL),
        y_s.reshape(nb_s, 1, D_MODEL),
        kv_p[0], kv_p[1], kv_p[2], kv_p[3],
        kv_p[4][:, :, seq - wl:], kv_p[5][:, :, seq - wl:],
        u_p.reshape(nb_p, seq, D_POOL)[None, :, seq - POOL_CTX:],
        kv_s[0], kv_s[1], kv_s[2], kv_s[3],
        jnp.concatenate([cache_win_k, kv_s[4]], axis=2)[:, :, 1:],
        jnp.concatenate([cache_win_v, kv_s[5]], axis=2)[:, :, 1:],
        ctx[None, :, 1:],
    )
```

```python
import functools

import jax
import jax.numpy as jnp
import numpy as np
from jax import lax
from jax.experimental import pallas as pl
from jax.experimental.pallas import tpu as pltpu

D_MODEL = 2048
D_POOL = 1024
POOL_WINDOWS = (2, 4, 8, 16)
POOL_GC = D_POOL // len(POOL_WINDOWS)
POOL_CTX = max(POOL_WINDOWS) - 1
HEAD_DIM = 128
N_HEADS = 8
N_KV_HEADS = 2
GROUP = N_HEADS // N_KV_HEADS
D_ATTN = N_HEADS * HEAD_DIM
D_KV = N_KV_HEADS * HEAD_DIM
CMP_BLOCK = 32
SLC_BLOCK = 64
TOP_N = 16
WINDOW = 512
PAGE_SIZE = 128
ATTN_SCALE = HEAD_DIM ** -0.5
DEPTH = 1
ALPHA = (2.0 * DEPTH) ** 0.25
LN_EPS = 1e-5
NEG = -1e30
FORCE = 1e6

F32 = jnp.float32
BF16 = jnp.bfloat16

SUBLANES = 8
LANES = 128
VMEM_LIMIT_BYTES = 56 * 1024 * 1024

PROJ_TN = 512
PROJ_RC = 512
N_MERGE_COLS = 2 * D_MODEL
SIG_GN = N_MERGE_COLS
N_GATE = 3 * N_HEADS
N_KV_PROJ = 6
ACT_NONE, ACT_SILU, ACT_SIGMOID, ACT_SCALE = range(4)
LOG2E = 1.4426950408889634
SATTN_PER_STEP = 2
TAIL_RC = 128
ROW_BLOCK = 256
ROW_CHUNK = 32

CMP_ROWS = CMP_BLOCK * N_KV_HEADS
CMP_PITCH = 72
BLOCKS_PER_PAGE = PAGE_SIZE // CMP_BLOCK
CMP_SLOTS = 3
RIDER_COLS = 1024
RIDER_EVERY = 4


def _params(sem):
    return pltpu.CompilerParams(dimension_semantics=sem, vmem_limit_bytes=VMEM_LIMIT_BYTES)


def _sigmoid(x):
    return 1.0 / (1.0 + jnp.exp(-x))


def _activate(acc, act):
    if act == ACT_SILU:
        return acc * _sigmoid(acc)
    if act == ACT_SIGMOID:
        return _sigmoid(acc)
    if act == ACT_SCALE:
        return acc * (ATTN_SCALE * LOG2E)
    return acc


def _xwt(x, wt, b):
    return lax.dot_general(x, wt, (((1,), (1,)), ((), ())), preferred_element_type=F32) + b


def _wt_spec(tn):
    return pl.BlockSpec((pl.Element(tn), pl.Element(D_MODEL)), lambda i, j, starts8: (starts8[j] * SUBLANES, 0))


def _proj_body(starts8_ref, x_ref, wt_ref, b_ref, o_ref, *, act, rc):
    wt = wt_ref[...].astype(BF16)
    for c in range(x_ref.shape[0] // rc):
        r = slice(c * rc, (c + 1) * rc)
        o_ref[r, :] = _activate(_xwt(x_ref[r, :], wt, b_ref[...]), act).astype(o_ref.dtype)


def _proj(x, wt, b, starts, act, tm, out_dtype):
    m = x.shape[0]
    starts8 = jnp.asarray([s // SUBLANES for s in starts], jnp.int32)
    b = jnp.concatenate([b[:, s:s + PROJ_TN] for s in starts], axis=1)
    return pl.pallas_call(
        functools.partial(_proj_body, act=act, rc=min(tm, PROJ_RC)),
        grid_spec=pltpu.PrefetchScalarGridSpec(
            num_scalar_prefetch=1,
            grid=(m // tm, len(starts)),
            in_specs=[
                pl.BlockSpec((tm, D_MODEL), lambda i, j, starts8: (i, 0)),
                _wt_spec(PROJ_TN),
                pl.BlockSpec((1, PROJ_TN), lambda i, j, starts8: (0, j)),
            ],
            out_specs=pl.BlockSpec((tm, PROJ_TN), lambda i, j, starts8: (i, j)),
        ),
        out_shape=jax.ShapeDtypeStruct((m, len(starts) * PROJ_TN), out_dtype),
        compiler_params=_params(("arbitrary", "arbitrary")),
        name="proj",
    )(starts8, x, wt, b)


def _proj_kv_body(starts8_ref, x_ref, wt_ref, b_ref, *o_refs, rc):
    j = pl.program_id(1)
    wt = wt_ref[...].astype(BF16)
    per_tile = PROJ_TN // D_KV
    for k in range(len(o_refs) // per_tile):
        @pl.when(j == k)
        def _(k=k):
            for c in range(x_ref.shape[0] // rc):
                acc = _xwt(x_ref[c * rc:(c + 1) * rc, :], wt, b_ref[...])
                for n in range(per_tile * N_KV_HEADS):
                    o_ref = o_refs[per_tile * k + n // N_KV_HEADS]
                    o_ref[pl.ds(N_KV_HEADS * c * rc + n % N_KV_HEADS, rc, stride=N_KV_HEADS), :] = (
                        acc[:, n * HEAD_DIM:(n + 1) * HEAD_DIM])


def _proj_kv(x, wt, b, start, tm):
    m = x.shape[0]
    flat = jax.ShapeDtypeStruct((m * N_KV_HEADS, HEAD_DIM), F32)
    n_tiles = N_KV_PROJ * D_KV // PROJ_TN
    starts8 = jnp.asarray([(start + k * PROJ_TN) // SUBLANES for k in range(n_tiles)], jnp.int32)
    b = b[:, start:start + N_KV_PROJ * D_KV]
    return pl.pallas_call(
        functools.partial(_proj_kv_body, rc=min(tm, PROJ_RC)),
        grid_spec=pltpu.PrefetchScalarGridSpec(
            num_scalar_prefetch=1,
            grid=(m // tm, n_tiles),
            in_specs=[
                pl.BlockSpec((tm, D_MODEL), lambda i, j, starts8: (i, 0)),
                _wt_spec(PROJ_TN),
                pl.BlockSpec((1, PROJ_TN), lambda i, j, starts8: (0, j)),
            ],
            out_specs=[pl.BlockSpec((tm * N_KV_HEADS, HEAD_DIM), lambda i, j, starts8: (i, 0))] * N_KV_PROJ,
        ),
        out_shape=[flat] * N_KV_PROJ,
        compiler_params=_params(("arbitrary", "arbitrary")),
        name="proj_kv",
    )(starts8, x, wt, b)


def _window_means(halo, u, row0, fixed_pos):
    tm = u.shape[0]
    ext = jnp.concatenate([halo, u], axis=0)
    if fixed_pos is None:
        pos = row0 + lax.broadcasted_iota(jnp.int32, (tm, 1), 0)
    else:
        pos = jnp.full((tm, 1), fixed_pos, jnp.int32)
    outs = []
    for g, w in enumerate(POOL_WINDOWS):
        a = ext[:, g * POOL_GC:(g + 1) * POOL_GC]
        s = a
        k = 1
        while k < w:
            s = s + pltpu.roll(s, k, 0)
            k *= 2
        cnt = jnp.minimum(pos + 1, w).astype(F32)
        outs.append((s[2 * SUBLANES:] / cnt - a[2 * SUBLANES:]).astype(BF16))
    return jnp.concatenate(outs, axis=1)


def _pool_body(halo_ref, u_ref, m_ref, *, tm, seq, fixed_pos):
    row0 = (pl.program_id(0) * tm) % seq
    halo = jnp.where(row0 == 0, 0.0, halo_ref[...])
    m_ref[...] = _window_means(halo, u_ref[...], row0, fixed_pos)


def _pool_m(src, col_block, rows, tm, seq, fixed_pos):
    halo_rows = 2 * SUBLANES
    per = tm // halo_rows
    return pl.pallas_call(
        functools.partial(_pool_body, tm=tm, seq=seq, fixed_pos=fixed_pos),
        grid=(rows // tm,),
        in_specs=[
            pl.BlockSpec((halo_rows, D_POOL), lambda i: (jnp.maximum(i * per - 1, 0), col_block)),
            pl.BlockSpec((tm, D_POOL), lambda i: (i, col_block)),
        ],
        out_specs=pl.BlockSpec((tm, D_POOL), lambda i: (i, 0)),
        out_shape=jax.ShapeDtypeStruct((rows, D_POOL), BF16),
        compiler_params=_params(("arbitrary",)),
        name="pool",
    )(src, src)


def _tail_body(*refs, seq):
    if seq is None:
        m = refs[0][...]
        refs = refs[1:]
    else:
        halo_ref, u_ref = refs[:2]
        refs = refs[2:]
        row0 = (pl.program_id(0) * u_ref.shape[0]) % seq
        m = _window_means(jnp.where(row0 == 0, 0.0, halo_ref[...]), u_ref[...], row0, None)
    zp_ref, o_ref, zn_ref, ga_ref, gb_ref, x_ref, pw_ref, ps_ref, wup_ref, wun_ref, wo_ref, lg_ref, lb_ref, y_ref = refs
    tm = x_ref.shape[0]
    rc = min(tm, TAIL_RC)
    for c in range(tm // rc):
        r = slice(c * rc, (c + 1) * rc)
        ys = [jnp.dot(m[r, g * POOL_GC:(g + 1) * POOL_GC], pw_ref[g], preferred_element_type=F32)
              for g in range(len(POOL_WINDOWS))]
        y_pool = jnp.concatenate(ys, axis=1) * ps_ref[...]
        a = jnp.dot((y_pool * zp_ref[r, :]).astype(BF16), wup_ref[...], preferred_element_type=F32)
        b = jnp.dot((o_ref[r, :] * zn_ref[r, :]).astype(BF16), wun_ref[...], preferred_element_type=F32)
        mix = ga_ref[r, :] * a + gb_ref[r, :] * b
        h = jnp.dot(mix.astype(BF16), wo_ref[...], preferred_element_type=F32)
        z = ALPHA * x_ref[r, :] + h
        mu = jnp.mean(z, axis=-1, keepdims=True)
        zc = z - mu
        var = jnp.mean(zc * zc, axis=-1, keepdims=True)
        y_ref[r, :] = zc * lax.rsqrt(var + LN_EPS) * lg_ref[...] + lb_ref[...]


def _tail(pool_in, h_silu, h_sig, o, x, pw, ps, wup, wun, wo, lg, lb, tm, seq=None):
    rows = x.shape[0]
    once = pl.Buffered(1)

    def const(shape):
        return pl.BlockSpec(shape, lambda i: (0,) * len(shape), pipeline_mode=once)

    pool_specs = [pl.BlockSpec((tm, D_POOL), lambda i: (i, 0))]
    pool_args = [pool_in]
    if seq is not None:
        halo_rows = 2 * SUBLANES
        per = tm // halo_rows
        pool_specs.insert(0, pl.BlockSpec((halo_rows, D_POOL), lambda i: (jnp.maximum(i * per - 1, 0), 0)))
        pool_args.insert(0, pool_in)
    return pl.pallas_call(
        functools.partial(_tail_body, seq=seq),
        grid=(rows // tm,),
        in_specs=pool_specs + [
            pl.BlockSpec((tm, D_POOL), lambda i: (i, 0)),
            pl.BlockSpec((tm, D_ATTN), lambda i: (i, 0)),
            pl.BlockSpec((tm, D_ATTN), lambda i: (i, D_POOL // D_ATTN)),
            pl.BlockSpec((tm, D_MODEL), lambda i: (i, 0)),
            pl.BlockSpec((tm, D_MODEL), lambda i: (i, 1)),
            pl.BlockSpec((tm, D_MODEL), lambda i: (i, 0)),
            const((len(POOL_WINDOWS), POOL_GC, POOL_GC)),
            const((1, D_POOL)),
            const((D_POOL, D_MODEL)),
            const((D_ATTN, D_MODEL)),
            const((D_MODEL, D_MODEL)),
            const((1, D_MODEL)),
            const((1, D_MODEL)),
        ],
        out_specs=pl.BlockSpec((tm, D_MODEL), lambda i: (i, 0)),
        out_shape=jax.ShapeDtypeStruct((rows, D_MODEL), F32),
        compiler_params=_params(("arbitrary",)),
        name="tail",
    )(*pool_args, h_silu, o, h_silu, h_sig, h_sig, x, pw, ps, wup, wun, wo, lg, lb)


def _compress_body(pt_ref, c_hbm, pe_ref, w1_ref, w2_ref, *rest, pps, n_chunks, rider_act):
    if rider_act is None:
        o_ref, buf, lhs, res, sem = rest
    else:
        x_ref, wt_ref, b_ref, o_ref, y_ref, buf, lhs, res, sem = rest
    b = pl.program_id(0)
    c = pl.program_id(1)
    step = b * n_chunks + c
    n_steps = pl.num_programs(0) * n_chunks
    nb = pps * BLOCKS_PER_PAGE

    def page_copy(seq_row, page0, slot, p):
        phys = pt_ref[seq_row, page0 + p]
        return pltpu.make_async_copy(c_hbm.at[pl.ds(phys * BLOCKS_PER_PAGE, BLOCKS_PER_PAGE)],
                                     buf.at[slot, pl.ds(BLOCKS_PER_PAGE * p, BLOCKS_PER_PAGE), pl.ds(0, CMP_ROWS), :],
                                     sem.at[slot])

    def start_step(s, slot):
        seq_row = s // n_chunks
        page0 = (s % n_chunks) * pps

        def pair(i, carry):
            page_copy(seq_row, page0, slot, 2 * i).start(priority=0)
            page_copy(seq_row, page0, slot, 2 * i + 1).start(priority=1)
            return carry
        lax.fori_loop(0, pps // 2, pair, 0)

    @pl.when(step == 0)
    def _():
        for s in range(CMP_SLOTS - 1):
            @pl.when(s < n_steps)
            def _(s=s):
                start_step(s, s)

    ahead = step + (CMP_SLOTS - 1)

    @pl.when(ahead < n_steps)
    def _():
        start_step(ahead, ahead % CMP_SLOTS)

    if rider_act is not None:
        @pl.when(step % RIDER_EVERY == 0)
        def _():
            y_ref[...] = _activate(_xwt(x_ref[...], wt_ref[...], b_ref[...]), rider_act).astype(y_ref.dtype)

    slot = step % CMP_SLOTS
    for p in range(pps):
        page_copy(b, c * pps, slot, p).wait()

    rows = buf.at[slot].reshape(nb * CMP_PITCH, LANES)
    for j in range(CMP_BLOCK):
        x0 = rows[pl.ds(2 * j, nb, stride=CMP_PITCH), :]
        x1 = rows[pl.ds(2 * j + 1, nb, stride=CMP_PITCH), :]
        pe = pe_ref[j:j + 1, :]
        lhs[:, (2 * j) * HEAD_DIM:(2 * j + 1) * HEAD_DIM] = (x0 + pe).astype(BF16)
        lhs[:, (2 * j + 1) * HEAD_DIM:(2 * j + 2) * HEAD_DIM] = (x1 + pe).astype(BF16)
    hid = jnp.dot(lhs[...], w1_ref[...], preferred_element_type=F32)
    hid = hid * _sigmoid(hid)
    out = jnp.dot(hid.astype(BF16), w2_ref[...], preferred_element_type=F32)
    half = nb // 2
    for h in range(N_KV_HEADS):
        hs = slice(h * HEAD_DIM, (h + 1) * HEAD_DIM)
        res[h] = out[:, hs]
        o_ref[0, 0:half, hs] = res[h, pl.ds(0, half, stride=2), :]
        o_ref[0, half:nb, hs] = res[h, pl.ds(1, half, stride=2), :]


def _compress(page_table, cache, pe, w1, w2, pps, rider=None):
    nbatch, n_pages = page_table.shape
    n_chunks = n_pages // pps
    nb = pps * BLOCKS_PER_PAGE
    kdim = CMP_BLOCK * D_KV
    once = pl.Buffered(1)
    in_specs = [
        pl.BlockSpec(memory_space=pl.ANY),
        pl.BlockSpec((CMP_BLOCK, HEAD_DIM), lambda b, c, pt: (0, 0)),
        pl.BlockSpec((kdim, D_KV), lambda b, c, pt: (0, 0), pipeline_mode=once),
        pl.BlockSpec((D_KV, D_KV), lambda b, c, pt: (0, 0)),
    ]
    args = [page_table, cache, pe, w1, w2]
    out_specs = [pl.BlockSpec((1, nb, D_KV), lambda b, c, pt: (b, c, 0))]
    out_shape = [jax.ShapeDtypeStruct((nbatch, n_chunks * nb, D_KV), F32)]
    scratch = [
        pltpu.VMEM((CMP_SLOTS, nb, CMP_PITCH, LANES), F32),
        pltpu.VMEM((nb, kdim), BF16),
        pltpu.VMEM((N_KV_HEADS, nb, HEAD_DIM), F32),
        pltpu.SemaphoreType.DMA((CMP_SLOTS,)),
    ]
    rider_act = None
    if rider is not None:
        x, wt, bias, rider_act, dtype = rider
        rows = x.shape[0] * RIDER_EVERY // (nbatch * n_chunks)
        in_specs += [
            pl.BlockSpec((rows, D_MODEL), lambda b, c, pt: ((b * n_chunks + c) // RIDER_EVERY, 0)),
            pl.BlockSpec((RIDER_COLS, D_MODEL), lambda b, c, pt: (0, 0), pipeline_mode=once),
            pl.BlockSpec((1, RIDER_COLS), lambda b, c, pt: (0, 0)),
        ]
        args += [x, wt, bias]
        out_specs.append(pl.BlockSpec((rows, RIDER_COLS), lambda b, c, pt: ((b * n_chunks + c) // RIDER_EVERY, 0)))
        out_shape.append(jax.ShapeDtypeStruct((x.shape[0], RIDER_COLS), dtype))
    outs = pl.pallas_call(
        functools.partial(_compress_body, pps=pps, n_chunks=n_chunks, rider_act=rider_act),
        grid_spec=pltpu.PrefetchScalarGridSpec(
            num_scalar_prefetch=1,
            grid=(nbatch, n_chunks),
            in_specs=in_specs,
            out_specs=out_specs,
            scratch_shapes=scratch,
        ),
        out_shape=out_shape,
        compiler_params=_params(("arbitrary", "arbitrary")),
        name="compress",
    )(*args)
    return outs[0] if rider is None else outs


def _kprep_body(ks_ref, vs_ref, kw_ref, vw_ref, kts_ref, vso_ref, ktw_ref, vwo_ref, *, tk):
    t = pl.program_id(1)
    kpos = t * tk + lax.broadcasted_iota(jnp.int32, (SLC_BLOCK, tk), 1)
    blk = lax.broadcasted_iota(jnp.int32, (SLC_BLOCK, tk), 0)
    onehot = jnp.where(kpos // SLC_BLOCK == blk, 1.0, 0.0).astype(BF16)
    for g in range(N_KV_HEADS):
        head = pl.ds(g, tk, stride=N_KV_HEADS)
        kts_ref[0, g, 0, 0:HEAD_DIM, :] = ks_ref[head, :].T.astype(BF16)
        kts_ref[0, g, 0, HEAD_DIM:HEAD_DIM + SLC_BLOCK, :] = onehot
        kts_ref[0, g, 0, HEAD_DIM + SLC_BLOCK:, :] = jnp.zeros((SLC_BLOCK, tk), BF16)
        kw = kw_ref[head, :]
        for c in range(tk // LANES):
            ktw_ref[0, g, c] = kw[c * LANES:(c + 1) * LANES, :].T.astype(BF16)
        ones = jnp.ones((tk, HEAD_DIM), BF16)
        vso_ref[0, g, :, 0:HEAD_DIM] = vs_ref[head, :].astype(BF16)
        vso_ref[0, g, :, HEAD_DIM:] = ones
        vwo_ref[0, g, :, 0:HEAD_DIM] = vw_ref[head, :].astype(BF16)
        vwo_ref[0, g, :, HEAD_DIM:] = ones


def _kprep(ks, vs, kw, vw, nbatch, seq, tk):
    nt = seq // tk
    src = pl.BlockSpec((tk * N_KV_HEADS, HEAD_DIM), lambda b, t: (b * nt + t, 0))
    return pl.pallas_call(
        functools.partial(_kprep_body, tk=tk),
        grid=(nbatch, nt),
        in_specs=[src, src, src, src],
        out_specs=[
            pl.BlockSpec((1, N_KV_HEADS, 1, 2 * HEAD_DIM, tk), lambda b, t: (b, 0, t, 0, 0)),
            pl.BlockSpec((1, N_KV_HEADS, tk, 2 * HEAD_DIM), lambda b, t: (b, 0, t, 0)),
            pl.BlockSpec((1, N_KV_HEADS, tk // LANES, HEAD_DIM, LANES), lambda b, t: (b, 0, t, 0, 0)),
            pl.BlockSpec((1, N_KV_HEADS, tk, 2 * HEAD_DIM), lambda b, t: (b, 0, t, 0)),
        ],
        out_shape=[
            jax.ShapeDtypeStruct((nbatch, N_KV_HEADS, nt, 2 * HEAD_DIM, tk), BF16),
            jax.ShapeDtypeStruct((nbatch, N_KV_HEADS, seq, 2 * HEAD_DIM), BF16),
            jax.ShapeDtypeStruct((nbatch, N_KV_HEADS, seq // LANES, HEAD_DIM, LANES), BF16),
            jax.ShapeDtypeStruct((nbatch, N_KV_HEADS, seq, 2 * HEAD_DIM), BF16),
        ],
        compiler_params=_params(("arbitrary", "arbitrary")),
        name="kprep",
    )(ks, vs, kw, vw)


def _masked_softmax(s, mask):
    s = jnp.where(mask, s, NEG)
    mx = jnp.max(s, axis=-1, keepdims=True)
    e = jnp.where(mask, jnp.exp2(s - mx), 0.0)
    l = jnp.sum(e, axis=-1, keepdims=True)
    return e / jnp.where(l > 0.0, l, 1.0)


def _select_bias(p_slc_t, qpos_row, n_slc):
    shape = p_slc_t.shape
    blk = lax.broadcasted_iota(jnp.int32, shape, 0)
    valid = blk * SLC_BLOCK <= qpos_row
    cur = qpos_row // SLC_BLOCK
    forced = (blk == 0) | (blk == cur) | (blk == cur - 1)
    score = jnp.where(valid & forced, FORCE, jnp.where(valid, p_slc_t, -FORCE))
    n_chunks = n_slc // SUBLANES
    chunks = [score[c * SUBLANES:(c + 1) * SUBLANES] for c in range(n_chunks)]
    ranks = [jnp.zeros((SUBLANES, shape[1]), F32) for _ in range(n_chunks)]
    sub = lax.broadcasted_iota(jnp.int32, (SUBLANES, shape[1]), 0)
    for j in range(n_slc):
        row = jnp.broadcast_to(score[j:j + 1, :], (SUBLANES, shape[1]))
        for c in range(n_chunks):
            lo = c * SUBLANES
            if lo > j:
                ahead = row >= chunks[c]
            elif lo + SUBLANES - 1 < j:
                ahead = row > chunks[c]
            else:
                ahead = (row > chunks[c]) | ((row == chunks[c]) & (sub + lo > j))
            ranks[c] = ranks[c] + jnp.where(ahead, 1.0, 0.0)
    rank = jnp.concatenate(ranks, axis=0)
    return jnp.where(rank < min(TOP_N, n_slc), 0.0, NEG)


def _attend_tile(q_ref, qcols, kt, v, width, mask_fn, s_ref, m_scr, acc_scr):
    rows = s_ref.shape[0]
    reps = width // LANES
    for blk in range(rows // ROW_BLOCK):
        rb = slice(blk * ROW_BLOCK, (blk + 1) * ROW_BLOCK)
        s_ref[rb, 0:width] = jnp.dot(q_ref[rb, 0:qcols], kt, preferred_element_type=F32)
    for blk in range(rows // ROW_BLOCK):
        alphas, ps = [], []
        for c in range(ROW_BLOCK // ROW_CHUNK):
            r0 = blk * ROW_BLOCK + c * ROW_CHUNK
            r = slice(r0, r0 + ROW_CHUNK)
            s = s_ref[r, 0:width]
            if mask_fn is not None:
                s = jnp.where(mask_fn(r0, ROW_CHUNK), s, NEG)
            m_old = m_scr[r, :]
            m_new = jnp.maximum(m_old, jnp.max(s, axis=-1, keepdims=True))
            ps.append(jnp.exp2(s - jnp.concatenate([m_new] * reps, axis=1)).astype(BF16))
            alphas.append(jnp.exp2(m_old - m_new))
            m_scr[r, :] = m_new
        alpha = jnp.concatenate(alphas, axis=0)
        rb = slice(blk * ROW_BLOCK, (blk + 1) * ROW_BLOCK)
        acc_scr[rb, :] = (jnp.concatenate([alpha, alpha], axis=1) * acc_scr[rb, :]
                          + jnp.dot(jnp.concatenate(ps, axis=0), v, preferred_element_type=F32))


def _attn_body(q_ref, gn_ref, kc_ref, vc_ref, kts_ref, vs_ref, ktw_ref, vw_ref, o_ref,
               qaug, s_scr, m_scr, acc_scr, *, tq, tk, seq):
    q0 = pl.program_id(1) * tq
    n_cmp = seq // CMP_BLOCK
    n_slc = -(-seq // SLC_BLOCK)
    half = n_cmp // 2
    qpos = q0 + lax.broadcasted_iota(jnp.int32, (tq, 1), 0)
    qpos4 = jnp.concatenate([qpos] * GROUP, axis=0)
    qpos_row = q0 + lax.broadcasted_iota(jnp.int32, (n_slc, tq), 1)
    lane = lax.broadcasted_iota(jnp.int32, (1, n_cmp), 1)
    cmp_blk = 2 * (lane % half) + lane // half
    m_cmp = ((cmp_blk + 1) * CMP_BLOCK - 1) <= qpos4
    gates = gn_ref[...].astype(F32)
    w0 = pl.multiple_of(jnp.maximum(q0 - WINDOW, 0), LANES)
    last = (q0 + tq + tk - 1) // tk - 1

    def chunk_qpos(r0, n):
        return q0 + r0 % tq + lax.broadcasted_iota(jnp.int32, (n, 1), 0)

    def reset(g):
        m_scr[g] = jnp.full(m_scr.shape[1:], -jnp.inf, F32)
        acc_scr[g] = jnp.zeros(acc_scr.shape[1:], F32)

    def result(g):
        acc = acc_scr[g]
        return acc[:, 0:HEAD_DIM] / acc[:, HEAD_DIM:]

    def attend(g, qcols, kt, v, width, mask_fn):
        _attend_tile(qaug.at[g], qcols, kt, v, width, mask_fn, s_scr.at[g], m_scr.at[g], acc_scr.at[g])

    groups = range(N_KV_HEADS)

    o_cmp = []
    for g in groups:
        hs = slice(g * HEAD_DIM, (g + 1) * HEAD_DIM)
        for h in range(GROUP):
            qaug[g, h * tq:(h + 1) * tq, 0:HEAD_DIM] = (
                q_ref[:, (GROUP * g + h) * HEAD_DIM:(GROUP * g + h + 1) * HEAD_DIM])
        kc = kc_ref[0][:, hs].astype(BF16)
        vc = vc_ref[0][:, hs].astype(BF16)
        s = lax.dot_general(qaug[g, :, 0:HEAD_DIM], kc, (((1,), (1,)), ((), ())), preferred_element_type=F32)
        p = _masked_softmax(s, m_cmp)
        o_cmp.append(jnp.dot(p.astype(BF16), vc, preferred_element_type=F32))
        p_grp = p[0:tq]
        for h in range(1, GROUP):
            p_grp = p_grp + p[h * tq:(h + 1) * tq]
        p_slc = p_grp + pltpu.roll(p_grp, half, 1)
        bias_t = _select_bias(p_slc.T[0:n_slc], qpos_row, n_slc)
        bias = jnp.concatenate([bias_t, jnp.zeros((LANES - n_slc, tq), F32)], axis=0).T.astype(BF16)
        for h in range(GROUP):
            qaug[g, h * tq:(h + 1) * tq, HEAD_DIM:] = bias
        reset(g)

    def interior(t, carry):
        k0 = pl.multiple_of(t * tk, tk)
        for g in groups:
            attend(g, 2 * HEAD_DIM, kts_ref[0, g, t], vs_ref[0, g, pl.ds(k0, tk), :], tk, None)
        return carry

    lax.fori_loop(0, last, interior, 0)
    k_last = pl.multiple_of(last * tk, tk)

    def causal(r0, n):
        return k_last + lax.broadcasted_iota(jnp.int32, (1, tk), 1) <= chunk_qpos(r0, n)

    o_sel = []
    for g in groups:
        attend(g, 2 * HEAD_DIM, kts_ref[0, g, last], vs_ref[0, g, pl.ds(k_last, tk), :], tk, causal)
        o_sel.append(result(g))
        reset(g)

    span = WINDOW + tq

    def in_window(r0, n):
        dist = chunk_qpos(r0, n) - (w0 + lax.broadcasted_iota(jnp.int32, (1, span), 1))
        return (dist >= 0) & (dist <= WINDOW)

    for g in groups:
        ktw = jnp.concatenate([ktw_ref[0, g, w0 // LANES + c] for c in range(span // LANES)], axis=1)
        attend(g, HEAD_DIM, ktw, vw_ref[0, g, pl.ds(w0, span), :], span, in_window)

    for g in groups:
        o_win = result(g)
        for h in range(GROUP):
            hd = GROUP * g + h
            r = slice(h * tq, (h + 1) * tq)
            o_ref[:, hd * HEAD_DIM:(hd + 1) * HEAD_DIM] = (
                gates[:, hd:hd + 1] * o_cmp[g][r]
                + gates[:, N_HEADS + hd:N_HEADS + hd + 1] * o_sel[g][r]
                + gates[:, 2 * N_HEADS + hd:2 * N_HEADS + hd + 1] * o_win[r]).astype(o_ref.dtype)


def _prompt_attn(h_q, h_sig, kc, vc, kts, vs, ktw, vw, nbatch, seq, tq, tk):
    nq = seq // tq
    n_cmp = seq // CMP_BLOCK
    nt = seq // tk
    rows = GROUP * tq
    return pl.pallas_call(
        functools.partial(_attn_body, tq=tq, tk=tk, seq=seq),
        grid=(nbatch, nq),
        in_specs=[
            pl.BlockSpec((tq, D_ATTN), lambda b, i: (b * nq + i, 0)),
            pl.BlockSpec((tq, LANES), lambda b, i: (b * nq + i, SIG_GN // LANES)),
            pl.BlockSpec((1, n_cmp, D_KV), lambda b, i: (b, 0, 0)),
            pl.BlockSpec((1, n_cmp, D_KV), lambda b, i: (b, 0, 0)),
            pl.BlockSpec((1, N_KV_HEADS, nt, 2 * HEAD_DIM, tk), lambda b, i: (b, 0, 0, 0, 0)),
            pl.BlockSpec((1, N_KV_HEADS, seq, 2 * HEAD_DIM), lambda b, i: (b, 0, 0, 0)),
            pl.BlockSpec((1, N_KV_HEADS, seq // LANES, HEAD_DIM, LANES), lambda b, i: (b, 0, 0, 0, 0)),
            pl.BlockSpec((1, N_KV_HEADS, seq, 2 * HEAD_DIM), lambda b, i: (b, 0, 0, 0)),
        ],
        out_specs=pl.BlockSpec((tq, D_ATTN), lambda b, i: (b * nq + i, 0)),
        scratch_shapes=[
            pltpu.VMEM((N_KV_HEADS, rows, 2 * HEAD_DIM), BF16),
            pltpu.VMEM((N_KV_HEADS, rows, max(tk, WINDOW + tq)), F32),
            pltpu.VMEM((N_KV_HEADS, rows, LANES), F32),
            pltpu.VMEM((N_KV_HEADS, rows, 2 * HEAD_DIM), F32),
        ],
        out_shape=jax.ShapeDtypeStruct((nbatch * seq, D_ATTN), BF16),
        compiler_params=_params(("arbitrary", "arbitrary")),
        name="attn",
    )(h_q, h_sig, kc, vc, kts, vs, ktw, vw)


def _group_queries(q_row, g):
    heads = [q_row[:, (GROUP * g + h) * HEAD_DIM:(GROUP * g + h + 1) * HEAD_DIM] for h in range(GROUP)]
    pad = jnp.zeros((2 * SUBLANES - GROUP, HEAD_DIM), F32)
    return jnp.concatenate(heads + [pad], axis=0).astype(BF16)


def _scmp_body(q_ref, kc_ref, vc_ref, ocmp_ref, pslc_ref, *, q_pos, chunk):
    b = pl.program_id(0)
    q_row = q_ref[pl.ds(b, 1), :]
    n_cmp = kc_ref.shape[1]
    half = chunk // 2
    lane = lax.broadcasted_iota(jnp.int32, (1, n_cmp), 1)
    within = lane % chunk
    cmp_blk = (lane // chunk) * chunk + 2 * (within % half) + within // half
    m_cmp = ((cmp_blk + 1) * CMP_BLOCK - 1) <= q_pos
    outs = []
    for g in range(N_KV_HEADS):
        hs = slice(g * HEAD_DIM, (g + 1) * HEAD_DIM)
        qg = _group_queries(q_row, g)
        s = lax.dot_general(qg, kc_ref[0][:, hs].astype(BF16), (((1,), (1,)), ((), ())), preferred_element_type=F32)
        p = _masked_softmax(s, m_cmp)
        o = jnp.dot(p.astype(BF16), vc_ref[0][:, hs].astype(BF16), preferred_element_type=F32)
        outs.append(o[0:GROUP])
        p_grp = p[0:1]
        for h in range(1, GROUP):
            p_grp = p_grp + p[h:h + 1]
        parts = []
        for c in range(n_cmp // chunk):
            pc = p_grp[:, c * chunk:(c + 1) * chunk]
            parts.append(pc + pltpu.roll(pc, half, 1))
        pslc_ref[0, g:g + 1, :] = jnp.concatenate(parts, axis=1)
    ocmp_ref[0] = jnp.concatenate(outs, axis=0)


def _sample_cmp(q, kc, vc, q_pos, chunk):
    nbatch, n_cmp, _ = kc.shape
    return pl.pallas_call(
        functools.partial(_scmp_body, q_pos=q_pos, chunk=chunk),
        grid=(nbatch,),
        in_specs=[
            pl.BlockSpec((nbatch, D_ATTN), lambda b: (0, 0)),
            pl.BlockSpec((1, n_cmp, D_KV), lambda b: (b, 0, 0)),
            pl.BlockSpec((1, n_cmp, D_KV), lambda b: (b, 0, 0)),
        ],
        out_specs=[
            pl.BlockSpec((1, N_HEADS, HEAD_DIM), lambda b: (b, 0, 0)),
            pl.BlockSpec((1, N_KV_HEADS, n_cmp), lambda b: (b, 0, 0)),
        ],
        out_shape=[
            jax.ShapeDtypeStruct((nbatch, N_HEADS, HEAD_DIM), F32),
            jax.ShapeDtypeStruct((nbatch, N_KV_HEADS, n_cmp), F32),
        ],
        compiler_params=_params(("arbitrary",)),
        name="s_cmp",
    )(q, kc, vc)


def _stopk_body(p_ref, idx_ref, *, q_pos, chunk, n_slc):
    p = p_ref[...]
    rows, width = p.shape
    half = chunk // 2
    n_in = (width // chunk) * half
    lane = lax.broadcasted_iota(jnp.int32, (1, width), 1)
    within = lane % chunk
    blk = jnp.where(within < half, (lane // chunk) * half + within, -1)
    extra = (lane >= half) & (lane < half + (n_slc - n_in))
    blk = jnp.where(extra, n_in + lane - half, blk)
    real = blk >= 0
    valid = real & (blk * SLC_BLOCK <= q_pos)
    cur = q_pos // SLC_BLOCK
    forced = (blk == 0) | (blk == cur) | (blk == cur - 1)
    base = jnp.where(extra, 0.0, p)
    x = jnp.where(valid & forced, FORCE, jnp.where(valid, base, -FORCE))
    x = jnp.where(real, x, -jnp.inf)
    blk_f = blk.astype(F32)
    out_lane = lax.broadcasted_iota(jnp.int32, (rows, LANES), 1)
    out = jnp.zeros((rows, LANES), F32)
    for r in range(min(TOP_N, n_slc)):
        mx = jnp.max(x, axis=-1, keepdims=True)
        pick = jnp.min(jnp.where(x == mx, blk_f, float(2 ** 30)), axis=-1, keepdims=True)
        out = jnp.where(out_lane == r, pick, out)
        x = jnp.where(blk_f == pick, -jnp.inf, x)
    idx_ref[...] = out.astype(jnp.int32)


def _sample_topk(pslc, q_pos, chunk, n_slc):
    rows, width = pslc.shape
    return pl.pallas_call(
        functools.partial(_stopk_body, q_pos=q_pos, chunk=chunk, n_slc=n_slc),
        grid=(1,),
        in_specs=[pl.BlockSpec((rows, width), lambda i: (0, 0))],
        out_specs=pl.BlockSpec((rows, LANES), lambda i: (0, 0)),
        out_shape=jax.ShapeDtypeStruct((rows, LANES), jnp.int32),
        compiler_params=_params(("arbitrary",)),
        name="s_topk",
    )(pslc)


def _sattn_body(idx_ref, pt_ref, q_ref, gn_ref, ocmp_ref, knew_ref, vnew_ref, kwn_ref, vwn_ref, wk_ref, wv_ref,
                sk_hbm, sv_hbm, o_ref, kbuf, vbuf, sem, *, q_pos, n_top, n_pages, past, per_step):
    step = pl.program_id(0)
    n_steps = pl.num_programs(0)
    half_rows = SLC_BLOCK * N_KV_HEADS

    def copies(s, slot, sub, g, i):
        bb = s * per_step + sub
        blk = idx_ref[bb * N_KV_HEADS + g, i]
        page = jnp.minimum(blk // 2, n_pages - 1)
        start = pl.multiple_of(pt_ref[bb, page] * (PAGE_SIZE * N_KV_HEADS) + (blk % 2) * half_rows, half_rows)
        return [pltpu.make_async_copy(hbm.at[pl.ds(start, half_rows), :], dst.at[slot, sub, g, i],
                                      sem.at[t, slot, sub, g, i])
                for t, (hbm, dst) in enumerate(((sk_hbm, kbuf), (sv_hbm, vbuf)))]

    def all_copies(s, slot):
        return [cp for sub in range(per_step) for g in range(N_KV_HEADS) for i in range(n_top)
                for cp in copies(s, slot, sub, g, i)]

    @pl.when(step == 0)
    def _():
        def one(n, carry):
            sub, rest = n // (N_KV_HEADS * n_top), n % (N_KV_HEADS * n_top)
            for cp in copies(0, 0, sub, rest // n_top, rest % n_top):
                cp.start()
            return carry
        lax.fori_loop(0, per_step * N_KV_HEADS * n_top, one, 0)

    @pl.when(step + 1 < n_steps)
    def _():
        for cp in all_copies(step + 1, (step + 1) % 2):
            cp.start()

    slot = step % 2
    for cp in all_copies(step, slot):
        cp.wait()

    pad = jnp.zeros((2 * SUBLANES - 1, HEAD_DIM), F32)
    for sub in range(per_step):
        _sattn_one(step * per_step + sub, slot, sub, pad, idx_ref, q_ref, gn_ref, ocmp_ref, knew_ref, vnew_ref, kwn_ref,
                   vwn_ref, wk_ref, wv_ref, o_ref, kbuf, vbuf, q_pos=q_pos, n_top=n_top, past=past)


def _sattn_one(b, slot, sub, pad, idx_ref, q_ref, gn_ref, ocmp_ref, knew_ref, vnew_ref, kwn_ref, vwn_ref, wk_ref, wv_ref,
               o_ref, kbuf, vbuf, *, q_pos, n_top, past):
    tail_blk = past // SLC_BLOCK
    q_row = q_ref[pl.ds(b, 1), :]
    gates = gn_ref[pl.ds(b, 1), :]

    def new_row(ref, g):
        return jnp.concatenate([ref[pl.ds(b * N_KV_HEADS + g, 1), :], pad], axis=0).astype(BF16)

    outs = []
    for g in range(N_KV_HEADS):
        qg = _group_queries(q_row, g)
        ks, vs = [], []
        key = lax.broadcasted_iota(jnp.int32, (1, n_top * SLC_BLOCK), 1)
        key_blk = jnp.zeros((1, n_top * SLC_BLOCK), jnp.int32)
        for i in range(n_top):
            blk = idx_ref[b * N_KV_HEADS + g, i]
            ks.append(kbuf[slot, sub, g, i, pl.ds(g, SLC_BLOCK, stride=N_KV_HEADS), :])
            vs.append(vbuf[slot, sub, g, i, pl.ds(g, SLC_BLOCK, stride=N_KV_HEADS), :])
            key_blk = jnp.where(key // SLC_BLOCK == i, blk, key_blk)
        k_sel = jnp.concatenate(ks, axis=0).astype(BF16)
        v_sel = jnp.concatenate(vs, axis=0).astype(BF16)
        m_sel = (key_blk * SLC_BLOCK + key % SLC_BLOCK <= q_pos) & (key_blk < tail_blk)
        tail_sel = jnp.max(jnp.where(key_blk == tail_blk, 1.0, 0.0), axis=-1, keepdims=True) > 0.5
        first = lax.broadcasted_iota(jnp.int32, (1, 2 * SUBLANES), 1) == 0
        o_sel = _two_part_attention(qg, k_sel, v_sel, m_sel, new_row(knew_ref, g), new_row(vnew_ref, g),
                                    first & tail_sel)
        wbuf = wk_ref.shape[1] // N_KV_HEADS
        k_win = wk_ref[sub, pl.ds(g, wbuf, stride=N_KV_HEADS), :].astype(BF16)
        v_win = wv_ref[sub, pl.ds(g, wbuf, stride=N_KV_HEADS), :].astype(BF16)
        dist = q_pos - (past - wbuf + lax.broadcasted_iota(jnp.int32, (1, wbuf), 1))
        m_win = (dist >= 0) & (dist <= WINDOW)
        o_win = _two_part_attention(qg, k_win, v_win, m_win, new_row(kwn_ref, g), new_row(vwn_ref, g), first)
        o_cmp = ocmp_ref[sub, GROUP * g:GROUP * (g + 1), :]
        for h in range(GROUP):
            hd = GROUP * g + h
            outs.append(gates[:, hd:hd + 1] * o_cmp[h:h + 1]
                        + gates[:, N_HEADS + hd:N_HEADS + hd + 1] * o_sel[h:h + 1]
                        + gates[:, 2 * N_HEADS + hd:2 * N_HEADS + hd + 1] * o_win[h:h + 1])
    o_ref[sub] = jnp.concatenate(outs, axis=0)


def _two_part_attention(q, k1, v1, m1, k2, v2, m2):
    dn = (((1,), (1,)), ((), ()))
    s1 = jnp.where(m1, lax.dot_general(q, k1, dn, preferred_element_type=F32), NEG)
    s2 = jnp.where(m2, lax.dot_general(q, k2, dn, preferred_element_type=F32), NEG)
    mx = jnp.maximum(jnp.max(s1, axis=-1, keepdims=True), jnp.max(s2, axis=-1, keepdims=True))
    e1 = jnp.where(m1, jnp.exp2(s1 - mx), 0.0)
    e2 = jnp.where(m2, jnp.exp2(s2 - mx), 0.0)
    l = jnp.sum(e1, axis=-1, keepdims=True) + jnp.sum(e2, axis=-1, keepdims=True)
    inv = 1.0 / jnp.where(l > 0.0, l, 1.0)
    o = (jnp.dot((e1 * inv).astype(BF16), v1, preferred_element_type=F32)
         + jnp.dot((e2 * inv).astype(BF16), v2, preferred_element_type=F32))
    return o


def _sample_attn(idx, page_table, h_q, h_sig, ocmp, new_rows, win_k, win_v, slc_k, slc_v, q_pos, past):
    nbatch, n_pages = page_table.shape
    n_top = min(TOP_N, past // SLC_BLOCK + 1)
    half_rows = SLC_BLOCK * N_KV_HEADS
    flat_new = pl.BlockSpec((nbatch * N_KV_HEADS, HEAD_DIM), lambda b, idx, pt: (0, 0))

    def whole(col_block, width):
        return pl.BlockSpec((nbatch, width), lambda b, idx, pt: (0, col_block))

    per_step = SATTN_PER_STEP
    return pl.pallas_call(
        functools.partial(_sattn_body, q_pos=q_pos, n_top=n_top, n_pages=n_pages, past=past, per_step=per_step),
        grid_spec=pltpu.PrefetchScalarGridSpec(
            num_scalar_prefetch=2,
            grid=(nbatch // per_step,),
            in_specs=[
                whole(0, D_ATTN),
                whole(SIG_GN // LANES, LANES),
                pl.BlockSpec((per_step, N_HEADS, HEAD_DIM), lambda b, idx, pt: (b, 0, 0)),
                flat_new, flat_new, flat_new, flat_new,
                pl.BlockSpec((per_step,) + win_k.shape[1:], lambda b, idx, pt: (b, 0, 0)),
                pl.BlockSpec((per_step,) + win_v.shape[1:], lambda b, idx, pt: (b, 0, 0)),
                pl.BlockSpec(memory_space=pl.ANY),
                pl.BlockSpec(memory_space=pl.ANY),
            ],
            out_specs=pl.BlockSpec((per_step, N_HEADS, HEAD_DIM), lambda b, idx, pt: (b, 0, 0)),
            scratch_shapes=[
                pltpu.VMEM((2, per_step, N_KV_HEADS, n_top, half_rows, LANES), F32),
                pltpu.VMEM((2, per_step, N_KV_HEADS, n_top, half_rows, LANES), F32),
                pltpu.SemaphoreType.DMA((2, 2, per_step, N_KV_HEADS, n_top)),
            ],
        ),
        out_shape=jax.ShapeDtypeStruct((nbatch, N_HEADS, HEAD_DIM), F32),
        compiler_params=_params(("arbitrary",)),
        name="s_attn",
    )(idx, page_table, h_q, h_sig, ocmp, *new_rows, win_k, win_v, slc_k, slc_v)


REF_KV = 3 * D_POOL
REF_GN = REF_KV + N_KV_PROJ * D_KV
REF_ZN = REF_GN + N_GATE
REF_GM = REF_ZN + D_ATTN


def _tiles(start, width):
    return tuple(range(start, start + width, PROJ_TN))


STARTS_U = _tiles(0, D_POOL)
STARTS_SILU = _tiles(D_POOL, D_POOL) + _tiles(REF_ZN, D_ATTN)
STARTS_Q = _tiles(2 * D_POOL, D_ATTN)
STARTS_SIG = _tiles(REF_GM, N_MERGE_COLS) + (REF_GN,)


def _block_diag2(w):
    z = jnp.zeros_like(w)
    return jnp.concatenate([jnp.concatenate([w, z], axis=-1), jnp.concatenate([z, w], axis=-1)], axis=-2)


def _project(x, wt, b, tm, tm_kv, act_dtype, with_uq=True):
    h_silu = _proj(x, wt, b, STARTS_SILU, ACT_SILU, tm, act_dtype)
    h_sig = _proj(x, wt, b, STARTS_SIG, ACT_SIGMOID, tm, act_dtype)
    kv = _proj_kv(x, wt, b, REF_KV, tm_kv)
    if not with_uq:
        return h_silu, h_sig, kv
    u = _proj(x, wt, b, STARTS_U, ACT_NONE, tm, F32)
    h_q = _proj(x, wt, b, STARTS_Q, ACT_SCALE, tm, act_dtype)
    return u, h_silu, h_q, h_sig, kv


def kernel(x_prompt, x_sample, cache_cmp_k, cache_cmp_v, cache_slc_k, cache_slc_v, cache_win_k, cache_win_v,
           state_pool, page_table, w_in, b_in, pool_w, pool_scale, cmp_pe_k, cmp_w1_k, cmp_w2_k, cmp_pe_v, cmp_w1_v,
           cmp_w2_v, w_up_pool, w_up_nsa, w_out, ln_g, ln_b):
    nb_p, seq, _ = x_prompt.shape
    nb_s = x_sample.shape[0]
    n_pages = page_table.shape[1]
    past = n_pages * PAGE_SIZE
    n_phys = cache_cmp_k.shape[1]
    wbuf = cache_win_k.shape[2]

    wt = w_in[0].T
    b = b_in[0][None, :]
    kdim = CMP_BLOCK * D_KV
    cmp_k = (cmp_pe_k[0], _block_diag2(cmp_w1_k[0]).reshape(kdim, D_KV).astype(BF16),
             _block_diag2(cmp_w2_k[0]).astype(BF16))
    cmp_v = (cmp_pe_v[0], _block_diag2(cmp_w1_v[0]).reshape(kdim, D_KV).astype(BF16),
             _block_diag2(cmp_w2_v[0]).astype(BF16))
    pw = pool_w[0].astype(BF16)
    ps = pool_scale[0][None, :]
    wup = w_up_pool[0].astype(BF16)
    wun = w_up_nsa[0].astype(BF16)
    wo = w_out[0].astype(BF16)
    lg = ln_g[0][None, :]
    lb = ln_b[0][None, :]

    xp = x_prompt.reshape(nb_p * seq, D_MODEL)
    xs = x_sample.reshape(nb_s, D_MODEL)
    xp_bf = xp.astype(BF16)
    hsilu_p, hsig_p, kv_p = _project(xp_bf, wt, b, tm=2048, tm_kv=2048, act_dtype=BF16, with_uq=False)
    u_s, hsilu_s, hq_s, hsig_s, kv_s = _project(xs.astype(BF16), wt, b, tm=nb_s, tm_kv=nb_s, act_dtype=F32)

    blocks = (-1, CMP_ROWS, LANES)
    chunk_pages = 64
    q0 = STARTS_Q[0]
    kc_s, u_p = _compress(page_table, cache_cmp_k.reshape(blocks), *cmp_k, pps=chunk_pages,
                          rider=(xp_bf, wt[:RIDER_COLS].astype(BF16), b[:, :RIDER_COLS], ACT_NONE, F32))
    vc_s, hq_p = _compress(page_table, cache_cmp_v.reshape(blocks), *cmp_v, pps=chunk_pages,
                           rider=(xp_bf, wt[q0:q0 + RIDER_COLS].astype(BF16), b[:, q0:q0 + RIDER_COLS],
                                  ACT_SCALE, BF16))

    prompt_pages = seq // PAGE_SIZE
    ident = jnp.arange(nb_p * prompt_pages, dtype=jnp.int32).reshape(nb_p, prompt_pages)
    kc_p = _compress(ident, kv_p[0].reshape(blocks), *cmp_k, pps=prompt_pages)
    vc_p = _compress(ident, kv_p[1].reshape(blocks), *cmp_v, pps=prompt_pages)
    kts, vs, ktw, vw = _kprep(kv_p[2], kv_p[3], kv_p[4], kv_p[5], nb_p, seq, tk=512)
    o_p = _prompt_attn(hq_p, hsig_p, kc_p, vc_p, kts, vs, ktw, vw, nb_p, seq, tq=256, tk=512)
    y_p = _tail(u_p, hsilu_p, hsig_p, o_p, xp, pw, ps, wup, wun, wo, lg, lb, tm=512, seq=seq)

    chunk = chunk_pages * BLOCKS_PER_PAGE
    n_slc = past // SLC_BLOCK + 1
    ocmp_s, pslc_s = _sample_cmp(hq_s, kc_s, vc_s, past, chunk)
    idx = _sample_topk(pslc_s.reshape(nb_s * N_KV_HEADS, -1), past, chunk, n_slc)
    flat = (n_phys * PAGE_SIZE * N_KV_HEADS, HEAD_DIM)
    o_s = _sample_attn(idx, page_table, hq_s, hsig_s, ocmp_s, kv_s[2:6],
                       cache_win_k.reshape(nb_s, wbuf * N_KV_HEADS, HEAD_DIM),
                       cache_win_v.reshape(nb_s, wbuf * N_KV_HEADS, HEAD_DIM),
                       cache_slc_k.reshape(flat), cache_slc_v.reshape(flat), past, past)
    ctx = jnp.concatenate([state_pool[0], u_s[:, None, :]], axis=1)
    m_s = _pool_m(ctx.reshape(nb_s * (POOL_CTX + 1), D_POOL), 0, nb_s * (POOL_CTX + 1), tm=nb_s * (POOL_CTX + 1),
                  seq=nb_s * (POOL_CTX + 1), fixed_pos=past)
    m_s = m_s.reshape(nb_s, POOL_CTX + 1, D_POOL)[:, POOL_CTX]
    y_s = _tail(m_s, hsilu_s, hsig_s, o_s.reshape(nb_s, D_ATTN), xs, pw, ps, wup, wun, wo, lg, lb, tm=nb_s)

    wl = min(WINDOW, seq)
    kv_p = [a.reshape(1, nb_p, seq, N_KV_HEADS, HEAD_DIM) for a in kv_p]
    kv_s = [a.reshape(1, nb_s, 1, N_KV_HEADS, HEAD_DIM) for a in kv_s]
    return (
        y_p.reshape(nb_p, seq, D_MODEL),
        y_s.reshape(nb_s, 1, D_MODEL),
        kv_p[0], kv_p[1], kv_p[2], kv_p[3],
        kv_p[4][:, :, seq - wl:], kv_p[5][:, :, seq - wl:],
        u_p.reshape(nb_p, seq, D_POOL)[None, :, seq - POOL_CTX:],
        kv_s[0], kv_s[1], kv_s[2], kv_s[3],
        jnp.concatenate([cache_win_k, kv_s[4]], axis=2)[:, :, 1:],
        jnp.concatenate([cache_win_v, kv_s[5]], axis=2)[:, :, 1:],
        ctx[None, :, 1:],
    )
```

```python
import functools

import jax
import jax.numpy as jnp
import numpy as np
from jax import lax
from jax.experimental import pallas as pl
from jax.experimental.pallas import tpu as pltpu

D_MODEL = 2048
D_POOL = 1024
POOL_WINDOWS = (2, 4, 8, 16)
POOL_GC = D_POOL // len(POOL_WINDOWS)
POOL_CTX = max(POOL_WINDOWS) - 1
HEAD_DIM = 128
N_HEADS = 8
N_KV_HEADS = 2
GROUP = N_HEADS // N_KV_HEADS
D_ATTN = N_HEADS * HEAD_DIM
D_KV = N_KV_HEADS * HEAD_DIM
CMP_BLOCK = 32
SLC_BLOCK = 64
TOP_N = 16
WINDOW = 512
PAGE_SIZE = 128
ATTN_SCALE = HEAD_DIM ** -0.5
DEPTH = 1
ALPHA = (2.0 * DEPTH) ** 0.25
LN_EPS = 1e-5
NEG = -1e30
FORCE = 1e6

F32 = jnp.float32
BF16 = jnp.bfloat16

SUBLANES = 8
LANES = 128
VMEM_LIMIT_BYTES = 56 * 1024 * 1024

PROJ_TN = 512
PROJ_RC = 512
N_MERGE_COLS = 2 * D_MODEL
SIG_GN = N_MERGE_COLS
N_GATE = 3 * N_HEADS
N_KV_PROJ = 6
ACT_NONE, ACT_SILU, ACT_SIGMOID, ACT_SCALE = range(4)
LOG2E = 1.4426950408889634
SATTN_PER_STEP = 2
TAIL_RC = 128
ROW_BLOCK = 256
ROW_CHUNK = 32

CMP_ROWS = CMP_BLOCK * N_KV_HEADS
CMP_PITCH = 72
BLOCKS_PER_PAGE = PAGE_SIZE // CMP_BLOCK
CMP_SLOTS = 3
CMP_ISSUE_PARTS = 4
RIDER_COLS = 1024
RIDER_EVERY = 4


def _params(sem):
    return pltpu.CompilerParams(dimension_semantics=sem, vmem_limit_bytes=VMEM_LIMIT_BYTES)


def _sigmoid(x):
    return 1.0 / (1.0 + jnp.exp(-x))


def _activate(acc, act):
    if act == ACT_SILU:
        return acc * _sigmoid(acc)
    if act == ACT_SIGMOID:
        return _sigmoid(acc)
    if act == ACT_SCALE:
        return acc * (ATTN_SCALE * LOG2E)
    return acc


def _xwt(x, wt, b):
    return lax.dot_general(x, wt, (((1,), (1,)), ((), ())), preferred_element_type=F32) + b


def _wt_spec(tn):
    return pl.BlockSpec((pl.Element(tn), pl.Element(D_MODEL)), lambda i, j, starts8: (starts8[j] * SUBLANES, 0))


def _proj_body(starts8_ref, x_ref, wt_ref, b_ref, o_ref, *, act, rc):
    wt = wt_ref[...].astype(BF16)
    for c in range(x_ref.shape[0] // rc):
        r = slice(c * rc, (c + 1) * rc)
        o_ref[r, :] = _activate(_xwt(x_ref[r, :], wt, b_ref[...]), act).astype(o_ref.dtype)


def _proj(x, wt, b, starts, act, tm, out_dtype):
    m = x.shape[0]
    starts8 = jnp.asarray([s // SUBLANES for s in starts], jnp.int32)
    b = jnp.concatenate([b[:, s:s + PROJ_TN] for s in starts], axis=1)
    return pl.pallas_call(
        functools.partial(_proj_body, act=act, rc=min(tm, PROJ_RC)),
        grid_spec=pltpu.PrefetchScalarGridSpec(
            num_scalar_prefetch=1,
            grid=(m // tm, len(starts)),
            in_specs=[
                pl.BlockSpec((tm, D_MODEL), lambda i, j, starts8: (i, 0)),
                _wt_spec(PROJ_TN),
                pl.BlockSpec((1, PROJ_TN), lambda i, j, starts8: (0, j)),
            ],
            out_specs=pl.BlockSpec((tm, PROJ_TN), lambda i, j, starts8: (i, j)),
        ),
        out_shape=jax.ShapeDtypeStruct((m, len(starts) * PROJ_TN), out_dtype),
        compiler_params=_params(("arbitrary", "arbitrary")),
        name="proj",
    )(starts8, x, wt, b)


def _proj_kv_body(starts8_ref, x_ref, wt_ref, b_ref, *o_refs, rc):
    j = pl.program_id(1)
    wt = wt_ref[...].astype(BF16)
    per_tile = PROJ_TN // D_KV
    for k in range(len(o_refs) // per_tile):
        @pl.when(j == k)
        def _(k=k):
            for c in range(x_ref.shape[0] // rc):
                acc = _xwt(x_ref[c * rc:(c + 1) * rc, :], wt, b_ref[...])
                for n in range(per_tile * N_KV_HEADS):
                    o_ref = o_refs[per_tile * k + n // N_KV_HEADS]
                    o_ref[pl.ds(N_KV_HEADS * c * rc + n % N_KV_HEADS, rc, stride=N_KV_HEADS), :] = (
                        acc[:, n * HEAD_DIM:(n + 1) * HEAD_DIM])


def _proj_kv(x, wt, b, start, tm):
    m = x.shape[0]
    flat = jax.ShapeDtypeStruct((m * N_KV_HEADS, HEAD_DIM), F32)
    n_tiles = N_KV_PROJ * D_KV // PROJ_TN
    starts8 = jnp.asarray([(start + k * PROJ_TN) // SUBLANES for k in range(n_tiles)], jnp.int32)
    b = b[:, start:start + N_KV_PROJ * D_KV]
    return pl.pallas_call(
        functools.partial(_proj_kv_body, rc=min(tm, PROJ_RC)),
        grid_spec=pltpu.PrefetchScalarGridSpec(
            num_scalar_prefetch=1,
            grid=(m // tm, n_tiles),
            in_specs=[
                pl.BlockSpec((tm, D_MODEL), lambda i, j, starts8: (i, 0)),
                _wt_spec(PROJ_TN),
                pl.BlockSpec((1, PROJ_TN), lambda i, j, starts8: (0, j)),
            ],
            out_specs=[pl.BlockSpec((tm * N_KV_HEADS, HEAD_DIM), lambda i, j, starts8: (i, 0))] * N_KV_PROJ,
        ),
        out_shape=[flat] * N_KV_PROJ,
        compiler_params=_params(("arbitrary", "arbitrary")),
        name="proj_kv",
    )(starts8, x, wt, b)


def _window_means(halo, u, row0, fixed_pos):
    tm = u.shape[0]
    ext = jnp.concatenate([halo, u], axis=0)
    if fixed_pos is None:
        pos = row0 + lax.broadcasted_iota(jnp.int32, (tm, 1), 0)
    else:
        pos = jnp.full((tm, 1), fixed_pos, jnp.int32)
    outs = []
    for g, w in enumerate(POOL_WINDOWS):
        a = ext[:, g * POOL_GC:(g + 1) * POOL_GC]
        s = a
        k = 1
        while k < w:
            s = s + pltpu.roll(s, k, 0)
            k *= 2
        cnt = jnp.minimum(pos + 1, w).astype(F32)
        outs.append((s[2 * SUBLANES:] / cnt - a[2 * SUBLANES:]).astype(BF16))
    return jnp.concatenate(outs, axis=1)


def _pool_body(halo_ref, u_ref, m_ref, *, tm, seq, fixed_pos):
    row0 = (pl.program_id(0) * tm) % seq
    halo = jnp.where(row0 == 0, 0.0, halo_ref[...])
    m_ref[...] = _window_means(halo, u_ref[...], row0, fixed_pos)


def _pool_m(src, col_block, rows, tm, seq, fixed_pos):
    halo_rows = 2 * SUBLANES
    per = tm // halo_rows
    return pl.pallas_call(
        functools.partial(_pool_body, tm=tm, seq=seq, fixed_pos=fixed_pos),
        grid=(rows // tm,),
        in_specs=[
            pl.BlockSpec((halo_rows, D_POOL), lambda i: (jnp.maximum(i * per - 1, 0), col_block)),
            pl.BlockSpec((tm, D_POOL), lambda i: (i, col_block)),
        ],
        out_specs=pl.BlockSpec((tm, D_POOL), lambda i: (i, 0)),
        out_shape=jax.ShapeDtypeStruct((rows, D_POOL), BF16),
        compiler_params=_params(("arbitrary",)),
        name="pool",
    )(src, src)


def _tail_body(*refs, seq):
    if seq is None:
        m = refs[0][...]
        refs = refs[1:]
    else:
        halo_ref, u_ref = refs[:2]
        refs = refs[2:]
        row0 = (pl.program_id(0) * u_ref.shape[0]) % seq
        m = _window_means(jnp.where(row0 == 0, 0.0, halo_ref[...]), u_ref[...], row0, None)
    zp_ref, o_ref, zn_ref, ga_ref, gb_ref, x_ref, pw_ref, ps_ref, wup_ref, wun_ref, wo_ref, lg_ref, lb_ref, y_ref = refs
    tm = x_ref.shape[0]
    rc = min(tm, TAIL_RC)
    for c in range(tm // rc):
        r = slice(c * rc, (c + 1) * rc)
        ys = [jnp.dot(m[r, g * POOL_GC:(g + 1) * POOL_GC], pw_ref[g], preferred_element_type=F32)
              for g in range(len(POOL_WINDOWS))]
        y_pool = jnp.concatenate(ys, axis=1) * ps_ref[...]
        a = jnp.dot((y_pool * zp_ref[r, :]).astype(BF16), wup_ref[...], preferred_element_type=F32)
        b = jnp.dot((o_ref[r, :] * zn_ref[r, :]).astype(BF16), wun_ref[...], preferred_element_type=F32)
        mix = ga_ref[r, :] * a + gb_ref[r, :] * b
        h = jnp.dot(mix.astype(BF16), wo_ref[...], preferred_element_type=F32)
        z = ALPHA * x_ref[r, :] + h
        mu = jnp.mean(z, axis=-1, keepdims=True)
        zc = z - mu
        var = jnp.mean(zc * zc, axis=-1, keepdims=True)
        y_ref[r, :] = zc * lax.rsqrt(var + LN_EPS) * lg_ref[...] + lb_ref[...]


def _tail(pool_in, h_silu, h_sig, o, x, pw, ps, wup, wun, wo, lg, lb, tm, seq=None):
    rows = x.shape[0]
    once = pl.Buffered(1)

    def const(shape):
        return pl.BlockSpec(shape, lambda i: (0,) * len(shape), pipeline_mode=once)

    pool_specs = [pl.BlockSpec((tm, D_POOL), lambda i: (i, 0))]
    pool_args = [pool_in]
    if seq is not None:
        halo_rows = 2 * SUBLANES
        per = tm // halo_rows
        pool_specs.insert(0, pl.BlockSpec((halo_rows, D_POOL), lambda i: (jnp.maximum(i * per - 1, 0), 0)))
        pool_args.insert(0, pool_in)
    return pl.pallas_call(
        functools.partial(_tail_body, seq=seq),
        grid=(rows // tm,),
        in_specs=pool_specs + [
            pl.BlockSpec((tm, D_POOL), lambda i: (i, 0)),
            pl.BlockSpec((tm, D_ATTN), lambda i: (i, 0)),
            pl.BlockSpec((tm, D_ATTN), lambda i: (i, D_POOL // D_ATTN)),
            pl.BlockSpec((tm, D_MODEL), lambda i: (i, 0)),
            pl.BlockSpec((tm, D_MODEL), lambda i: (i, 1)),
            pl.BlockSpec((tm, D_MODEL), lambda i: (i, 0)),
            const((len(POOL_WINDOWS), POOL_GC, POOL_GC)),
            const((1, D_POOL)),
            const((D_POOL, D_MODEL)),
            const((D_ATTN, D_MODEL)),
            const((D_MODEL, D_MODEL)),
            const((1, D_MODEL)),
            const((1, D_MODEL)),
        ],
        out_specs=pl.BlockSpec((tm, D_MODEL), lambda i: (i, 0)),
        out_shape=jax.ShapeDtypeStruct((rows, D_MODEL), F32),
        compiler_params=_params(("arbitrary",)),
        name="tail",
    )(*pool_args, h_silu, o, h_silu, h_sig, h_sig, x, pw, ps, wup, wun, wo, lg, lb)


def _compress_body(pt_ref, c_hbm, pe_ref, w1_ref, w2_ref, *rest, pps, n_chunks, rider_act):
    if rider_act is None:
        o_ref, buf, lhs, res, sem = rest
    else:
        x_ref, wt_ref, b_ref, o_ref, y_ref, buf, lhs, res, sem = rest
    b = pl.program_id(0)
    c = pl.program_id(1)
    step = b * n_chunks + c
    n_steps = pl.num_programs(0) * n_chunks
    nb = pps * BLOCKS_PER_PAGE

    def page_copy(seq_row, page0, slot, p):
        phys = pt_ref[seq_row, page0 + p]
        return pltpu.make_async_copy(c_hbm.at[pl.ds(phys * BLOCKS_PER_PAGE, BLOCKS_PER_PAGE)],
                                     buf.at[slot, pl.ds(BLOCKS_PER_PAGE * p, BLOCKS_PER_PAGE), pl.ds(0, CMP_ROWS), :],
                                     sem.at[slot])

    def start_pages(s, slot, part, parts):
        seq_row = s // n_chunks
        page0 = (s % n_chunks) * pps
        pairs = pps // (2 * parts)

        def pair(i, carry):
            page_copy(seq_row, page0, slot, 2 * i).start(priority=0)
            page_copy(seq_row, page0, slot, 2 * i + 1).start(priority=1)
            return carry
        lax.fori_loop(part * pairs, (part + 1) * pairs, pair, 0)

    @pl.when(step == 0)
    def _():
        for s in range(CMP_SLOTS - 1):
            @pl.when(s < n_steps)
            def _(s=s):
                start_pages(s, s, 0, 1)

    ahead = step + (CMP_SLOTS - 1)

    def request_ahead(part):
        @pl.when(ahead < n_steps)
        def _():
            start_pages(ahead, ahead % CMP_SLOTS, part, CMP_ISSUE_PARTS)

    request_ahead(0)
    if rider_act is not None:
        @pl.when(step % RIDER_EVERY == 0)
        def _():
            y_ref[...] = _activate(_xwt(x_ref[...], wt_ref[...], b_ref[...]), rider_act).astype(y_ref.dtype)

    slot = step % CMP_SLOTS
    for p in range(pps):
        page_copy(b, c * pps, slot, p).wait()

    rows = buf.at[slot].reshape(nb * CMP_PITCH, LANES)
    for j in range(CMP_BLOCK):
        if j % (CMP_BLOCK // CMP_ISSUE_PARTS) == 0 and j > 0:
            request_ahead(j // (CMP_BLOCK // CMP_ISSUE_PARTS))
        x0 = rows[pl.ds(2 * j, nb, stride=CMP_PITCH), :]
        x1 = rows[pl.ds(2 * j + 1, nb, stride=CMP_PITCH), :]
        pe = pe_ref[j:j + 1, :]
        lhs[:, (2 * j) * HEAD_DIM:(2 * j + 1) * HEAD_DIM] = (x0 + pe).astype(BF16)
        lhs[:, (2 * j + 1) * HEAD_DIM:(2 * j + 2) * HEAD_DIM] = (x1 + pe).astype(BF16)
    hid = jnp.dot(lhs[...], w1_ref[...], preferred_element_type=F32)
    hid = hid * _sigmoid(hid)
    out = jnp.dot(hid.astype(BF16), w2_ref[...], preferred_element_type=F32)
    half = nb // 2
    for h in range(N_KV_HEADS):
        hs = slice(h * HEAD_DIM, (h + 1) * HEAD_DIM)
        res[h] = out[:, hs]
        o_ref[0, 0:half, hs] = res[h, pl.ds(0, half, stride=2), :]
        o_ref[0, half:nb, hs] = res[h, pl.ds(1, half, stride=2), :]


def _compress(page_table, cache, pe, w1, w2, pps, rider=None):
    nbatch, n_pages = page_table.shape
    n_chunks = n_pages // pps
    nb = pps * BLOCKS_PER_PAGE
    kdim = CMP_BLOCK * D_KV
    once = pl.Buffered(1)
    in_specs = [
        pl.BlockSpec(memory_space=pl.ANY),
        pl.BlockSpec((CMP_BLOCK, HEAD_DIM), lambda b, c, pt: (0, 0)),
        pl.BlockSpec((kdim, D_KV), lambda b, c, pt: (0, 0), pipeline_mode=once),
        pl.BlockSpec((D_KV, D_KV), lambda b, c, pt: (0, 0)),
    ]
    args = [page_table, cache, pe, w1, w2]
    out_specs = [pl.BlockSpec((1, nb, D_KV), lambda b, c, pt: (b, c, 0))]
    out_shape = [jax.ShapeDtypeStruct((nbatch, n_chunks * nb, D_KV), F32)]
    scratch = [
        pltpu.VMEM((CMP_SLOTS, nb, CMP_PITCH, LANES), F32),
        pltpu.VMEM((nb, kdim), BF16),
        pltpu.VMEM((N_KV_HEADS, nb, HEAD_DIM), F32),
        pltpu.SemaphoreType.DMA((CMP_SLOTS,)),
    ]
    rider_act = None
    if rider is not None:
        x, wt, bias, rider_act, dtype = rider
        rows = x.shape[0] * RIDER_EVERY // (nbatch * n_chunks)
        in_specs += [
            pl.BlockSpec((rows, D_MODEL), lambda b, c, pt: ((b * n_chunks + c) // RIDER_EVERY, 0)),
            pl.BlockSpec((RIDER_COLS, D_MODEL), lambda b, c, pt: (0, 0), pipeline_mode=once),
            pl.BlockSpec((1, RIDER_COLS), lambda b, c, pt: (0, 0)),
        ]
        args += [x, wt, bias]
        out_specs.append(pl.BlockSpec((rows, RIDER_COLS), lambda b, c, pt: ((b * n_chunks + c) // RIDER_EVERY, 0)))
        out_shape.append(jax.ShapeDtypeStruct((x.shape[0], RIDER_COLS), dtype))
    outs = pl.pallas_call(
        functools.partial(_compress_body, pps=pps, n_chunks=n_chunks, rider_act=rider_act),
        grid_spec=pltpu.PrefetchScalarGridSpec(
            num_scalar_prefetch=1,
            grid=(nbatch, n_chunks),
            in_specs=in_specs,
            out_specs=out_specs,
            scratch_shapes=scratch,
        ),
        out_shape=out_shape,
        compiler_params=_params(("arbitrary", "arbitrary")),
        name="compress",
    )(*args)
    return outs[0] if rider is None else outs


def _kprep_body(ks_ref, vs_ref, kw_ref, vw_ref, kts_ref, vso_ref, ktw_ref, vwo_ref, *, tk):
    t = pl.program_id(1)
    kpos = t * tk + lax.broadcasted_iota(jnp.int32, (SLC_BLOCK, tk), 1)
    blk = lax.broadcasted_iota(jnp.int32, (SLC_BLOCK, tk), 0)
    onehot = jnp.where(kpos // SLC_BLOCK == blk, 1.0, 0.0).astype(BF16)
    for g in range(N_KV_HEADS):
        head = pl.ds(g, tk, stride=N_KV_HEADS)
        kts_ref[0, g, 0, 0:HEAD_DIM, :] = ks_ref[head, :].T.astype(BF16)
        kts_ref[0, g, 0, HEAD_DIM:HEAD_DIM + SLC_BLOCK, :] = onehot
        kts_ref[0, g, 0, HEAD_DIM + SLC_BLOCK:, :] = jnp.zeros((SLC_BLOCK, tk), BF16)
        kw = kw_ref[head, :]
        for c in range(tk // LANES):
            ktw_ref[0, g, c] = kw[c * LANES:(c + 1) * LANES, :].T.astype(BF16)
        ones = jnp.ones((tk, HEAD_DIM), BF16)
        vso_ref[0, g, :, 0:HEAD_DIM] = vs_ref[head, :].astype(BF16)
        vso_ref[0, g, :, HEAD_DIM:] = ones
        vwo_ref[0, g, :, 0:HEAD_DIM] = vw_ref[head, :].astype(BF16)
        vwo_ref[0, g, :, HEAD_DIM:] = ones


def _kprep(ks, vs, kw, vw, nbatch, seq, tk):
    nt = seq // tk
    src = pl.BlockSpec((tk * N_KV_HEADS, HEAD_DIM), lambda b, t: (b * nt + t, 0))
    return pl.pallas_call(
        functools.partial(_kprep_body, tk=tk),
        grid=(nbatch, nt),
        in_specs=[src, src, src, src],
        out_specs=[
            pl.BlockSpec((1, N_KV_HEADS, 1, 2 * HEAD_DIM, tk), lambda b, t: (b, 0, t, 0, 0)),
            pl.BlockSpec((1, N_KV_HEADS, tk, 2 * HEAD_DIM), lambda b, t: (b, 0, t, 0)),
            pl.BlockSpec((1, N_KV_HEADS, tk // LANES, HEAD_DIM, LANES), lambda b, t: (b, 0, t, 0, 0)),
            pl.BlockSpec((1, N_KV_HEADS, tk, 2 * HEAD_DIM), lambda b, t: (b, 0, t, 0)),
        ],
        out_shape=[
            jax.ShapeDtypeStruct((nbatch, N_KV_HEADS, nt, 2 * HEAD_DIM, tk), BF16),
            jax.ShapeDtypeStruct((nbatch, N_KV_HEADS, seq, 2 * HEAD_DIM), BF16),
            jax.ShapeDtypeStruct((nbatch, N_KV_HEADS, seq // LANES, HEAD_DIM, LANES), BF16),
            jax.ShapeDtypeStruct((nbatch, N_KV_HEADS, seq, 2 * HEAD_DIM), BF16),
        ],
        compiler_params=_params(("arbitrary", "arbitrary")),
        name="kprep",
    )(ks, vs, kw, vw)


def _masked_softmax(s, mask):
    s = jnp.where(mask, s, NEG)
    mx = jnp.max(s, axis=-1, keepdims=True)
    e = jnp.where(mask, jnp.exp2(s - mx), 0.0)
    l = jnp.sum(e, axis=-1, keepdims=True)
    return e / jnp.where(l > 0.0, l, 1.0)


def _select_bias(p_slc_t, qpos_row, n_slc):
    shape = p_slc_t.shape
    blk = lax.broadcasted_iota(jnp.int32, shape, 0)
    valid = blk * SLC_BLOCK <= qpos_row
    cur = qpos_row // SLC_BLOCK
    forced = (blk == 0) | (blk == cur) | (blk == cur - 1)
    score = jnp.where(valid & forced, FORCE, jnp.where(valid, p_slc_t, -FORCE))
    n_chunks = n_slc // SUBLANES
    chunks = [score[c * SUBLANES:(c + 1) * SUBLANES] for c in range(n_chunks)]
    ranks = [jnp.zeros((SUBLANES, shape[1]), F32) for _ in range(n_chunks)]
    sub = lax.broadcasted_iota(jnp.int32, (SUBLANES, shape[1]), 0)
    for j in range(n_slc):
        row = jnp.broadcast_to(score[j:j + 1, :], (SUBLANES, shape[1]))
        for c in range(n_chunks):
            lo = c * SUBLANES
            if lo > j:
                ahead = row >= chunks[c]
            elif lo + SUBLANES - 1 < j:
                ahead = row > chunks[c]
            else:
                ahead = (row > chunks[c]) | ((row == chunks[c]) & (sub + lo > j))
            ranks[c] = ranks[c] + jnp.where(ahead, 1.0, 0.0)
    rank = jnp.concatenate(ranks, axis=0)
    return jnp.where(rank < min(TOP_N, n_slc), 0.0, NEG)


def _attend_tile(q_ref, qcols, kt, v, width, mask_fn, s_ref, m_scr, acc_scr):
    rows = s_ref.shape[0]
    reps = width // LANES
    for blk in range(rows // ROW_BLOCK):
        rb = slice(blk * ROW_BLOCK, (blk + 1) * ROW_BLOCK)
        s_ref[rb, 0:width] = jnp.dot(q_ref[rb, 0:qcols], kt, preferred_element_type=F32)
    for blk in range(rows // ROW_BLOCK):
        alphas, ps = [], []
        for c in range(ROW_BLOCK // ROW_CHUNK):
            r0 = blk * ROW_BLOCK + c * ROW_CHUNK
            r = slice(r0, r0 + ROW_CHUNK)
            s = s_ref[r, 0:width]
            if mask_fn is not None:
                s = jnp.where(mask_fn(r0, ROW_CHUNK), s, NEG)
            m_old = m_scr[r, :]
            m_new = jnp.maximum(m_old, jnp.max(s, axis=-1, keepdims=True))
            ps.append(jnp.exp2(s - jnp.concatenate([m_new] * reps, axis=1)).astype(BF16))
            alphas.append(jnp.exp2(m_old - m_new))
            m_scr[r, :] = m_new
        alpha = jnp.concatenate(alphas, axis=0)
        rb = slice(blk * ROW_BLOCK, (blk + 1) * ROW_BLOCK)
        acc_scr[rb, :] = (jnp.concatenate([alpha, alpha], axis=1) * acc_scr[rb, :]
                          + jnp.dot(jnp.concatenate(ps, axis=0), v, preferred_element_type=F32))


def _attn_body(q_ref, gn_ref, kc_ref, vc_ref, kts_ref, vs_ref, ktw_ref, vw_ref, o_ref,
               qaug, s_scr, m_scr, acc_scr, *, tq, tk, seq):
    q0 = pl.program_id(1) * tq
    n_cmp = seq // CMP_BLOCK
    n_slc = -(-seq // SLC_BLOCK)
    half = n_cmp // 2
    qpos = q0 + lax.broadcasted_iota(jnp.int32, (tq, 1), 0)
    qpos4 = jnp.concatenate([qpos] * GROUP, axis=0)
    qpos_row = q0 + lax.broadcasted_iota(jnp.int32, (n_slc, tq), 1)
    lane = lax.broadcasted_iota(jnp.int32, (1, n_cmp), 1)
    cmp_blk = 2 * (lane % half) + lane // half
    m_cmp = ((cmp_blk + 1) * CMP_BLOCK - 1) <= qpos4
    gates = gn_ref[...].astype(F32)
    w0 = pl.multiple_of(jnp.maximum(q0 - WINDOW, 0), LANES)
    last = (q0 + tq + tk - 1) // tk - 1

    def chunk_qpos(r0, n):
        return q0 + r0 % tq + lax.broadcasted_iota(jnp.int32, (n, 1), 0)

    def reset(g):
        m_scr[g] = jnp.full(m_scr.shape[1:], -jnp.inf, F32)
        acc_scr[g] = jnp.zeros(acc_scr.shape[1:], F32)

    def result(g):
        acc = acc_scr[g]
        return acc[:, 0:HEAD_DIM] / acc[:, HEAD_DIM:]

    def attend(g, qcols, kt, v, width, mask_fn):
        _attend_tile(qaug.at[g], qcols, kt, v, width, mask_fn, s_scr.at[g], m_scr.at[g], acc_scr.at[g])

    groups = range(N_KV_HEADS)

    o_cmp = []
    for g in groups:
        hs = slice(g * HEAD_DIM, (g + 1) * HEAD_DIM)
        for h in range(GROUP):
            qaug[g, h * tq:(h + 1) * tq, 0:HEAD_DIM] = (
                q_ref[:, (GROUP * g + h) * HEAD_DIM:(GROUP * g + h + 1) * HEAD_DIM])
        kc = kc_ref[0][:, hs].astype(BF16)
        vc = vc_ref[0][:, hs].astype(BF16)
        s = lax.dot_general(qaug[g, :, 0:HEAD_DIM], kc, (((1,), (1,)), ((), ())), preferred_element_type=F32)
        p = _masked_softmax(s, m_cmp)
        o_cmp.append(jnp.dot(p.astype(BF16), vc, preferred_element_type=F32))
        p_grp = p[0:tq]
        for h in range(1, GROUP):
            p_grp = p_grp + p[h * tq:(h + 1) * tq]
        p_slc = p_grp + pltpu.roll(p_grp, half, 1)
        bias_t = _select_bias(p_slc.T[0:n_slc], qpos_row, n_slc)
        bias = jnp.concatenate([bias_t, jnp.zeros((LANES - n_slc, tq), F32)], axis=0).T.astype(BF16)
        for h in range(GROUP):
            qaug[g, h * tq:(h + 1) * tq, HEAD_DIM:] = bias
        reset(g)

    def interior(t, carry):
        k0 = pl.multiple_of(t * tk, tk)
        for g in groups:
            attend(g, 2 * HEAD_DIM, kts_ref[0, g, t], vs_ref[0, g, pl.ds(k0, tk), :], tk, None)
        return carry

    lax.fori_loop(0, last, interior, 0)
    k_last = pl.multiple_of(last * tk, tk)

    def causal(r0, n):
        return k_last + lax.broadcasted_iota(jnp.int32, (1, tk), 1) <= chunk_qpos(r0, n)

    o_sel = []
    for g in groups:
        attend(g, 2 * HEAD_DIM, kts_ref[0, g, last], vs_ref[0, g, pl.ds(k_last, tk), :], tk, causal)
        o_sel.append(result(g))
        reset(g)

    span = WINDOW + tq

    def in_window(r0, n):
        dist = chunk_qpos(r0, n) - (w0 + lax.broadcasted_iota(jnp.int32, (1, span), 1))
        return (dist >= 0) & (dist <= WINDOW)

    for g in groups:
        ktw = jnp.concatenate([ktw_ref[0, g, w0 // LANES + c] for c in range(span // LANES)], axis=1)
        attend(g, HEAD_DIM, ktw, vw_ref[0, g, pl.ds(w0, span), :], span, in_window)

    for g in groups:
        o_win = result(g)
        for h in range(GROUP):
            hd = GROUP * g + h
            r = slice(h * tq, (h + 1) * tq)
            o_ref[:, hd * HEAD_DIM:(hd + 1) * HEAD_DIM] = (
                gates[:, hd:hd + 1] * o_cmp[g][r]
                + gates[:, N_HEADS + hd:N_HEADS + hd + 1] * o_sel[g][r]
                + gates[:, 2 * N_HEADS + hd:2 * N_HEADS + hd + 1] * o_win[r]).astype(o_ref.dtype)


def _prompt_attn(h_q, h_sig, kc, vc, kts, vs, ktw, vw, nbatch, seq, tq, tk):
    nq = seq // tq
    n_cmp = seq // CMP_BLOCK
    nt = seq // tk
    rows = GROUP * tq
    return pl.pallas_call(
        functools.partial(_attn_body, tq=tq, tk=tk, seq=seq),
        grid=(nbatch, nq),
        in_specs=[
            pl.BlockSpec((tq, D_ATTN), lambda b, i: (b * nq + i, 0)),
            pl.BlockSpec((tq, LANES), lambda b, i: (b * nq + i, SIG_GN // LANES)),
            pl.BlockSpec((1, n_cmp, D_KV), lambda b, i: (b, 0, 0)),
            pl.BlockSpec((1, n_cmp, D_KV), lambda b, i: (b, 0, 0)),
            pl.BlockSpec((1, N_KV_HEADS, nt, 2 * HEAD_DIM, tk), lambda b, i: (b, 0, 0, 0, 0)),
            pl.BlockSpec((1, N_KV_HEADS, seq, 2 * HEAD_DIM), lambda b, i: (b, 0, 0, 0)),
            pl.BlockSpec((1, N_KV_HEADS, seq // LANES, HEAD_DIM, LANES), lambda b, i: (b, 0, 0, 0, 0)),
            pl.BlockSpec((1, N_KV_HEADS, seq, 2 * HEAD_DIM), lambda b, i: (b, 0, 0, 0)),
        ],
        out_specs=pl.BlockSpec((tq, D_ATTN), lambda b, i: (b * nq + i, 0)),
        scratch_shapes=[
            pltpu.VMEM((N_KV_HEADS, rows, 2 * HEAD_DIM), BF16),
            pltpu.VMEM((N_KV_HEADS, rows, max(tk, WINDOW + tq)), F32),
            pltpu.VMEM((N_KV_HEADS, rows, LANES), F32),
            pltpu.VMEM((N_KV_HEADS, rows, 2 * HEAD_DIM), F32),
        ],
        out_shape=jax.ShapeDtypeStruct((nbatch * seq, D_ATTN), BF16),
        compiler_params=_params(("arbitrary", "arbitrary")),
        name="attn",
    )(h_q, h_sig, kc, vc, kts, vs, ktw, vw)


def _group_queries(q_row, g):
    heads = [q_row[:, (GROUP * g + h) * HEAD_DIM:(GROUP * g + h + 1) * HEAD_DIM] for h in range(GROUP)]
    pad = jnp.zeros((2 * SUBLANES - GROUP, HEAD_DIM), F32)
    return jnp.concatenate(heads + [pad], axis=0).astype(BF16)


def _scmp_body(q_ref, kc_ref, vc_ref, ocmp_ref, pslc_ref, *, q_pos, chunk):
    b = pl.program_id(0)
    q_row = q_ref[pl.ds(b, 1), :]
    n_cmp = kc_ref.shape[1]
    half = chunk // 2
    lane = lax.broadcasted_iota(jnp.int32, (1, n_cmp), 1)
    within = lane % chunk
    cmp_blk = (lane // chunk) * chunk + 2 * (within % half) + within // half
    m_cmp = ((cmp_blk + 1) * CMP_BLOCK - 1) <= q_pos
    outs = []
    for g in range(N_KV_HEADS):
        hs = slice(g * HEAD_DIM, (g + 1) * HEAD_DIM)
        qg = _group_queries(q_row, g)
        s = lax.dot_general(qg, kc_ref[0][:, hs].astype(BF16), (((1,), (1,)), ((), ())), preferred_element_type=F32)
        p = _masked_softmax(s, m_cmp)
        o = jnp.dot(p.astype(BF16), vc_ref[0][:, hs].astype(BF16), preferred_element_type=F32)
        outs.append(o[0:GROUP])
        p_grp = p[0:1]
        for h in range(1, GROUP):
            p_grp = p_grp + p[h:h + 1]
        parts = []
        for c in range(n_cmp // chunk):
            pc = p_grp[:, c * chunk:(c + 1) * chunk]
            parts.append(pc + pltpu.roll(pc, half, 1))
        pslc_ref[0, g:g + 1, :] = jnp.concatenate(parts, axis=1)
    ocmp_ref[0] = jnp.concatenate(outs, axis=0)


def _sample_cmp(q, kc, vc, q_pos, chunk):
    nbatch, n_cmp, _ = kc.shape
    return pl.pallas_call(
        functools.partial(_scmp_body, q_pos=q_pos, chunk=chunk),
        grid=(nbatch,),
        in_specs=[
            pl.BlockSpec((nbatch, D_ATTN), lambda b: (0, 0)),
            pl.BlockSpec((1, n_cmp, D_KV), lambda b: (b, 0, 0)),
            pl.BlockSpec((1, n_cmp, D_KV), lambda b: (b, 0, 0)),
        ],
        out_specs=[
            pl.BlockSpec((1, N_HEADS, HEAD_DIM), lambda b: (b, 0, 0)),
            pl.BlockSpec((1, N_KV_HEADS, n_cmp), lambda b: (b, 0, 0)),
        ],
        out_shape=[
            jax.ShapeDtypeStruct((nbatch, N_HEADS, HEAD_DIM), F32),
            jax.ShapeDtypeStruct((nbatch, N_KV_HEADS, n_cmp), F32),
        ],
        compiler_params=_params(("arbitrary",)),
        name="s_cmp",
    )(q, kc, vc)


def _stopk_body(p_ref, idx_ref, *, q_pos, chunk, n_slc):
    p = p_ref[...]
    rows, width = p.shape
    half = chunk // 2
    n_in = (width // chunk) * half
    lane = lax.broadcasted_iota(jnp.int32, (1, width), 1)
    within = lane % chunk
    blk = jnp.where(within < half, (lane // chunk) * half + within, -1)
    extra = (lane >= half) & (lane < half + (n_slc - n_in))
    blk = jnp.where(extra, n_in + lane - half, blk)
    real = blk >= 0
    valid = real & (blk * SLC_BLOCK <= q_pos)
    cur = q_pos // SLC_BLOCK
    forced = (blk == 0) | (blk == cur) | (blk == cur - 1)
    base = jnp.where(extra, 0.0, p)
    x = jnp.where(valid & forced, FORCE, jnp.where(valid, base, -FORCE))
    x = jnp.where(real, x, -jnp.inf)
    blk_f = blk.astype(F32)
    out_lane = lax.broadcasted_iota(jnp.int32, (rows, LANES), 1)
    out = jnp.zeros((rows, LANES), F32)
    for r in range(min(TOP_N, n_slc)):
        mx = jnp.max(x, axis=-1, keepdims=True)
        pick = jnp.min(jnp.where(x == mx, blk_f, float(2 ** 30)), axis=-1, keepdims=True)
        out = jnp.where(out_lane == r, pick, out)
        x = jnp.where(blk_f == pick, -jnp.inf, x)
    idx_ref[...] = out.astype(jnp.int32)


def _sample_topk(pslc, q_pos, chunk, n_slc):
    rows, width = pslc.shape
    return pl.pallas_call(
        functools.partial(_stopk_body, q_pos=q_pos, chunk=chunk, n_slc=n_slc),
        grid=(1,),
        in_specs=[pl.BlockSpec((rows, width), lambda i: (0, 0))],
        out_specs=pl.BlockSpec((rows, LANES), lambda i: (0, 0)),
        out_shape=jax.ShapeDtypeStruct((rows, LANES), jnp.int32),
        compiler_params=_params(("arbitrary",)),
        name="s_topk",
    )(pslc)


def _sattn_body(idx_ref, pt_ref, q_ref, gn_ref, ocmp_ref, knew_ref, vnew_ref, kwn_ref, vwn_ref, wk_ref, wv_ref,
                sk_hbm, sv_hbm, o_ref, kbuf, vbuf, sem, *, q_pos, n_top, n_pages, past, per_step):
    step = pl.program_id(0)
    n_steps = pl.num_programs(0)
    half_rows = SLC_BLOCK * N_KV_HEADS

    def copies(s, slot, sub, g, i):
        bb = s * per_step + sub
        blk = idx_ref[bb * N_KV_HEADS + g, i]
        page = jnp.minimum(blk // 2, n_pages - 1)
        start = pl.multiple_of(pt_ref[bb, page] * (PAGE_SIZE * N_KV_HEADS) + (blk % 2) * half_rows, half_rows)
        return [pltpu.make_async_copy(hbm.at[pl.ds(start, half_rows), :], dst.at[slot, sub, g, i],
                                      sem.at[t, slot, sub, g, i])
                for t, (hbm, dst) in enumerate(((sk_hbm, kbuf), (sv_hbm, vbuf)))]

    def all_copies(s, slot):
        return [cp for sub in range(per_step) for g in range(N_KV_HEADS) for i in range(n_top)
                for cp in copies(s, slot, sub, g, i)]

    @pl.when(step == 0)
    def _():
        def one(n, carry):
            sub, rest = n // (N_KV_HEADS * n_top), n % (N_KV_HEADS * n_top)
            for cp in copies(0, 0, sub, rest // n_top, rest % n_top):
                cp.start()
            return carry
        lax.fori_loop(0, per_step * N_KV_HEADS * n_top, one, 0)

    @pl.when(step + 1 < n_steps)
    def _():
        for cp in all_copies(step + 1, (step + 1) % 2):
            cp.start()

    slot = step % 2
    for cp in all_copies(step, slot):
        cp.wait()

    pad = jnp.zeros((2 * SUBLANES - 1, HEAD_DIM), F32)
    for sub in range(per_step):
        _sattn_one(step * per_step + sub, slot, sub, pad, idx_ref, q_ref, gn_ref, ocmp_ref, knew_ref, vnew_ref, kwn_ref,
                   vwn_ref, wk_ref, wv_ref, o_ref, kbuf, vbuf, q_pos=q_pos, n_top=n_top, past=past)


def _sattn_one(b, slot, sub, pad, idx_ref, q_ref, gn_ref, ocmp_ref, knew_ref, vnew_ref, kwn_ref, vwn_ref, wk_ref, wv_ref,
               o_ref, kbuf, vbuf, *, q_pos, n_top, past):
    tail_blk = past // SLC_BLOCK
    q_row = q_ref[pl.ds(b, 1), :]
    gates = gn_ref[pl.ds(b, 1), :]

    def new_row(ref, g):
        return jnp.concatenate([ref[pl.ds(b * N_KV_HEADS + g, 1), :], pad], axis=0).astype(BF16)

    outs = []
    for g in range(N_KV_HEADS):
        qg = _group_queries(q_row, g)
        ks, vs = [], []
        key = lax.broadcasted_iota(jnp.int32, (1, n_top * SLC_BLOCK), 1)
        key_blk = jnp.zeros((1, n_top * SLC_BLOCK), jnp.int32)
        for i in range(n_top):
            blk = idx_ref[b * N_KV_HEADS + g, i]
            ks.append(kbuf[slot, sub, g, i, pl.ds(g, SLC_BLOCK, stride=N_KV_HEADS), :])
            vs.append(vbuf[slot, sub, g, i, pl.ds(g, SLC_BLOCK, stride=N_KV_HEADS), :])
            key_blk = jnp.where(key // SLC_BLOCK == i, blk, key_blk)
        k_sel = jnp.concatenate(ks, axis=0).astype(BF16)
        v_sel = jnp.concatenate(vs, axis=0).astype(BF16)
        m_sel = (key_blk * SLC_BLOCK + key % SLC_BLOCK <= q_pos) & (key_blk < tail_blk)
        tail_sel = jnp.max(jnp.where(key_blk == tail_blk, 1.0, 0.0), axis=-1, keepdims=True) > 0.5
        first = lax.broadcasted_iota(jnp.int32, (1, 2 * SUBLANES), 1) == 0
        o_sel = _two_part_attention(qg, k_sel, v_sel, m_sel, new_row(knew_ref, g), new_row(vnew_ref, g),
                                    first & tail_sel)
        wbuf = wk_ref.shape[1] // N_KV_HEADS
        k_win = wk_ref[sub, pl.ds(g, wbuf, stride=N_KV_HEADS), :].astype(BF16)
        v_win = wv_ref[sub, pl.ds(g, wbuf, stride=N_KV_HEADS), :].astype(BF16)
        dist = q_pos - (past - wbuf + lax.broadcasted_iota(jnp.int32, (1, wbuf), 1))
        m_win = (dist >= 0) & (dist <= WINDOW)
        o_win = _two_part_attention(qg, k_win, v_win, m_win, new_row(kwn_ref, g), new_row(vwn_ref, g), first)
        o_cmp = ocmp_ref[sub, GROUP * g:GROUP * (g + 1), :]
        for h in range(GROUP):
            hd = GROUP * g + h
            outs.append(gates[:, hd:hd + 1] * o_cmp[h:h + 1]
                        + gates[:, N_HEADS + hd:N_HEADS + hd + 1] * o_sel[h:h + 1]
                        + gates[:, 2 * N_HEADS + hd:2 * N_HEADS + hd + 1] * o_win[h:h + 1])
    o_ref[sub] = jnp.concatenate(outs, axis=0)


def _two_part_attention(q, k1, v1, m1, k2, v2, m2):
    dn = (((1,), (1,)), ((), ()))
    s1 = jnp.where(m1, lax.dot_general(q, k1, dn, preferred_element_type=F32), NEG)
    s2 = jnp.where(m2, lax.dot_general(q, k2, dn, preferred_element_type=F32), NEG)
    mx = jnp.maximum(jnp.max(s1, axis=-1, keepdims=True), jnp.max(s2, axis=-1, keepdims=True))
    e1 = jnp.where(m1, jnp.exp2(s1 - mx), 0.0)
    e2 = jnp.where(m2, jnp.exp2(s2 - mx), 0.0)
    l = jnp.sum(e1, axis=-1, keepdims=True) + jnp.sum(e2, axis=-1, keepdims=True)
    inv = 1.0 / jnp.where(l > 0.0, l, 1.0)
    o = (jnp.dot((e1 * inv).astype(BF16), v1, preferred_element_type=F32)
         + jnp.dot((e2 * inv).astype(BF16), v2, preferred_element_type=F32))
    return o


def _sample_attn(idx, page_table, h_q, h_sig, ocmp, new_rows, win_k, win_v, slc_k, slc_v, q_pos, past):
    nbatch, n_pages = page_table.shape
    n_top = min(TOP_N, past // SLC_BLOCK + 1)
    half_rows = SLC_BLOCK * N_KV_HEADS
    flat_new = pl.BlockSpec((nbatch * N_KV_HEADS, HEAD_DIM), lambda b, idx, pt: (0, 0))

    def whole(col_block, width):
        return pl.BlockSpec((nbatch, width), lambda b, idx, pt: (0, col_block))

    per_step = SATTN_PER_STEP
    return pl.pallas_call(
        functools.partial(_sattn_body, q_pos=q_pos, n_top=n_top, n_pages=n_pages, past=past, per_step=per_step),
        grid_spec=pltpu.PrefetchScalarGridSpec(
            num_scalar_prefetch=2,
            grid=(nbatch // per_step,),
            in_specs=[
                whole(0, D_ATTN),
                whole(SIG_GN // LANES, LANES),
                pl.BlockSpec((per_step, N_HEADS, HEAD_DIM), lambda b, idx, pt: (b, 0, 0)),
                flat_new, flat_new, flat_new, flat_new,
                pl.BlockSpec((per_step,) + win_k.shape[1:], lambda b, idx, pt: (b, 0, 0)),
                pl.BlockSpec((per_step,) + win_v.shape[1:], lambda b, idx, pt: (b, 0, 0)),
                pl.BlockSpec(memory_space=pl.ANY),
                pl.BlockSpec(memory_space=pl.ANY),
            ],
            out_specs=pl.BlockSpec((per_step, N_HEADS, HEAD_DIM), lambda b, idx, pt: (b, 0, 0)),
            scratch_shapes=[
                pltpu.VMEM((2, per_step, N_KV_HEADS, n_top, half_rows, LANES), F32),
                pltpu.VMEM((2, per_step, N_KV_HEADS, n_top, half_rows, LANES), F32),
                pltpu.SemaphoreType.DMA((2, 2, per_step, N_KV_HEADS, n_top)),
            ],
        ),
        out_shape=jax.ShapeDtypeStruct((nbatch, N_HEADS, HEAD_DIM), F32),
        compiler_params=_params(("arbitrary",)),
        name="s_attn",
    )(idx, page_table, h_q, h_sig, ocmp, *new_rows, win_k, win_v, slc_k, slc_v)


REF_KV = 3 * D_POOL
REF_GN = REF_KV + N_KV_PROJ * D_KV
REF_ZN = REF_GN + N_GATE
REF_GM = REF_ZN + D_ATTN


def _tiles(start, width):
    return tuple(range(start, start + width, PROJ_TN))


STARTS_U = _tiles(0, D_POOL)
STARTS_SILU = _tiles(D_POOL, D_POOL) + _tiles(REF_ZN, D_ATTN)
STARTS_Q = _tiles(2 * D_POOL, D_ATTN)
STARTS_SIG = _tiles(REF_GM, N_MERGE_COLS) + (REF_GN,)


def _block_diag2(w):
    z = jnp.zeros_like(w)
    return jnp.concatenate([jnp.concatenate([w, z], axis=-1), jnp.concatenate([z, w], axis=-1)], axis=-2)


def _project(x, wt, b, tm, tm_kv, act_dtype, with_uq=True):
    h_silu = _proj(x, wt, b, STARTS_SILU, ACT_SILU, tm, act_dtype)
    h_sig = _proj(x, wt, b, STARTS_SIG, ACT_SIGMOID, tm, act_dtype)
    kv = _proj_kv(x, wt, b, REF_KV, tm_kv)
    if not with_uq:
        return h_silu, h_sig, kv
    u = _proj(x, wt, b, STARTS_U, ACT_NONE, tm, F32)
    h_q = _proj(x, wt, b, STARTS_Q, ACT_SCALE, tm, act_dtype)
    return u, h_silu, h_q, h_sig, kv


def kernel(x_prompt, x_sample, cache_cmp_k, cache_cmp_v, cache_slc_k, cache_slc_v, cache_win_k, cache_win_v,
           state_pool, page_table, w_in, b_in, pool_w, pool_scale, cmp_pe_k, cmp_w1_k, cmp_w2_k, cmp_pe_v, cmp_w1_v,
           cmp_w2_v, w_up_pool, w_up_nsa, w_out, ln_g, ln_b):
    nb_p, seq, _ = x_prompt.shape
    nb_s = x_sample.shape[0]
    n_pages = page_table.shape[1]
    past = n_pages * PAGE_SIZE
    n_phys = cache_cmp_k.shape[1]
    wbuf = cache_win_k.shape[2]

    wt = w_in[0].T
    b = b_in[0][None, :]
    kdim = CMP_BLOCK * D_KV
    cmp_k = (cmp_pe_k[0], _block_diag2(cmp_w1_k[0]).reshape(kdim, D_KV).astype(BF16),
             _block_diag2(cmp_w2_k[0]).astype(BF16))
    cmp_v = (cmp_pe_v[0], _block_diag2(cmp_w1_v[0]).reshape(kdim, D_KV).astype(BF16),
             _block_diag2(cmp_w2_v[0]).astype(BF16))
    pw = pool_w[0].astype(BF16)
    ps = pool_scale[0][None, :]
    wup = w_up_pool[0].astype(BF16)
    wun = w_up_nsa[0].astype(BF16)
    wo = w_out[0].astype(BF16)
    lg = ln_g[0][None, :]
    lb = ln_b[0][None, :]

    xp = x_prompt.reshape(nb_p * seq, D_MODEL)
    xs = x_sample.reshape(nb_s, D_MODEL)
    xp_bf = xp.astype(BF16)
    hsilu_p, hsig_p, kv_p = _project(xp_bf, wt, b, tm=2048, tm_kv=2048, act_dtype=BF16, with_uq=False)
    u_s, hsilu_s, hq_s, hsig_s, kv_s = _project(xs.astype(BF16), wt, b, tm=nb_s, tm_kv=nb_s, act_dtype=F32)

    blocks = (-1, CMP_ROWS, LANES)
    chunk_pages = 64
    q0 = STARTS_Q[0]
    wt_u, wt_q = lax.optimization_barrier((wt[:RIDER_COLS], wt[q0:q0 + RIDER_COLS]))
    kc_s, u_p = _compress(page_table, cache_cmp_k.reshape(blocks), *cmp_k, pps=chunk_pages,
                          rider=(xp_bf, wt_u.astype(BF16), b[:, :RIDER_COLS], ACT_NONE, F32))
    vc_s, hq_p = _compress(page_table, cache_cmp_v.reshape(blocks), *cmp_v, pps=chunk_pages,
                           rider=(xp_bf, wt_q.astype(BF16), b[:, q0:q0 + RIDER_COLS], ACT_SCALE, BF16))

    prompt_pages = seq // PAGE_SIZE
    ident = jnp.arange(nb_p * prompt_pages, dtype=jnp.int32).reshape(nb_p, prompt_pages)
    kc_p = _compress(ident, kv_p[0].reshape(blocks), *cmp_k, pps=prompt_pages)
    vc_p = _compress(ident, kv_p[1].reshape(blocks), *cmp_v, pps=prompt_pages)
    kts, vs, ktw, vw = _kprep(kv_p[2], kv_p[3], kv_p[4], kv_p[5], nb_p, seq, tk=512)
    o_p = _prompt_attn(hq_p, hsig_p, kc_p, vc_p, kts, vs, ktw, vw, nb_p, seq, tq=256, tk=512)
    y_p = _tail(u_p, hsilu_p, hsig_p, o_p, xp, pw, ps, wup, wun, wo, lg, lb, tm=512, seq=seq)

    chunk = chunk_pages * BLOCKS_PER_PAGE
    n_slc = past // SLC_BLOCK + 1
    ocmp_s, pslc_s = _sample_cmp(hq_s, kc_s, vc_s, past, chunk)
    idx = _sample_topk(pslc_s.reshape(nb_s * N_KV_HEADS, -1), past, chunk, n_slc)
    flat = (n_phys * PAGE_SIZE * N_KV_HEADS, HEAD_DIM)
    o_s = _sample_attn(idx, page_table, hq_s, hsig_s, ocmp_s, kv_s[2:6],
                       cache_win_k.reshape(nb_s, wbuf * N_KV_HEADS, HEAD_DIM),
                       cache_win_v.reshape(nb_s, wbuf * N_KV_HEADS, HEAD_DIM),
                       cache_slc_k.reshape(flat), cache_slc_v.reshape(flat), past, past)
    ctx = jnp.concatenate([state_pool[0], u_s[:, None, :]], axis=1)
    m_s = _pool_m(ctx.reshape(nb_s * (POOL_CTX + 1), D_POOL), 0, nb_s * (POOL_CTX + 1), tm=nb_s * (POOL_CTX + 1),
                  seq=nb_s * (POOL_CTX + 1), fixed_pos=past)
    m_s = m_s.reshape(nb_s, POOL_CTX + 1, D_POOL)[:, POOL_CTX]
    y_s = _tail(m_s, hsilu_s, hsig_s, o_s.reshape(nb_s, D_ATTN), xs, pw, ps, wup, wun, wo, lg, lb, tm=nb_s)

    wl = min(WINDOW, seq)
    kv_p = [a.reshape(1, nb_p, seq, N_KV_HEADS, HEAD_DIM) for a in kv_p]
    kv_s = [a.reshape(1, nb_s, 1, N_KV_HEADS, HEAD_DIM) for a in kv_s]
    return (
        y_p.reshape(nb_p, seq, D_MODEL),
        y_s.reshape(nb_s, 1, D_MODEL),
        kv_p[0], kv_p[1], kv_p[2], kv_p[3],
        kv_p[4][:, :, seq - wl:], kv_p[5][:, :, seq - wl:],
        u_p.reshape(nb_p, seq, D_POOL)[None, :, seq - POOL_CTX:],
        kv_s[0], kv_s[1], kv_s[2], kv_s[3],
        jnp.concatenate([cache_win_k, kv_s[4]], axis=2)[:, :, 1:],
        jnp.concatenate([cache_win_v, kv_s[5]], axis=2)[:, :, 1:],
        ctx[None, :, 1:],
    )
```

```python
import functools

import jax
import jax.numpy as jnp
import numpy as np
from jax import lax
from jax.experimental import pallas as pl
from jax.experimental.pallas import tpu as pltpu

D_MODEL = 2048
D_POOL = 1024
POOL_WINDOWS = (2, 4, 8, 16)
POOL_GC = D_POOL // len(POOL_WINDOWS)
POOL_CTX = max(POOL_WINDOWS) - 1
HEAD_DIM = 128
N_HEADS = 8
N_KV_HEADS = 2
GROUP = N_HEADS // N_KV_HEADS
D_ATTN = N_HEADS * HEAD_DIM
D_KV = N_KV_HEADS * HEAD_DIM
CMP_BLOCK = 32
SLC_BLOCK = 64
TOP_N = 16
WINDOW = 512
PAGE_SIZE = 128
ATTN_SCALE = HEAD_DIM ** -0.5
DEPTH = 1
ALPHA = (2.0 * DEPTH) ** 0.25
LN_EPS = 1e-5
NEG = -1e30
FORCE = 1e6

F32 = jnp.float32
BF16 = jnp.bfloat16

SUBLANES = 8
LANES = 128
VMEM_LIMIT_BYTES = 56 * 1024 * 1024

PROJ_TN = 512
PROJ_RC = 512
N_MERGE_COLS = 2 * D_MODEL
SIG_GN = N_MERGE_COLS
N_GATE = 3 * N_HEADS
N_KV_PROJ = 6
ACT_NONE, ACT_SILU, ACT_SIGMOID, ACT_SCALE = range(4)
LOG2E = 1.4426950408889634
SATTN_PER_STEP = 2
TAIL_RC = 128
ROW_BLOCK = 256
ROW_CHUNK = 32

CMP_ROWS = CMP_BLOCK * N_KV_HEADS
CMP_PITCH = 72
BLOCKS_PER_PAGE = PAGE_SIZE // CMP_BLOCK
CMP_SLOTS = 3


def _params(sem):
    return pltpu.CompilerParams(dimension_semantics=sem, vmem_limit_bytes=VMEM_LIMIT_BYTES)


def _sigmoid(x):
    return 1.0 / (1.0 + jnp.exp(-x))


def _activate(acc, act):
    if act == ACT_SILU:
        return acc * _sigmoid(acc)
    if act == ACT_SIGMOID:
        return _sigmoid(acc)
    if act == ACT_SCALE:
        return acc * (ATTN_SCALE * LOG2E)
    return acc


def _xwt(x, wt, b):
    return lax.dot_general(x, wt, (((1,), (1,)), ((), ())), preferred_element_type=F32) + b


def _wt_spec(tn):
    return pl.BlockSpec((pl.Element(tn), pl.Element(D_MODEL)), lambda i, j, starts8: (starts8[j] * SUBLANES, 0))


def _proj_body(starts8_ref, x_ref, wt_ref, b_ref, o_ref, *, act, rc):
    wt = wt_ref[...].astype(BF16)
    for c in range(x_ref.shape[0] // rc):
        r = slice(c * rc, (c + 1) * rc)
        o_ref[r, :] = _activate(_xwt(x_ref[r, :], wt, b_ref[...]), act).astype(o_ref.dtype)


def _proj(x, wt, b, starts, act, tm, out_dtype):
    m = x.shape[0]
    starts8 = jnp.asarray([s // SUBLANES for s in starts], jnp.int32)
    b = jnp.concatenate([b[:, s:s + PROJ_TN] for s in starts], axis=1)
    return pl.pallas_call(
        functools.partial(_proj_body, act=act, rc=min(tm, PROJ_RC)),
        grid_spec=pltpu.PrefetchScalarGridSpec(
            num_scalar_prefetch=1,
            grid=(m // tm, len(starts)),
            in_specs=[
                pl.BlockSpec((tm, D_MODEL), lambda i, j, starts8: (i, 0)),
                _wt_spec(PROJ_TN),
                pl.BlockSpec((1, PROJ_TN), lambda i, j, starts8: (0, j)),
            ],
            out_specs=pl.BlockSpec((tm, PROJ_TN), lambda i, j, starts8: (i, j)),
        ),
        out_shape=jax.ShapeDtypeStruct((m, len(starts) * PROJ_TN), out_dtype),
        compiler_params=_params(("arbitrary", "arbitrary")),
        name="proj",
    )(starts8, x, wt, b)


def _proj_kv_body(starts8_ref, x_ref, wt_ref, b_ref, *o_refs, rc):
    j = pl.program_id(1)
    wt = wt_ref[...].astype(BF16)
    per_tile = PROJ_TN // D_KV
    for k in range(len(o_refs) // per_tile):
        @pl.when(j == k)
        def _(k=k):
            for c in range(x_ref.shape[0] // rc):
                acc = _xwt(x_ref[c * rc:(c + 1) * rc, :], wt, b_ref[...])
                for n in range(per_tile * N_KV_HEADS):
                    o_ref = o_refs[per_tile * k + n // N_KV_HEADS]
                    o_ref[pl.ds(N_KV_HEADS * c * rc + n % N_KV_HEADS, rc, stride=N_KV_HEADS), :] = (
                        acc[:, n * HEAD_DIM:(n + 1) * HEAD_DIM])


def _proj_kv(x, wt, b, start, tm):
    m = x.shape[0]
    flat = jax.ShapeDtypeStruct((m * N_KV_HEADS, HEAD_DIM), F32)
    n_tiles = N_KV_PROJ * D_KV // PROJ_TN
    starts8 = jnp.asarray([(start + k * PROJ_TN) // SUBLANES for k in range(n_tiles)], jnp.int32)
    b = b[:, start:start + N_KV_PROJ * D_KV]
    return pl.pallas_call(
        functools.partial(_proj_kv_body, rc=min(tm, PROJ_RC)),
        grid_spec=pltpu.PrefetchScalarGridSpec(
            num_scalar_prefetch=1,
            grid=(m // tm, n_tiles),
            in_specs=[
                pl.BlockSpec((tm, D_MODEL), lambda i, j, starts8: (i, 0)),
                _wt_spec(PROJ_TN),
                pl.BlockSpec((1, PROJ_TN), lambda i, j, starts8: (0, j)),
            ],
            out_specs=[pl.BlockSpec((tm * N_KV_HEADS, HEAD_DIM), lambda i, j, starts8: (i, 0))] * N_KV_PROJ,
        ),
        out_shape=[flat] * N_KV_PROJ,
        compiler_params=_params(("arbitrary", "arbitrary")),
        name="proj_kv",
    )(starts8, x, wt, b)


def _window_means(halo, u, row0, fixed_pos):
    tm = u.shape[0]
    ext = jnp.concatenate([halo, u], axis=0)
    if fixed_pos is None:
        pos = row0 + lax.broadcasted_iota(jnp.int32, (tm, 1), 0)
    else:
        pos = jnp.full((tm, 1), fixed_pos, jnp.int32)
    outs = []
    for g, w in enumerate(POOL_WINDOWS):
        a = ext[:, g * POOL_GC:(g + 1) * POOL_GC]
        s = a
        k = 1
        while k < w:
            s = s + pltpu.roll(s, k, 0)
            k *= 2
        cnt = jnp.minimum(pos + 1, w).astype(F32)
        outs.append((s[2 * SUBLANES:] / cnt - a[2 * SUBLANES:]).astype(BF16))
    return jnp.concatenate(outs, axis=1)


def _pool_body(halo_ref, u_ref, m_ref, *, tm, seq, fixed_pos):
    row0 = (pl.program_id(0) * tm) % seq
    halo = jnp.where(row0 == 0, 0.0, halo_ref[...])
    m_ref[...] = _window_means(halo, u_ref[...], row0, fixed_pos)


def _pool_m(src, col_block, rows, tm, seq, fixed_pos):
    halo_rows = 2 * SUBLANES
    per = tm // halo_rows
    return pl.pallas_call(
        functools.partial(_pool_body, tm=tm, seq=seq, fixed_pos=fixed_pos),
        grid=(rows // tm,),
        in_specs=[
            pl.BlockSpec((halo_rows, D_POOL), lambda i: (jnp.maximum(i * per - 1, 0), col_block)),
            pl.BlockSpec((tm, D_POOL), lambda i: (i, col_block)),
        ],
        out_specs=pl.BlockSpec((tm, D_POOL), lambda i: (i, 0)),
        out_shape=jax.ShapeDtypeStruct((rows, D_POOL), BF16),
        compiler_params=_params(("arbitrary",)),
        name="pool",
    )(src, src)


def _tail_body(*refs, seq):
    if seq is None:
        m = refs[0][...]
        refs = refs[1:]
    else:
        halo_ref, u_ref = refs[:2]
        refs = refs[2:]
        row0 = (pl.program_id(0) * u_ref.shape[0]) % seq
        m = _window_means(jnp.where(row0 == 0, 0.0, halo_ref[...]), u_ref[...], row0, None)
    zp_ref, o_ref, zn_ref, ga_ref, gb_ref, x_ref, pw_ref, ps_ref, wup_ref, wun_ref, wo_ref, lg_ref, lb_ref, y_ref = refs
    tm = x_ref.shape[0]
    rc = min(tm, TAIL_RC)
    for c in range(tm // rc):
        r = slice(c * rc, (c + 1) * rc)
        ys = [jnp.dot(m[r, g * POOL_GC:(g + 1) * POOL_GC], pw_ref[g], preferred_element_type=F32)
              for g in range(len(POOL_WINDOWS))]
        y_pool = jnp.concatenate(ys, axis=1) * ps_ref[...]
        a = jnp.dot((y_pool * zp_ref[r, :]).astype(BF16), wup_ref[...], preferred_element_type=F32)
        b = jnp.dot((o_ref[r, :] * zn_ref[r, :]).astype(BF16), wun_ref[...], preferred_element_type=F32)
        mix = ga_ref[r, :] * a + gb_ref[r, :] * b
        h = jnp.dot(mix.astype(BF16), wo_ref[...], preferred_element_type=F32)
        z = ALPHA * x_ref[r, :] + h
        mu = jnp.mean(z, axis=-1, keepdims=True)
        zc = z - mu
        var = jnp.mean(zc * zc, axis=-1, keepdims=True)
        y_ref[r, :] = zc * lax.rsqrt(var + LN_EPS) * lg_ref[...] + lb_ref[...]


def _tail(pool_in, h_silu, h_sig, o, x, pw, ps, wup, wun, wo, lg, lb, tm, seq=None):
    rows = x.shape[0]
    once = pl.Buffered(1)

    def const(shape):
        return pl.BlockSpec(shape, lambda i: (0,) * len(shape), pipeline_mode=once)

    pool_specs = [pl.BlockSpec((tm, D_POOL), lambda i: (i, 0))]
    pool_args = [pool_in]
    if seq is not None:
        halo_rows = 2 * SUBLANES
        per = tm // halo_rows
        pool_specs.insert(0, pl.BlockSpec((halo_rows, D_POOL), lambda i: (jnp.maximum(i * per - 1, 0), 0)))
        pool_args.insert(0, pool_in)
    return pl.pallas_call(
        functools.partial(_tail_body, seq=seq),
        grid=(rows // tm,),
        in_specs=pool_specs + [
            pl.BlockSpec((tm, D_POOL), lambda i: (i, 0)),
            pl.BlockSpec((tm, D_ATTN), lambda i: (i, 0)),
            pl.BlockSpec((tm, D_ATTN), lambda i: (i, D_POOL // D_ATTN)),
            pl.BlockSpec((tm, D_MODEL), lambda i: (i, 0)),
            pl.BlockSpec((tm, D_MODEL), lambda i: (i, 1)),
            pl.BlockSpec((tm, D_MODEL), lambda i: (i, 0)),
            const((len(POOL_WINDOWS), POOL_GC, POOL_GC)),
            const((1, D_POOL)),
            const((D_POOL, D_MODEL)),
            const((D_ATTN, D_MODEL)),
            const((D_MODEL, D_MODEL)),
            const((1, D_MODEL)),
            const((1, D_MODEL)),
        ],
        out_specs=pl.BlockSpec((tm, D_MODEL), lambda i: (i, 0)),
        out_shape=jax.ShapeDtypeStruct((rows, D_MODEL), F32),
        compiler_params=_params(("arbitrary",)),
        name="tail",
    )(*pool_args, h_silu, o, h_silu, h_sig, h_sig, x, pw, ps, wup, wun, wo, lg, lb)


def _compress_body(pt_ref, c_hbm, pe_ref, w1_ref, w2_ref, o_ref, buf, lhs, res, sem, *, pps, n_chunks):
    b = pl.program_id(0)
    c = pl.program_id(1)
    step = b * n_chunks + c
    n_steps = pl.num_programs(0) * n_chunks
    nb = pps * BLOCKS_PER_PAGE

    def page_copy(seq_row, page0, slot, p):
        phys = pt_ref[seq_row, page0 + p]
        return pltpu.make_async_copy(c_hbm.at[pl.ds(phys * BLOCKS_PER_PAGE, BLOCKS_PER_PAGE)],
                                     buf.at[slot, pl.ds(BLOCKS_PER_PAGE * p, BLOCKS_PER_PAGE), pl.ds(0, CMP_ROWS), :],
                                     sem.at[slot])

    def start_step(s, slot):
        seq_row = s // n_chunks
        page0 = (s % n_chunks) * pps

        def pair(i, carry):
            page_copy(seq_row, page0, slot, 2 * i).start(priority=0)
            page_copy(seq_row, page0, slot, 2 * i + 1).start(priority=1)
            return carry
        lax.fori_loop(0, pps // 2, pair, 0)

    @pl.when(step == 0)
    def _():
        for s in range(CMP_SLOTS - 1):
            @pl.when(s < n_steps)
            def _(s=s):
                start_step(s, s)

    ahead = step + (CMP_SLOTS - 1)

    @pl.when(ahead < n_steps)
    def _():
        start_step(ahead, ahead % CMP_SLOTS)

    slot = step % CMP_SLOTS
    for p in range(pps):
        page_copy(b, c * pps, slot, p).wait()

    rows = buf.at[slot].reshape(nb * CMP_PITCH, LANES)
    for j in range(CMP_BLOCK):
        x0 = rows[pl.ds(2 * j, nb, stride=CMP_PITCH), :]
        x1 = rows[pl.ds(2 * j + 1, nb, stride=CMP_PITCH), :]
        pe = pe_ref[j:j + 1, :]
        lhs[:, (2 * j) * HEAD_DIM:(2 * j + 1) * HEAD_DIM] = (x0 + pe).astype(BF16)
        lhs[:, (2 * j + 1) * HEAD_DIM:(2 * j + 2) * HEAD_DIM] = (x1 + pe).astype(BF16)
    hid = jnp.dot(lhs[...], w1_ref[...], preferred_element_type=F32)
    hid = hid * _sigmoid(hid)
    out = jnp.dot(hid.astype(BF16), w2_ref[...], preferred_element_type=F32)
    half = nb // 2
    for h in range(N_KV_HEADS):
        hs = slice(h * HEAD_DIM, (h + 1) * HEAD_DIM)
        res[h] = out[:, hs]
        o_ref[0, 0:half, hs] = res[h, pl.ds(0, half, stride=2), :]
        o_ref[0, half:nb, hs] = res[h, pl.ds(1, half, stride=2), :]


def _compress(page_table, cache, pe, w1, w2, pps):
    nbatch, n_pages = page_table.shape
    n_chunks = n_pages // pps
    nb = pps * BLOCKS_PER_PAGE
    kdim = CMP_BLOCK * D_KV
    return pl.pallas_call(
        functools.partial(_compress_body, pps=pps, n_chunks=n_chunks),
        grid_spec=pltpu.PrefetchScalarGridSpec(
            num_scalar_prefetch=1,
            grid=(nbatch, n_chunks),
            in_specs=[
                pl.BlockSpec(memory_space=pl.ANY),
                pl.BlockSpec((CMP_BLOCK, HEAD_DIM), lambda b, c, pt: (0, 0)),
                pl.BlockSpec((kdim, D_KV), lambda b, c, pt: (0, 0), pipeline_mode=pl.Buffered(1)),
                pl.BlockSpec((D_KV, D_KV), lambda b, c, pt: (0, 0)),
            ],
            out_specs=pl.BlockSpec((1, nb, D_KV), lambda b, c, pt: (b, c, 0)),
            scratch_shapes=[
                pltpu.VMEM((CMP_SLOTS, nb, CMP_PITCH, LANES), F32),
                pltpu.VMEM((nb, kdim), BF16),
                pltpu.VMEM((N_KV_HEADS, nb, HEAD_DIM), F32),
                pltpu.SemaphoreType.DMA((CMP_SLOTS,)),
            ],
        ),
        out_shape=jax.ShapeDtypeStruct((nbatch, n_chunks * nb, D_KV), F32),
        compiler_params=_params(("arbitrary", "arbitrary")),
        name="compress",
    )(page_table, cache, pe, w1, w2)


def _kprep_body(ks_ref, vs_ref, kw_ref, vw_ref, kts_ref, vso_ref, ktw_ref, vwo_ref, *, tk):
    t = pl.program_id(1)
    kpos = t * tk + lax.broadcasted_iota(jnp.int32, (SLC_BLOCK, tk), 1)
    blk = lax.broadcasted_iota(jnp.int32, (SLC_BLOCK, tk), 0)
    onehot = jnp.where(kpos // SLC_BLOCK == blk, 1.0, 0.0).astype(BF16)
    for g in range(N_KV_HEADS):
        head = pl.ds(g, tk, stride=N_KV_HEADS)
        kts_ref[0, g, 0, 0:HEAD_DIM, :] = ks_ref[head, :].T.astype(BF16)
        kts_ref[0, g, 0, HEAD_DIM:HEAD_DIM + SLC_BLOCK, :] = onehot
        kts_ref[0, g, 0, HEAD_DIM + SLC_BLOCK:, :] = jnp.zeros((SLC_BLOCK, tk), BF16)
        kw = kw_ref[head, :]
        for c in range(tk // LANES):
            ktw_ref[0, g, c] = kw[c * LANES:(c + 1) * LANES, :].T.astype(BF16)
        ones = jnp.ones((tk, HEAD_DIM), BF16)
        vso_ref[0, g, :, 0:HEAD_DIM] = vs_ref[head, :].astype(BF16)
        vso_ref[0, g, :, HEAD_DIM:] = ones
        vwo_ref[0, g, :, 0:HEAD_DIM] = vw_ref[head, :].astype(BF16)
        vwo_ref[0, g, :, HEAD_DIM:] = ones


def _kprep(ks, vs, kw, vw, nbatch, seq, tk):
    nt = seq // tk
    src = pl.BlockSpec((tk * N_KV_HEADS, HEAD_DIM), lambda b, t: (b * nt + t, 0))
    return pl.pallas_call(
        functools.partial(_kprep_body, tk=tk),
        grid=(nbatch, nt),
        in_specs=[src, src, src, src],
        out_specs=[
            pl.BlockSpec((1, N_KV_HEADS, 1, 2 * HEAD_DIM, tk), lambda b, t: (b, 0, t, 0, 0)),
            pl.BlockSpec((1, N_KV_HEADS, tk, 2 * HEAD_DIM), lambda b, t: (b, 0, t, 0)),
            pl.BlockSpec((1, N_KV_HEADS, tk // LANES, HEAD_DIM, LANES), lambda b, t: (b, 0, t, 0, 0)),
            pl.BlockSpec((1, N_KV_HEADS, tk, 2 * HEAD_DIM), lambda b, t: (b, 0, t, 0)),
        ],
        out_shape=[
            jax.ShapeDtypeStruct((nbatch, N_KV_HEADS, nt, 2 * HEAD_DIM, tk), BF16),
            jax.ShapeDtypeStruct((nbatch, N_KV_HEADS, seq, 2 * HEAD_DIM), BF16),
            jax.ShapeDtypeStruct((nbatch, N_KV_HEADS, seq // LANES, HEAD_DIM, LANES), BF16),
            jax.ShapeDtypeStruct((nbatch, N_KV_HEADS, seq, 2 * HEAD_DIM), BF16),
        ],
        compiler_params=_params(("arbitrary", "arbitrary")),
        name="kprep",
    )(ks, vs, kw, vw)


def _masked_softmax(s, mask):
    s = jnp.where(mask, s, NEG)
    mx = jnp.max(s, axis=-1, keepdims=True)
    e = jnp.where(mask, jnp.exp2(s - mx), 0.0)
    l = jnp.sum(e, axis=-1, keepdims=True)
    return e / jnp.where(l > 0.0, l, 1.0)


def _select_bias(p_slc_t, qpos_row, n_slc):
    shape = p_slc_t.shape
    blk = lax.broadcasted_iota(jnp.int32, shape, 0)
    valid = blk * SLC_BLOCK <= qpos_row
    cur = qpos_row // SLC_BLOCK
    forced = (blk == 0) | (blk == cur) | (blk == cur - 1)
    score = jnp.where(valid & forced, FORCE, jnp.where(valid, p_slc_t, -FORCE))
    n_chunks = n_slc // SUBLANES
    chunks = [score[c * SUBLANES:(c + 1) * SUBLANES] for c in range(n_chunks)]
    ranks = [jnp.zeros((SUBLANES, shape[1]), F32) for _ in range(n_chunks)]
    sub = lax.broadcasted_iota(jnp.int32, (SUBLANES, shape[1]), 0)
    for j in range(n_slc):
        row = jnp.broadcast_to(score[j:j + 1, :], (SUBLANES, shape[1]))
        for c in range(n_chunks):
            lo = c * SUBLANES
            if lo > j:
                ahead = row >= chunks[c]
            elif lo + SUBLANES - 1 < j:
                ahead = row > chunks[c]
            else:
                ahead = (row > chunks[c]) | ((row == chunks[c]) & (sub + lo > j))
            ranks[c] = ranks[c] + jnp.where(ahead, 1.0, 0.0)
    rank = jnp.concatenate(ranks, axis=0)
    return jnp.where(rank < min(TOP_N, n_slc), 0.0, NEG)


def _attend_tile(q_ref, qcols, kt, v, width, mask_fn, s_ref, m_scr, acc_scr):
    rows = s_ref.shape[0]
    reps = width // LANES
    for blk in range(rows // ROW_BLOCK):
        rb = slice(blk * ROW_BLOCK, (blk + 1) * ROW_BLOCK)
        s_ref[rb, 0:width] = jnp.dot(q_ref[rb, 0:qcols], kt, preferred_element_type=F32)
    for blk in range(rows // ROW_BLOCK):
        alphas, ps = [], []
        for c in range(ROW_BLOCK // ROW_CHUNK):
            r0 = blk * ROW_BLOCK + c * ROW_CHUNK
            r = slice(r0, r0 + ROW_CHUNK)
            s = s_ref[r, 0:width]
            if mask_fn is not None:
                s = jnp.where(mask_fn(r0, ROW_CHUNK), s, NEG)
            m_old = m_scr[r, :]
            m_new = jnp.maximum(m_old, jnp.max(s, axis=-1, keepdims=True))
            ps.append(jnp.exp2(s - jnp.concatenate([m_new] * reps, axis=1)).astype(BF16))
            alphas.append(jnp.exp2(m_old - m_new))
            m_scr[r, :] = m_new
        alpha = jnp.concatenate(alphas, axis=0)
        rb = slice(blk * ROW_BLOCK, (blk + 1) * ROW_BLOCK)
        acc_scr[rb, :] = (jnp.concatenate([alpha, alpha], axis=1) * acc_scr[rb, :]
                          + jnp.dot(jnp.concatenate(ps, axis=0), v, preferred_element_type=F32))


def _attn_body(q_ref, gn_ref, kc_ref, vc_ref, kts_ref, vs_ref, ktw_ref, vw_ref, o_ref,
               qaug, s_scr, m_scr, acc_scr, *, tq, tk, seq):
    q0 = pl.program_id(1) * tq
    n_cmp = seq // CMP_BLOCK
    n_slc = -(-seq // SLC_BLOCK)
    half = n_cmp // 2
    qpos = q0 + lax.broadcasted_iota(jnp.int32, (tq, 1), 0)
    qpos4 = jnp.concatenate([qpos] * GROUP, axis=0)
    qpos_row = q0 + lax.broadcasted_iota(jnp.int32, (n_slc, tq), 1)
    lane = lax.broadcasted_iota(jnp.int32, (1, n_cmp), 1)
    cmp_blk = 2 * (lane % half) + lane // half
    m_cmp = ((cmp_blk + 1) * CMP_BLOCK - 1) <= qpos4
    gates = gn_ref[...].astype(F32)
    w0 = pl.multiple_of(jnp.maximum(q0 - WINDOW, 0), LANES)
    last = (q0 + tq + tk - 1) // tk - 1

    def chunk_qpos(r0, n):
        return q0 + r0 % tq + lax.broadcasted_iota(jnp.int32, (n, 1), 0)

    def reset(g):
        m_scr[g] = jnp.full(m_scr.shape[1:], -jnp.inf, F32)
        acc_scr[g] = jnp.zeros(acc_scr.shape[1:], F32)

    def result(g):
        acc = acc_scr[g]
        return acc[:, 0:HEAD_DIM] / acc[:, HEAD_DIM:]

    def attend(g, qcols, kt, v, width, mask_fn):
        _attend_tile(qaug.at[g], qcols, kt, v, width, mask_fn, s_scr.at[g], m_scr.at[g], acc_scr.at[g])

    groups = range(N_KV_HEADS)

    o_cmp = []
    for g in groups:
        hs = slice(g * HEAD_DIM, (g + 1) * HEAD_DIM)
        for h in range(GROUP):
            qaug[g, h * tq:(h + 1) * tq, 0:HEAD_DIM] = (
                q_ref[:, (GROUP * g + h) * HEAD_DIM:(GROUP * g + h + 1) * HEAD_DIM])
        kc = kc_ref[0][:, hs].astype(BF16)
        vc = vc_ref[0][:, hs].astype(BF16)
        s = lax.dot_general(qaug[g, :, 0:HEAD_DIM], kc, (((1,), (1,)), ((), ())), preferred_element_type=F32)
        p = _masked_softmax(s, m_cmp)
        o_cmp.append(jnp.dot(p.astype(BF16), vc, preferred_element_type=F32))
        p_grp = p[0:tq]
        for h in range(1, GROUP):
            p_grp = p_grp + p[h * tq:(h + 1) * tq]
        p_slc = p_grp + pltpu.roll(p_grp, half, 1)
        bias_t = _select_bias(p_slc.T[0:n_slc], qpos_row, n_slc)
        bias = jnp.concatenate([bias_t, jnp.zeros((LANES - n_slc, tq), F32)], axis=0).T.astype(BF16)
        for h in range(GROUP):
            qaug[g, h * tq:(h + 1) * tq, HEAD_DIM:] = bias
        reset(g)

    def interior(t, carry):
        k0 = pl.multiple_of(t * tk, tk)
        for g in groups:
            attend(g, 2 * HEAD_DIM, kts_ref[0, g, t], vs_ref[0, g, pl.ds(k0, tk), :], tk, None)
        return carry

    lax.fori_loop(0, last, interior, 0)
    k_last = pl.multiple_of(last * tk, tk)

    def causal(r0, n):
        return k_last + lax.broadcasted_iota(jnp.int32, (1, tk), 1) <= chunk_qpos(r0, n)

    o_sel = []
    for g in groups:
        attend(g, 2 * HEAD_DIM, kts_ref[0, g, last], vs_ref[0, g, pl.ds(k_last, tk), :], tk, causal)
        o_sel.append(result(g))
        reset(g)

    span = WINDOW + tq

    def in_window(r0, n):
        dist = chunk_qpos(r0, n) - (w0 + lax.broadcasted_iota(jnp.int32, (1, span), 1))
        return (dist >= 0) & (dist <= WINDOW)

    for g in groups:
        ktw = jnp.concatenate([ktw_ref[0, g, w0 // LANES + c] for c in range(span // LANES)], axis=1)
        attend(g, HEAD_DIM, ktw, vw_ref[0, g, pl.ds(w0, span), :], span, in_window)

    for g in groups:
        o_win = result(g)
        for h in range(GROUP):
            hd = GROUP * g + h
            r = slice(h * tq, (h + 1) * tq)
            o_ref[:, hd * HEAD_DIM:(hd + 1) * HEAD_DIM] = (
                gates[:, hd:hd + 1] * o_cmp[g][r]
                + gates[:, N_HEADS + hd:N_HEADS + hd + 1] * o_sel[g][r]
                + gates[:, 2 * N_HEADS + hd:2 * N_HEADS + hd + 1] * o_win[r]).astype(o_ref.dtype)


def _prompt_attn(h_q, h_sig, kc, vc, kts, vs, ktw, vw, nbatch, seq, tq, tk):
    nq = seq // tq
    n_cmp = seq // CMP_BLOCK
    nt = seq // tk
    rows = GROUP * tq
    return pl.pallas_call(
        functools.partial(_attn_body, tq=tq, tk=tk, seq=seq),
        grid=(nbatch, nq),
        in_specs=[
            pl.BlockSpec((tq, D_ATTN), lambda b, i: (b * nq + i, 0)),
            pl.BlockSpec((tq, LANES), lambda b, i: (b * nq + i, SIG_GN // LANES)),
            pl.BlockSpec((1, n_cmp, D_KV), lambda b, i: (b, 0, 0)),
            pl.BlockSpec((1, n_cmp, D_KV), lambda b, i: (b, 0, 0)),
            pl.BlockSpec((1, N_KV_HEADS, nt, 2 * HEAD_DIM, tk), lambda b, i: (b, 0, 0, 0, 0)),
            pl.BlockSpec((1, N_KV_HEADS, seq, 2 * HEAD_DIM), lambda b, i: (b, 0, 0, 0)),
            pl.BlockSpec((1, N_KV_HEADS, seq // LANES, HEAD_DIM, LANES), lambda b, i: (b, 0, 0, 0, 0)),
            pl.BlockSpec((1, N_KV_HEADS, seq, 2 * HEAD_DIM), lambda b, i: (b, 0, 0, 0)),
        ],
        out_specs=pl.BlockSpec((tq, D_ATTN), lambda b, i: (b * nq + i, 0)),
        scratch_shapes=[
            pltpu.VMEM((N_KV_HEADS, rows, 2 * HEAD_DIM), BF16),
            pltpu.VMEM((N_KV_HEADS, rows, max(tk, WINDOW + tq)), F32),
            pltpu.VMEM((N_KV_HEADS, rows, LANES), F32),
            pltpu.VMEM((N_KV_HEADS, rows, 2 * HEAD_DIM), F32),
        ],
        out_shape=jax.ShapeDtypeStruct((nbatch * seq, D_ATTN), BF16),
        compiler_params=_params(("arbitrary", "arbitrary")),
        name="attn",
    )(h_q, h_sig, kc, vc, kts, vs, ktw, vw)


def _group_queries(q_row, g):
    heads = [q_row[:, (GROUP * g + h) * HEAD_DIM:(GROUP * g + h + 1) * HEAD_DIM] for h in range(GROUP)]
    pad = jnp.zeros((2 * SUBLANES - GROUP, HEAD_DIM), F32)
    return jnp.concatenate(heads + [pad], axis=0).astype(BF16)


def _scmp_body(q_ref, kc_ref, vc_ref, ocmp_ref, pslc_ref, *, q_pos, chunk):
    b = pl.program_id(0)
    q_row = q_ref[pl.ds(b, 1), :]
    n_cmp = kc_ref.shape[1]
    half = chunk // 2
    lane = lax.broadcasted_iota(jnp.int32, (1, n_cmp), 1)
    within = lane % chunk
    cmp_blk = (lane // chunk) * chunk + 2 * (within % half) + within // half
    m_cmp = ((cmp_blk + 1) * CMP_BLOCK - 1) <= q_pos
    outs = []
    for g in range(N_KV_HEADS):
        hs = slice(g * HEAD_DIM, (g + 1) * HEAD_DIM)
        qg = _group_queries(q_row, g)
        s = lax.dot_general(qg, kc_ref[0][:, hs].astype(BF16), (((1,), (1,)), ((), ())), preferred_element_type=F32)
        p = _masked_softmax(s, m_cmp)
        o = jnp.dot(p.astype(BF16), vc_ref[0][:, hs].astype(BF16), preferred_element_type=F32)
        outs.append(o[0:GROUP])
        p_grp = p[0:1]
        for h in range(1, GROUP):
            p_grp = p_grp + p[h:h + 1]
        parts = []
        for c in range(n_cmp // chunk):
            pc = p_grp[:, c * chunk:(c + 1) * chunk]
            parts.append(pc + pltpu.roll(pc, half, 1))
        pslc_ref[0, g:g + 1, :] = jnp.concatenate(parts, axis=1)
    ocmp_ref[0] = jnp.concatenate(outs, axis=0)


def _sample_cmp(q, kc, vc, q_pos, chunk):
    nbatch, n_cmp, _ = kc.shape
    return pl.pallas_call(
        functools.partial(_scmp_body, q_pos=q_pos, chunk=chunk),
        grid=(nbatch,),
        in_specs=[
            pl.BlockSpec((nbatch, D_ATTN), lambda b: (0, 0)),
            pl.BlockSpec((1, n_cmp, D_KV), lambda b: (b, 0, 0)),
            pl.BlockSpec((1, n_cmp, D_KV), lambda b: (b, 0, 0)),
        ],
        out_specs=[
            pl.BlockSpec((1, N_HEADS, HEAD_DIM), lambda b: (b, 0, 0)),
            pl.BlockSpec((1, N_KV_HEADS, n_cmp), lambda b: (b, 0, 0)),
        ],
        out_shape=[
            jax.ShapeDtypeStruct((nbatch, N_HEADS, HEAD_DIM), F32),
            jax.ShapeDtypeStruct((nbatch, N_KV_HEADS, n_cmp), F32),
        ],
        compiler_params=_params(("arbitrary",)),
        name="s_cmp",
    )(q, kc, vc)


def _stopk_body(p_ref, idx_ref, *, q_pos, chunk, n_slc):
    p = p_ref[...]
    rows, width = p.shape
    half = chunk // 2
    n_in = (width // chunk) * half
    lane = lax.broadcasted_iota(jnp.int32, (1, width), 1)
    within = lane % chunk
    blk = jnp.where(within < half, (lane // chunk) * half + within, -1)
    extra = (lane >= half) & (lane < half + (n_slc - n_in))
    blk = jnp.where(extra, n_in + lane - half, blk)
    real = blk >= 0
    valid = real & (blk * SLC_BLOCK <= q_pos)
    cur = q_pos // SLC_BLOCK
    forced = (blk == 0) | (blk == cur) | (blk == cur - 1)
    base = jnp.where(extra, 0.0, p)
    x = jnp.where(valid & forced, FORCE, jnp.where(valid, base, -FORCE))
    x = jnp.where(real, x, -jnp.inf)
    blk_f = blk.astype(F32)
    out_lane = lax.broadcasted_iota(jnp.int32, (rows, LANES), 1)
    out = jnp.zeros((rows, LANES), F32)
    for r in range(min(TOP_N, n_slc)):
        mx = jnp.max(x, axis=-1, keepdims=True)
        pick = jnp.min(jnp.where(x == mx, blk_f, float(2 ** 30)), axis=-1, keepdims=True)
        out = jnp.where(out_lane == r, pick, out)
        x = jnp.where(blk_f == pick, -jnp.inf, x)
    idx_ref[...] = out.astype(jnp.int32)


def _sample_topk(pslc, q_pos, chunk, n_slc):
    rows, width = pslc.shape
    return pl.pallas_call(
        functools.partial(_stopk_body, q_pos=q_pos, chunk=chunk, n_slc=n_slc),
        grid=(1,),
        in_specs=[pl.BlockSpec((rows, width), lambda i: (0, 0))],
        out_specs=pl.BlockSpec((rows, LANES), lambda i: (0, 0)),
        out_shape=jax.ShapeDtypeStruct((rows, LANES), jnp.int32),
        compiler_params=_params(("arbitrary",)),
        name="s_topk",
    )(pslc)


def _sattn_body(idx_ref, pt_ref, q_ref, gn_ref, ocmp_ref, knew_ref, vnew_ref, kwn_ref, vwn_ref, wk_ref, wv_ref,
                sk_hbm, sv_hbm, o_ref, kbuf, vbuf, sem, *, q_pos, n_top, n_pages, past, per_step):
    step = pl.program_id(0)
    n_steps = pl.num_programs(0)
    half_rows = SLC_BLOCK * N_KV_HEADS

    def copies(s, slot, sub, g, i):
        bb = s * per_step + sub
        blk = idx_ref[bb * N_KV_HEADS + g, i]
        page = jnp.minimum(blk // 2, n_pages - 1)
        start = pl.multiple_of(pt_ref[bb, page] * (PAGE_SIZE * N_KV_HEADS) + (blk % 2) * half_rows, half_rows)
        return [pltpu.make_async_copy(hbm.at[pl.ds(start, half_rows), :], dst.at[slot, sub, g, i],
                                      sem.at[t, slot, sub, g, i])
                for t, (hbm, dst) in enumerate(((sk_hbm, kbuf), (sv_hbm, vbuf)))]

    def all_copies(s, slot):
        return [cp for sub in range(per_step) for g in range(N_KV_HEADS) for i in range(n_top)
                for cp in copies(s, slot, sub, g, i)]

    @pl.when(step == 0)
    def _():
        def one(n, carry):
            sub, rest = n // (N_KV_HEADS * n_top), n % (N_KV_HEADS * n_top)
            for cp in copies(0, 0, sub, rest // n_top, rest % n_top):
                cp.start()
            return carry
        lax.fori_loop(0, per_step * N_KV_HEADS * n_top, one, 0)

    @pl.when(step + 1 < n_steps)
    def _():
        for cp in all_copies(step + 1, (step + 1) % 2):
            cp.start()

    slot = step % 2
    for cp in all_copies(step, slot):
        cp.wait()

    pad = jnp.zeros((2 * SUBLANES - 1, HEAD_DIM), F32)
    for sub in range(per_step):
        _sattn_one(step * per_step + sub, slot, sub, pad, idx_ref, q_ref, gn_ref, ocmp_ref, knew_ref, vnew_ref, kwn_ref,
                   vwn_ref, wk_ref, wv_ref, o_ref, kbuf, vbuf, q_pos=q_pos, n_top=n_top, past=past)


def _sattn_one(b, slot, sub, pad, idx_ref, q_ref, gn_ref, ocmp_ref, knew_ref, vnew_ref, kwn_ref, vwn_ref, wk_ref, wv_ref,
               o_ref, kbuf, vbuf, *, q_pos, n_top, past):
    tail_blk = past // SLC_BLOCK
    q_row = q_ref[pl.ds(b, 1), :]
    gates = gn_ref[pl.ds(b, 1), :]

    def new_row(ref, g):
        return jnp.concatenate([ref[pl.ds(b * N_KV_HEADS + g, 1), :], pad], axis=0).astype(BF16)

    outs = []
    for g in range(N_KV_HEADS):
        qg = _group_queries(q_row, g)
        ks, vs = [], []
        key = lax.broadcasted_iota(jnp.int32, (1, n_top * SLC_BLOCK), 1)
        key_blk = jnp.zeros((1, n_top * SLC_BLOCK), jnp.int32)
        for i in range(n_top):
            blk = idx_ref[b * N_KV_HEADS + g, i]
            ks.append(kbuf[slot, sub, g, i, pl.ds(g, SLC_BLOCK, stride=N_KV_HEADS), :])
            vs.append(vbuf[slot, sub, g, i, pl.ds(g, SLC_BLOCK, stride=N_KV_HEADS), :])
            key_blk = jnp.where(key // SLC_BLOCK == i, blk, key_blk)
        k_sel = jnp.concatenate(ks, axis=0).astype(BF16)
        v_sel = jnp.concatenate(vs, axis=0).astype(BF16)
        m_sel = (key_blk * SLC_BLOCK + key % SLC_BLOCK <= q_pos) & (key_blk < tail_blk)
        tail_sel = jnp.max(jnp.where(key_blk == tail_blk, 1.0, 0.0), axis=-1, keepdims=True) > 0.5
        first = lax.broadcasted_iota(jnp.int32, (1, 2 * SUBLANES), 1) == 0
        o_sel = _two_part_attention(qg, k_sel, v_sel, m_sel, new_row(knew_ref, g), new_row(vnew_ref, g),
                                    first & tail_sel)
        wbuf = wk_ref.shape[1] // N_KV_HEADS
        k_win = wk_ref[sub, pl.ds(g, wbuf, stride=N_KV_HEADS), :].astype(BF16)
        v_win = wv_ref[sub, pl.ds(g, wbuf, stride=N_KV_HEADS), :].astype(BF16)
        dist = q_pos - (past - wbuf + lax.broadcasted_iota(jnp.int32, (1, wbuf), 1))
        m_win = (dist >= 0) & (dist <= WINDOW)
        o_win = _two_part_attention(qg, k_win, v_win, m_win, new_row(kwn_ref, g), new_row(vwn_ref, g), first)
        o_cmp = ocmp_ref[sub, GROUP * g:GROUP * (g + 1), :]
        for h in range(GROUP):
            hd = GROUP * g + h
            outs.append(gates[:, hd:hd + 1] * o_cmp[h:h + 1]
                        + gates[:, N_HEADS + hd:N_HEADS + hd + 1] * o_sel[h:h + 1]
                        + gates[:, 2 * N_HEADS + hd:2 * N_HEADS + hd + 1] * o_win[h:h + 1])
    o_ref[sub] = jnp.concatenate(outs, axis=0)


def _two_part_attention(q, k1, v1, m1, k2, v2, m2):
    dn = (((1,), (1,)), ((), ()))
    s1 = jnp.where(m1, lax.dot_general(q, k1, dn, preferred_element_type=F32), NEG)
    s2 = jnp.where(m2, lax.dot_general(q, k2, dn, preferred_element_type=F32), NEG)
    mx = jnp.maximum(jnp.max(s1, axis=-1, keepdims=True), jnp.max(s2, axis=-1, keepdims=True))
    e1 = jnp.where(m1, jnp.exp2(s1 - mx), 0.0)
    e2 = jnp.where(m2, jnp.exp2(s2 - mx), 0.0)
    l = jnp.sum(e1, axis=-1, keepdims=True) + jnp.sum(e2, axis=-1, keepdims=True)
    inv = 1.0 / jnp.where(l > 0.0, l, 1.0)
    o = (jnp.dot((e1 * inv).astype(BF16), v1, preferred_element_type=F32)
         + jnp.dot((e2 * inv).astype(BF16), v2, preferred_element_type=F32))
    return o


def _sample_attn(idx, page_table, h_q, h_sig, ocmp, new_rows, win_k, win_v, slc_k, slc_v, q_pos, past):
    nbatch, n_pages = page_table.shape
    n_top = min(TOP_N, past // SLC_BLOCK + 1)
    half_rows = SLC_BLOCK * N_KV_HEADS
    flat_new = pl.BlockSpec((nbatch * N_KV_HEADS, HEAD_DIM), lambda b, idx, pt: (0, 0))

    def whole(col_block, width):
        return pl.BlockSpec((nbatch, width), lambda b, idx, pt: (0, col_block))

    per_step = SATTN_PER_STEP
    return pl.pallas_call(
        functools.partial(_sattn_body, q_pos=q_pos, n_top=n_top, n_pages=n_pages, past=past, per_step=per_step),
        grid_spec=pltpu.PrefetchScalarGridSpec(
            num_scalar_prefetch=2,
            grid=(nbatch // per_step,),
            in_specs=[
                whole(0, D_ATTN),
                whole(SIG_GN // LANES, LANES),
                pl.BlockSpec((per_step, N_HEADS, HEAD_DIM), lambda b, idx, pt: (b, 0, 0)),
                flat_new, flat_new, flat_new, flat_new,
                pl.BlockSpec((per_step,) + win_k.shape[1:], lambda b, idx, pt: (b, 0, 0)),
                pl.BlockSpec((per_step,) + win_v.shape[1:], lambda b, idx, pt: (b, 0, 0)),
                pl.BlockSpec(memory_space=pl.ANY),
                pl.BlockSpec(memory_space=pl.ANY),
            ],
            out_specs=pl.BlockSpec((per_step, N_HEADS, HEAD_DIM), lambda b, idx, pt: (b, 0, 0)),
            scratch_shapes=[
                pltpu.VMEM((2, per_step, N_KV_HEADS, n_top, half_rows, LANES), F32),
                pltpu.VMEM((2, per_step, N_KV_HEADS, n_top, half_rows, LANES), F32),
                pltpu.SemaphoreType.DMA((2, 2, per_step, N_KV_HEADS, n_top)),
            ],
        ),
        out_shape=jax.ShapeDtypeStruct((nbatch, N_HEADS, HEAD_DIM), F32),
        compiler_params=_params(("arbitrary",)),
        name="s_attn",
    )(idx, page_table, h_q, h_sig, ocmp, *new_rows, win_k, win_v, slc_k, slc_v)


REF_KV = 3 * D_POOL
REF_GN = REF_KV + N_KV_PROJ * D_KV
REF_ZN = REF_GN + N_GATE
REF_GM = REF_ZN + D_ATTN


def _tiles(start, width):
    return tuple(range(start, start + width, PROJ_TN))


STARTS_U = _tiles(0, D_POOL)
STARTS_SILU = _tiles(D_POOL, D_POOL) + _tiles(REF_ZN, D_ATTN)
STARTS_Q = _tiles(2 * D_POOL, D_ATTN)
STARTS_SIG = _tiles(REF_GM, N_MERGE_COLS) + (REF_GN,)


def _block_diag2(w):
    z = jnp.zeros_like(w)
    return jnp.concatenate([jnp.concatenate([w, z], axis=-1), jnp.concatenate([z, w], axis=-1)], axis=-2)


def _project(x, wt, b, tm, tm_kv, act_dtype):
    u = _proj(x, wt, b, STARTS_U, ACT_NONE, tm, F32)
    h_silu = _proj(x, wt, b, STARTS_SILU, ACT_SILU, tm, act_dtype)
    h_q = _proj(x, wt, b, STARTS_Q, ACT_SCALE, tm, act_dtype)
    h_sig = _proj(x, wt, b, STARTS_SIG, ACT_SIGMOID, tm, act_dtype)
    kv = _proj_kv(x, wt, b, REF_KV, tm_kv)
    return u, h_silu, h_q, h_sig, kv


def kernel(x_prompt, x_sample, cache_cmp_k, cache_cmp_v, cache_slc_k, cache_slc_v, cache_win_k, cache_win_v,
           state_pool, page_table, w_in, b_in, pool_w, pool_scale, cmp_pe_k, cmp_w1_k, cmp_w2_k, cmp_pe_v, cmp_w1_v,
           cmp_w2_v, w_up_pool, w_up_nsa, w_out, ln_g, ln_b):
    nb_p, seq, _ = x_prompt.shape
    nb_s = x_sample.shape[0]
    n_pages = page_table.shape[1]
    past = n_pages * PAGE_SIZE
    n_phys = cache_cmp_k.shape[1]
    wbuf = cache_win_k.shape[2]

    wt = w_in[0].T
    b = b_in[0][None, :]
    kdim = CMP_BLOCK * D_KV
    cmp_k = (cmp_pe_k[0], _block_diag2(cmp_w1_k[0]).reshape(kdim, D_KV).astype(BF16),
             _block_diag2(cmp_w2_k[0]).astype(BF16))
    cmp_v = (cmp_pe_v[0], _block_diag2(cmp_w1_v[0]).reshape(kdim, D_KV).astype(BF16),
             _block_diag2(cmp_w2_v[0]).astype(BF16))
    pw = pool_w[0].astype(BF16)
    ps = pool_scale[0][None, :]
    wup = w_up_pool[0].astype(BF16)
    wun = w_up_nsa[0].astype(BF16)
    wo = w_out[0].astype(BF16)
    lg = ln_g[0][None, :]
    lb = ln_b[0][None, :]

    xp = x_prompt.reshape(nb_p * seq, D_MODEL)
    xs = x_sample.reshape(nb_s, D_MODEL)
    u_p, hsilu_p, hq_p, hsig_p, kv_p = _project(xp.astype(BF16), wt, b, tm=2048, tm_kv=2048, act_dtype=BF16)
    u_s, hsilu_s, hq_s, hsig_s, kv_s = _project(xs.astype(BF16), wt, b, tm=nb_s, tm_kv=nb_s, act_dtype=F32)

    blocks = (-1, CMP_ROWS, LANES)
    prompt_pages = seq // PAGE_SIZE
    ident = jnp.arange(nb_p * prompt_pages, dtype=jnp.int32).reshape(nb_p, prompt_pages)
    kc_p = _compress(ident, kv_p[0].reshape(blocks), *cmp_k, pps=prompt_pages)
    vc_p = _compress(ident, kv_p[1].reshape(blocks), *cmp_v, pps=prompt_pages)
    kts, vs, ktw, vw = _kprep(kv_p[2], kv_p[3], kv_p[4], kv_p[5], nb_p, seq, tk=512)
    o_p = _prompt_attn(hq_p, hsig_p, kc_p, vc_p, kts, vs, ktw, vw, nb_p, seq, tq=256, tk=512)
    y_p = _tail(u_p, hsilu_p, hsig_p, o_p, xp, pw, ps, wup, wun, wo, lg, lb, tm=512, seq=seq)

    chunk_pages = 64
    kc_s = _compress(page_table, cache_cmp_k.reshape(blocks), *cmp_k, pps=chunk_pages)
    vc_s = _compress(page_table, cache_cmp_v.reshape(blocks), *cmp_v, pps=chunk_pages)
    chunk = chunk_pages * BLOCKS_PER_PAGE
    n_slc = past // SLC_BLOCK + 1
    ocmp_s, pslc_s = _sample_cmp(hq_s, kc_s, vc_s, past, chunk)
    idx = _sample_topk(pslc_s.reshape(nb_s * N_KV_HEADS, -1), past, chunk, n_slc)
    flat = (n_phys * PAGE_SIZE * N_KV_HEADS, HEAD_DIM)
    o_s = _sample_attn(idx, page_table, hq_s, hsig_s, ocmp_s, kv_s[2:6],
                       cache_win_k.reshape(nb_s, wbuf * N_KV_HEADS, HEAD_DIM),
                       cache_win_v.reshape(nb_s, wbuf * N_KV_HEADS, HEAD_DIM),
                       cache_slc_k.reshape(flat), cache_slc_v.reshape(flat), past, past)
    ctx = jnp.concatenate([state_pool[0], u_s[:, None, :]], axis=1)
    m_s = _pool_m(ctx.reshape(nb_s * (POOL_CTX + 1), D_POOL), 0, nb_s * (POOL_CTX + 1), tm=nb_s * (POOL_CTX + 1),
                  seq=nb_s * (POOL_CTX + 1), fixed_pos=past)
    m_s = m_s.reshape(nb_s, POOL_CTX + 1, D_POOL)[:, POOL_CTX]
    y_s = _tail(m_s, hsilu_s, hsig_s, o_s.reshape(nb_s, D_ATTN), xs, pw, ps, wup, wun, wo, lg, lb, tm=nb_s)

    wl = min(WINDOW, seq)
    kv_p = [a.reshape(1, nb_p, seq, N_KV_HEADS, HEAD_DIM) for a in kv_p]
    kv_s = [a.reshape(1, nb_s, 1, N_KV_HEADS, HEAD_DIM) for a in kv_s]
    return (
        y_p.reshape(nb_p, seq, D_MODEL),
        y_s.reshape(nb_s, 1, D_MODEL),
        kv_p[0], kv_p[1], kv_p[2], kv_p[3],
        kv_p[4][:, :, seq - wl:], kv_p[5][:, :, seq - wl:],
        u_p.reshape(nb_p, seq, D_POOL)[None, :, seq - POOL_CTX:],
        kv_s[0], kv_s[1], kv_s[2], kv_s[3],
        jnp.concatenate([cache_win_k, kv_s[4]], axis=2)[:, :, 1:],
        jnp.concatenate([cache_win_v, kv_s[5]], axis=2)[:, :, 1:],
        ctx[None, :, 1:],
    )
```

```python
import functools

import jax
import jax.numpy as jnp
import numpy as np
from jax import lax
from jax.experimental import pallas as pl
from jax.experimental.pallas import tpu as pltpu

D_MODEL = 2048
D_POOL = 1024
POOL_WINDOWS = (2, 4, 8, 16)
POOL_GC = D_POOL // len(POOL_WINDOWS)
POOL_CTX = max(POOL_WINDOWS) - 1
HEAD_DIM = 128
N_HEADS = 8
N_KV_HEADS = 2
GROUP = N_HEADS // N_KV_HEADS
D_ATTN = N_HEADS * HEAD_DIM
D_KV = N_KV_HEADS * HEAD_DIM
CMP_BLOCK = 32
SLC_BLOCK = 64
TOP_N = 16
WINDOW = 512
PAGE_SIZE = 128
ATTN_SCALE = HEAD_DIM ** -0.5
DEPTH = 1
ALPHA = (2.0 * DEPTH) ** 0.25
LN_EPS = 1e-5
NEG = -1e30
FORCE = 1e6

F32 = jnp.float32
BF16 = jnp.bfloat16

SUBLANES = 8
LANES = 128
VMEM_LIMIT_BYTES = 56 * 1024 * 1024

PROJ_TN = 512
PROJ_RC = 512
N_MERGE_COLS = 2 * D_MODEL
SIG_GN = N_MERGE_COLS
N_GATE = 3 * N_HEADS
N_KV_PROJ = 6
ACT_NONE, ACT_SILU, ACT_SIGMOID, ACT_SCALE = range(4)
LOG2E = 1.4426950408889634
SCMP_PER_STEP = 4
SATTN_PER_STEP = 2
TAIL_RC = 128
ROW_BLOCK = 256
ROW_CHUNK = 32

CMP_ROWS = CMP_BLOCK * N_KV_HEADS
CMP_PITCH = 72
BLOCKS_PER_PAGE = PAGE_SIZE // CMP_BLOCK
CMP_SLOTS = 3


def _params(sem):
    return pltpu.CompilerParams(dimension_semantics=sem, vmem_limit_bytes=VMEM_LIMIT_BYTES)


def _sigmoid(x):
    return 1.0 / (1.0 + jnp.exp(-x))


def _activate(acc, act):
    if act == ACT_SILU:
        return acc * _sigmoid(acc)
    if act == ACT_SIGMOID:
        return _sigmoid(acc)
    if act == ACT_SCALE:
        return acc * (ATTN_SCALE * LOG2E)
    return acc


def _xwt(x, wt, b):
    return lax.dot_general(x, wt, (((1,), (1,)), ((), ())), preferred_element_type=F32) + b


def _wt_spec(tn):
    return pl.BlockSpec((pl.Element(tn), pl.Element(D_MODEL)), lambda i, j, starts8: (starts8[j] * SUBLANES, 0))


def _proj_body(starts8_ref, x_ref, wt_ref, b_ref, o_ref, *, act, rc):
    wt = wt_ref[...].astype(BF16)
    for c in range(x_ref.shape[0] // rc):
        r = slice(c * rc, (c + 1) * rc)
        o_ref[r, :] = _activate(_xwt(x_ref[r, :], wt, b_ref[...]), act).astype(o_ref.dtype)


def _proj(x, wt, b, starts, act, tm, out_dtype):
    m = x.shape[0]
    starts8 = jnp.asarray([s // SUBLANES for s in starts], jnp.int32)
    b = jnp.concatenate([b[:, s:s + PROJ_TN] for s in starts], axis=1)
    return pl.pallas_call(
        functools.partial(_proj_body, act=act, rc=min(tm, PROJ_RC)),
        grid_spec=pltpu.PrefetchScalarGridSpec(
            num_scalar_prefetch=1,
            grid=(m // tm, len(starts)),
            in_specs=[
                pl.BlockSpec((tm, D_MODEL), lambda i, j, starts8: (i, 0)),
                _wt_spec(PROJ_TN),
                pl.BlockSpec((1, PROJ_TN), lambda i, j, starts8: (0, j)),
            ],
            out_specs=pl.BlockSpec((tm, PROJ_TN), lambda i, j, starts8: (i, j)),
        ),
        out_shape=jax.ShapeDtypeStruct((m, len(starts) * PROJ_TN), out_dtype),
        compiler_params=_params(("arbitrary", "arbitrary")),
        name="proj",
    )(starts8, x, wt, b)


def _proj_kv_body(starts8_ref, x_ref, wt_ref, b_ref, *o_refs, rc):
    j = pl.program_id(1)
    wt = wt_ref[...].astype(BF16)
    per_tile = PROJ_TN // D_KV
    for k in range(len(o_refs) // per_tile):
        @pl.when(j == k)
        def _(k=k):
            for c in range(x_ref.shape[0] // rc):
                acc = _xwt(x_ref[c * rc:(c + 1) * rc, :], wt, b_ref[...])
                for n in range(per_tile * N_KV_HEADS):
                    o_ref = o_refs[per_tile * k + n // N_KV_HEADS]
                    o_ref[pl.ds(N_KV_HEADS * c * rc + n % N_KV_HEADS, rc, stride=N_KV_HEADS), :] = (
                        acc[:, n * HEAD_DIM:(n + 1) * HEAD_DIM])


def _proj_kv(x, wt, b, start, tm):
    m = x.shape[0]
    flat = jax.ShapeDtypeStruct((m * N_KV_HEADS, HEAD_DIM), F32)
    n_tiles = N_KV_PROJ * D_KV // PROJ_TN
    starts8 = jnp.asarray([(start + k * PROJ_TN) // SUBLANES for k in range(n_tiles)], jnp.int32)
    b = b[:, start:start + N_KV_PROJ * D_KV]
    return pl.pallas_call(
        functools.partial(_proj_kv_body, rc=min(tm, PROJ_RC)),
        grid_spec=pltpu.PrefetchScalarGridSpec(
            num_scalar_prefetch=1,
            grid=(m // tm, n_tiles),
            in_specs=[
                pl.BlockSpec((tm, D_MODEL), lambda i, j, starts8: (i, 0)),
                _wt_spec(PROJ_TN),
                pl.BlockSpec((1, PROJ_TN), lambda i, j, starts8: (0, j)),
            ],
            out_specs=[pl.BlockSpec((tm * N_KV_HEADS, HEAD_DIM), lambda i, j, starts8: (i, 0))] * N_KV_PROJ,
        ),
        out_shape=[flat] * N_KV_PROJ,
        compiler_params=_params(("arbitrary", "arbitrary")),
        name="proj_kv",
    )(starts8, x, wt, b)


def _window_means(halo, u, row0, fixed_pos):
    tm = u.shape[0]
    ext = jnp.concatenate([halo, u], axis=0)
    if fixed_pos is None:
        pos = row0 + lax.broadcasted_iota(jnp.int32, (tm, 1), 0)
    else:
        pos = jnp.full((tm, 1), fixed_pos, jnp.int32)
    outs = []
    for g, w in enumerate(POOL_WINDOWS):
        a = ext[:, g * POOL_GC:(g + 1) * POOL_GC]
        s = a
        k = 1
        while k < w:
            s = s + pltpu.roll(s, k, 0)
            k *= 2
        cnt = jnp.minimum(pos + 1, w).astype(F32)
        outs.append((s[2 * SUBLANES:] / cnt - a[2 * SUBLANES:]).astype(BF16))
    return jnp.concatenate(outs, axis=1)


def _pool_body(halo_ref, u_ref, m_ref, *, tm, seq, fixed_pos):
    row0 = (pl.program_id(0) * tm) % seq
    halo = jnp.where(row0 == 0, 0.0, halo_ref[...])
    m_ref[...] = _window_means(halo, u_ref[...], row0, fixed_pos)


def _pool_m(src, col_block, rows, tm, seq, fixed_pos):
    halo_rows = 2 * SUBLANES
    per = tm // halo_rows
    return pl.pallas_call(
        functools.partial(_pool_body, tm=tm, seq=seq, fixed_pos=fixed_pos),
        grid=(rows // tm,),
        in_specs=[
            pl.BlockSpec((halo_rows, D_POOL), lambda i: (jnp.maximum(i * per - 1, 0), col_block)),
            pl.BlockSpec((tm, D_POOL), lambda i: (i, col_block)),
        ],
        out_specs=pl.BlockSpec((tm, D_POOL), lambda i: (i, 0)),
        out_shape=jax.ShapeDtypeStruct((rows, D_POOL), BF16),
        compiler_params=_params(("arbitrary",)),
        name="pool",
    )(src, src)


def _tail_body(*refs, seq):
    if seq is None:
        m = refs[0][...]
        refs = refs[1:]
    else:
        halo_ref, u_ref = refs[:2]
        refs = refs[2:]
        row0 = (pl.program_id(0) * u_ref.shape[0]) % seq
        m = _window_means(jnp.where(row0 == 0, 0.0, halo_ref[...]), u_ref[...], row0, None)
    zp_ref, o_ref, zn_ref, ga_ref, gb_ref, x_ref, pw_ref, ps_ref, wup_ref, wun_ref, wo_ref, lg_ref, lb_ref, y_ref = refs
    tm = x_ref.shape[0]
    rc = min(tm, TAIL_RC)
    for c in range(tm // rc):
        r = slice(c * rc, (c + 1) * rc)
        ys = [jnp.dot(m[r, g * POOL_GC:(g + 1) * POOL_GC], pw_ref[g], preferred_element_type=F32)
              for g in range(len(POOL_WINDOWS))]
        y_pool = jnp.concatenate(ys, axis=1) * ps_ref[...]
        a = jnp.dot((y_pool * zp_ref[r, :]).astype(BF16), wup_ref[...], preferred_element_type=F32)
        b = jnp.dot((o_ref[r, :] * zn_ref[r, :]).astype(BF16), wun_ref[...], preferred_element_type=F32)
        mix = ga_ref[r, :] * a + gb_ref[r, :] * b
        h = jnp.dot(mix.astype(BF16), wo_ref[...], preferred_element_type=F32)
        z = ALPHA * x_ref[r, :] + h
        mu = jnp.mean(z, axis=-1, keepdims=True)
        zc = z - mu
        var = jnp.mean(zc * zc, axis=-1, keepdims=True)
        y_ref[r, :] = zc * lax.rsqrt(var + LN_EPS) * lg_ref[...] + lb_ref[...]


def _tail(pool_in, h_silu, h_sig, o, x, pw, ps, wup, wun, wo, lg, lb, tm, seq=None):
    rows = x.shape[0]
    once = pl.Buffered(1)

    def const(shape):
        return pl.BlockSpec(shape, lambda i: (0,) * len(shape), pipeline_mode=once)

    pool_specs = [pl.BlockSpec((tm, D_POOL), lambda i: (i, 0))]
    pool_args = [pool_in]
    if seq is not None:
        halo_rows = 2 * SUBLANES
        per = tm // halo_rows
        pool_specs.insert(0, pl.BlockSpec((halo_rows, D_POOL), lambda i: (jnp.maximum(i * per - 1, 0), 0)))
        pool_args.insert(0, pool_in)
    return pl.pallas_call(
        functools.partial(_tail_body, seq=seq),
        grid=(rows // tm,),
        in_specs=pool_specs + [
            pl.BlockSpec((tm, D_POOL), lambda i: (i, 0)),
            pl.BlockSpec((tm, D_ATTN), lambda i: (i, 0)),
            pl.BlockSpec((tm, D_ATTN), lambda i: (i, D_POOL // D_ATTN)),
            pl.BlockSpec((tm, D_MODEL), lambda i: (i, 0)),
            pl.BlockSpec((tm, D_MODEL), lambda i: (i, 1)),
            pl.BlockSpec((tm, D_MODEL), lambda i: (i, 0)),
            const((len(POOL_WINDOWS), POOL_GC, POOL_GC)),
            const((1, D_POOL)),
            const((D_POOL, D_MODEL)),
            const((D_ATTN, D_MODEL)),
            const((D_MODEL, D_MODEL)),
            const((1, D_MODEL)),
            const((1, D_MODEL)),
        ],
        out_specs=pl.BlockSpec((tm, D_MODEL), lambda i: (i, 0)),
        out_shape=jax.ShapeDtypeStruct((rows, D_MODEL), F32),
        compiler_params=_params(("arbitrary",)),
        name="tail",
    )(*pool_args, h_silu, o, h_silu, h_sig, h_sig, x, pw, ps, wup, wun, wo, lg, lb)


def _compress_body(pt_ref, c_hbm, pe_ref, w1_ref, w2_ref, o_ref, buf, lhs, res, sem, *, pps, n_chunks):
    b = pl.program_id(0)
    c = pl.program_id(1)
    step = b * n_chunks + c
    n_steps = pl.num_programs(0) * n_chunks
    nb = pps * BLOCKS_PER_PAGE

    def page_copy(seq_row, page0, slot, p):
        phys = pt_ref[seq_row, page0 + p]
        return pltpu.make_async_copy(c_hbm.at[pl.ds(phys * BLOCKS_PER_PAGE, BLOCKS_PER_PAGE)],
                                     buf.at[slot, pl.ds(BLOCKS_PER_PAGE * p, BLOCKS_PER_PAGE), pl.ds(0, CMP_ROWS), :],
                                     sem.at[slot])

    def start_step(s, slot):
        seq_row = s // n_chunks
        page0 = (s % n_chunks) * pps

        def pair(i, carry):
            page_copy(seq_row, page0, slot, 2 * i).start(priority=0)
            page_copy(seq_row, page0, slot, 2 * i + 1).start(priority=1)
            return carry
        lax.fori_loop(0, pps // 2, pair, 0)

    @pl.when(step == 0)
    def _():
        for s in range(CMP_SLOTS - 1):
            @pl.when(s < n_steps)
            def _(s=s):
                start_step(s, s)

    ahead = step + (CMP_SLOTS - 1)

    @pl.when(ahead < n_steps)
    def _():
        start_step(ahead, ahead % CMP_SLOTS)

    slot = step % CMP_SLOTS
    for p in range(pps):
        page_copy(b, c * pps, slot, p).wait()

    rows = buf.at[slot].reshape(nb * CMP_PITCH, LANES)
    for j in range(CMP_BLOCK):
        x0 = rows[pl.ds(2 * j, nb, stride=CMP_PITCH), :]
        x1 = rows[pl.ds(2 * j + 1, nb, stride=CMP_PITCH), :]
        pe = pe_ref[j:j + 1, :]
        lhs[:, (2 * j) * HEAD_DIM:(2 * j + 1) * HEAD_DIM] = (x0 + pe).astype(BF16)
        lhs[:, (2 * j + 1) * HEAD_DIM:(2 * j + 2) * HEAD_DIM] = (x1 + pe).astype(BF16)
    hid = jnp.dot(lhs[...], w1_ref[...], preferred_element_type=F32)
    hid = hid * _sigmoid(hid)
    out = jnp.dot(hid.astype(BF16), w2_ref[...], preferred_element_type=F32)
    half = nb // 2
    for h in range(N_KV_HEADS):
        hs = slice(h * HEAD_DIM, (h + 1) * HEAD_DIM)
        res[h] = out[:, hs]
        o_ref[0, 0:half, hs] = res[h, pl.ds(0, half, stride=2), :]
        o_ref[0, half:nb, hs] = res[h, pl.ds(1, half, stride=2), :]


def _compress(page_table, cache, pe, w1, w2, pps):
    nbatch, n_pages = page_table.shape
    n_chunks = n_pages // pps
    nb = pps * BLOCKS_PER_PAGE
    kdim = CMP_BLOCK * D_KV
    return pl.pallas_call(
        functools.partial(_compress_body, pps=pps, n_chunks=n_chunks),
        grid_spec=pltpu.PrefetchScalarGridSpec(
            num_scalar_prefetch=1,
            grid=(nbatch, n_chunks),
            in_specs=[
                pl.BlockSpec(memory_space=pl.ANY),
                pl.BlockSpec((CMP_BLOCK, HEAD_DIM), lambda b, c, pt: (0, 0)),
                pl.BlockSpec((kdim, D_KV), lambda b, c, pt: (0, 0), pipeline_mode=pl.Buffered(1)),
                pl.BlockSpec((D_KV, D_KV), lambda b, c, pt: (0, 0)),
            ],
            out_specs=pl.BlockSpec((1, nb, D_KV), lambda b, c, pt: (b, c, 0)),
            scratch_shapes=[
                pltpu.VMEM((CMP_SLOTS, nb, CMP_PITCH, LANES), F32),
                pltpu.VMEM((nb, kdim), BF16),
                pltpu.VMEM((N_KV_HEADS, nb, HEAD_DIM), F32),
                pltpu.SemaphoreType.DMA((CMP_SLOTS,)),
            ],
        ),
        out_shape=jax.ShapeDtypeStruct((nbatch, n_chunks * nb, D_KV), F32),
        compiler_params=_params(("arbitrary", "arbitrary")),
        name="compress",
    )(page_table, cache, pe, w1, w2)


def _kprep_body(ks_ref, vs_ref, kw_ref, vw_ref, kts_ref, vso_ref, ktw_ref, vwo_ref, *, tk):
    t = pl.program_id(1)
    kpos = t * tk + lax.broadcasted_iota(jnp.int32, (SLC_BLOCK, tk), 1)
    blk = lax.broadcasted_iota(jnp.int32, (SLC_BLOCK, tk), 0)
    onehot = jnp.where(kpos // SLC_BLOCK == blk, 1.0, 0.0).astype(BF16)
    for g in range(N_KV_HEADS):
        head = pl.ds(g, tk, stride=N_KV_HEADS)
        kts_ref[0, g, 0, 0:HEAD_DIM, :] = ks_ref[head, :].T.astype(BF16)
        kts_ref[0, g, 0, HEAD_DIM:HEAD_DIM + SLC_BLOCK, :] = onehot
        kts_ref[0, g, 0, HEAD_DIM + SLC_BLOCK:, :] = jnp.zeros((SLC_BLOCK, tk), BF16)
        kw = kw_ref[head, :]
        for c in range(tk // LANES):
            ktw_ref[0, g, c] = kw[c * LANES:(c + 1) * LANES, :].T.astype(BF16)
        ones = jnp.ones((tk, HEAD_DIM), BF16)
        vso_ref[0, g, :, 0:HEAD_DIM] = vs_ref[head, :].astype(BF16)
        vso_ref[0, g, :, HEAD_DIM:] = ones
        vwo_ref[0, g, :, 0:HEAD_DIM] = vw_ref[head, :].astype(BF16)
        vwo_ref[0, g, :, HEAD_DIM:] = ones


def _kprep(ks, vs, kw, vw, nbatch, seq, tk):
    nt = seq // tk
    src = pl.BlockSpec((tk * N_KV_HEADS, HEAD_DIM), lambda b, t: (b * nt + t, 0))
    return pl.pallas_call(
        functools.partial(_kprep_body, tk=tk),
        grid=(nbatch, nt),
        in_specs=[src, src, src, src],
        out_specs=[
            pl.BlockSpec((1, N_KV_HEADS, 1, 2 * HEAD_DIM, tk), lambda b, t: (b, 0, t, 0, 0)),
            pl.BlockSpec((1, N_KV_HEADS, tk, 2 * HEAD_DIM), lambda b, t: (b, 0, t, 0)),
            pl.BlockSpec((1, N_KV_HEADS, tk // LANES, HEAD_DIM, LANES), lambda b, t: (b, 0, t, 0, 0)),
            pl.BlockSpec((1, N_KV_HEADS, tk, 2 * HEAD_DIM), lambda b, t: (b, 0, t, 0)),
        ],
        out_shape=[
            jax.ShapeDtypeStruct((nbatch, N_KV_HEADS, nt, 2 * HEAD_DIM, tk), BF16),
            jax.ShapeDtypeStruct((nbatch, N_KV_HEADS, seq, 2 * HEAD_DIM), BF16),
            jax.ShapeDtypeStruct((nbatch, N_KV_HEADS, seq // LANES, HEAD_DIM, LANES), BF16),
            jax.ShapeDtypeStruct((nbatch, N_KV_HEADS, seq, 2 * HEAD_DIM), BF16),
        ],
        compiler_params=_params(("arbitrary", "arbitrary")),
        name="kprep",
    )(ks, vs, kw, vw)


def _masked_softmax(s, mask):
    s = jnp.where(mask, s, NEG)
    mx = jnp.max(s, axis=-1, keepdims=True)
    e = jnp.where(mask, jnp.exp2(s - mx), 0.0)
    l = jnp.sum(e, axis=-1, keepdims=True)
    return e / jnp.where(l > 0.0, l, 1.0)


def _select_bias(p_slc_t, qpos_row, n_slc):
    shape = p_slc_t.shape
    blk = lax.broadcasted_iota(jnp.int32, shape, 0)
    valid = blk * SLC_BLOCK <= qpos_row
    cur = qpos_row // SLC_BLOCK
    forced = (blk == 0) | (blk == cur) | (blk == cur - 1)
    score = jnp.where(valid & forced, FORCE, jnp.where(valid, p_slc_t, -FORCE))
    n_chunks = n_slc // SUBLANES
    chunks = [score[c * SUBLANES:(c + 1) * SUBLANES] for c in range(n_chunks)]
    ranks = [jnp.zeros((SUBLANES, shape[1]), F32) for _ in range(n_chunks)]
    sub = lax.broadcasted_iota(jnp.int32, (SUBLANES, shape[1]), 0)
    for j in range(n_slc):
        row = jnp.broadcast_to(score[j:j + 1, :], (SUBLANES, shape[1]))
        for c in range(n_chunks):
            lo = c * SUBLANES
            if lo > j:
                ahead = row >= chunks[c]
            elif lo + SUBLANES - 1 < j:
                ahead = row > chunks[c]
            else:
                ahead = (row > chunks[c]) | ((row == chunks[c]) & (sub + lo > j))
            ranks[c] = ranks[c] + jnp.where(ahead, 1.0, 0.0)
    rank = jnp.concatenate(ranks, axis=0)
    return jnp.where(rank < min(TOP_N, n_slc), 0.0, NEG)


def _attend_tile(q_ref, qcols, kt, v, width, mask_fn, s_ref, m_scr, acc_scr):
    rows = s_ref.shape[0]
    reps = width // LANES
    for blk in range(rows // ROW_BLOCK):
        rb = slice(blk * ROW_BLOCK, (blk + 1) * ROW_BLOCK)
        s_ref[rb, 0:width] = jnp.dot(q_ref[rb, 0:qcols], kt, preferred_element_type=F32)
    for blk in range(rows // ROW_BLOCK):
        alphas, ps = [], []
        for c in range(ROW_BLOCK // ROW_CHUNK):
            r0 = blk * ROW_BLOCK + c * ROW_CHUNK
            r = slice(r0, r0 + ROW_CHUNK)
            s = s_ref[r, 0:width]
            if mask_fn is not None:
                s = jnp.where(mask_fn(r0, ROW_CHUNK), s, NEG)
            m_old = m_scr[r, :]
            m_new = jnp.maximum(m_old, jnp.max(s, axis=-1, keepdims=True))
            ps.append(jnp.exp2(s - jnp.concatenate([m_new] * reps, axis=1)).astype(BF16))
            alphas.append(jnp.exp2(m_old - m_new))
            m_scr[r, :] = m_new
        alpha = jnp.concatenate(alphas, axis=0)
        rb = slice(blk * ROW_BLOCK, (blk + 1) * ROW_BLOCK)
        acc_scr[rb, :] = (jnp.concatenate([alpha, alpha], axis=1) * acc_scr[rb, :]
                          + jnp.dot(jnp.concatenate(ps, axis=0), v, preferred_element_type=F32))


def _attn_body(q_ref, gn_ref, kc_ref, vc_ref, kts_ref, vs_ref, ktw_ref, vw_ref, o_ref,
               qaug, s_scr, m_scr, acc_scr, *, tq, tk, seq):
    q0 = pl.program_id(1) * tq
    n_cmp = seq // CMP_BLOCK
    n_slc = -(-seq // SLC_BLOCK)
    half = n_cmp // 2
    qpos = q0 + lax.broadcasted_iota(jnp.int32, (tq, 1), 0)
    qpos4 = jnp.concatenate([qpos] * GROUP, axis=0)
    qpos_row = q0 + lax.broadcasted_iota(jnp.int32, (n_slc, tq), 1)
    lane = lax.broadcasted_iota(jnp.int32, (1, n_cmp), 1)
    cmp_blk = 2 * (lane % half) + lane // half
    m_cmp = ((cmp_blk + 1) * CMP_BLOCK - 1) <= qpos4
    gates = gn_ref[...].astype(F32)
    w0 = pl.multiple_of(jnp.maximum(q0 - WINDOW, 0), LANES)
    last = (q0 + tq + tk - 1) // tk - 1

    def chunk_qpos(r0, n):
        return q0 + r0 % tq + lax.broadcasted_iota(jnp.int32, (n, 1), 0)

    def reset(g):
        m_scr[g] = jnp.full(m_scr.shape[1:], -jnp.inf, F32)
        acc_scr[g] = jnp.zeros(acc_scr.shape[1:], F32)

    def result(g):
        acc = acc_scr[g]
        return acc[:, 0:HEAD_DIM] / acc[:, HEAD_DIM:]

    def attend(g, qcols, kt, v, width, mask_fn):
        _attend_tile(qaug.at[g], qcols, kt, v, width, mask_fn, s_scr.at[g], m_scr.at[g], acc_scr.at[g])

    groups = range(N_KV_HEADS)

    o_cmp = []
    for g in groups:
        hs = slice(g * HEAD_DIM, (g + 1) * HEAD_DIM)
        for h in range(GROUP):
            qaug[g, h * tq:(h + 1) * tq, 0:HEAD_DIM] = (
                q_ref[:, (GROUP * g + h) * HEAD_DIM:(GROUP * g + h + 1) * HEAD_DIM])
        kc = kc_ref[0][:, hs].astype(BF16)
        vc = vc_ref[0][:, hs].astype(BF16)
        s = lax.dot_general(qaug[g, :, 0:HEAD_DIM], kc, (((1,), (1,)), ((), ())), preferred_element_type=F32)
        p = _masked_softmax(s, m_cmp)
        o_cmp.append(jnp.dot(p.astype(BF16), vc, preferred_element_type=F32))
        p_grp = p[0:tq]
        for h in range(1, GROUP):
            p_grp = p_grp + p[h * tq:(h + 1) * tq]
        p_slc = p_grp + pltpu.roll(p_grp, half, 1)
        bias_t = _select_bias(p_slc.T[0:n_slc], qpos_row, n_slc)
        bias = jnp.concatenate([bias_t, jnp.zeros((LANES - n_slc, tq), F32)], axis=0).T.astype(BF16)
        for h in range(GROUP):
            qaug[g, h * tq:(h + 1) * tq, HEAD_DIM:] = bias
        reset(g)

    def interior(t, carry):
        k0 = pl.multiple_of(t * tk, tk)
        for g in groups:
            attend(g, 2 * HEAD_DIM, kts_ref[0, g, t], vs_ref[0, g, pl.ds(k0, tk), :], tk, None)
        return carry

    lax.fori_loop(0, last, interior, 0)
    k_last = pl.multiple_of(last * tk, tk)

    def causal(r0, n):
        return k_last + lax.broadcasted_iota(jnp.int32, (1, tk), 1) <= chunk_qpos(r0, n)

    o_sel = []
    for g in groups:
        attend(g, 2 * HEAD_DIM, kts_ref[0, g, last], vs_ref[0, g, pl.ds(k_last, tk), :], tk, causal)
        o_sel.append(result(g))
        reset(g)

    span = WINDOW + tq

    def in_window(r0, n):
        dist = chunk_qpos(r0, n) - (w0 + lax.broadcasted_iota(jnp.int32, (1, span), 1))
        return (dist >= 0) & (dist <= WINDOW)

    for g in groups:
        ktw = jnp.concatenate([ktw_ref[0, g, w0 // LANES + c] for c in range(span // LANES)], axis=1)
        attend(g, HEAD_DIM, ktw, vw_ref[0, g, pl.ds(w0, span), :], span, in_window)

    for g in groups:
        o_win = result(g)
        for h in range(GROUP):
            hd = GROUP * g + h
            r = slice(h * tq, (h + 1) * tq)
            o_ref[:, hd * HEAD_DIM:(hd + 1) * HEAD_DIM] = (
                gates[:, hd:hd + 1] * o_cmp[g][r]
                + gates[:, N_HEADS + hd:N_HEADS + hd + 1] * o_sel[g][r]
                + gates[:, 2 * N_HEADS + hd:2 * N_HEADS + hd + 1] * o_win[r]).astype(o_ref.dtype)


def _prompt_attn(h_q, h_sig, kc, vc, kts, vs, ktw, vw, nbatch, seq, tq, tk):
    nq = seq // tq
    n_cmp = seq // CMP_BLOCK
    nt = seq // tk
    rows = GROUP * tq
    return pl.pallas_call(
        functools.partial(_attn_body, tq=tq, tk=tk, seq=seq),
        grid=(nbatch, nq),
        in_specs=[
            pl.BlockSpec((tq, D_ATTN), lambda b, i: (b * nq + i, 0)),
            pl.BlockSpec((tq, LANES), lambda b, i: (b * nq + i, SIG_GN // LANES)),
            pl.BlockSpec((1, n_cmp, D_KV), lambda b, i: (b, 0, 0)),
            pl.BlockSpec((1, n_cmp, D_KV), lambda b, i: (b, 0, 0)),
            pl.BlockSpec((1, N_KV_HEADS, nt, 2 * HEAD_DIM, tk), lambda b, i: (b, 0, 0, 0, 0)),
            pl.BlockSpec((1, N_KV_HEADS, seq, 2 * HEAD_DIM), lambda b, i: (b, 0, 0, 0)),
            pl.BlockSpec((1, N_KV_HEADS, seq // LANES, HEAD_DIM, LANES), lambda b, i: (b, 0, 0, 0, 0)),
            pl.BlockSpec((1, N_KV_HEADS, seq, 2 * HEAD_DIM), lambda b, i: (b, 0, 0, 0)),
        ],
        out_specs=pl.BlockSpec((tq, D_ATTN), lambda b, i: (b * nq + i, 0)),
        scratch_shapes=[
            pltpu.VMEM((N_KV_HEADS, rows, 2 * HEAD_DIM), BF16),
            pltpu.VMEM((N_KV_HEADS, rows, max(tk, WINDOW + tq)), F32),
            pltpu.VMEM((N_KV_HEADS, rows, LANES), F32),
            pltpu.VMEM((N_KV_HEADS, rows, 2 * HEAD_DIM), F32),
        ],
        out_shape=jax.ShapeDtypeStruct((nbatch * seq, D_ATTN), BF16),
        compiler_params=_params(("arbitrary", "arbitrary")),
        name="attn",
    )(h_q, h_sig, kc, vc, kts, vs, ktw, vw)


def _group_queries(q_row, g):
    heads = [q_row[:, (GROUP * g + h) * HEAD_DIM:(GROUP * g + h + 1) * HEAD_DIM] for h in range(GROUP)]
    pad = jnp.zeros((2 * SUBLANES - GROUP, HEAD_DIM), F32)
    return jnp.concatenate(heads + [pad], axis=0).astype(BF16)


def _scmp_body(q_ref, kc_ref, vc_ref, ocmp_ref, pslc_ref, *, q_pos, chunk):
    per_step, n_cmp = kc_ref.shape[0], kc_ref.shape[1]
    half = chunk // 2
    lane = lax.broadcasted_iota(jnp.int32, (1, n_cmp), 1)
    within = lane % chunk
    cmp_blk = (lane // chunk) * chunk + 2 * (within % half) + within // half
    m_cmp = ((cmp_blk + 1) * CMP_BLOCK - 1) <= q_pos
    for sub in range(per_step):
        q_row = q_ref[pl.ds(pl.program_id(0) * per_step + sub, 1), :]
        outs = []
        for g in range(N_KV_HEADS):
            hs = slice(g * HEAD_DIM, (g + 1) * HEAD_DIM)
            qg = _group_queries(q_row, g)
            s = lax.dot_general(qg, kc_ref[sub][:, hs].astype(BF16), (((1,), (1,)), ((), ())),
                                preferred_element_type=F32)
            p = _masked_softmax(s, m_cmp)
            o = jnp.dot(p.astype(BF16), vc_ref[sub][:, hs].astype(BF16), preferred_element_type=F32)
            outs.append(o[0:GROUP])
            p_grp = p[0:1]
            for h in range(1, GROUP):
                p_grp = p_grp + p[h:h + 1]
            parts = []
            for c in range(n_cmp // chunk):
                pc = p_grp[:, c * chunk:(c + 1) * chunk]
                parts.append(pc + pltpu.roll(pc, half, 1))
            pslc_ref[sub, g:g + 1, :] = jnp.concatenate(parts, axis=1)
        ocmp_ref[sub] = jnp.concatenate(outs, axis=0)


def _sample_cmp(q, kc, vc, q_pos, chunk):
    nbatch, n_cmp, _ = kc.shape
    per_step = SCMP_PER_STEP
    return pl.pallas_call(
        functools.partial(_scmp_body, q_pos=q_pos, chunk=chunk),
        grid=(nbatch // per_step,),
        in_specs=[
            pl.BlockSpec((nbatch, D_ATTN), lambda b: (0, 0)),
            pl.BlockSpec((per_step, n_cmp, D_KV), lambda b: (b, 0, 0)),
            pl.BlockSpec((per_step, n_cmp, D_KV), lambda b: (b, 0, 0)),
        ],
        out_specs=[
            pl.BlockSpec((per_step, N_HEADS, HEAD_DIM), lambda b: (b, 0, 0)),
            pl.BlockSpec((per_step, N_KV_HEADS, n_cmp), lambda b: (b, 0, 0)),
        ],
        out_shape=[
            jax.ShapeDtypeStruct((nbatch, N_HEADS, HEAD_DIM), F32),
            jax.ShapeDtypeStruct((nbatch, N_KV_HEADS, n_cmp), F32),
        ],
        compiler_params=_params(("arbitrary",)),
        name="s_cmp",
    )(q, kc, vc)


def _stopk_body(p_ref, idx_ref, *, q_pos, chunk, n_slc):
    p = p_ref[...]
    rows, width = p.shape
    half = chunk // 2
    n_in = (width // chunk) * half
    lane = lax.broadcasted_iota(jnp.int32, (1, width), 1)
    within = lane % chunk
    blk = jnp.where(within < half, (lane // chunk) * half + within, -1)
    extra = (lane >= half) & (lane < half + (n_slc - n_in))
    blk = jnp.where(extra, n_in + lane - half, blk)
    real = blk >= 0
    valid = real & (blk * SLC_BLOCK <= q_pos)
    cur = q_pos // SLC_BLOCK
    forced = (blk == 0) | (blk == cur) | (blk == cur - 1)
    base = jnp.where(extra, 0.0, p)
    x = jnp.where(valid & forced, FORCE, jnp.where(valid, base, -FORCE))
    x = jnp.where(real, x, -jnp.inf)
    blk_f = blk.astype(F32)
    out_lane = lax.broadcasted_iota(jnp.int32, (rows, LANES), 1)
    out = jnp.zeros((rows, LANES), F32)
    for r in range(min(TOP_N, n_slc)):
        mx = jnp.max(x, axis=-1, keepdims=True)
        pick = jnp.min(jnp.where(x == mx, blk_f, float(2 ** 30)), axis=-1, keepdims=True)
        out = jnp.where(out_lane == r, pick, out)
        x = jnp.where(blk_f == pick, -jnp.inf, x)
    idx_ref[...] = out.astype(jnp.int32)


def _sample_topk(pslc, q_pos, chunk, n_slc):
    rows, width = pslc.shape
    return pl.pallas_call(
        functools.partial(_stopk_body, q_pos=q_pos, chunk=chunk, n_slc=n_slc),
        grid=(1,),
        in_specs=[pl.BlockSpec((rows, width), lambda i: (0, 0))],
        out_specs=pl.BlockSpec((rows, LANES), lambda i: (0, 0)),
        out_shape=jax.ShapeDtypeStruct((rows, LANES), jnp.int32),
        compiler_params=_params(("arbitrary",)),
        name="s_topk",
    )(pslc)


def _sattn_body(idx_ref, pt_ref, q_ref, gn_ref, ocmp_ref, knew_ref, vnew_ref, kwn_ref, vwn_ref, wk_ref, wv_ref,
                sk_hbm, sv_hbm, o_ref, kbuf, vbuf, sem, *, q_pos, n_top, n_pages, past, per_step):
    step = pl.program_id(0)
    n_steps = pl.num_programs(0)
    half_rows = SLC_BLOCK * N_KV_HEADS

    def copies(s, slot, sub, g, i):
        bb = s * per_step + sub
        blk = idx_ref[bb * N_KV_HEADS + g, i]
        page = jnp.minimum(blk // 2, n_pages - 1)
        start = pl.multiple_of(pt_ref[bb, page] * (PAGE_SIZE * N_KV_HEADS) + (blk % 2) * half_rows, half_rows)
        return [pltpu.make_async_copy(hbm.at[pl.ds(start, half_rows), :], dst.at[slot, sub, g, i],
                                      sem.at[t, slot, sub, g, i])
                for t, (hbm, dst) in enumerate(((sk_hbm, kbuf), (sv_hbm, vbuf)))]

    def all_copies(s, slot):
        return [cp for sub in range(per_step) for g in range(N_KV_HEADS) for i in range(n_top)
                for cp in copies(s, slot, sub, g, i)]

    @pl.when(step == 0)
    def _():
        def one(n, carry):
            sub, rest = n // (N_KV_HEADS * n_top), n % (N_KV_HEADS * n_top)
            for cp in copies(0, 0, sub, rest // n_top, rest % n_top):
                cp.start()
            return carry
        lax.fori_loop(0, per_step * N_KV_HEADS * n_top, one, 0)

    @pl.when(step + 1 < n_steps)
    def _():
        for cp in all_copies(step + 1, (step + 1) % 2):
            cp.start()

    slot = step % 2
    for cp in all_copies(step, slot):
        cp.wait()

    pad = jnp.zeros((2 * SUBLANES - 1, HEAD_DIM), F32)
    for sub in range(per_step):
        _sattn_one(step * per_step + sub, slot, sub, pad, idx_ref, q_ref, gn_ref, ocmp_ref, knew_ref, vnew_ref, kwn_ref,
                   vwn_ref, wk_ref, wv_ref, o_ref, kbuf, vbuf, q_pos=q_pos, n_top=n_top, past=past)


def _sattn_one(b, slot, sub, pad, idx_ref, q_ref, gn_ref, ocmp_ref, knew_ref, vnew_ref, kwn_ref, vwn_ref, wk_ref, wv_ref,
               o_ref, kbuf, vbuf, *, q_pos, n_top, past):
    tail_blk = past // SLC_BLOCK
    q_row = q_ref[pl.ds(b, 1), :]
    gates = gn_ref[pl.ds(b, 1), :]

    def new_row(ref, g):
        return jnp.concatenate([ref[pl.ds(b * N_KV_HEADS + g, 1), :], pad], axis=0).astype(BF16)

    outs = []
    for g in range(N_KV_HEADS):
        qg = _group_queries(q_row, g)
        ks, vs = [], []
        key = lax.broadcasted_iota(jnp.int32, (1, n_top * SLC_BLOCK), 1)
        key_blk = jnp.zeros((1, n_top * SLC_BLOCK), jnp.int32)
        for i in range(n_top):
            blk = idx_ref[b * N_KV_HEADS + g, i]
            ks.append(kbuf[slot, sub, g, i, pl.ds(g, SLC_BLOCK, stride=N_KV_HEADS), :])
            vs.append(vbuf[slot, sub, g, i, pl.ds(g, SLC_BLOCK, stride=N_KV_HEADS), :])
            key_blk = jnp.where(key // SLC_BLOCK == i, blk, key_blk)
        k_sel = jnp.concatenate(ks, axis=0).astype(BF16)
        v_sel = jnp.concatenate(vs, axis=0).astype(BF16)
        m_sel = (key_blk * SLC_BLOCK + key % SLC_BLOCK <= q_pos) & (key_blk < tail_blk)
        tail_sel = jnp.max(jnp.where(key_blk == tail_blk, 1.0, 0.0), axis=-1, keepdims=True) > 0.5
        first = lax.broadcasted_iota(jnp.int32, (1, 2 * SUBLANES), 1) == 0
        o_sel = _two_part_attention(qg, k_sel, v_sel, m_sel, new_row(knew_ref, g), new_row(vnew_ref, g),
                                    first & tail_sel)
        wbuf = wk_ref.shape[1] // N_KV_HEADS
        k_win = wk_ref[sub, pl.ds(g, wbuf, stride=N_KV_HEADS), :].astype(BF16)
        v_win = wv_ref[sub, pl.ds(g, wbuf, stride=N_KV_HEADS), :].astype(BF16)
        dist = q_pos - (past - wbuf + lax.broadcasted_iota(jnp.int32, (1, wbuf), 1))
        m_win = (dist >= 0) & (dist <= WINDOW)
        o_win = _two_part_attention(qg, k_win, v_win, m_win, new_row(kwn_ref, g), new_row(vwn_ref, g), first)
        o_cmp = ocmp_ref[sub, GROUP * g:GROUP * (g + 1), :]
        for h in range(GROUP):
            hd = GROUP * g + h
            outs.append(gates[:, hd:hd + 1] * o_cmp[h:h + 1]
                        + gates[:, N_HEADS + hd:N_HEADS + hd + 1] * o_sel[h:h + 1]
                        + gates[:, 2 * N_HEADS + hd:2 * N_HEADS + hd + 1] * o_win[h:h + 1])
    o_ref[sub] = jnp.concatenate(outs, axis=0)


def _two_part_attention(q, k1, v1, m1, k2, v2, m2):
    dn = (((1,), (1,)), ((), ()))
    s1 = jnp.where(m1, lax.dot_general(q, k1, dn, preferred_element_type=F32), NEG)
    s2 = jnp.where(m2, lax.dot_general(q, k2, dn, preferred_element_type=F32), NEG)
    mx = jnp.maximum(jnp.max(s1, axis=-1, keepdims=True), jnp.max(s2, axis=-1, keepdims=True))
    e1 = jnp.where(m1, jnp.exp2(s1 - mx), 0.0)
    e2 = jnp.where(m2, jnp.exp2(s2 - mx), 0.0)
    l = jnp.sum(e1, axis=-1, keepdims=True) + jnp.sum(e2, axis=-1, keepdims=True)
    inv = 1.0 / jnp.where(l > 0.0, l, 1.0)
    o = (jnp.dot((e1 * inv).astype(BF16), v1, preferred_element_type=F32)
         + jnp.dot((e2 * inv).astype(BF16), v2, preferred_element_type=F32))
    return o


def _sample_attn(idx, page_table, h_q, h_sig, ocmp, new_rows, win_k, win_v, slc_k, slc_v, q_pos, past):
    nbatch, n_pages = page_table.shape
    n_top = min(TOP_N, past // SLC_BLOCK + 1)
    half_rows = SLC_BLOCK * N_KV_HEADS
    flat_new = pl.BlockSpec((nbatch * N_KV_HEADS, HEAD_DIM), lambda b, idx, pt: (0, 0))

    def whole(col_block, width):
        return pl.BlockSpec((nbatch, width), lambda b, idx, pt: (0, col_block))

    per_step = SATTN_PER_STEP
    return pl.pallas_call(
        functools.partial(_sattn_body, q_pos=q_pos, n_top=n_top, n_pages=n_pages, past=past, per_step=per_step),
        grid_spec=pltpu.PrefetchScalarGridSpec(
            num_scalar_prefetch=2,
            grid=(nbatch // per_step,),
            in_specs=[
                whole(0, D_ATTN),
                whole(SIG_GN // LANES, LANES),
                pl.BlockSpec((per_step, N_HEADS, HEAD_DIM), lambda b, idx, pt: (b, 0, 0)),
                flat_new, flat_new, flat_new, flat_new,
                pl.BlockSpec((per_step,) + win_k.shape[1:], lambda b, idx, pt: (b, 0, 0)),
                pl.BlockSpec((per_step,) + win_v.shape[1:], lambda b, idx, pt: (b, 0, 0)),
                pl.BlockSpec(memory_space=pl.ANY),
                pl.BlockSpec(memory_space=pl.ANY),
            ],
            out_specs=pl.BlockSpec((per_step, N_HEADS, HEAD_DIM), lambda b, idx, pt: (b, 0, 0)),
            scratch_shapes=[
                pltpu.VMEM((2, per_step, N_KV_HEADS, n_top, half_rows, LANES), F32),
                pltpu.VMEM((2, per_step, N_KV_HEADS, n_top, half_rows, LANES), F32),
                pltpu.SemaphoreType.DMA((2, 2, per_step, N_KV_HEADS, n_top)),
            ],
        ),
        out_shape=jax.ShapeDtypeStruct((nbatch, N_HEADS, HEAD_DIM), F32),
        compiler_params=_params(("arbitrary",)),
        name="s_attn",
    )(idx, page_table, h_q, h_sig, ocmp, *new_rows, win_k, win_v, slc_k, slc_v)


REF_KV = 3 * D_POOL
REF_GN = REF_KV + N_KV_PROJ * D_KV
REF_ZN = REF_GN + N_GATE
REF_GM = REF_ZN + D_ATTN


def _tiles(start, width):
    return tuple(range(start, start + width, PROJ_TN))


STARTS_U = _tiles(0, D_POOL)
STARTS_SILU = _tiles(D_POOL, D_POOL) + _tiles(REF_ZN, D_ATTN)
STARTS_Q = _tiles(2 * D_POOL, D_ATTN)
STARTS_SIG = _tiles(REF_GM, N_MERGE_COLS) + (REF_GN,)


def _block_diag2(w):
    z = jnp.zeros_like(w)
    return jnp.concatenate([jnp.concatenate([w, z], axis=-1), jnp.concatenate([z, w], axis=-1)], axis=-2)


def _project(x, wt, b, tm, tm_kv, act_dtype):
    u = _proj(x, wt, b, STARTS_U, ACT_NONE, tm, F32)
    h_silu = _proj(x, wt, b, STARTS_SILU, ACT_SILU, tm, act_dtype)
    h_q = _proj(x, wt, b, STARTS_Q, ACT_SCALE, tm, act_dtype)
    h_sig = _proj(x, wt, b, STARTS_SIG, ACT_SIGMOID, tm, act_dtype)
    kv = _proj_kv(x, wt, b, REF_KV, tm_kv)
    return u, h_silu, h_q, h_sig, kv


def kernel(x_prompt, x_sample, cache_cmp_k, cache_cmp_v, cache_slc_k, cache_slc_v, cache_win_k, cache_win_v,
           state_pool, page_table, w_in, b_in, pool_w, pool_scale, cmp_pe_k, cmp_w1_k, cmp_w2_k, cmp_pe_v, cmp_w1_v,
           cmp_w2_v, w_up_pool, w_up_nsa, w_out, ln_g, ln_b):
    nb_p, seq, _ = x_prompt.shape
    nb_s = x_sample.shape[0]
    n_pages = page_table.shape[1]
    past = n_pages * PAGE_SIZE
    n_phys = cache_cmp_k.shape[1]
    wbuf = cache_win_k.shape[2]

    wt = w_in[0].T
    b = b_in[0][None, :]
    kdim = CMP_BLOCK * D_KV
    cmp_k = (cmp_pe_k[0], _block_diag2(cmp_w1_k[0]).reshape(kdim, D_KV).astype(BF16),
             _block_diag2(cmp_w2_k[0]).astype(BF16))
    cmp_v = (cmp_pe_v[0], _block_diag2(cmp_w1_v[0]).reshape(kdim, D_KV).astype(BF16),
             _block_diag2(cmp_w2_v[0]).astype(BF16))
    pw = pool_w[0].astype(BF16)
    ps = pool_scale[0][None, :]
    wup = w_up_pool[0].astype(BF16)
    wun = w_up_nsa[0].astype(BF16)
    wo = w_out[0].astype(BF16)
    lg = ln_g[0][None, :]
    lb = ln_b[0][None, :]

    xp = x_prompt.reshape(nb_p * seq, D_MODEL)
    xs = x_sample.reshape(nb_s, D_MODEL)
    u_p, hsilu_p, hq_p, hsig_p, kv_p = _project(xp.astype(BF16), wt, b, tm=2048, tm_kv=2048, act_dtype=BF16)
    u_s, hsilu_s, hq_s, hsig_s, kv_s = _project(xs.astype(BF16), wt, b, tm=nb_s, tm_kv=nb_s, act_dtype=F32)

    blocks = (-1, CMP_ROWS, LANES)
    prompt_pages = seq // PAGE_SIZE
    ident = jnp.arange(nb_p * prompt_pages, dtype=jnp.int32).reshape(nb_p, prompt_pages)
    kc_p = _compress(ident, kv_p[0].reshape(blocks), *cmp_k, pps=prompt_pages)
    vc_p = _compress(ident, kv_p[1].reshape(blocks), *cmp_v, pps=prompt_pages)
    kts, vs, ktw, vw = _kprep(kv_p[2], kv_p[3], kv_p[4], kv_p[5], nb_p, seq, tk=512)
    o_p = _prompt_attn(hq_p, hsig_p, kc_p, vc_p, kts, vs, ktw, vw, nb_p, seq, tq=256, tk=512)
    y_p = _tail(u_p, hsilu_p, hsig_p, o_p, xp, pw, ps, wup, wun, wo, lg, lb, tm=512, seq=seq)

    chunk_pages = 64
    kc_s = _compress(page_table, cache_cmp_k.reshape(blocks), *cmp_k, pps=chunk_pages)
    vc_s = _compress(page_table, cache_cmp_v.reshape(blocks), *cmp_v, pps=chunk_pages)
    chunk = chunk_pages * BLOCKS_PER_PAGE
    n_slc = past // SLC_BLOCK + 1
    ocmp_s, pslc_s = _sample_cmp(hq_s, kc_s, vc_s, past, chunk)
    idx = _sample_topk(pslc_s.reshape(nb_s * N_KV_HEADS, -1), past, chunk, n_slc)
    flat = (n_phys * PAGE_SIZE * N_KV_HEADS, HEAD_DIM)
    o_s = _sample_attn(idx, page_table, hq_s, hsig_s, ocmp_s, kv_s[2:6],
                       cache_win_k.reshape(nb_s, wbuf * N_KV_HEADS, HEAD_DIM),
                       cache_win_v.reshape(nb_s, wbuf * N_KV_HEADS, HEAD_DIM),
                       cache_slc_k.reshape(flat), cache_slc_v.reshape(flat), past, past)
    ctx = jnp.concatenate([state_pool[0], u_s[:, None, :]], axis=1)
    m_s = _pool_m(ctx.reshape(nb_s * (POOL_CTX + 1), D_POOL), 0, nb_s * (POOL_CTX + 1), tm=nb_s * (POOL_CTX + 1),
                  seq=nb_s * (POOL_CTX + 1), fixed_pos=past)
    m_s = m_s.reshape(nb_s, POOL_CTX + 1, D_POOL)[:, POOL_CTX]
    y_s = _tail(m_s, hsilu_s, hsig_s, o_s.reshape(nb_s, D_ATTN), xs, pw, ps, wup, wun, wo, lg, lb, tm=nb_s)

    wl = min(WINDOW, seq)
    kv_p = [a.reshape(1, nb_p, seq, N_KV_HEADS, HEAD_DIM) for a in kv_p]
    kv_s = [a.reshape(1, nb_s, 1, N_KV_HEADS, HEAD_DIM) for a in kv_s]
    return (
        y_p.reshape(nb_p, seq, D_MODEL),
        y_s.reshape(nb_s, 1, D_MODEL),
        kv_p[0], kv_p[1], kv_p[2], kv_p[3],
        kv_p[4][:, :, seq - wl:], kv_p[5][:, :, seq - wl:],
        u_p.reshape(nb_p, seq, D_POOL)[None, :, seq - POOL_CTX:],
        kv_s[0], kv_s[1], kv_s[2], kv_s[3],
        jnp.concatenate([cache_win_k, kv_s[4]], axis=2)[:, :, 1:],
        jnp.concatenate([cache_win_v, kv_s[5]], axis=2)[:, :, 1:],
        ctx[None, :, 1:],
    )
```

```python
import functools

import jax
import jax.numpy as jnp
import numpy as np
from jax import lax
from jax.experimental import pallas as pl
from jax.experimental.pallas import tpu as pltpu

D_MODEL = 2048
D_POOL = 1024
POOL_WINDOWS = (2, 4, 8, 16)
POOL_GC = D_POOL // len(POOL_WINDOWS)
POOL_CTX = max(POOL_WINDOWS) - 1
HEAD_DIM = 128
N_HEADS = 8
N_KV_HEADS = 2
GROUP = N_HEADS // N_KV_HEADS
D_ATTN = N_HEADS * HEAD_DIM
D_KV = N_KV_HEADS * HEAD_DIM
CMP_BLOCK = 32
SLC_BLOCK = 64
TOP_N = 16
WINDOW = 512
PAGE_SIZE = 128
ATTN_SCALE = HEAD_DIM ** -0.5
DEPTH = 1
ALPHA = (2.0 * DEPTH) ** 0.25
LN_EPS = 1e-5
NEG = -1e30
FORCE = 1e6

F32 = jnp.float32
BF16 = jnp.bfloat16

SUBLANES = 8
LANES = 128
VMEM_LIMIT_BYTES = 56 * 1024 * 1024

PROJ_TN = 512
PROJ_RC = 512
N_MERGE_COLS = 2 * D_MODEL
SIG_GN = N_MERGE_COLS
N_GATE = 3 * N_HEADS
N_KV_PROJ = 6
ACT_NONE, ACT_SILU, ACT_SIGMOID, ACT_SCALE = range(4)
LOG2E = 1.4426950408889634
SCMP_PER_STEP = 4
SATTN_PER_STEP = 2
TAIL_RC = 128
ROW_BLOCK = 256
ROW_CHUNK = 32

CMP_ROWS = CMP_BLOCK * N_KV_HEADS
CMP_PITCH = 72
BLOCKS_PER_PAGE = PAGE_SIZE // CMP_BLOCK
CMP_SLOTS = 3


def _params(sem):
    return pltpu.CompilerParams(dimension_semantics=sem, vmem_limit_bytes=VMEM_LIMIT_BYTES)


def _sigmoid(x):
    return 1.0 / (1.0 + jnp.exp(-x))


def _activate(acc, act):
    if act == ACT_SILU:
        return acc * _sigmoid(acc)
    if act == ACT_SIGMOID:
        return _sigmoid(acc)
    if act == ACT_SCALE:
        return acc * (ATTN_SCALE * LOG2E)
    return acc


def _xwt(x, wt, b):
    return lax.dot_general(x, wt, (((1,), (1,)), ((), ())), preferred_element_type=F32) + b


def _wt_spec(tn):
    return pl.BlockSpec((pl.Element(tn), pl.Element(D_MODEL)), lambda i, j, starts8: (starts8[j] * SUBLANES, 0))


def _proj_body(starts8_ref, x_ref, wt_ref, b_ref, o_ref, *, act, rc):
    wt = wt_ref[...].astype(BF16)
    for c in range(x_ref.shape[0] // rc):
        r = slice(c * rc, (c + 1) * rc)
        o_ref[r, :] = _activate(_xwt(x_ref[r, :], wt, b_ref[...]), act).astype(o_ref.dtype)


def _proj(x, wt, b, starts, act, tm, out_dtype):
    m = x.shape[0]
    starts8 = jnp.asarray([s // SUBLANES for s in starts], jnp.int32)
    b = jnp.concatenate([b[:, s:s + PROJ_TN] for s in starts], axis=1)
    return pl.pallas_call(
        functools.partial(_proj_body, act=act, rc=min(tm, PROJ_RC)),
        grid_spec=pltpu.PrefetchScalarGridSpec(
            num_scalar_prefetch=1,
            grid=(m // tm, len(starts)),
            in_specs=[
                pl.BlockSpec((tm, D_MODEL), lambda i, j, starts8: (i, 0)),
                _wt_spec(PROJ_TN),
                pl.BlockSpec((1, PROJ_TN), lambda i, j, starts8: (0, j)),
            ],
            out_specs=pl.BlockSpec((tm, PROJ_TN), lambda i, j, starts8: (i, j)),
        ),
        out_shape=jax.ShapeDtypeStruct((m, len(starts) * PROJ_TN), out_dtype),
        compiler_params=_params(("arbitrary", "arbitrary")),
        name="proj",
    )(starts8, x, wt, b)


def _proj_kv_body(starts8_ref, x_ref, wt_ref, b_ref, *o_refs, rc):
    j = pl.program_id(1)
    wt = wt_ref[...].astype(BF16)
    per_tile = PROJ_TN // D_KV
    for k in range(len(o_refs) // per_tile):
        @pl.when(j == k)
        def _(k=k):
            for c in range(x_ref.shape[0] // rc):
                acc = _xwt(x_ref[c * rc:(c + 1) * rc, :], wt, b_ref[...])
                for n in range(per_tile * N_KV_HEADS):
                    o_ref = o_refs[per_tile * k + n // N_KV_HEADS]
                    o_ref[pl.ds(N_KV_HEADS * c * rc + n % N_KV_HEADS, rc, stride=N_KV_HEADS), :] = (
                        acc[:, n * HEAD_DIM:(n + 1) * HEAD_DIM])


def _proj_kv(x, wt, b, start, tm):
    m = x.shape[0]
    flat = jax.ShapeDtypeStruct((m * N_KV_HEADS, HEAD_DIM), F32)
    n_tiles = N_KV_PROJ * D_KV // PROJ_TN
    starts8 = jnp.asarray([(start + k * PROJ_TN) // SUBLANES for k in range(n_tiles)], jnp.int32)
    b = b[:, start:start + N_KV_PROJ * D_KV]
    return pl.pallas_call(
        functools.partial(_proj_kv_body, rc=min(tm, PROJ_RC)),
        grid_spec=pltpu.PrefetchScalarGridSpec(
            num_scalar_prefetch=1,
            grid=(m // tm, n_tiles),
            in_specs=[
                pl.BlockSpec((tm, D_MODEL), lambda i, j, starts8: (i, 0)),
                _wt_spec(PROJ_TN),
                pl.BlockSpec((1, PROJ_TN), lambda i, j, starts8: (0, j)),
            ],
            out_specs=[pl.BlockSpec((tm * N_KV_HEADS, HEAD_DIM), lambda i, j, starts8: (i, 0))] * N_KV_PROJ,
        ),
        out_shape=[flat] * N_KV_PROJ,
        compiler_params=_params(("arbitrary", "arbitrary")),
        name="proj_kv",
    )(starts8, x, wt, b)


def _window_means(halo, u, row0, fixed_pos):
    tm = u.shape[0]
    ext = jnp.concatenate([halo, u], axis=0)
    if fixed_pos is None:
        pos = row0 + lax.broadcasted_iota(jnp.int32, (tm, 1), 0)
    else:
        pos = jnp.full((tm, 1), fixed_pos, jnp.int32)
    outs = []
    for g, w in enumerate(POOL_WINDOWS):
        a = ext[:, g * POOL_GC:(g + 1) * POOL_GC]
        s = a
        k = 1
        while k < w:
            s = s + pltpu.roll(s, k, 0)
            k *= 2
        cnt = jnp.minimum(pos + 1, w).astype(F32)
        outs.append((s[2 * SUBLANES:] / cnt - a[2 * SUBLANES:]).astype(BF16))
    return jnp.concatenate(outs, axis=1)


def _pool_body(halo_ref, u_ref, m_ref, *, tm, seq, fixed_pos):
    row0 = (pl.program_id(0) * tm) % seq
    halo = jnp.where(row0 == 0, 0.0, halo_ref[...])
    m_ref[...] = _window_means(halo, u_ref[...], row0, fixed_pos)


def _pool_m(src, col_block, rows, tm, seq, fixed_pos):
    halo_rows = 2 * SUBLANES
    per = tm // halo_rows
    return pl.pallas_call(
        functools.partial(_pool_body, tm=tm, seq=seq, fixed_pos=fixed_pos),
        grid=(rows // tm,),
        in_specs=[
            pl.BlockSpec((halo_rows, D_POOL), lambda i: (jnp.maximum(i * per - 1, 0), col_block)),
            pl.BlockSpec((tm, D_POOL), lambda i: (i, col_block)),
        ],
        out_specs=pl.BlockSpec((tm, D_POOL), lambda i: (i, 0)),
        out_shape=jax.ShapeDtypeStruct((rows, D_POOL), BF16),
        compiler_params=_params(("arbitrary",)),
        name="pool",
    )(src, src)


def _tail_body(*refs, seq):
    if seq is None:
        m = refs[0][...]
        refs = refs[1:]
    else:
        halo_ref, u_ref = refs[:2]
        refs = refs[2:]
        row0 = (pl.program_id(0) * u_ref.shape[0]) % seq
        m = _window_means(jnp.where(row0 == 0, 0.0, halo_ref[...]), u_ref[...], row0, None)
    zp_ref, o_ref, zn_ref, ga_ref, gb_ref, x_ref, pw_ref, ps_ref, wup_ref, wun_ref, wo_ref, lg_ref, lb_ref, y_ref = refs
    tm = x_ref.shape[0]
    rc = min(tm, TAIL_RC)
    for c in range(tm // rc):
        r = slice(c * rc, (c + 1) * rc)
        ys = [jnp.dot(m[r, g * POOL_GC:(g + 1) * POOL_GC], pw_ref[g], preferred_element_type=F32)
              for g in range(len(POOL_WINDOWS))]
        y_pool = jnp.concatenate(ys, axis=1) * ps_ref[...]
        a = jnp.dot((y_pool * zp_ref[r, :]).astype(BF16), wup_ref[...], preferred_element_type=F32)
        b = jnp.dot((o_ref[r, :] * zn_ref[r, :]).astype(BF16), wun_ref[...], preferred_element_type=F32)
        mix = ga_ref[r, :] * a + gb_ref[r, :] * b
        h = jnp.dot(mix.astype(BF16), wo_ref[...], preferred_element_type=F32)
        z = ALPHA * x_ref[r, :] + h
        mu = jnp.mean(z, axis=-1, keepdims=True)
        zc = z - mu
        var = jnp.mean(zc * zc, axis=-1, keepdims=True)
        y_ref[r, :] = zc * lax.rsqrt(var + LN_EPS) * lg_ref[...] + lb_ref[...]


def _tail(pool_in, h_silu, h_sig, o, x, pw, ps, wup, wun, wo, lg, lb, tm, seq=None):
    rows = x.shape[0]
    once = pl.Buffered(1)

    def const(shape):
        return pl.BlockSpec(shape, lambda i: (0,) * len(shape), pipeline_mode=once)

    pool_specs = [pl.BlockSpec((tm, D_POOL), lambda i: (i, 0))]
    pool_args = [pool_in]
    if seq is not None:
        halo_rows = 2 * SUBLANES
        per = tm // halo_rows
        pool_specs.insert(0, pl.BlockSpec((halo_rows, D_POOL), lambda i: (jnp.maximum(i * per - 1, 0), 0)))
        pool_args.insert(0, pool_in)
    return pl.pallas_call(
        functools.partial(_tail_body, seq=seq),
        grid=(rows // tm,),
        in_specs=pool_specs + [
            pl.BlockSpec((tm, D_POOL), lambda i: (i, 0)),
            pl.BlockSpec((tm, D_ATTN), lambda i: (i, 0)),
            pl.BlockSpec((tm, D_ATTN), lambda i: (i, D_POOL // D_ATTN)),
            pl.BlockSpec((tm, D_MODEL), lambda i: (i, 0)),
            pl.BlockSpec((tm, D_MODEL), lambda i: (i, 1)),
            pl.BlockSpec((tm, D_MODEL), lambda i: (i, 0)),
            const((len(POOL_WINDOWS), POOL_GC, POOL_GC)),
            const((1, D_POOL)),
            const((D_POOL, D_MODEL)),
            const((D_ATTN, D_MODEL)),
            const((D_MODEL, D_MODEL)),
            const((1, D_MODEL)),
            const((1, D_MODEL)),
        ],
        out_specs=pl.BlockSpec((tm, D_MODEL), lambda i: (i, 0)),
        out_shape=jax.ShapeDtypeStruct((rows, D_MODEL), F32),
        compiler_params=_params(("arbitrary",)),
        name="tail",
    )(*pool_args, h_silu, o, h_silu, h_sig, h_sig, x, pw, ps, wup, wun, wo, lg, lb)


def _compress_body(pt_ref, c_hbm, pe_ref, w1_ref, w2_ref, o_ref, buf, lhs, res, sem, *, pps, n_chunks):
    b = pl.program_id(0)
    c = pl.program_id(1)
    step = b * n_chunks + c
    n_steps = pl.num_programs(0) * n_chunks
    nb = pps * BLOCKS_PER_PAGE

    def page_copy(seq_row, page0, slot, p):
        phys = pt_ref[seq_row, page0 + p]
        return pltpu.make_async_copy(c_hbm.at[pl.ds(phys * BLOCKS_PER_PAGE, BLOCKS_PER_PAGE)],
                                     buf.at[slot, pl.ds(BLOCKS_PER_PAGE * p, BLOCKS_PER_PAGE), pl.ds(0, CMP_ROWS), :],
                                     sem.at[slot])

    def start_step(s, slot):
        seq_row = s // n_chunks
        page0 = (s % n_chunks) * pps

        def pair(i, carry):
            page_copy(seq_row, page0, slot, 2 * i).start(priority=0)
            page_copy(seq_row, page0, slot, 2 * i + 1).start(priority=1)
            return carry
        lax.fori_loop(0, pps // 2, pair, 0)

    @pl.when(step == 0)
    def _():
        for s in range(CMP_SLOTS - 1):
            @pl.when(s < n_steps)
            def _(s=s):
                start_step(s, s)

    ahead = step + (CMP_SLOTS - 1)

    @pl.when(ahead < n_steps)
    def _():
        start_step(ahead, ahead % CMP_SLOTS)

    slot = step % CMP_SLOTS
    for p in range(pps):
        page_copy(b, c * pps, slot, p).wait()

    rows = buf.at[slot].reshape(nb * CMP_PITCH, LANES)
    for j in range(CMP_BLOCK):
        x0 = rows[pl.ds(2 * j, nb, stride=CMP_PITCH), :]
        x1 = rows[pl.ds(2 * j + 1, nb, stride=CMP_PITCH), :]
        pe = pe_ref[j:j + 1, :]
        lhs[:, (2 * j) * HEAD_DIM:(2 * j + 1) * HEAD_DIM] = (x0 + pe).astype(BF16)
        lhs[:, (2 * j + 1) * HEAD_DIM:(2 * j + 2) * HEAD_DIM] = (x1 + pe).astype(BF16)
    hid = jnp.dot(lhs[...], w1_ref[...], preferred_element_type=F32)
    hid = hid * _sigmoid(hid)
    out = jnp.dot(hid.astype(BF16), w2_ref[...], preferred_element_type=F32)
    half = nb // 2
    for h in range(N_KV_HEADS):
        hs = slice(h * HEAD_DIM, (h + 1) * HEAD_DIM)
        res[h] = out[:, hs]
        o_ref[0, 0:half, hs] = res[h, pl.ds(0, half, stride=2), :]
        o_ref[0, half:nb, hs] = res[h, pl.ds(1, half, stride=2), :]


def _compress(page_table, cache, pe, w1, w2, pps):
    nbatch, n_pages = page_table.shape
    n_chunks = n_pages // pps
    nb = pps * BLOCKS_PER_PAGE
    kdim = CMP_BLOCK * D_KV
    return pl.pallas_call(
        functools.partial(_compress_body, pps=pps, n_chunks=n_chunks),
        grid_spec=pltpu.PrefetchScalarGridSpec(
            num_scalar_prefetch=1,
            grid=(nbatch, n_chunks),
            in_specs=[
                pl.BlockSpec(memory_space=pl.ANY),
                pl.BlockSpec((CMP_BLOCK, HEAD_DIM), lambda b, c, pt: (0, 0)),
                pl.BlockSpec((kdim, D_KV), lambda b, c, pt: (0, 0), pipeline_mode=pl.Buffered(1)),
                pl.BlockSpec((D_KV, D_KV), lambda b, c, pt: (0, 0)),
            ],
            out_specs=pl.BlockSpec((1, nb, D_KV), lambda b, c, pt: (b, c, 0)),
            scratch_shapes=[
                pltpu.VMEM((CMP_SLOTS, nb, CMP_PITCH, LANES), F32),
                pltpu.VMEM((nb, kdim), BF16),
                pltpu.VMEM((N_KV_HEADS, nb, HEAD_DIM), F32),
                pltpu.SemaphoreType.DMA((CMP_SLOTS,)),
            ],
        ),
        out_shape=jax.ShapeDtypeStruct((nbatch, n_chunks * nb, D_KV), F32),
        compiler_params=_params(("arbitrary", "arbitrary")),
        name="compress",
    )(page_table, cache, pe, w1, w2)


def _kprep_body(ks_ref, vs_ref, kw_ref, vw_ref, kts_ref, vso_ref, ktw_ref, vwo_ref, *, tk):
    t = pl.program_id(1)
    kpos = t * tk + lax.broadcasted_iota(jnp.int32, (SLC_BLOCK, tk), 1)
    blk = lax.broadcasted_iota(jnp.int32, (SLC_BLOCK, tk), 0)
    onehot = jnp.where(kpos // SLC_BLOCK == blk, 1.0, 0.0).astype(BF16)
    for g in range(N_KV_HEADS):
        head = pl.ds(g, tk, stride=N_KV_HEADS)
        kts_ref[0, g, 0, 0:HEAD_DIM, :] = ks_ref[head, :].T.astype(BF16)
        kts_ref[0, g, 0, HEAD_DIM:HEAD_DIM + SLC_BLOCK, :] = onehot
        kts_ref[0, g, 0, HEAD_DIM + SLC_BLOCK:, :] = jnp.zeros((SLC_BLOCK, tk), BF16)
        kw = kw_ref[head, :]
        for c in range(tk // LANES):
            ktw_ref[0, g, c] = kw[c * LANES:(c + 1) * LANES, :].T.astype(BF16)
        ones = jnp.ones((tk, HEAD_DIM), BF16)
        vso_ref[0, g, :, 0:HEAD_DIM] = vs_ref[head, :].astype(BF16)
        vso_ref[0, g, :, HEAD_DIM:] = ones
        vwo_ref[0, g, :, 0:HEAD_DIM] = vw_ref[head, :].astype(BF16)
        vwo_ref[0, g, :, HEAD_DIM:] = ones


def _kprep(ks, vs, kw, vw, nbatch, seq, tk):
    nt = seq // tk
    src = pl.BlockSpec((tk * N_KV_HEADS, HEAD_DIM), lambda b, t: (b * nt + t, 0))
    return pl.pallas_call(
        functools.partial(_kprep_body, tk=tk),
        grid=(nbatch, nt),
        in_specs=[src, src, src, src],
        out_specs=[
            pl.BlockSpec((1, N_KV_HEADS, 1, 2 * HEAD_DIM, tk), lambda b, t: (b, 0, t, 0, 0)),
            pl.BlockSpec((1, N_KV_HEADS, tk, 2 * HEAD_DIM), lambda b, t: (b, 0, t, 0)),
            pl.BlockSpec((1, N_KV_HEADS, tk // LANES, HEAD_DIM, LANES), lambda b, t: (b, 0, t, 0, 0)),
            pl.BlockSpec((1, N_KV_HEADS, tk, 2 * HEAD_DIM), lambda b, t: (b, 0, t, 0)),
        ],
        out_shape=[
            jax.ShapeDtypeStruct((nbatch, N_KV_HEADS, nt, 2 * HEAD_DIM, tk), BF16),
            jax.ShapeDtypeStruct((nbatch, N_KV_HEADS, seq, 2 * HEAD_DIM), BF16),
            jax.ShapeDtypeStruct((nbatch, N_KV_HEADS, seq // LANES, HEAD_DIM, LANES), BF16),
            jax.ShapeDtypeStruct((nbatch, N_KV_HEADS, seq, 2 * HEAD_DIM), BF16),
        ],
        compiler_params=_params(("arbitrary", "arbitrary")),
        name="kprep",
    )(ks, vs, kw, vw)


def _masked_softmax(s, mask):
    s = jnp.where(mask, s, NEG)
    mx = jnp.max(s, axis=-1, keepdims=True)
    e = jnp.where(mask, jnp.exp2(s - mx), 0.0)
    l = jnp.sum(e, axis=-1, keepdims=True)
    return e / jnp.where(l > 0.0, l, 1.0)


def _select_bias(p_slc_t, qpos_row, n_slc):
    shape = p_slc_t.shape
    blk = lax.broadcasted_iota(jnp.int32, shape, 0)
    valid = blk * SLC_BLOCK <= qpos_row
    cur = qpos_row // SLC_BLOCK
    forced = (blk == 0) | (blk == cur) | (blk == cur - 1)
    score = jnp.where(valid & forced, FORCE, jnp.where(valid, p_slc_t, -FORCE))
    n_chunks = n_slc // SUBLANES
    chunks = [score[c * SUBLANES:(c + 1) * SUBLANES] for c in range(n_chunks)]
    ranks = [jnp.zeros((SUBLANES, shape[1]), F32) for _ in range(n_chunks)]
    sub = lax.broadcasted_iota(jnp.int32, (SUBLANES, shape[1]), 0)
    for j in range(n_slc):
        row = jnp.broadcast_to(score[j:j + 1, :], (SUBLANES, shape[1]))
        for c in range(n_chunks):
            lo = c * SUBLANES
            if lo > j:
                ahead = row >= chunks[c]
            elif lo + SUBLANES - 1 < j:
                ahead = row > chunks[c]
            else:
                ahead = (row > chunks[c]) | ((row == chunks[c]) & (sub + lo > j))
            ranks[c] = ranks[c] + jnp.where(ahead, 1.0, 0.0)
    rank = jnp.concatenate(ranks, axis=0)
    return jnp.where(rank < min(TOP_N, n_slc), 0.0, NEG)


def _attend_tile(q_ref, qcols, kt, v, width, mask_fn, s_ref, m_scr, acc_scr):
    rows = s_ref.shape[0]
    reps = width // LANES
    for blk in range(rows // ROW_BLOCK):
        rb = slice(blk * ROW_BLOCK, (blk + 1) * ROW_BLOCK)
        s_ref[rb, 0:width] = jnp.dot(q_ref[rb, 0:qcols], kt, preferred_element_type=F32)
    for blk in range(rows // ROW_BLOCK):
        alphas, ps = [], []
        for c in range(ROW_BLOCK // ROW_CHUNK):
            r0 = blk * ROW_BLOCK + c * ROW_CHUNK
            r = slice(r0, r0 + ROW_CHUNK)
            s = s_ref[r, 0:width]
            if mask_fn is not None:
                s = jnp.where(mask_fn(r0, ROW_CHUNK), s, NEG)
            m_old = m_scr[r, :]
            m_new = jnp.maximum(m_old, jnp.max(s, axis=-1, keepdims=True))
            ps.append(jnp.exp2(s - jnp.concatenate([m_new] * reps, axis=1)).astype(BF16))
            alphas.append(jnp.exp2(m_old - m_new))
            m_scr[r, :] = m_new
        alpha = jnp.concatenate(alphas, axis=0)
        rb = slice(blk * ROW_BLOCK, (blk + 1) * ROW_BLOCK)
        acc_scr[rb, :] = (jnp.concatenate([alpha, alpha], axis=1) * acc_scr[rb, :]
                          + jnp.dot(jnp.concatenate(ps, axis=0), v, preferred_element_type=F32))


def _attn_body(q_ref, gn_ref, kc_ref, vc_ref, kts_ref, vs_ref, ktw_ref, vw_ref, o_ref,
               qaug, s_scr, m_scr, acc_scr, *, tq, tk, seq):
    q0 = pl.program_id(1) * tq
    n_cmp = seq // CMP_BLOCK
    n_slc = -(-seq // SLC_BLOCK)
    half = n_cmp // 2
    qpos = q0 + lax.broadcasted_iota(jnp.int32, (tq, 1), 0)
    qpos4 = jnp.concatenate([qpos] * GROUP, axis=0)
    qpos_row = q0 + lax.broadcasted_iota(jnp.int32, (n_slc, tq), 1)
    lane = lax.broadcasted_iota(jnp.int32, (1, n_cmp), 1)
    cmp_blk = 2 * (lane % half) + lane // half
    m_cmp = ((cmp_blk + 1) * CMP_BLOCK - 1) <= qpos4
    gates = gn_ref[...].astype(F32)
    w0 = pl.multiple_of(jnp.maximum(q0 - WINDOW, 0), LANES)
    last = (q0 + tq + tk - 1) // tk - 1

    def chunk_qpos(r0, n):
        return q0 + r0 % tq + lax.broadcasted_iota(jnp.int32, (n, 1), 0)

    def reset(g):
        m_scr[g] = jnp.full(m_scr.shape[1:], -jnp.inf, F32)
        acc_scr[g] = jnp.zeros(acc_scr.shape[1:], F32)

    def result(g):
        acc = acc_scr[g]
        return acc[:, 0:HEAD_DIM] / acc[:, HEAD_DIM:]

    def attend(g, qcols, kt, v, width, mask_fn):
        _attend_tile(qaug.at[g], qcols, kt, v, width, mask_fn, s_scr.at[g], m_scr.at[g], acc_scr.at[g])

    groups = range(N_KV_HEADS)

    o_cmp = []
    for g in groups:
        hs = slice(g * HEAD_DIM, (g + 1) * HEAD_DIM)
        for h in range(GROUP):
            qaug[g, h * tq:(h + 1) * tq, 0:HEAD_DIM] = (
                q_ref[:, (GROUP * g + h) * HEAD_DIM:(GROUP * g + h + 1) * HEAD_DIM])
        kc = kc_ref[0][:, hs].astype(BF16)
        vc = vc_ref[0][:, hs].astype(BF16)
        s = lax.dot_general(qaug[g, :, 0:HEAD_DIM], kc, (((1,), (1,)), ((), ())), preferred_element_type=F32)
        p = _masked_softmax(s, m_cmp)
        o_cmp.append(jnp.dot(p.astype(BF16), vc, preferred_element_type=F32))
        p_grp = p[0:tq]
        for h in range(1, GROUP):
            p_grp = p_grp + p[h * tq:(h + 1) * tq]
        p_slc = p_grp + pltpu.roll(p_grp, half, 1)
        bias_t = _select_bias(p_slc.T[0:n_slc], qpos_row, n_slc)
        bias = jnp.concatenate([bias_t, jnp.zeros((LANES - n_slc, tq), F32)], axis=0).T.astype(BF16)
        for h in range(GROUP):
            qaug[g, h * tq:(h + 1) * tq, HEAD_DIM:] = bias
        reset(g)

    def interior(t, carry):
        k0 = pl.multiple_of(t * tk, tk)
        for g in groups:
            attend(g, 2 * HEAD_DIM, kts_ref[0, g, t], vs_ref[0, g, pl.ds(k0, tk), :], tk, None)
        return carry

    lax.fori_loop(0, last, interior, 0)
    k_last = pl.multiple_of(last * tk, tk)

    def causal(r0, n):
        return k_last + lax.broadcasted_iota(jnp.int32, (1, tk), 1) <= chunk_qpos(r0, n)

    o_sel = []
    for g in groups:
        attend(g, 2 * HEAD_DIM, kts_ref[0, g, last], vs_ref[0, g, pl.ds(k_last, tk), :], tk, causal)
        o_sel.append(result(g))
        reset(g)

    span = WINDOW + tq

    def in_window(r0, n):
        dist = chunk_qpos(r0, n) - (w0 + lax.broadcasted_iota(jnp.int32, (1, span), 1))
        return (dist >= 0) & (dist <= WINDOW)

    for g in groups:
        ktw = jnp.concatenate([ktw_ref[0, g, w0 // LANES + c] for c in range(span // LANES)], axis=1)
        attend(g, HEAD_DIM, ktw, vw_ref[0, g, pl.ds(w0, span), :], span, in_window)

    for g in groups:
        o_win = result(g)
        for h in range(GROUP):
            hd = GROUP * g + h
            r = slice(h * tq, (h + 1) * tq)
            o_ref[:, hd * HEAD_DIM:(hd + 1) * HEAD_DIM] = (
                gates[:, hd:hd + 1] * o_cmp[g][r]
                + gates[:, N_HEADS + hd:N_HEADS + hd + 1] * o_sel[g][r]
                + gates[:, 2 * N_HEADS + hd:2 * N_HEADS + hd + 1] * o_win[r]).astype(o_ref.dtype)


def _prompt_attn(h_q, h_sig, kc, vc, kts, vs, ktw, vw, nbatch, seq, tq, tk):
    nq = seq // tq
    n_cmp = seq // CMP_BLOCK
    nt = seq // tk
    rows = GROUP * tq
    return pl.pallas_call(
        functools.partial(_attn_body, tq=tq, tk=tk, seq=seq),
        grid=(nbatch, nq),
        in_specs=[
            pl.BlockSpec((tq, D_ATTN), lambda b, i: (b * nq + i, 0)),
            pl.BlockSpec((tq, LANES), lambda b, i: (b * nq + i, SIG_GN // LANES)),
            pl.BlockSpec((1, n_cmp, D_KV), lambda b, i: (b, 0, 0)),
            pl.BlockSpec((1, n_cmp, D_KV), lambda b, i: (b, 0, 0)),
            pl.BlockSpec((1, N_KV_HEADS, nt, 2 * HEAD_DIM, tk), lambda b, i: (b, 0, 0, 0, 0)),
            pl.BlockSpec((1, N_KV_HEADS, seq, 2 * HEAD_DIM), lambda b, i: (b, 0, 0, 0)),
            pl.BlockSpec((1, N_KV_HEADS, seq // LANES, HEAD_DIM, LANES), lambda b, i: (b, 0, 0, 0, 0)),
            pl.BlockSpec((1, N_KV_HEADS, seq, 2 * HEAD_DIM), lambda b, i: (b, 0, 0, 0)),
        ],
        out_specs=pl.BlockSpec((tq, D_ATTN), lambda b, i: (b * nq + i, 0)),
        scratch_shapes=[
            pltpu.VMEM((N_KV_HEADS, rows, 2 * HEAD_DIM), BF16),
            pltpu.VMEM((N_KV_HEADS, rows, max(tk, WINDOW + tq)), F32),
            pltpu.VMEM((N_KV_HEADS, rows, LANES), F32),
            pltpu.VMEM((N_KV_HEADS, rows, 2 * HEAD_DIM), F32),
        ],
        out_shape=jax.ShapeDtypeStruct((nbatch * seq, D_ATTN), BF16),
        compiler_params=_params(("arbitrary", "arbitrary")),
        name="attn",
    )(h_q, h_sig, kc, vc, kts, vs, ktw, vw)


def _group_queries(q_row, g):
    heads = [q_row[:, (GROUP * g + h) * HEAD_DIM:(GROUP * g + h + 1) * HEAD_DIM] for h in range(GROUP)]
    pad = jnp.zeros((2 * SUBLANES - GROUP, HEAD_DIM), F32)
    return jnp.concatenate(heads + [pad], axis=0).astype(BF16)


def _scmp_body(q_ref, kc_ref, vc_ref, ocmp_ref, pslc_ref, *, q_pos, chunk):
    per_step, n_cmp = kc_ref.shape[0], kc_ref.shape[1]
    half = chunk // 2
    lane = lax.broadcasted_iota(jnp.int32, (1, n_cmp), 1)
    within = lane % chunk
    cmp_blk = (lane // chunk) * chunk + 2 * (within % half) + within // half
    m_cmp = ((cmp_blk + 1) * CMP_BLOCK - 1) <= q_pos
    for sub in range(per_step):
        q_row = q_ref[pl.ds(pl.program_id(0) * per_step + sub, 1), :]
        outs = []
        for g in range(N_KV_HEADS):
            hs = slice(g * HEAD_DIM, (g + 1) * HEAD_DIM)
            qg = _group_queries(q_row, g)
            s = lax.dot_general(qg, kc_ref[sub][:, hs].astype(BF16), (((1,), (1,)), ((), ())),
                                preferred_element_type=F32)
            p = _masked_softmax(s, m_cmp)
            o = jnp.dot(p.astype(BF16), vc_ref[sub][:, hs].astype(BF16), preferred_element_type=F32)
            outs.append(o[0:GROUP])
            p_grp = p[0:1]
            for h in range(1, GROUP):
                p_grp = p_grp + p[h:h + 1]
            parts = []
            for c in range(n_cmp // chunk):
                pc = p_grp[:, c * chunk:(c + 1) * chunk]
                parts.append(pc + pltpu.roll(pc, half, 1))
            pslc_ref[sub, g:g + 1, :] = jnp.concatenate(parts, axis=1)
        ocmp_ref[sub] = jnp.concatenate(outs, axis=0)


def _sample_cmp(q, kc, vc, q_pos, chunk):
    nbatch, n_cmp, _ = kc.shape
    per_step = SCMP_PER_STEP
    return pl.pallas_call(
        functools.partial(_scmp_body, q_pos=q_pos, chunk=chunk),
        grid=(nbatch // per_step,),
        in_specs=[
            pl.BlockSpec((nbatch, D_ATTN), lambda b: (0, 0)),
            pl.BlockSpec((per_step, n_cmp, D_KV), lambda b: (b, 0, 0)),
            pl.BlockSpec((per_step, n_cmp, D_KV), lambda b: (b, 0, 0)),
        ],
        out_specs=[
            pl.BlockSpec((per_step, N_HEADS, HEAD_DIM), lambda b: (b, 0, 0)),
            pl.BlockSpec((per_step, N_KV_HEADS, n_cmp), lambda b: (b, 0, 0)),
        ],
        out_shape=[
            jax.ShapeDtypeStruct((nbatch, N_HEADS, HEAD_DIM), F32),
            jax.ShapeDtypeStruct((nbatch, N_KV_HEADS, n_cmp), F32),
        ],
        compiler_params=_params(("arbitrary",)),
        name="s_cmp",
    )(q, kc, vc)


def _stopk_body(p_ref, idx_ref, *, q_pos, chunk, n_slc):
    p = p_ref[...]
    rows, width = p.shape
    half = chunk // 2
    n_in = (width // chunk) * half
    lane = lax.broadcasted_iota(jnp.int32, (1, width), 1)
    within = lane % chunk
    blk = jnp.where(within < half, (lane // chunk) * half + within, -1)
    extra = (lane >= half) & (lane < half + (n_slc - n_in))
    blk = jnp.where(extra, n_in + lane - half, blk)
    real = blk >= 0
    valid = real & (blk * SLC_BLOCK <= q_pos)
    cur = q_pos // SLC_BLOCK
    forced = (blk == 0) | (blk == cur) | (blk == cur - 1)
    base = jnp.where(extra, 0.0, p)
    x = jnp.where(valid & forced, FORCE, jnp.where(valid, base, -FORCE))
    x = jnp.where(real, x, -jnp.inf)
    blk_f = blk.astype(F32)
    out_lane = lax.broadcasted_iota(jnp.int32, (rows, LANES), 1)
    out = jnp.zeros((rows, LANES), F32)
    for r in range(min(TOP_N, n_slc)):
        mx = jnp.max(x, axis=-1, keepdims=True)
        pick = jnp.min(jnp.where(x == mx, blk_f, float(2 ** 30)), axis=-1, keepdims=True)
        out = jnp.where(out_lane == r, pick, out)
        x = jnp.where(blk_f == pick, -jnp.inf, x)
    idx_ref[...] = out.astype(jnp.int32)


def _sample_topk(pslc, q_pos, chunk, n_slc):
    rows, width = pslc.shape
    return pl.pallas_call(
        functools.partial(_stopk_body, q_pos=q_pos, chunk=chunk, n_slc=n_slc),
        grid=(1,),
        in_specs=[pl.BlockSpec((rows, width), lambda i: (0, 0))],
        out_specs=pl.BlockSpec((rows, LANES), lambda i: (0, 0)),
        out_shape=jax.ShapeDtypeStruct((rows, LANES), jnp.int32),
        compiler_params=_params(("arbitrary",)),
        name="s_topk",
    )(pslc)


def _sattn_body(idx_ref, pt_ref, q_ref, gn_ref, ocmp_ref, knew_ref, vnew_ref, kwn_ref, vwn_ref, wk_ref, wv_ref,
                sk_hbm, sv_hbm, o_ref, kbuf, vbuf, sem, *, q_pos, n_top, n_pages, past, per_step):
    step = pl.program_id(0)
    n_steps = pl.num_programs(0)
    half_rows = SLC_BLOCK * N_KV_HEADS

    def copies(s, slot, sub, g, i):
        bb = s * per_step + sub
        blk = idx_ref[bb * N_KV_HEADS + g, i]
        page = jnp.minimum(blk // 2, n_pages - 1)
        start = pl.multiple_of(pt_ref[bb, page] * (PAGE_SIZE * N_KV_HEADS) + (blk % 2) * half_rows, half_rows)
        return [pltpu.make_async_copy(hbm.at[pl.ds(start, half_rows), :], dst.at[slot, sub, g, i],
                                      sem.at[t, slot, sub, g, i])
                for t, (hbm, dst) in enumerate(((sk_hbm, kbuf), (sv_hbm, vbuf)))]

    def all_copies(s, slot):
        return [cp for sub in range(per_step) for g in range(N_KV_HEADS) for i in range(n_top)
                for cp in copies(s, slot, sub, g, i)]

    @pl.when(step == 0)
    def _():
        def one(n, carry):
            sub, rest = n // (N_KV_HEADS * n_top), n % (N_KV_HEADS * n_top)
            for cp in copies(0, 0, sub, rest // n_top, rest % n_top):
                cp.start()
            return carry
        lax.fori_loop(0, per_step * N_KV_HEADS * n_top, one, 0)

    @pl.when(step + 1 < n_steps)
    def _():
        for n, cp in enumerate(all_copies(step + 1, (step + 1) % 2)):
            cp.start(priority=n % 2)

    slot = step % 2
    for cp in all_copies(step, slot):
        cp.wait()

    pad = jnp.zeros((2 * SUBLANES - 1, HEAD_DIM), F32)
    for sub in range(per_step):
        _sattn_one(step * per_step + sub, slot, sub, pad, idx_ref, q_ref, gn_ref, ocmp_ref, knew_ref, vnew_ref, kwn_ref,
                   vwn_ref, wk_ref, wv_ref, o_ref, kbuf, vbuf, q_pos=q_pos, n_top=n_top, past=past)


def _sattn_one(b, slot, sub, pad, idx_ref, q_ref, gn_ref, ocmp_ref, knew_ref, vnew_ref, kwn_ref, vwn_ref, wk_ref, wv_ref,
               o_ref, kbuf, vbuf, *, q_pos, n_top, past):
    tail_blk = past // SLC_BLOCK
    q_row = q_ref[pl.ds(b, 1), :]
    gates = gn_ref[pl.ds(b, 1), :]

    def new_row(ref, g):
        return jnp.concatenate([ref[pl.ds(b * N_KV_HEADS + g, 1), :], pad], axis=0).astype(BF16)

    outs = []
    for g in range(N_KV_HEADS):
        qg = _group_queries(q_row, g)
        ks, vs = [], []
        key = lax.broadcasted_iota(jnp.int32, (1, n_top * SLC_BLOCK), 1)
        key_blk = jnp.zeros((1, n_top * SLC_BLOCK), jnp.int32)
        for i in range(n_top):
            blk = idx_ref[b * N_KV_HEADS + g, i]
            ks.append(kbuf[slot, sub, g, i, pl.ds(g, SLC_BLOCK, stride=N_KV_HEADS), :])
            vs.append(vbuf[slot, sub, g, i, pl.ds(g, SLC_BLOCK, stride=N_KV_HEADS), :])
            key_blk = jnp.where(key // SLC_BLOCK == i, blk, key_blk)
        k_sel = jnp.concatenate(ks, axis=0).astype(BF16)
        v_sel = jnp.concatenate(vs, axis=0).astype(BF16)
        m_sel = (key_blk * SLC_BLOCK + key % SLC_BLOCK <= q_pos) & (key_blk < tail_blk)
        tail_sel = jnp.max(jnp.where(key_blk == tail_blk, 1.0, 0.0), axis=-1, keepdims=True) > 0.5
        first = lax.broadcasted_iota(jnp.int32, (1, 2 * SUBLANES), 1) == 0
        o_sel = _two_part_attention(qg, k_sel, v_sel, m_sel, new_row(knew_ref, g), new_row(vnew_ref, g),
                                    first & tail_sel)
        wbuf = wk_ref.shape[1] // N_KV_HEADS
        k_win = wk_ref[sub, pl.ds(g, wbuf, stride=N_KV_HEADS), :].astype(BF16)
        v_win = wv_ref[sub, pl.ds(g, wbuf, stride=N_KV_HEADS), :].astype(BF16)
        dist = q_pos - (past - wbuf + lax.broadcasted_iota(jnp.int32, (1, wbuf), 1))
        m_win = (dist >= 0) & (dist <= WINDOW)
        o_win = _two_part_attention(qg, k_win, v_win, m_win, new_row(kwn_ref, g), new_row(vwn_ref, g), first)
        o_cmp = ocmp_ref[sub, GROUP * g:GROUP * (g + 1), :]
        for h in range(GROUP):
            hd = GROUP * g + h
            outs.append(gates[:, hd:hd + 1] * o_cmp[h:h + 1]
                        + gates[:, N_HEADS + hd:N_HEADS + hd + 1] * o_sel[h:h + 1]
                        + gates[:, 2 * N_HEADS + hd:2 * N_HEADS + hd + 1] * o_win[h:h + 1])
    o_ref[sub] = jnp.concatenate(outs, axis=0)


def _two_part_attention(q, k1, v1, m1, k2, v2, m2):
    dn = (((1,), (1,)), ((), ()))
    s1 = jnp.where(m1, lax.dot_general(q, k1, dn, preferred_element_type=F32), NEG)
    s2 = jnp.where(m2, lax.dot_general(q, k2, dn, preferred_element_type=F32), NEG)
    mx = jnp.maximum(jnp.max(s1, axis=-1, keepdims=True), jnp.max(s2, axis=-1, keepdims=True))
    e1 = jnp.where(m1, jnp.exp2(s1 - mx), 0.0)
    e2 = jnp.where(m2, jnp.exp2(s2 - mx), 0.0)
    l = jnp.sum(e1, axis=-1, keepdims=True) + jnp.sum(e2, axis=-1, keepdims=True)
    inv = 1.0 / jnp.where(l > 0.0, l, 1.0)
    o = (jnp.dot((e1 * inv).astype(BF16), v1, preferred_element_type=F32)
         + jnp.dot((e2 * inv).astype(BF16), v2, preferred_element_type=F32))
    return o


def _sample_attn(idx, page_table, h_q, h_sig, ocmp, new_rows, win_k, win_v, slc_k, slc_v, q_pos, past):
    nbatch, n_pages = page_table.shape
    n_top = min(TOP_N, past // SLC_BLOCK + 1)
    half_rows = SLC_BLOCK * N_KV_HEADS
    flat_new = pl.BlockSpec((nbatch * N_KV_HEADS, HEAD_DIM), lambda b, idx, pt: (0, 0))

    def whole(col_block, width):
        return pl.BlockSpec((nbatch, width), lambda b, idx, pt: (0, col_block))

    per_step = SATTN_PER_STEP
    return pl.pallas_call(
        functools.partial(_sattn_body, q_pos=q_pos, n_top=n_top, n_pages=n_pages, past=past, per_step=per_step),
        grid_spec=pltpu.PrefetchScalarGridSpec(
            num_scalar_prefetch=2,
            grid=(nbatch // per_step,),
            in_specs=[
                whole(0, D_ATTN),
                whole(SIG_GN // LANES, LANES),
                pl.BlockSpec((per_step, N_HEADS, HEAD_DIM), lambda b, idx, pt: (b, 0, 0)),
                flat_new, flat_new, flat_new, flat_new,
                pl.BlockSpec((per_step,) + win_k.shape[1:], lambda b, idx, pt: (b, 0, 0)),
                pl.BlockSpec((per_step,) + win_v.shape[1:], lambda b, idx, pt: (b, 0, 0)),
                pl.BlockSpec(memory_space=pl.ANY),
                pl.BlockSpec(memory_space=pl.ANY),
            ],
            out_specs=pl.BlockSpec((per_step, N_HEADS, HEAD_DIM), lambda b, idx, pt: (b, 0, 0)),
            scratch_shapes=[
                pltpu.VMEM((2, per_step, N_KV_HEADS, n_top, half_rows, LANES), F32),
                pltpu.VMEM((2, per_step, N_KV_HEADS, n_top, half_rows, LANES), F32),
                pltpu.SemaphoreType.DMA((2, 2, per_step, N_KV_HEADS, n_top)),
            ],
        ),
        out_shape=jax.ShapeDtypeStruct((nbatch, N_HEADS, HEAD_DIM), F32),
        compiler_params=_params(("arbitrary",)),
        name="s_attn",
    )(idx, page_table, h_q, h_sig, ocmp, *new_rows, win_k, win_v, slc_k, slc_v)


REF_KV = 3 * D_POOL
REF_GN = REF_KV + N_KV_PROJ * D_KV
REF_ZN = REF_GN + N_GATE
REF_GM = REF_ZN + D_ATTN


def _tiles(start, width):
    return tuple(range(start, start + width, PROJ_TN))


STARTS_U = _tiles(0, D_POOL)
STARTS_SILU = _tiles(D_POOL, D_POOL) + _tiles(REF_ZN, D_ATTN)
STARTS_Q = _tiles(2 * D_POOL, D_ATTN)
STARTS_SIG = _tiles(REF_GM, N_MERGE_COLS) + (REF_GN,)


def _block_diag2(w):
    z = jnp.zeros_like(w)
    return jnp.concatenate([jnp.concatenate([w, z], axis=-1), jnp.concatenate([z, w], axis=-1)], axis=-2)


def _project(x, wt, b, tm, tm_kv, act_dtype):
    u = _proj(x, wt, b, STARTS_U, ACT_NONE, tm, F32)
    h_silu = _proj(x, wt, b, STARTS_SILU, ACT_SILU, tm, act_dtype)
    h_q = _proj(x, wt, b, STARTS_Q, ACT_SCALE, tm, act_dtype)
    h_sig = _proj(x, wt, b, STARTS_SIG, ACT_SIGMOID, tm, act_dtype)
    kv = _proj_kv(x, wt, b, REF_KV, tm_kv)
    return u, h_silu, h_q, h_sig, kv


def kernel(x_prompt, x_sample, cache_cmp_k, cache_cmp_v, cache_slc_k, cache_slc_v, cache_win_k, cache_win_v,
           state_pool, page_table, w_in, b_in, pool_w, pool_scale, cmp_pe_k, cmp_w1_k, cmp_w2_k, cmp_pe_v, cmp_w1_v,
           cmp_w2_v, w_up_pool, w_up_nsa, w_out, ln_g, ln_b):
    nb_p, seq, _ = x_prompt.shape
    nb_s = x_sample.shape[0]
    n_pages = page_table.shape[1]
    past = n_pages * PAGE_SIZE
    n_phys = cache_cmp_k.shape[1]
    wbuf = cache_win_k.shape[2]

    wt = w_in[0].T
    b = b_in[0][None, :]
    kdim = CMP_BLOCK * D_KV
    cmp_k = (cmp_pe_k[0], _block_diag2(cmp_w1_k[0]).reshape(kdim, D_KV).astype(BF16),
             _block_diag2(cmp_w2_k[0]).astype(BF16))
    cmp_v = (cmp_pe_v[0], _block_diag2(cmp_w1_v[0]).reshape(kdim, D_KV).astype(BF16),
             _block_diag2(cmp_w2_v[0]).astype(BF16))
    pw = pool_w[0].astype(BF16)
    ps = pool_scale[0][None, :]
    wup = w_up_pool[0].astype(BF16)
    wun = w_up_nsa[0].astype(BF16)
    wo = w_out[0].astype(BF16)
    lg = ln_g[0][None, :]
    lb = ln_b[0][None, :]

    xp = x_prompt.reshape(nb_p * seq, D_MODEL)
    xs = x_sample.reshape(nb_s, D_MODEL)
    u_p, hsilu_p, hq_p, hsig_p, kv_p = _project(xp.astype(BF16), wt, b, tm=2048, tm_kv=2048, act_dtype=BF16)
    u_s, hsilu_s, hq_s, hsig_s, kv_s = _project(xs.astype(BF16), wt, b, tm=nb_s, tm_kv=nb_s, act_dtype=F32)

    blocks = (-1, CMP_ROWS, LANES)
    prompt_pages = seq // PAGE_SIZE
    ident = jnp.arange(nb_p * prompt_pages, dtype=jnp.int32).reshape(nb_p, prompt_pages)
    kc_p = _compress(ident, kv_p[0].reshape(blocks), *cmp_k, pps=prompt_pages)
    vc_p = _compress(ident, kv_p[1].reshape(blocks), *cmp_v, pps=prompt_pages)
    kts, vs, ktw, vw = _kprep(kv_p[2], kv_p[3], kv_p[4], kv_p[5], nb_p, seq, tk=512)
    o_p = _prompt_attn(hq_p, hsig_p, kc_p, vc_p, kts, vs, ktw, vw, nb_p, seq, tq=256, tk=512)
    y_p = _tail(u_p, hsilu_p, hsig_p, o_p, xp, pw, ps, wup, wun, wo, lg, lb, tm=512, seq=seq)

    chunk_pages = 64
    kc_s = _compress(page_table, cache_cmp_k.reshape(blocks), *cmp_k, pps=chunk_pages)
    vc_s = _compress(page_table, cache_cmp_v.reshape(blocks), *cmp_v, pps=chunk_pages)
    chunk = chunk_pages * BLOCKS_PER_PAGE
    n_slc = past // SLC_BLOCK + 1
    ocmp_s, pslc_s = _sample_cmp(hq_s, kc_s, vc_s, past, chunk)
    idx = _sample_topk(pslc_s.reshape(nb_s * N_KV_HEADS, -1), past, chunk, n_slc)
    flat = (n_phys * PAGE_SIZE * N_KV_HEADS, HEAD_DIM)
    o_s = _sample_attn(idx, page_table, hq_s, hsig_s, ocmp_s, kv_s[2:6],
                       cache_win_k.reshape(nb_s, wbuf * N_KV_HEADS, HEAD_DIM),
                       cache_win_v.reshape(nb_s, wbuf * N_KV_HEADS, HEAD_DIM),
                       cache_slc_k.reshape(flat), cache_slc_v.reshape(flat), past, past)
    ctx = jnp.concatenate([state_pool[0], u_s[:, None, :]], axis=1)
    m_s = _pool_m(ctx.reshape(nb_s * (POOL_CTX + 1), D_POOL), 0, nb_s * (POOL_CTX + 1), tm=nb_s * (POOL_CTX + 1),
                  seq=nb_s * (POOL_CTX + 1), fixed_pos=past)
    m_s = m_s.reshape(nb_s, POOL_CTX + 1, D_POOL)[:, POOL_CTX]
    y_s = _tail(m_s, hsilu_s, hsig_s, o_s.reshape(nb_s, D_ATTN), xs, pw, ps, wup, wun, wo, lg, lb, tm=nb_s)

    wl = min(WINDOW, seq)
    kv_p = [a.reshape(1, nb_p, seq, N_KV_HEADS, HEAD_DIM) for a in kv_p]
    kv_s = [a.reshape(1, nb_s, 1, N_KV_HEADS, HEAD_DIM) for a in kv_s]
    return (
        y_p.reshape(nb_p, seq, D_MODEL),
        y_s.reshape(nb_s, 1, D_MODEL),
        kv_p[0], kv_p[1], kv_p[2], kv_p[3],
        kv_p[4][:, :, seq - wl:], kv_p[5][:, :, seq - wl:],
        u_p.reshape(nb_p, seq, D_POOL)[None, :, seq - POOL_CTX:],
        kv_s[0], kv_s[1], kv_s[2], kv_s[3],
        jnp.concatenate([cache_win_k, kv_s[4]], axis=2)[:, :, 1:],
        jnp.concatenate([cache_win_v, kv_s[5]], axis=2)[:, :, 1:],
        ctx[None, :, 1:],
    )
```
